```python
import math, functools
import jax, jax.numpy as jnp
from jax import lax
import numpy as np

D_MODEL = 1024
BATCH = 8
SEQ = 4096
DEPTH = 4

CTX_LEN = 256
GRID_W = 64
N_BRANCH = 4
BRANCH_W = D_MODEL // N_BRANCH
EPS = 1e-6
HY_W = BRANCH_W
HY_ORDER = 2
HY_SHORT = 3
HY_EMB = 33
HY_FFN = 64
HY_TARGET = 1e-2
HY_FAST = 0.3
HY_SLOW = 1.5
CF_W = BRANCH_W
CF_WIDTH = 31
HEAD_DIM = 64
GQA_QH = BRANCH_W // HEAD_DIM
GQA_KVH = GQA_QH // 2
DIFF_H = 4
DIFF_D = BRANCH_W // DIFF_H // 2
DIFF_DV = 2 * DIFF_D
ROPE_BASE = 10000.0
QBLK = 128
D_FF = 2816
N_EXPERTS = 8
TOP_K = 2
D_FF_EXPERT = 3584
MOE_BLK = 256
N_DENSE = (DEPTH + 1) // 2
N_MOE = DEPTH // 2
HY_IN = (HY_ORDER + 1) * HY_W
CF_IN = 2 * CF_W
GQA_IN = (GQA_QH + 2 * GQA_KVH) * HEAD_DIM
DIFF_IN = 2 * (DIFF_H * 2 * DIFF_D) + DIFF_H * DIFF_DV
GATE_IN = N_BRANCH * D_MODEL
D_IN = HY_IN + CF_IN + GQA_IN + DIFF_IN + GATE_IN
IN_OFFSETS = [HY_IN, HY_IN + CF_IN, HY_IN + CF_IN + GQA_IN, HY_IN + CF_IN + GQA_IN + DIFF_IN]
HY_FILT_OUT = 2 * HY_ORDER * HY_W

kernel_name = 'hybrid_gated_mixers_ctx_prefix_dit'


def rmsnorm(x, g):
    xf = x.astype(jnp.float32)
    return (xf * lax.rsqrt(jnp.mean(xf * xf, axis=-1, keepdims=True) + EPS)).astype(x.dtype) * g


def layernorm(x, g, b):
    xf = x.astype(jnp.float32)
    mu = jnp.mean(xf, axis=-1, keepdims=True)
    var = jnp.mean(jnp.square(xf - mu), axis=-1, keepdims=True)
    return ((xf - mu) * lax.rsqrt(var + EPS)).astype(x.dtype) * g + b


def modulate(h, shift, scale):
    return h * (1.0 + scale) + shift


def depthwise_conv(x, w, b):
    pad = w.shape[0] // 2
    y = lax.conv_general_dilated(x, w[:, None, :].astype(x.dtype), window_strides=(1,), padding=[(pad, pad)],
                                 dimension_numbers=('NWC', 'WIO', 'NWC'), feature_group_count=x.shape[-1])
    return y + b


def rope_2d(x, rows, cols):
    half = x.shape[-1] // 2
    nf = half // 2
    inv = ROPE_BASE ** (-jnp.arange(nf, dtype=jnp.float32) / nf)
    bshape = (x.shape[1],) + (1,) * (x.ndim - 3) + (nf,)

    def rot(xa, p):
        ang = (p[:, None] * inv).reshape(bshape)
        cs, sn = jnp.cos(ang), jnp.sin(ang)
        x1, x2 = xa[..., :nf], xa[..., nf:]
        return jnp.concatenate([x1 * cs - x2 * sn, x2 * cs + x1 * sn], axis=-1)

    xf = x.astype(jnp.float32)
    return jnp.concatenate([rot(xf[..., :half], rows), rot(xf[..., half:], cols)], axis=-1).astype(x.dtype)


def hyena_filter_fft(L, w1, b1, w2, b2, w3, b3, w4, freq):
    f32 = jnp.float32
    t = jnp.linspace(0.0, 1.0, L, dtype=f32)[:, None]
    bands = (HY_EMB - 1) // 2
    fr = jnp.linspace(1e-4, bands - 1, bands, dtype=f32)[None, :]
    wpos = 2.0 * math.pi * jnp.arange(L, dtype=f32)[:, None] / L
    z = jnp.concatenate([t, jnp.cos(fr * wpos), -jnp.sin(fr * wpos)], axis=-1)
    fq = freq.astype(f32)
    a = jnp.sin(fq * (z @ w1.astype(f32) + b1.astype(f32)))
    a = jnp.sin(fq * (a @ w2.astype(f32) + b2.astype(f32)))
    a = jnp.sin(fq * (a @ w3.astype(f32) + b3.astype(f32)))
    hf = (a @ w4.astype(f32)).reshape(L, 2, HY_ORDER, HY_W)
    deltas = jnp.abs(jnp.linspace(math.log(HY_TARGET) / HY_SLOW, math.log(HY_TARGET) / HY_FAST, HY_W, dtype=f32))
    hf = hf * jnp.exp(-t[:, :, None, None] * deltas)
    fwd, bwd = hf[:, 0], hf[:, 1]
    k = jnp.concatenate([fwd, jnp.zeros((1, HY_ORDER, HY_W), f32), bwd[1:][::-1]], axis=0)
    k = k * lax.rsqrt(jnp.sum(k * k, axis=0, keepdims=True) + EPS)
    return jnp.fft.rfft(k, axis=0)


def hyena(u, short_w, short_b, w1, b1, w2, b2, w3, b3, w4, freq, bias):
    L = u.shape[1]
    u = depthwise_conv(u, short_w, short_b)
    x1, x2, v = jnp.split(u, 3, axis=-1)
    kf = hyena_filter_fft(L, w1, b1, w2, b2, w3, b3, w4, freq)
    z = v
    for o, gate in enumerate((x1, x2)):
        zf = jnp.fft.rfft(z.astype(jnp.float32), n=2 * L, axis=1)
        conv = jnp.fft.irfft(zf * kf[:, o], n=2 * L, axis=1)[:, :L]
        z = gate * (conv.astype(z.dtype) + bias[o] * z)
    return z


def conformer_conv(u, dw_w, dw_b, ln_g, ln_b):
    a, g = jnp.split(u, 2, axis=-1)
    y = depthwise_conv(a * jax.nn.sigmoid(g), dw_w, dw_b)
    return jax.nn.silu(layernorm(y, ln_g, ln_b))


def gqa_qkv(u, qn, kn, pos):
    B, L, _ = u.shape
    nq, nk = GQA_QH * HEAD_DIM, GQA_KVH * HEAD_DIM
    q = rmsnorm(u[..., :nq].reshape(B, L, GQA_QH, HEAD_DIM), qn)
    k = rmsnorm(u[..., nq:nq + nk].reshape(B, L, GQA_KVH, HEAD_DIM), kn)
    v = u[..., nq + nk:].reshape(B, L, GQA_KVH, HEAD_DIM)
    if pos is not None:
        q, k = rope_2d(q, *pos), rope_2d(k, *pos)
    return q.reshape(B, L, GQA_KVH, GQA_QH // GQA_KVH, HEAD_DIM), k, v


def gqa_core(qb, k, v):
    s = jnp.einsum('bqhgd,bkhd->bhgqk', qb, k).astype(jnp.float32) * (HEAD_DIM ** -0.5)
    p = jax.nn.softmax(s, axis=-1)
    return jnp.einsum('bhgqk,bkhd->bqhgd', p.astype(v.dtype), v)


def diff_qkv(u, qn, kn, pos):
    B, L, _ = u.shape
    nq = DIFF_H * 2 * DIFF_D
    q = rmsnorm(u[..., :nq].reshape(B, L, DIFF_H, 2, DIFF_D), qn)
    k = rmsnorm(u[..., nq:2 * nq].reshape(B, L, DIFF_H, 2, DIFF_D), kn)
    v = u[..., 2 * nq:].reshape(B, L, DIFF_H, DIFF_DV)
    if pos is not None:
        q, k = rope_2d(q, *pos), rope_2d(k, *pos)
    return q, k, v


def diff_core(qb, k, v, lam):
    s = jnp.einsum('bqhmd,bkhmd->bhmqk', qb, k).astype(jnp.float32) * (DIFF_D ** -0.5)
    p = jax.nn.softmax(s, axis=-1)
    a = p[:, :, 0] - lam * p[:, :, 1]
    return jnp.einsum('bhqk,bkhd->bqhd', a.astype(v.dtype), v)


def diff_out(y, g, lam_init):
    B, L = y.shape[:2]
    return (rmsnorm(y, g) * (1.0 - lam_init)).reshape(B, L, DIFF_H * DIFF_DV)


def sweep_query_blocks(fn, q, k, v):
    B, L = q.shape[:2]
    nb = L // QBLK
    qb = jnp.moveaxis(q.reshape((B, nb, QBLK) + q.shape[2:]), 1, 0)
    out = lax.map(lambda qi: fn(qi, k, v), qb)
    return jnp.moveaxis(out, 0, 1).reshape((B, L) + out.shape[3:])


def merge_branches(branches, u_gate, w_branch, w_out):
    D = w_out.shape[0]
    g = jax.nn.sigmoid(u_gate.astype(jnp.float32)).astype(u_gate.dtype)
    acc = g[..., :D] * (branches[0] @ w_branch[0])
    for i in range(1, N_BRANCH):
        acc = acc + g[..., i * D:(i + 1) * D] * (branches[i] @ w_branch[i])
    return acc @ w_out


def swiglu(t, w1, w3, w2):
    return (jax.nn.silu(t @ w1) * (t @ w3)) @ w2


def moe_swiglu(h, w_router, w1, w3, w2):
    n, D = h.shape
    logits = (h @ w_router).astype(jnp.float32)
    top_v, top_e = lax.top_k(logits, TOP_K)
    gate = jax.nn.softmax(top_v, axis=-1)
    nk = n * TOP_K
    flat_e = top_e.reshape(nk)
    flat_tok = jnp.repeat(jnp.arange(n, dtype=jnp.int32), TOP_K)
    flat_g = gate.reshape(nk)
    order = jnp.argsort(flat_e)
    s_e, s_tok, s_g = flat_e[order], flat_tok[order], flat_g[order]
    counts = jnp.bincount(flat_e, length=N_EXPERTS)
    padded = (counts + MOE_BLK - 1) // MOE_BLK * MOE_BLK
    start = jnp.cumsum(counts) - counts
    pend = jnp.cumsum(padded)
    pstart = pend - padded
    dest = pstart[s_e] + jnp.arange(nk, dtype=jnp.int32) - start[s_e]
    n_blocks = -(-nk // MOE_BLK) + N_EXPERTS
    p = n_blocks * MOE_BLK
    tok_p = jnp.full((p,), n, jnp.int32).at[dest].set(s_tok)
    g_p = jnp.zeros((p,), jnp.float32).at[dest].set(s_g)
    blk_e = jnp.minimum(jnp.searchsorted(pend, jnp.arange(n_blocks, dtype=jnp.int32) * MOE_BLK, side='right'),
                        N_EXPERTS - 1)
    h_pad = jnp.concatenate([h, jnp.zeros((1, D), h.dtype)], axis=0)
    xb = h_pad[tok_p].reshape(n_blocks, MOE_BLK, D)

    def expert_block(args):
        xi, e = args
        return (jax.nn.silu(xi @ w1[e]) * (xi @ w3[e])) @ w2[e]

    yb = lax.map(expert_block, (xb, blk_e)).reshape(p, D)
    yb = yb * g_p[:, None].astype(yb.dtype)
    return jnp.zeros((n + 1, D), yb.dtype).at[tok_p].add(yb)[:n]


def setup_inputs(seed: int = 0) -> dict:
    key = jax.random.key(seed)
    ks = iter(jax.random.split(key, 48))
    f32 = jnp.float32
    D = D_MODEL

    def nrm(shape, scale):
        return jax.random.normal(next(ks), shape, f32) * scale

    def gain(shape):
        return 1.0 + nrm(shape, 0.02)

    return {
        'x': nrm((BATCH, SEQ, D), 1.0),
        'c': nrm((BATCH, D), 1.0),
        'ctx': nrm((BATCH, CTX_LEN, D), 1.0),
        'c_ctx': nrm((D,), 1.0),
        'w_ada': nrm((DEPTH, D, 6 * D), 0.5 * D ** -0.5),
        'b_ada': nrm((DEPTH, 6 * D), 0.02),
        'norm_mix': gain((DEPTH, D)),
        'norm_ffn': gain((DEPTH, D)),
        'w_in': nrm((DEPTH, D, D_IN), D ** -0.5),
        'hy_short_w': nrm((DEPTH, HY_SHORT, HY_IN), HY_SHORT ** -0.5),
        'hy_short_b': nrm((DEPTH, HY_IN), 0.02),
        'hy_w1': nrm((DEPTH, HY_EMB, HY_FFN), HY_EMB ** -0.5),
        'hy_b1': nrm((DEPTH, HY_FFN), 0.02),
        'hy_w2': nrm((DEPTH, HY_FFN, HY_FFN), HY_FFN ** -0.5),
        'hy_b2': nrm((DEPTH, HY_FFN), 0.02),
        'hy_w3': nrm((DEPTH, HY_FFN, HY_FFN), HY_FFN ** -0.5),
        'hy_b3': nrm((DEPTH, HY_FFN), 0.02),
        'hy_w4': nrm((DEPTH, HY_FFN, HY_FILT_OUT), HY_FFN ** -0.5),
        'hy_freq': gain((DEPTH, HY_FFN)),
        'hy_bias': nrm((DEPTH, HY_ORDER, HY_W), 1.0),
        'cf_dw_w': nrm((DEPTH, CF_WIDTH, CF_W), CF_WIDTH ** -0.5),
        'cf_dw_b': nrm((DEPTH, CF_W), 0.02),
        'cf_ln_g': gain((DEPTH, CF_W)),
        'cf_ln_b': nrm((DEPTH, CF_W), 0.02),
        'gqa_qn': gain((DEPTH, HEAD_DIM)),
        'gqa_kn': gain((DEPTH, HEAD_DIM)),
        'diff_qn': gain((DEPTH, 2, DIFF_D)),
        'diff_kn': gain((DEPTH, 2, DIFF_D)),
        'diff_lq1': nrm((DEPTH, DIFF_D), 0.1),
        'diff_lk1': nrm((DEPTH, DIFF_D), 0.1),
        'diff_lq2': nrm((DEPTH, DIFF_D), 0.1),
        'diff_lk2': nrm((DEPTH, DIFF_D), 0.1),
        'diff_subln': gain((DEPTH, DIFF_DV)),
        'w_branch': nrm((DEPTH, N_BRANCH, BRANCH_W, D), BRANCH_W ** -0.5),
        'w_out': nrm((DEPTH, D, D), D ** -0.5),
        'ffn_w1': nrm((N_DENSE, D, D_FF), D ** -0.5),
        'ffn_w3': nrm((N_DENSE, D, D_FF), D ** -0.5),
        'ffn_w2': nrm((N_DENSE, D_FF, D), D_FF ** -0.5),
        'moe_router': nrm((N_MOE, D, N_EXPERTS), D ** -0.5),
        'moe_w1': nrm((N_MOE, N_EXPERTS, D, D_FF_EXPERT), D ** -0.5),
        'moe_w3': nrm((N_MOE, N_EXPERTS, D, D_FF_EXPERT), D ** -0.5),
        'moe_w2': nrm((N_MOE, N_EXPERTS, D_FF_EXPERT, D), D_FF_EXPERT ** -0.5),
    }


def reference(x, c, ctx, c_ctx, w_ada, b_ada, norm_mix, norm_ffn, w_in,
              hy_short_w, hy_short_b, hy_w1, hy_b1, hy_w2, hy_b2, hy_w3, hy_b3, hy_w4, hy_freq, hy_bias,
              cf_dw_w, cf_dw_b, cf_ln_g, cf_ln_b, gqa_qn, gqa_kn,
              diff_qn, diff_kn, diff_lq1, diff_lk1, diff_lq2, diff_lk2, diff_subln,
              w_branch, w_out, ffn_w1, ffn_w3, ffn_w2, moe_router, moe_w1, moe_w3, moe_w2):
    f32 = jnp.float32
    B, S, D = x.shape
    n_ctx = ctx.shape[1]
    n_rows = S // GRID_W
    rows = jnp.broadcast_to(jnp.arange(n_rows, dtype=f32)[:, None], (n_rows, GRID_W)).reshape(S)
    cols = jnp.broadcast_to(jnp.arange(GRID_W, dtype=f32)[None, :], (n_rows, GRID_W)).reshape(S)
    pos = (rows, cols)
    s_lat = jax.nn.silu(c)
    s_ctx = jax.nn.silu(c_ctx)
    xc = ctx
    for l in range(DEPTH):
        last = l == DEPTH - 1
        mod = (s_lat @ w_ada[l] + b_ada[l]).reshape(B, 1, 6, D)
        modc = (s_ctx @ w_ada[l] + b_ada[l]).reshape(6, D)
        h = modulate(rmsnorm(x, norm_mix[l]), mod[:, :, 0], mod[:, :, 1])
        hc = modulate(rmsnorm(xc, norm_mix[l]), modc[0], modc[1])
        u_hy, u_cf, u_gqa, u_diff, u_gate = jnp.split(h @ w_in[l], IN_OFFSETS, axis=-1)
        uc_hy, uc_cf, uc_gqa, uc_diff, uc_gate = jnp.split(hc @ w_in[l], IN_OFFSETS, axis=-1)
        lam_init = 0.8 - 0.6 * math.exp(-0.3 * l)
        lam = (jnp.exp(jnp.sum(diff_lq1[l].astype(f32) * diff_lk1[l].astype(f32)))
               - jnp.exp(jnp.sum(diff_lq2[l].astype(f32) * diff_lk2[l].astype(f32))) + lam_init)
        qg, kg, vg = gqa_qkv(u_gqa, gqa_qn[l], gqa_kn[l], pos)
        qgc, kgc, vgc = gqa_qkv(uc_gqa, gqa_qn[l], gqa_kn[l], None)
        qd, kd, vd = diff_qkv(u_diff, diff_qn[l], diff_kn[l], pos)
        qdc, kdc, vdc = diff_qkv(uc_diff, diff_qn[l], diff_kn[l], None)
        hy_p = (hy_short_w[l], hy_short_b[l], hy_w1[l], hy_b1[l], hy_w2[l], hy_b2[l], hy_w3[l], hy_b3[l],
                hy_w4[l], hy_freq[l], hy_bias[l])
        cf_p = (cf_dw_w[l], cf_dw_b[l], cf_ln_g[l], cf_ln_b[l])
        y_gqa = sweep_query_blocks(gqa_core, qg, jnp.concatenate([kg, kgc], axis=1),
                                   jnp.concatenate([vg, vgc], axis=1)).reshape(B, S, BRANCH_W)
        y_diff = sweep_query_blocks(functools.partial(diff_core, lam=lam), qd, jnp.concatenate([kd, kdc], axis=1),
                                    jnp.concatenate([vd, vdc], axis=1))
        branches = (hyena(u_hy, *hy_p), conformer_conv(u_cf, *cf_p), y_gqa,
                    diff_out(y_diff, diff_subln[l], lam_init))
        mix = merge_branches(branches, u_gate, w_branch[l], w_out[l])
        if not last:
            branches_c = (hyena(uc_hy, *hy_p), conformer_conv(uc_cf, *cf_p),
                          gqa_core(qgc, kgc, vgc).reshape(B, n_ctx, BRANCH_W),
                          diff_out(diff_core(qdc, kdc, vdc, lam), diff_subln[l], lam_init))
            xc = xc + modc[2] * merge_branches(branches_c, uc_gate, w_branch[l], w_out[l])
        x = x + mod[:, :, 2] * mix
        tok = modulate(rmsnorm(x, norm_ffn[l]), mod[:, :, 3], mod[:, :, 4]).reshape(B * S, D)
        if not last:
            h2c = modulate(rmsnorm(xc, norm_ffn[l]), modc[3], modc[4]).reshape(B * n_ctx, D)
            tok = jnp.concatenate([tok, h2c], axis=0)
        i = l // 2
        if l % 2 == 0:
            f = swiglu(tok, ffn_w1[i], ffn_w3[i], ffn_w2[i])
        else:
            f = moe_swiglu(tok, moe_router[i], moe_w1[i], moe_w3[i], moe_w2[i])
        x = x + mod[:, :, 5] * f[:B * S].reshape(B, S, D)
        if not last:
            xc = xc + modc[5] * f[B * S:].reshape(B, n_ctx, D)
    return x
```

```python
import functools
import math

import numpy as np
import jax
import jax.numpy as jnp
from jax import lax
from jax.experimental import pallas as pl
from jax.experimental.pallas import tpu as pltpu

F32 = jnp.float32
BF16 = jnp.bfloat16

D_MODEL = 1024
GRID_W = 64
BRANCH_W = 256
EPS = 1e-6
HY_EMB = 33
HY_EMB_PAD = 128
HY_FFN = 64
HY_TARGET = 1e-2
HY_FAST = 0.3
HY_SLOW = 1.5
CF_WIDTH = 31
HEAD_DIM = 64
DIFF_D = 32
ROPE_BASE = 10000.0
N_EXPERTS = 8
LANES = 128
SMALL_COLS = 2560
GATE_COLS = 4 * D_MODEL
VMEM_LIMIT = 48 * 1024 * 1024


def _cp(sem, vmem=VMEM_LIMIT):
    return pltpu.CompilerParams(dimension_semantics=sem, vmem_limit_bytes=vmem)


def _sigmoid(x):
    return 1.0 / (1.0 + jnp.exp(-x))


def _dot(a, b):
    return jnp.dot(a, b, preferred_element_type=F32)


def _split(a):
    hi = a.astype(BF16)
    lo = (a - hi.astype(F32)).astype(BF16)
    return hi, lo


def _dot3(a, b):
    ah, al = _split(a)
    bh, bl = _split(b)
    return _dot(ah, bh) + (_dot(al, bh) + _dot(ah, bl))


def _norm_mod(x, gain, shift, scale):
    ms = jnp.mean(x * x, axis=-1, keepdims=True)
    return (x * lax.rsqrt(ms + EPS) * gain) * (1.0 + scale) + shift


def _ada_kernel(c_ref, w_ref, b_ref, o_ref):
    c = c_ref[...]
    s = c * _sigmoid(c)
    o_ref[0] = _dot3(s, w_ref[0]) + b_ref[0]


def _ada_mods(c16, w_ada, b_ada):
    depth, d, n6 = w_ada.shape
    tn = 512
    return pl.pallas_call(
        _ada_kernel,
        grid=(depth, n6 // tn),
        in_specs=[pl.BlockSpec((16, d), lambda l, j: (0, 0)),
                  pl.BlockSpec((1, d, tn), lambda l, j: (l, 0, j)),
                  pl.BlockSpec((1, 1, tn), lambda l, j: (l, 0, j))],
        out_specs=pl.BlockSpec((1, 16, tn), lambda l, j: (l, 0, j)),
        out_shape=jax.ShapeDtypeStruct((depth, 16, n6), F32),
        compiler_params=_cp(("arbitrary", "arbitrary")),
        name="ada_mods",
    )(c16, w_ada, b_ada.reshape(depth, 1, n6))


def _in_proj_kernel(x_ref, mod_ref, g_ref, w_ref, u_ref, gate_ref, h_scr, *, n_small):
    j = pl.program_id(1)

    @pl.when(j == 0)
    def _():
        m = mod_ref[0]
        h_scr[...] = _norm_mod(x_ref[...], g_ref[...], m[0:1, :], m[1:2, :]).astype(BF16)

    r = _dot(h_scr[...], w_ref[...])

    @pl.when(j < n_small)
    def _():
        u_ref[...] = r

    @pl.when(j >= n_small)
    def _():
        gate_ref[...] = _sigmoid(r).astype(BF16)


def _in_proj(x, mods_l, gain, w_bf, dims):
    n, d = x.shape
    tm, tn = dims["tm_big"], 256
    n_small = SMALL_COLS // tn
    n_cols = w_bf.shape[1] // tn
    mrow = dims["mod_row"](tm)
    return pl.pallas_call(
        functools.partial(_in_proj_kernel, n_small=n_small),
        grid=(n // tm, n_cols),
        in_specs=[pl.BlockSpec((tm, d), lambda i, j: (i, 0)),
                  pl.BlockSpec((1, 6, d), lambda i, j: (mrow(i), 0, 0)),
                  pl.BlockSpec((1, d), lambda i, j: (0, 0)),
                  pl.BlockSpec((d, tn), lambda i, j: (0, j))],
        out_specs=[pl.BlockSpec((tm, tn), lambda i, j: (i, jnp.minimum(j, n_small - 1))),
                   pl.BlockSpec((tm, tn), lambda i, j: (i, jnp.maximum(j - n_small, 0)))],
        out_shape=[jax.ShapeDtypeStruct((n, SMALL_COLS), F32),
                   jax.ShapeDtypeStruct((n, GATE_COLS), BF16)],
        scratch_shapes=[pltpu.VMEM((tm, d), BF16)],
        compiler_params=_cp(("arbitrary", "arbitrary")),
        name="in_proj",
    )(x, mods_l, gain, w_bf)


def _short_conv_kernel(u_ref, w_ref, b_ref, o_ref, pad_scr, *, L, tc):
    pad_scr[0:8, :] = jnp.zeros((8, BRANCH_W), F32)
    pad_scr[8 + L:16 + L, :] = jnp.zeros((8, BRANCH_W), F32)
    pad_scr[8:8 + L, :] = u_ref[...]
    w = w_ref[0]
    b = b_ref[0]

    def body(c, carry):
        t0 = pl.multiple_of(c * tc, tc)
        win = pad_scr[pl.ds(t0, tc + 16), :]
        acc = b + w[0:1, :] * win[7:7 + tc]
        acc = acc + w[1:2, :] * win[8:8 + tc]
        acc = acc + w[2:3, :] * win[9:9 + tc]
        o_ref[pl.ds(t0, tc), :] = acc
        return carry

    lax.fori_loop(0, L // tc, body, 0)


def _short_conv(u_small, w, b, L, nseq, row_base):
    w3 = jnp.transpose(w.reshape(3, 3, BRANCH_W), (1, 0, 2))
    w3 = jnp.pad(w3, ((0, 0), (0, 5), (0, 0)))
    b3 = b.reshape(3, 1, BRANCH_W)
    tc = min(L, 256)
    return pl.pallas_call(
        functools.partial(_short_conv_kernel, L=L, tc=tc),
        grid=(nseq, 3),
        in_specs=[pl.BlockSpec((L, BRANCH_W), lambda s, j: (row_base + s, j)),
                  pl.BlockSpec((1, 8, BRANCH_W), lambda s, j: (j, 0, 0)),
                  pl.BlockSpec((1, 1, BRANCH_W), lambda s, j: (j, 0, 0))],
        out_specs=pl.BlockSpec((L, BRANCH_W), lambda s, j: (s, j)),
        out_shape=jax.ShapeDtypeStruct((nseq * L, 3 * BRANCH_W), F32),
        scratch_shapes=[pltpu.VMEM((L + 16, BRANCH_W), F32)],
        compiler_params=_cp(("arbitrary", "arbitrary")),
        name="hy_short_conv",
    )(u_small, w3, b3)


def _filter_consts(L):
    t = np.linspace(0.0, 1.0, L)[:, None]
    bands = (HY_EMB - 1) // 2
    fr = np.linspace(1e-4, bands - 1, bands)[None, :]
    wpos = 2.0 * math.pi * np.arange(L)[:, None] / L
    z = np.concatenate([t, np.cos(fr * wpos), -np.sin(fr * wpos)], axis=-1)
    z = np.pad(z, ((0, 0), (0, HY_EMB_PAD - HY_EMB)))
    deltas = np.abs(np.linspace(math.log(HY_TARGET) / HY_SLOW, math.log(HY_TARGET) / HY_FAST, BRANCH_W))
    win = np.exp(-t * deltas[None, :])
    return jnp.asarray(z, F32), jnp.asarray(win, F32)


def _filter_kernel(z_ref, w1, b1, w2, b2, w3, b3, w4, fq, win_ref, hf_ref, ss_ref, *, tr):
    i = pl.program_id(0)
    f = fq[...]
    a = jnp.sin(f * (_dot3(z_ref[...], w1[...]) + b1[...]))
    a = jnp.sin(f * (_dot3(a, w2[...]) + b2[...]))
    a = jnp.sin(f * (_dot3(a, w3[...]) + b3[...]))
    h = _dot3(a, w4[...])
    win = win_ref[...]
    h = h * jnp.concatenate([win, win, win, win], axis=1)
    row = lax.broadcasted_iota(jnp.int32, h.shape, 0) + i * tr
    col = lax.broadcasted_iota(jnp.int32, h.shape, 1)
    h = jnp.where((row == 0) & (col >= 2 * BRANCH_W), 0.0, h)
    hf_ref[...] = h

    @pl.when(i == 0)
    def _():
        ss_ref[...] = jnp.zeros_like(ss_ref)

    ss_ref[...] += jnp.sum(h * h, axis=0, keepdims=True)


def _hyena_filter(L, w1, b1, w2, b2, w3, b3, w4, freq):
    z, win = _filter_consts(L)
    tr = min(L, 256)
    w1p = jnp.pad(w1, ((0, HY_EMB_PAD - HY_EMB), (0, 0)))
    full = lambda a: pl.BlockSpec(a.shape, lambda i: (0,) * a.ndim)
    args = [w1p, b1.reshape(1, -1), w2, b2.reshape(1, -1), w3, b3.reshape(1, -1), w4, freq.reshape(1, -1)]
    n_out = w4.shape[1]
    return pl.pallas_call(
        functools.partial(_filter_kernel, tr=tr),
        grid=(L // tr,),
        in_specs=[pl.BlockSpec((tr, HY_EMB_PAD), lambda i: (i, 0))] + [full(a) for a in args]
        + [pl.BlockSpec((tr, BRANCH_W), lambda i: (i, 0))],
        out_specs=[pl.BlockSpec((tr, n_out), lambda i: (i, 0)),
                   pl.BlockSpec((1, n_out), lambda i: (0, 0))],
        out_shape=[jax.ShapeDtypeStruct((L, n_out), F32), jax.ShapeDtypeStruct((1, n_out), F32)],
        compiler_params=_cp(("arbitrary",)),
        name="hy_filter",
    )(z, *args, win)


def _dft_mats(L):
    N = 2 * L
    blk = 64
    k = jnp.arange(L, dtype=jnp.int32)[:, None]
    nh = jnp.arange(L // blk, dtype=jnp.int32)[None, :]
    nl = jnp.arange(blk, dtype=jnp.int32)[None, :]
    w = 2.0 * math.pi / N
    a = ((k * (blk * nh)) % N).astype(F32) * w
    b = ((k * nl) % N).astype(F32) * w
    ca, sa, cb, sb = jnp.cos(a), jnp.sin(a), jnp.cos(b), jnp.sin(b)
    cos = (ca[:, :, None] * cb[:, None, :] - sa[:, :, None] * sb[:, None, :]).reshape(L, L)
    sin = (sa[:, :, None] * cb[:, None, :] + ca[:, :, None] * sb[:, None, :]).reshape(L, L)
    alt = jnp.where(jnp.arange(L) % 2 == 0, 1.0, -1.0).astype(F32)
    first = (jnp.arange(L) == 0)
    s_f = jnp.where(first[:, None], alt[None, :], -sin)
    fwd = jnp.concatenate([cos, s_f], axis=0).astype(BF16)
    colscale = jnp.where(first, 1.0 / N, 2.0 / N).astype(F32)
    g_c = cos * colscale[None, :]
    g_s = jnp.where(first[None, :], alt[:, None] / N, -sin * (2.0 / N))
    inv = jnp.concatenate([g_c, g_s], axis=1).astype(BF16)
    return fwd, inv


def _dft_fwd_kernel(f_ref, z_ref, o_ref):
    o_ref[0] = _dot(f_ref[...], z_ref[...].astype(BF16))


def _dft_fwd(fwd, z2d, L, nb, zmap):
    tm = min(2 * L, 1024)
    return pl.pallas_call(
        _dft_fwd_kernel,
        grid=(2 * L // tm, nb),
        in_specs=[pl.BlockSpec((tm, L), lambda i, b: (i, 0)),
                  pl.BlockSpec((L, BRANCH_W), lambda i, b: zmap(b))],
        out_specs=pl.BlockSpec((1, tm, BRANCH_W), lambda i, b: (b, i, 0)),
        out_shape=jax.ShapeDtypeStruct((nb, 2 * L, BRANCH_W), F32),
        compiler_params=_cp(("arbitrary", "arbitrary")),
        name="hy_dft_fwd",
    )(fwd, z2d)


def _spec_prod_kernel(z_ref, kf_ref, kb_ref, sf_ref, sb_ref, y_ref, *, tk):
    i = pl.program_id(1)
    s = lax.rsqrt(sf_ref[...] + sb_ref[...] + EPS)
    zr, zi = z_ref[0, 0], z_ref[0, 1]
    fr, fi = kf_ref[0, 0], kf_ref[0, 1]
    br, bi = kb_ref[0, 0], kb_ref[0, 1]
    row = lax.broadcasted_iota(jnp.int32, (tk, BRANCH_W), 0) + i * tk
    first = row == 0
    kr = (fr + br) * s
    ki = jnp.where(first, fi + bi, fi - bi) * s
    yr = jnp.where(first, zr * kr, zr * kr - zi * ki)
    yi = jnp.where(first, zi * ki, zr * ki + zi * kr)
    y_ref[0, 0] = yr.astype(BF16)
    y_ref[0, 1] = yi.astype(BF16)


def _spec_prod(zf, kfs, ss, order, L, nb):
    tk = min(L, 512)
    zf4 = zf.reshape(nb, 2, L, BRANCH_W)
    kf4 = kfs.reshape(4, 2, L, BRANCH_W)
    y = pl.pallas_call(
        functools.partial(_spec_prod_kernel, tk=tk),
        grid=(nb, L // tk),
        in_specs=[pl.BlockSpec((1, 2, tk, BRANCH_W), lambda b, i: (b, 0, i, 0)),
                  pl.BlockSpec((1, 2, tk, BRANCH_W), lambda b, i: (order, 0, i, 0)),
                  pl.BlockSpec((1, 2, tk, BRANCH_W), lambda b, i: (2 + order, 0, i, 0)),
                  pl.BlockSpec((1, BRANCH_W), lambda b, i: (0, order)),
                  pl.BlockSpec((1, BRANCH_W), lambda b, i: (0, 2 + order))],
        out_specs=pl.BlockSpec((1, 2, tk, BRANCH_W), lambda b, i: (b, 0, i, 0)),
        out_shape=jax.ShapeDtypeStruct((nb, 2, L, BRANCH_W), BF16),
        compiler_params=_cp(("arbitrary", "arbitrary")),
        name="hy_spec_prod",
    )(zf4, kf4, kf4, ss, ss)
    return y.reshape(nb, 2 * L, BRANCH_W)


def _dft_inv_kernel(g_ref, y_ref, gate_ref, zp_ref, bias_ref, o_ref):
    conv = _dot(g_ref[...], y_ref[0])
    o_ref[...] = (gate_ref[...] * (conv + bias_ref[...] * zp_ref[...])).astype(o_ref.dtype)


def _dft_inv(inv, y, xs, gate_col, zprev, zprev_col, bias, L, nb, out_dtype):
    tm = min(L, 512)
    nt = L // tm
    return pl.pallas_call(
        _dft_inv_kernel,
        grid=(nt, nb),
        in_specs=[pl.BlockSpec((tm, 2 * L), lambda i, b: (i, 0)),
                  pl.BlockSpec((1, 2 * L, BRANCH_W), lambda i, b: (b, 0, 0)),
                  pl.BlockSpec((tm, BRANCH_W), lambda i, b: (b * nt + i, gate_col)),
                  pl.BlockSpec((tm, BRANCH_W), lambda i, b: (b * nt + i, zprev_col)),
                  pl.BlockSpec((1, BRANCH_W), lambda i, b: (0, 0))],
        out_specs=pl.BlockSpec((tm, BRANCH_W), lambda i, b: (b * nt + i, 0)),
        out_shape=jax.ShapeDtypeStruct((nb * L, BRANCH_W), out_dtype),
        compiler_params=_cp(("arbitrary", "arbitrary")),
        name="hy_dft_inv",
    )(inv, y, xs, zprev, bias)


def _hyena(u_small, L, nseq, row_base, p, mats):
    (short_w, short_b, w1, b1, w2, b2, w3, b3, w4, freq, bias) = p
    fwd, inv = mats
    xs = _short_conv(u_small, short_w, short_b, L, nseq, row_base)
    hf, ss = _hyena_filter(L, w1, b1, w2, b2, w3, b3, w4, freq)
    kfs = _dft_fwd(fwd, hf, L, 4, lambda b: (0, b))
    zf = _dft_fwd(fwd, xs, L, nseq, lambda b: (b, 2))
    y = _spec_prod(zf, kfs, ss, 0, L, nseq)
    z1 = _dft_inv(inv, y, xs, 0, xs, 2, bias[0:1], L, nseq, F32)
    zf = _dft_fwd(fwd, z1, L, nseq, lambda b: (b, 0))
    y = _spec_prod(zf, kfs, ss, 1, L, nseq)
    return _dft_inv(inv, y, xs, 1, z1, 0, bias[1:2], L, nseq, BF16)


def _conformer_kernel(a_ref, g_ref, w_ref, b_ref, lg_ref, lb_ref, o_ref, pad_scr, *, L, tc):
    pad_scr[0:16, :] = jnp.zeros((16, BRANCH_W), F32)
    pad_scr[16 + L:32 + L, :] = jnp.zeros((16, BRANCH_W), F32)

    def glu(c, carry):
        t0 = pl.multiple_of(c * tc, tc)
        pad_scr[pl.ds(16 + t0, tc), :] = a_ref[pl.ds(t0, tc), :] * _sigmoid(g_ref[pl.ds(t0, tc), :])
        return carry

    lax.fori_loop(0, L // tc, glu, 0)
    b = b_ref[...]
    lg = lg_ref[...]
    lb = lb_ref[...]

    def body(c, carry):
        t0 = pl.multiple_of(c * tc, tc)
        win = pad_scr[pl.ds(t0, tc + 32), :]
        acc = jnp.zeros((tc, BRANCH_W), F32) + b
        for r in range(8):
            sh = win[r:r + tc + 24]
            for a in range(4):
                m = 8 * a + r
                if 1 <= m <= CF_WIDTH:
                    acc = acc + w_ref[m - 1:m, :] * sh[8 * a:8 * a + tc]
        mu = jnp.mean(acc, axis=-1, keepdims=True)
        xc = acc - mu
        var = jnp.mean(xc * xc, axis=-1, keepdims=True)
        y = xc * lax.rsqrt(var + EPS) * lg + lb
        o_ref[pl.ds(t0, tc), :] = (y * _sigmoid(y)).astype(o_ref.dtype)
        return carry

    lax.fori_loop(0, L // tc, body, 0)


def _conformer(u_small, L, nseq, row_base, p):
    dw_w, dw_b, ln_g, ln_b = p
    tc = 128
    wpad = jnp.pad(dw_w, ((0, 32 - CF_WIDTH), (0, 0)))
    row = lambda a: a.reshape(1, BRANCH_W)
    vec = pl.BlockSpec((1, BRANCH_W), lambda s: (0, 0))
    return pl.pallas_call(
        functools.partial(_conformer_kernel, L=L, tc=tc),
        grid=(nseq,),
        in_specs=[pl.BlockSpec((L, BRANCH_W), lambda s: (row_base + s, 3)),
                  pl.BlockSpec((L, BRANCH_W), lambda s: (row_base + s, 4)),
                  pl.BlockSpec((32, BRANCH_W), lambda s: (0, 0)), vec, vec, vec],
        out_specs=pl.BlockSpec((L, BRANCH_W), lambda s: (s, 0)),
        out_shape=jax.ShapeDtypeStruct((nseq * L, BRANCH_W), BF16),
        scratch_shapes=[pltpu.VMEM((L + 32, BRANCH_W), F32)],
        compiler_params=_cp(("arbitrary",)),
        name="conformer",
    )(u_small, u_small, wpad, row(dw_b), row(ln_g), row(ln_b))


def _rope_tables(S, pad_rows, head, scale_unused=None):
    half = head // 2
    nf = half // 2
    lane = np.arange(LANES)
    inv_lane = (ROPE_BASE ** (-(np.arange(nf)) / nf))[(lane % half) % nf]
    is_row = (lane % head) < half
    pos = jnp.arange(S, dtype=jnp.int32)
    rows = (pos // GRID_W).astype(F32)[:, None]
    cols = (pos % GRID_W).astype(F32)[:, None]
    ang = jnp.where(jnp.asarray(is_row)[None, :], rows, cols) * jnp.asarray(inv_lane, F32)[None, :]
    cos = jnp.concatenate([jnp.cos(ang), jnp.ones((pad_rows, LANES), F32)], axis=0)
    sin = jnp.concatenate([jnp.sin(ang), jnp.zeros((pad_rows, LANES), F32)], axis=0)
    return cos, sin


def _group_ones(width, group):
    idx = np.arange(width)
    return jnp.asarray((idx[:, None] // group) == (idx[None, :] // group), BF16)


def _head_norm_rope(x, ones, group, gain, cos, sin, nf, out_scale):
    w = x.shape[1]
    hi, lo = _split(x * x)
    ms = (_dot(hi, ones) + _dot(lo, ones)) * (1.0 / group)
    xn = x * lax.rsqrt(ms + EPS) * gain
    reps = w // LANES
    c = jnp.concatenate([cos] * reps, axis=1) if reps > 1 else cos
    s = jnp.concatenate([sin] * reps, axis=1) if reps > 1 else sin
    lane = lax.broadcasted_iota(jnp.int32, x.shape, 1)
    first = (lane % (2 * nf)) < nf
    rot = jnp.where(first, -pltpu.roll(xn, w - nf, 1), pltpu.roll(xn, nf, 1))
    return (xn * c + rot * s) * out_scale


def _qkv_kernel(gq_ref, gkv_ref, dq_ref, dk_ref, dv_ref, cg_ref, sg_ref, cd_ref, sd_ref,
                o64_ref, o32_ref, gqn_ref, gkn_ref, dqn_ref, dkn_ref,
                qg_ref, kg_ref, vag_ref, vbg_ref, qd_ref, kd_ref, vad_ref, vbd_ref):
    cg, sg, cd, sd = cg_ref[...], sg_ref[...], cd_ref[...], sd_ref[...]
    o64, o32 = o64_ref[...], o32_ref[...]
    q = _head_norm_rope(gq_ref[...], o64, HEAD_DIM, gqn_ref[...], cg, sg, HEAD_DIM // 4, HEAD_DIM ** -0.5)
    qg_ref[...] = q.astype(BF16)
    kv = gkv_ref[...]
    k = _head_norm_rope(kv[:, :LANES], o64[:LANES, :LANES], HEAD_DIM, gkn_ref[...], cg, sg, HEAD_DIM // 4, 1.0)
    v = kv[:, LANES:]
    kk = jnp.concatenate([k, k], axis=1)
    vv = jnp.concatenate([v, v], axis=1)
    quarter = lax.broadcasted_iota(jnp.int32, kk.shape, 1) // HEAD_DIM
    kr = pltpu.roll(kk, HEAD_DIM, 1)
    vr = pltpu.roll(vv, HEAD_DIM, 1)
    kg_ref[...] = jnp.where((quarter == 0) | (quarter == 3), kk, kr).astype(BF16)
    vag_ref[...] = jnp.where(quarter == 0, vv, jnp.where(quarter == 2, vr, 0.0)).astype(BF16)
    vbg_ref[...] = jnp.where(quarter == 1, vr, jnp.where(quarter == 3, vv, 0.0)).astype(BF16)
    qd = _head_norm_rope(dq_ref[...], o32, DIFF_D, dqn_ref[...], cd, sd, DIFF_D // 4, DIFF_D ** -0.5)
    qd_ref[...] = qd.astype(BF16)
    kd = _head_norm_rope(dk_ref[...], o32, DIFF_D, dkn_ref[...], cd, sd, DIFF_D // 4, 1.0)
    kd_ref[...] = kd.astype(BF16)
    vd = dv_ref[...]
    even = (lax.broadcasted_iota(jnp.int32, vd.shape, 1) // HEAD_DIM) % 2 == 0
    vad_ref[...] = jnp.where(even, vd, 0.0).astype(BF16)
    vbd_ref[...] = jnp.where(even, 0.0, vd).astype(BF16)


def _qkv_prep(u_small, gqn, gkn, dqn, dkn, tables, dims):
    n = u_small.shape[0]
    tm = dims["tm_small"]
    S = dims["S"]
    n_lat_tiles = dims["n_lat"] // tm
    per_seq = S // tm
    tmap = lambda i: (jnp.where(i < n_lat_tiles, i % per_seq, per_seq), 0)
    col = lambda c: pl.BlockSpec((tm, BRANCH_W), lambda i: (i, c))
    tab = pl.BlockSpec((tm, LANES), tmap)
    full = lambda a: pl.BlockSpec(a.shape, lambda i: (0,) * a.ndim)
    o64, o32 = _group_ones(BRANCH_W, HEAD_DIM), _group_ones(BRANCH_W, DIFF_D)
    gains = [jnp.tile(gqn, 4).reshape(1, 256), jnp.tile(gkn, 2).reshape(1, 128),
             jnp.tile(dqn.reshape(-1), 4).reshape(1, 256), jnp.tile(dkn.reshape(-1), 4).reshape(1, 256)]
    out = pl.BlockSpec((tm, BRANCH_W), lambda i: (i, 0))
    return pl.pallas_call(
        _qkv_kernel,
        grid=(n // tm,),
        in_specs=[col(5), col(6), col(7), col(8), col(9), tab, tab, tab, tab, full(o64), full(o32)]
        + [full(g) for g in gains],
        out_specs=[out] * 8,
        out_shape=[jax.ShapeDtypeStruct((n, BRANCH_W), BF16)] * 8,
        compiler_params=_cp(("arbitrary",)),
        name="qkv_prep",
    )(u_small, u_small, u_small, u_small, u_small, *tables, o64, o32, *gains)


def _lane_pick(lane_lo, a, b):
    return jnp.where(lane_lo, a, b)


def _flash_kernel(q_ref, k_ref, va_ref, vb_ref, aux_ref, o_ref, *, T, tk, tq, diff, lam_init):
    lane = lax.broadcasted_iota(jnp.int32, (1, LANES), 1)
    lane_lo = lane < HEAD_DIM
    if diff:
        masks = [(lane >= g * DIFF_D) & (lane < (g + 1) * DIFF_D) for g in range(4)]
        acc_of = [0, 1, 0, 1]
        use_a = [True, True, False, False]
        n_acc = 2
    else:
        masks = [lane_lo, ~lane_lo]
        acc_of = [0, 0]
        use_a = [True, False]
        n_acc = 1
    R = len(masks)
    nk = T // tk
    outs = []
    for p in range(2):
        ps = slice(p * LANES, (p + 1) * LANES)
        qp = q_ref[:, ps]
        qst = jnp.concatenate([jnp.where(m, qp, jnp.zeros_like(qp)) for m in masks], axis=0)

        def body(c, carry):
            m_run, l_run, accs = carry
            t0 = pl.multiple_of(c * tk, tk)
            k = k_ref[0, pl.ds(t0, tk), ps]
            s = lax.dot_general(qst, k, (((1,), (1,)), ((), ())), preferred_element_type=F32)
            m_new = jnp.maximum(m_run, jnp.max(s, axis=-1, keepdims=True))
            alpha = jnp.exp(m_run - m_new)
            pr = jnp.exp(s - m_new)
            l_new = alpha * l_run + jnp.sum(pr, axis=-1, keepdims=True)
            prb = pr.astype(BF16)
            va = va_ref[0, pl.ds(t0, tk), ps]
            vb = vb_ref[0, pl.ds(t0, tk), ps]
            new_accs = []
            for a in range(n_acc):
                rs = [r for r in range(R) if acc_of[r] == a]
                r_lo = [r for r in rs if use_a[r]][0]
                r_hi = [r for r in rs if not use_a[r]][0]
                al = _lane_pick(lane_lo, alpha[r_lo * tq:(r_lo + 1) * tq], alpha[r_hi * tq:(r_hi + 1) * tq])
                upd = _dot(prb[r_lo * tq:(r_lo + 1) * tq], va) + _dot(prb[r_hi * tq:(r_hi + 1) * tq], vb)
                new_accs.append(accs[a] * al + upd)
            return m_new, l_new, tuple(new_accs)

        init = (jnp.full((R * tq, 1), -jnp.inf, F32), jnp.zeros((R * tq, 1), F32),
                tuple(jnp.zeros((tq, LANES), F32) for _ in range(n_acc)))
        m_fin, l_fin, accs = lax.fori_loop(0, nk, body, init)
        inv_l = 1.0 / l_fin
        norm = []
        for a in range(n_acc):
            rs = [r for r in range(R) if acc_of[r] == a]
            r_lo = [r for r in rs if use_a[r]][0]
            r_hi = [r for r in rs if not use_a[r]][0]
            norm.append(accs[a] * _lane_pick(lane_lo, inv_l[r_lo * tq:(r_lo + 1) * tq],
                                             inv_l[r_hi * tq:(r_hi + 1) * tq]))
        if diff:
            aux = aux_ref[...]
            lam = (jnp.exp(jnp.sum(aux[0:1] * aux[1:2], axis=-1, keepdims=True))
                   - jnp.exp(jnp.sum(aux[2:3] * aux[3:4], axis=-1, keepdims=True)) + lam_init)
            o = norm[0] - lam * norm[1]
            sq = o * o
            s_lo = jnp.sum(jnp.where(lane_lo, sq, 0.0), axis=-1, keepdims=True)
            s_hi = jnp.sum(jnp.where(lane_lo, 0.0, sq), axis=-1, keepdims=True)
            ms = _lane_pick(lane_lo, s_lo, s_hi) * (1.0 / HEAD_DIM)
            o = o * lax.rsqrt(ms + EPS) * aux[4:5] * (1.0 - lam_init)
        else:
            o = norm[0]
        outs.append(o)
    o_ref[...] = jnp.concatenate(outs, axis=1).astype(o_ref.dtype)


def _flash(q, k, va, vb, aux, nb, Lq, q_row_base, diff, lam_init):
    T = k.shape[1]
    tq = 256
    tk = 256
    nt = Lq // tq
    kv = pl.BlockSpec((1, T, BRANCH_W), lambda b, i: (b, 0, 0))
    return pl.pallas_call(
        functools.partial(_flash_kernel, T=T, tk=tk, tq=tq, diff=diff, lam_init=lam_init),
        grid=(nb, nt),
        in_specs=[pl.BlockSpec((tq, BRANCH_W), lambda b, i: (q_row_base + b * nt + i, 0)), kv, kv, kv,
                  pl.BlockSpec((8, LANES), lambda b, i: (0, 0))],
        out_specs=pl.BlockSpec((tq, BRANCH_W), lambda b, i: (b * nt + i, 0)),
        out_shape=jax.ShapeDtypeStruct((nb * Lq, BRANCH_W), BF16),
        compiler_params=_cp(("arbitrary", "arbitrary")),
        name="flash_diff" if diff else "flash_gqa",
    )(q, k, va, vb, aux)


def _merge_kernel(y0, y1, y2, y3, g_ref, wb_ref, wo_ref, x_ref, mod_ref, o_ref):
    d = D_MODEL
    acc = g_ref[:, 0:d].astype(F32) * _dot(y0[...], wb_ref[0])
    acc = acc + g_ref[:, d:2 * d].astype(F32) * _dot(y1[...], wb_ref[1])
    acc = acc + g_ref[:, 2 * d:3 * d].astype(F32) * _dot(y2[...], wb_ref[2])
    acc = acc + g_ref[:, 3 * d:4 * d].astype(F32) * _dot(y3[...], wb_ref[3])
    mix = _dot(acc.astype(BF16), wo_ref[...])
    o_ref[...] = x_ref[...] + mod_ref[0][2:3, :] * mix


def _merge(ys, gate, wb, wo, x, mods_l, dims):
    n, d = x.shape
    tm = dims["tm_small"]
    mrow = dims["mod_row"](tm)
    yspec = pl.BlockSpec((tm, BRANCH_W), lambda i: (i, 0))
    return pl.pallas_call(
        _merge_kernel,
        grid=(n // tm,),
        in_specs=[yspec] * 4 + [pl.BlockSpec((tm, GATE_COLS), lambda i: (i, 0)),
                                pl.BlockSpec(wb.shape, lambda i: (0, 0, 0)),
                                pl.BlockSpec(wo.shape, lambda i: (0, 0)),
                                pl.BlockSpec((tm, d), lambda i: (i, 0)),
                                pl.BlockSpec((1, 6, d), lambda i: (mrow(i), 0, 0))],
        out_specs=pl.BlockSpec((tm, d), lambda i: (i, 0)),
        out_shape=jax.ShapeDtypeStruct((n, d), F32),
        compiler_params=_cp(("arbitrary",)),
        name="merge",
    )(*ys, gate, wb, wo, x, mods_l)


def _ffn_kernel(x_ref, mod_ref, g_ref, w1_ref, w3_ref, w2_ref, o_ref, h_scr, acc_scr):
    j = pl.program_id(1)

    @pl.when(j == 0)
    def _():
        m = mod_ref[0]
        h_scr[...] = _norm_mod(x_ref[...], g_ref[...], m[3:4, :], m[4:5, :]).astype(BF16)
        acc_scr[...] = jnp.zeros_like(acc_scr)

    h = h_scr[...]
    a = _dot(h, w1_ref[...])
    b = _dot(h, w3_ref[...])
    t = (a * _sigmoid(a) * b).astype(BF16)
    acc_scr[...] += _dot(t, w2_ref[...])

    @pl.when(j == pl.num_programs(1) - 1)
    def _():
        o_ref[...] = x_ref[...] + mod_ref[0][5:6, :] * acc_scr[...]


def _ffn(x, mods_l, gain, w1, w3, w2, dims):
    n, d = x.shape
    tm, tf = dims["tm_big"], 256
    mrow = dims["mod_row"](tm)
    return pl.pallas_call(
        _ffn_kernel,
        grid=(n // tm, w1.shape[1] // tf),
        in_specs=[pl.BlockSpec((tm, d), lambda i, j: (i, 0)),
                  pl.BlockSpec((1, 6, d), lambda i, j: (mrow(i), 0, 0)),
                  pl.BlockSpec((1, d), lambda i, j: (0, 0)),
                  pl.BlockSpec((d, tf), lambda i, j: (0, j)),
                  pl.BlockSpec((d, tf), lambda i, j: (0, j)),
                  pl.BlockSpec((tf, d), lambda i, j: (j, 0))],
        out_specs=pl.BlockSpec((tm, d), lambda i, j: (i, 0)),
        out_shape=jax.ShapeDtypeStruct((n, d), F32),
        scratch_shapes=[pltpu.VMEM((tm, d), BF16), pltpu.VMEM((tm, d), F32)],
        compiler_params=_cp(("arbitrary", "arbitrary")),
        name="ffn_dense",
    )(x, mods_l, gain, w1, w3, w2)


def _route_kernel(x_ref, mod_ref, g_ref, wr_ref, tok_ref, gate_ref):
    m = mod_ref[0]
    h = _norm_mod(x_ref[...], g_ref[...], m[3:4, :], m[4:5, :])
    tok_ref[...] = h.astype(BF16)
    logits = _dot3(h, wr_ref[...])
    lane = lax.broadcasted_iota(jnp.int32, logits.shape, 1)
    lg = jnp.where(lane < N_EXPERTS, logits, -jnp.inf)
    m1 = jnp.max(lg, axis=-1, keepdims=True)
    i1 = jnp.min(jnp.where(lg == m1, lane, LANES), axis=-1, keepdims=True)
    lg2 = jnp.where(lane == i1, -jnp.inf, lg)
    m2 = jnp.max(lg2, axis=-1, keepdims=True)
    i2 = jnp.min(jnp.where(lg2 == m2, lane, LANES), axis=-1, keepdims=True)
    e2 = jnp.exp(m2 - m1)
    g1 = 1.0 / (1.0 + e2)
    gate_ref[...] = jnp.where(lane == i1, g1, jnp.where(lane == i2, e2 * g1, 0.0))


def _route(x, mods_l, gain, w_router, dims):
    n, d = x.shape
    tm = dims["tm_small"]
    mrow = dims["mod_row"](tm)
    wr = jnp.pad(w_router, ((0, 0), (0, LANES - N_EXPERTS)))
    return pl.pallas_call(
        _route_kernel,
        grid=(n // tm,),
        in_specs=[pl.BlockSpec((tm, d), lambda i: (i, 0)),
                  pl.BlockSpec((1, 6, d), lambda i: (mrow(i), 0, 0)),
                  pl.BlockSpec((1, d), lambda i: (0, 0)),
                  pl.BlockSpec((d, LANES), lambda i: (0, 0))],
        out_specs=[pl.BlockSpec((tm, d), lambda i: (i, 0)), pl.BlockSpec((tm, LANES), lambda i: (i, 0))],
        out_shape=[jax.ShapeDtypeStruct((n, d), BF16), jax.ShapeDtypeStruct((n, LANES), F32)],
        compiler_params=_cp(("arbitrary",)),
        name="moe_route",
    )(x, mods_l, gain, wr)


MOE_SUB = 128


def _moe_kernel(tok_ref, gate_ref, tri_ref, w1_ref, w3_ref, w2_ref, o_ref,
                rank_scr, rank_t_scr, mask_t_scr, xe_scr, ye_scr, cnt_smem, *, tm):
    e = pl.program_id(1)
    j = pl.program_id(2)
    nj = pl.num_programs(2)
    nsub = tm // MOE_SUB
    lane = lax.broadcasted_iota(jnp.int32, (1, LANES), 1)

    @pl.when((e == 0) & (j == 0))
    def _():
        tri = tri_ref[...]
        carry = jnp.zeros((1, LANES), F32)
        for blk in range(tm // 256):
            rs = slice(blk * 256, (blk + 1) * 256)
            msk = (gate_ref[rs, :] > 0.0).astype(BF16)
            rank_scr[rs, :] = _dot(tri, msk) + carry
            carry = carry + jnp.sum(msk.astype(F32), axis=0, keepdims=True)
        rank_t_scr[...] = jnp.transpose(rank_scr[...])
        mask_t_scr[...] = jnp.transpose((gate_ref[...] > 0.0).astype(F32))
        for ee in range(N_EXPERTS):
            cnt_smem[ee] = jnp.sum(jnp.where(lane == ee, carry, 0.0)).astype(jnp.int32)
        o_ref[...] = jnp.zeros_like(o_ref)

    cnt = cnt_smem[e]

    @pl.when(j == 0)
    def _():
        rk = rank_t_scr[pl.ds(e, 1), :]
        mk = mask_t_scr[pl.ds(e, 1), :]
        for sb in range(nsub):
            @pl.when(sb * MOE_SUB < cnt)
            def _():
                r_iota = lax.broadcasted_iota(jnp.int32, (MOE_SUB, tm), 0) + sb * MOE_SUB
                sel = jnp.where((rk == r_iota.astype(F32)) & (mk > 0.0), 1.0, 0.0).astype(BF16)
                xe_scr[sb * MOE_SUB:(sb + 1) * MOE_SUB, :] = _dot(sel, tok_ref[...]).astype(BF16)

    for nb_ in range(1, nsub + 1):
        @pl.when((cnt > (nb_ - 1) * MOE_SUB) & (cnt <= nb_ * MOE_SUB))
        def _():
            rows = slice(0, nb_ * MOE_SUB)
            xe = xe_scr[rows, :]
            a = _dot(xe, w1_ref[0])
            b = _dot(xe, w3_ref[0])
            t = (a * _sigmoid(a) * b).astype(BF16)
            y = _dot(t, w2_ref[0])

            @pl.when(j == 0)
            def _():
                ye_scr[rows, :] = y

            @pl.when(j > 0)
            def _():
                ye_scr[rows, :] += y

    @pl.when(j == nj - 1)
    def _():
        def scatter(rc, carry):
            r0 = pl.multiple_of(rc * 256, 256)
            gt = gate_ref[pl.ds(r0, 256), :]
            g_e = jnp.sum(jnp.where(lane == e, gt, 0.0), axis=-1, keepdims=True)
            r_e = jnp.sum(jnp.where(lane == e, rank_scr[pl.ds(r0, 256), :], 0.0), axis=-1, keepdims=True)
            lo = jnp.min(jnp.where(g_e > 0.0, r_e, 1e9))
            hi = jnp.max(jnp.where(g_e > 0.0, r_e, -1.0))
            for sb in range(nsub):
                @pl.when((hi >= sb * MOE_SUB) & (lo < (sb + 1) * MOE_SUB))
                def _():
                    c_iota = lax.broadcasted_iota(jnp.int32, (256, MOE_SUB), 1) + sb * MOE_SUB
                    sel_t = jnp.where((r_e == c_iota.astype(F32)) & (g_e > 0.0), 1.0, 0.0).astype(BF16)
                    ye = ye_scr[sb * MOE_SUB:(sb + 1) * MOE_SUB, :].astype(BF16)
                    o_ref[pl.ds(r0, 256), :] += g_e * _dot(sel_t, ye)
            return carry

        lax.fori_loop(0, tm // 256, scatter, 0)


def _moe(tok, gates, w1, w3, w2, dims):
    n, d = tok.shape
    tm = dims["tm_big"]
    ne, _, ff = w1.shape
    tf = 896
    tri = jnp.asarray(np.tril(np.ones((256, 256)), -1), BF16)
    return pl.pallas_call(
        functools.partial(_moe_kernel, tm=tm),
        grid=(n // tm, ne, ff // tf),
        in_specs=[pl.BlockSpec((tm, d), lambda i, e, j: (i, 0)),
                  pl.BlockSpec((tm, LANES), lambda i, e, j: (i, 0)),
                  pl.BlockSpec((256, 256), lambda i, e, j: (0, 0)),
                  pl.BlockSpec((1, d, tf), lambda i, e, j: (e, 0, j)),
                  pl.BlockSpec((1, d, tf), lambda i, e, j: (e, 0, j)),
                  pl.BlockSpec((1, tf, d), lambda i, e, j: (e, j, 0))],
        out_specs=pl.BlockSpec((tm, d), lambda i, e, j: (i, 0)),
        out_shape=jax.ShapeDtypeStruct((n, d), F32),
        scratch_shapes=[pltpu.VMEM((tm, LANES), F32), pltpu.VMEM((LANES, tm), F32), pltpu.VMEM((LANES, tm), F32),
                        pltpu.VMEM((tm, d), BF16), pltpu.VMEM((tm, d), F32), pltpu.SMEM((N_EXPERTS,), jnp.int32)],
        compiler_params=_cp(("arbitrary", "arbitrary", "arbitrary")),
        name="moe_experts",
    )(tok, gates, tri, w1, w3, w2)


def _resid_kernel(x_ref, f_ref, mod_ref, o_ref):
    o_ref[...] = x_ref[...] + mod_ref[0][5:6, :] * f_ref[...]


def _resid(x, f, mods_l, dims):
    n, d = x.shape
    tm = dims["tm_small"]
    mrow = dims["mod_row"](tm)
    spec = pl.BlockSpec((tm, d), lambda i: (i, 0))
    return pl.pallas_call(
        _resid_kernel,
        grid=(n // tm,),
        in_specs=[spec, spec, pl.BlockSpec((1, 6, d), lambda i: (mrow(i), 0, 0))],
        out_specs=spec,
        out_shape=jax.ShapeDtypeStruct((n, d), F32),
        compiler_params=_cp(("arbitrary",)),
        name="moe_resid",
    )(x, f, mods_l)


def _make_dims(B, S, n_ctx):
    n_lat, n = B * S, B * (S + n_ctx)

    def pick(prefs):
        for t in prefs:
            if S % t == 0 and n_lat % t == 0 and (n - n_lat) % t == 0:
                return t
        raise ValueError("no row tile divides the latent and context token counts")

    def mod_row(tm):
        return lambda i: jnp.minimum((i * tm) // S, B)

    return {"B": B, "S": S, "n_ctx": n_ctx, "n_lat": n_lat, "n": n,
            "tm_big": pick((1024, 512, 256)), "tm_small": pick((512, 256)), "mod_row": mod_row}


def kernel(x, c, ctx, c_ctx, w_ada, b_ada, norm_mix, norm_ffn, w_in, hy_short_w, hy_short_b, hy_w1, hy_b1, hy_w2, hy_b2, hy_w3, hy_b3, hy_w4, hy_freq, hy_bias, cf_dw_w, cf_dw_b, cf_ln_g, cf_ln_b, gqa_qn, gqa_kn, diff_qn, diff_kn, diff_lq1, diff_lk1, diff_lq2, diff_lk2, diff_subln, w_branch, w_out, ffn_w1, ffn_w3, ffn_w2, moe_router, moe_w1, moe_w3, moe_w2):
    B, S, D = x.shape
    n_ctx = ctx.shape[1]
    depth = w_ada.shape[0]
    dims = _make_dims(B, S, n_ctx)
    n_lat = dims["n_lat"]
    tm_s = dims["tm_small"]

    c16 = jnp.concatenate([c, c_ctx[None, :], jnp.zeros((16 - B - 1, D), F32)], axis=0)
    mods = _ada_mods(c16, w_ada, b_ada).reshape(depth, 16, 6, D)

    xa = jnp.concatenate([x.reshape(n_lat, D), ctx.reshape(B * n_ctx, D)], axis=0)
    tables = _rope_tables(S, tm_s, HEAD_DIM) + _rope_tables(S, tm_s, DIFF_D)
    mats_lat = _dft_mats(S)
    mats_ctx = _dft_mats(n_ctx)

    for l in range(depth):
        lam_init = 0.8 - 0.6 * math.exp(-0.3 * l)
        u_small, gate = _in_proj(xa, mods[l], norm_mix[l].reshape(1, D), w_in[l].astype(BF16), dims)

        hy_p = (hy_short_w[l], hy_short_b[l], hy_w1[l], hy_b1[l], hy_w2[l], hy_b2[l], hy_w3[l], hy_b3[l],
                hy_w4[l], hy_freq[l], hy_bias[l])
        cf_p = (cf_dw_w[l], cf_dw_b[l], cf_ln_g[l], cf_ln_b[l])
        y_hy = jnp.concatenate([_hyena(u_small, S, B, 0, hy_p, mats_lat),
                                _hyena(u_small, n_ctx, B, n_lat // n_ctx, hy_p, mats_ctx)], axis=0)
        y_cf = jnp.concatenate([_conformer(u_small, S, B, 0, cf_p),
                                _conformer(u_small, n_ctx, B, n_lat // n_ctx, cf_p)], axis=0)

        qg, kg, vag, vbg, qd, kd, vad, vbd = _qkv_prep(u_small, gqa_qn[l], gqa_kn[l], diff_qn[l], diff_kn[l],
                                                      tables, dims)

        def keys(a):
            return (jnp.concatenate([a[:n_lat].reshape(B, S, BRANCH_W), a[n_lat:].reshape(B, n_ctx, BRANCH_W)],
                                    axis=1), a[n_lat:].reshape(B, n_ctx, BRANCH_W))

        aux = jnp.zeros((8, LANES), F32)
        aux = aux.at[0, :DIFF_D].set(diff_lq1[l]).at[1, :DIFF_D].set(diff_lk1[l])
        aux = aux.at[2, :DIFF_D].set(diff_lq2[l]).at[3, :DIFF_D].set(diff_lk2[l])
        aux = aux.at[4, :].set(jnp.tile(diff_subln[l], 2))
        y_att = []
        for (q, k, va, vb, is_diff) in ((qg, kg, vag, vbg, False), (qd, kd, vad, vbd, True)):
            (k_all, k_ctx), (va_all, va_ctx), (vb_all, vb_ctx) = keys(k), keys(va), keys(vb)
            y_lat = _flash(q, k_all, va_all, vb_all, aux, B, S, 0, is_diff, lam_init)
            y_c = _flash(q, k_ctx, va_ctx, vb_ctx, aux, B, n_ctx, n_lat // 256, is_diff, lam_init)
            y_att.append(jnp.concatenate([y_lat, y_c], axis=0))

        xa = _merge((y_hy, y_cf, y_att[0], y_att[1]), gate, w_branch[l].astype(BF16), w_out[l].astype(BF16),
                    xa, mods[l], dims)

        i = l // 2
        if l % 2 == 0:
            xa = _ffn(xa, mods[l], norm_ffn[l].reshape(1, D), ffn_w1[i].astype(BF16), ffn_w3[i].astype(BF16),
                      ffn_w2[i].astype(BF16), dims)
        else:
            tok, gates = _route(xa, mods[l], norm_ffn[l].reshape(1, D), moe_router[i], dims)
            f = _moe(tok, gates, moe_w1[i].astype(BF16), moe_w3[i].astype(BF16), moe_w2[i].astype(BF16), dims)
            xa = _resid(xa, f, mods[l], dims)
    return xa[:n_lat].reshape(B, S, D)
```

```python
import functools
import math

import numpy as np
import jax
import jax.numpy as jnp
from jax import lax
from jax.experimental import pallas as pl
from jax.experimental.pallas import tpu as pltpu

F32 = jnp.float32
BF16 = jnp.bfloat16

D_MODEL = 1024
GRID_W = 64
BRANCH_W = 256
EPS = 1e-6
HY_EMB = 33
HY_EMB_PAD = 128
HY_FFN = 64
HY_TARGET = 1e-2
HY_FAST = 0.3
HY_SLOW = 1.5
CF_WIDTH = 31
HEAD_DIM = 64
DIFF_D = 32
ROPE_BASE = 10000.0
N_EXPERTS = 8
LANES = 128
LOG2E = 1.4426950408889634
SMALL_COLS = 2560
GATE_COLS = 4 * D_MODEL
VMEM_LIMIT = 48 * 1024 * 1024


def _cp(sem, vmem=VMEM_LIMIT):
    return pltpu.CompilerParams(dimension_semantics=sem, vmem_limit_bytes=vmem)


def _sigmoid(x):
    return 1.0 / (1.0 + jnp.exp(-x))


def _dot(a, b):
    return jnp.dot(a, b, preferred_element_type=F32)


def _split(a):
    hi = a.astype(BF16)
    lo = (a - hi.astype(F32)).astype(BF16)
    return hi, lo


def _dot3(a, b):
    ah, al = _split(a)
    bh, bl = _split(b)
    return _dot(ah, bh) + (_dot(al, bh) + _dot(ah, bl))


def _norm_mod(x, gain, shift, scale):
    ms = jnp.mean(x * x, axis=-1, keepdims=True)
    return (x * lax.rsqrt(ms + EPS) * gain) * (1.0 + scale) + shift


def _ada_kernel(c_ref, w_ref, b_ref, o_ref):
    c = c_ref[...]
    s = c * _sigmoid(c)
    o_ref[0] = _dot3(s, w_ref[0]) + b_ref[0]


def _ada_mods(c16, w_ada, b_ada):
    depth, d, n6 = w_ada.shape
    tn = 512
    return pl.pallas_call(
        _ada_kernel,
        grid=(depth, n6 // tn),
        in_specs=[pl.BlockSpec((16, d), lambda l, j: (0, 0)),
                  pl.BlockSpec((1, d, tn), lambda l, j: (l, 0, j)),
                  pl.BlockSpec((1, 1, tn), lambda l, j: (l, 0, j))],
        out_specs=pl.BlockSpec((1, 16, tn), lambda l, j: (l, 0, j)),
        out_shape=jax.ShapeDtypeStruct((depth, 16, n6), F32),
        compiler_params=_cp(("arbitrary", "arbitrary")),
        name="ada_mods",
    )(c16, w_ada, b_ada.reshape(depth, 1, n6))


def _in_proj_kernel(x_ref, mod_ref, g_ref, w_ref, u_ref, gate_ref, h_scr, *, n_small):
    j = pl.program_id(1)

    @pl.when(j == 0)
    def _():
        m = mod_ref[0]
        h_scr[...] = _norm_mod(x_ref[...], g_ref[...], m[0:1, :], m[1:2, :]).astype(BF16)

    r = _dot(h_scr[...], w_ref[...])

    @pl.when(j < n_small)
    def _():
        u_ref[...] = r

    @pl.when(j >= n_small)
    def _():
        gate_ref[...] = _sigmoid(r).astype(BF16)


def _in_proj(x, mods_l, gain, w_bf, dims):
    n, d = x.shape
    tm, tn = dims["tm_big"], 512
    n_small = SMALL_COLS // tn
    n_cols = w_bf.shape[1] // tn
    mrow = dims["mod_row"](tm)
    return pl.pallas_call(
        functools.partial(_in_proj_kernel, n_small=n_small),
        grid=(n // tm, n_cols),
        in_specs=[pl.BlockSpec((tm, d), lambda i, j: (i, 0)),
                  pl.BlockSpec((1, 6, d), lambda i, j: (mrow(i), 0, 0)),
                  pl.BlockSpec((1, d), lambda i, j: (0, 0)),
                  pl.BlockSpec((d, tn), lambda i, j: (0, j))],
        out_specs=[pl.BlockSpec((tm, tn), lambda i, j: (i, jnp.minimum(j, n_small - 1))),
                   pl.BlockSpec((tm, tn), lambda i, j: (i, jnp.maximum(j - n_small, 0)))],
        out_shape=[jax.ShapeDtypeStruct((n, SMALL_COLS), F32),
                   jax.ShapeDtypeStruct((n, GATE_COLS), BF16)],
        scratch_shapes=[pltpu.VMEM((tm, d), BF16)],
        compiler_params=_cp(("arbitrary", "arbitrary")),
        name="in_proj",
    )(x, mods_l, gain, w_bf)


def _short_conv_kernel(u_ref, w_ref, b_ref, o_ref, pad_scr, *, L, tc):
    pad_scr[0:8, :] = jnp.zeros((8, BRANCH_W), F32)
    pad_scr[8 + L:16 + L, :] = jnp.zeros((8, BRANCH_W), F32)
    pad_scr[8:8 + L, :] = u_ref[...]
    w = w_ref[0]
    b = b_ref[0]

    def body(c, carry):
        t0 = pl.multiple_of(c * tc, tc)
        win = pad_scr[pl.ds(t0, tc + 16), :]
        acc = b + w[0:1, :] * win[7:7 + tc]
        acc = acc + w[1:2, :] * win[8:8 + tc]
        acc = acc + w[2:3, :] * win[9:9 + tc]
        o_ref[pl.ds(t0, tc), :] = acc
        return carry

    lax.fori_loop(0, L // tc, body, 0)


def _short_conv(u_small, w, b, L, nseq, row_base):
    w3 = jnp.transpose(w.reshape(3, 3, BRANCH_W), (1, 0, 2))
    w3 = jnp.pad(w3, ((0, 0), (0, 5), (0, 0)))
    b3 = b.reshape(3, 1, BRANCH_W)
    tc = min(L, 256)
    return pl.pallas_call(
        functools.partial(_short_conv_kernel, L=L, tc=tc),
        grid=(nseq, 3),
        in_specs=[pl.BlockSpec((L, BRANCH_W), lambda s, j: (row_base + s, j)),
                  pl.BlockSpec((1, 8, BRANCH_W), lambda s, j: (j, 0, 0)),
                  pl.BlockSpec((1, 1, BRANCH_W), lambda s, j: (j, 0, 0))],
        out_specs=pl.BlockSpec((L, BRANCH_W), lambda s, j: (s, j)),
        out_shape=jax.ShapeDtypeStruct((nseq * L, 3 * BRANCH_W), F32),
        scratch_shapes=[pltpu.VMEM((L + 16, BRANCH_W), F32)],
        compiler_params=_cp(("arbitrary", "arbitrary")),
        name="hy_short_conv",
    )(u_small, w3, b3)


def _filter_consts(L):
    t = np.linspace(0.0, 1.0, L)[:, None]
    bands = (HY_EMB - 1) // 2
    fr = np.linspace(1e-4, bands - 1, bands)[None, :]
    wpos = 2.0 * math.pi * np.arange(L)[:, None] / L
    z = np.concatenate([t, np.cos(fr * wpos), -np.sin(fr * wpos)], axis=-1)
    z = np.pad(z, ((0, 0), (0, HY_EMB_PAD - HY_EMB)))
    deltas = np.abs(np.linspace(math.log(HY_TARGET) / HY_SLOW, math.log(HY_TARGET) / HY_FAST, BRANCH_W))
    win = np.exp(-t * deltas[None, :])
    return jnp.asarray(z, F32), jnp.asarray(win, F32)


def _filter_kernel(z_ref, w1, b1, w2, b2, w3, b3, w4, fq, win_ref, hf_ref, ss_ref, *, tr):
    i = pl.program_id(0)
    f = fq[...]
    a = jnp.sin(f * (_dot3(z_ref[...], w1[...]) + b1[...]))
    a = jnp.sin(f * (_dot3(a, w2[...]) + b2[...]))
    a = jnp.sin(f * (_dot3(a, w3[...]) + b3[...]))
    h = _dot3(a, w4[...])
    win = win_ref[...]
    h = h * jnp.concatenate([win, win, win, win], axis=1)
    row = lax.broadcasted_iota(jnp.int32, h.shape, 0) + i * tr
    col = lax.broadcasted_iota(jnp.int32, h.shape, 1)
    h = jnp.where((row == 0) & (col >= 2 * BRANCH_W), 0.0, h)
    hf_ref[...] = h

    @pl.when(i == 0)
    def _():
        ss_ref[...] = jnp.zeros_like(ss_ref)

    ss_ref[...] += jnp.sum(h * h, axis=0, keepdims=True)


def _hyena_filter(L, w1, b1, w2, b2, w3, b3, w4, freq):
    z, win = _filter_consts(L)
    tr = min(L, 256)
    w1p = jnp.pad(w1, ((0, HY_EMB_PAD - HY_EMB), (0, 0)))
    full = lambda a: pl.BlockSpec(a.shape, lambda i: (0,) * a.ndim)
    args = [w1p, b1.reshape(1, -1), w2, b2.reshape(1, -1), w3, b3.reshape(1, -1), w4, freq.reshape(1, -1)]
    n_out = w4.shape[1]
    return pl.pallas_call(
        functools.partial(_filter_kernel, tr=tr),
        grid=(L // tr,),
        in_specs=[pl.BlockSpec((tr, HY_EMB_PAD), lambda i: (i, 0))] + [full(a) for a in args]
        + [pl.BlockSpec((tr, BRANCH_W), lambda i: (i, 0))],
        out_specs=[pl.BlockSpec((tr, n_out), lambda i: (i, 0)),
                   pl.BlockSpec((1, n_out), lambda i: (0, 0))],
        out_shape=[jax.ShapeDtypeStruct((L, n_out), F32), jax.ShapeDtypeStruct((1, n_out), F32)],
        compiler_params=_cp(("arbitrary",)),
        name="hy_filter",
    )(z, *args, win)


def _dft_mats(L):
    N = 2 * L
    blk = 64
    k = jnp.arange(L, dtype=jnp.int32)[:, None]
    nh = jnp.arange(L // blk, dtype=jnp.int32)[None, :]
    nl = jnp.arange(blk, dtype=jnp.int32)[None, :]
    w = 2.0 * math.pi / N
    a = ((k * (blk * nh)) % N).astype(F32) * w
    b = ((k * nl) % N).astype(F32) * w
    ca, sa, cb, sb = jnp.cos(a), jnp.sin(a), jnp.cos(b), jnp.sin(b)
    cos = (ca[:, :, None] * cb[:, None, :] - sa[:, :, None] * sb[:, None, :]).reshape(L, L)
    sin = (sa[:, :, None] * cb[:, None, :] + ca[:, :, None] * sb[:, None, :]).reshape(L, L)
    alt = jnp.where(jnp.arange(L) % 2 == 0, 1.0, -1.0).astype(F32)
    first = (jnp.arange(L) == 0)
    s_f = jnp.where(first[:, None], alt[None, :], -sin)
    fwd = jnp.concatenate([cos, s_f], axis=0).astype(BF16)
    colscale = jnp.where(first, 1.0 / N, 2.0 / N).astype(F32)
    g_c = cos * colscale[None, :]
    g_s = jnp.where(first[None, :], alt[:, None] / N, -sin * (2.0 / N))
    inv = jnp.concatenate([g_c, g_s], axis=1).astype(BF16)
    return fwd, inv


def _dft_fwd_kernel(f_ref, z_ref, o_ref):
    o_ref[0] = _dot(f_ref[...], z_ref[...].astype(BF16))


def _dft_fwd(fwd, z2d, L, nb, zmap):
    tm = min(2 * L, 1024)
    return pl.pallas_call(
        _dft_fwd_kernel,
        grid=(2 * L // tm, nb),
        in_specs=[pl.BlockSpec((tm, L), lambda i, b: (i, 0)),
                  pl.BlockSpec((L, BRANCH_W), lambda i, b: zmap(b))],
        out_specs=pl.BlockSpec((1, tm, BRANCH_W), lambda i, b: (b, i, 0)),
        out_shape=jax.ShapeDtypeStruct((nb, 2 * L, BRANCH_W), F32),
        compiler_params=_cp(("arbitrary", "arbitrary")),
        name="hy_dft_fwd",
    )(fwd, z2d)


def _spec_prod_kernel(z_ref, kf_ref, kb_ref, sf_ref, sb_ref, y_ref, *, tk):
    i = pl.program_id(1)
    s = lax.rsqrt(sf_ref[...] + sb_ref[...] + EPS)
    zr, zi = z_ref[0, 0], z_ref[0, 1]
    fr, fi = kf_ref[0, 0], kf_ref[0, 1]
    br, bi = kb_ref[0, 0], kb_ref[0, 1]
    row = lax.broadcasted_iota(jnp.int32, (tk, BRANCH_W), 0) + i * tk
    first = row == 0
    kr = (fr + br) * s
    ki = jnp.where(first, fi + bi, fi - bi) * s
    yr = jnp.where(first, zr * kr, zr * kr - zi * ki)
    yi = jnp.where(first, zi * ki, zr * ki + zi * kr)
    y_ref[0, 0] = yr.astype(BF16)
    y_ref[0, 1] = yi.astype(BF16)


def _spec_prod(zf, kfs, ss, order, L, nb):
    tk = min(L, 512)
    zf4 = zf.reshape(nb, 2, L, BRANCH_W)
    kf4 = kfs.reshape(4, 2, L, BRANCH_W)
    y = pl.pallas_call(
        functools.partial(_spec_prod_kernel, tk=tk),
        grid=(nb, L // tk),
        in_specs=[pl.BlockSpec((1, 2, tk, BRANCH_W), lambda b, i: (b, 0, i, 0)),
                  pl.BlockSpec((1, 2, tk, BRANCH_W), lambda b, i: (order, 0, i, 0)),
                  pl.BlockSpec((1, 2, tk, BRANCH_W), lambda b, i: (2 + order, 0, i, 0)),
                  pl.BlockSpec((1, BRANCH_W), lambda b, i: (0, order)),
                  pl.BlockSpec((1, BRANCH_W), lambda b, i: (0, 2 + order))],
        out_specs=pl.BlockSpec((1, 2, tk, BRANCH_W), lambda b, i: (b, 0, i, 0)),
        out_shape=jax.ShapeDtypeStruct((nb, 2, L, BRANCH_W), BF16),
        compiler_params=_cp(("arbitrary", "arbitrary")),
        name="hy_spec_prod",
    )(zf4, kf4, kf4, ss, ss)
    return y.reshape(nb, 2 * L, BRANCH_W)


def _dft_inv_kernel(g_ref, y_ref, gate_ref, zp_ref, bias_ref, *rest):
    o_ref = rest[-1]
    conv = _dot(g_ref[...], y_ref[0])
    o_ref[...] = (gate_ref[...] * (conv + bias_ref[...] * zp_ref[...])).astype(o_ref.dtype)


def _dft_inv(inv, y, xs, gate_col, zprev, zprev_col, bias, L, nb, out_dtype, out_rows=None, out_base=0,
             prev=None):
    tm = min(L, 512)
    nt = L // tm
    args = [inv, y, xs, zprev, bias]
    in_specs = [pl.BlockSpec((tm, 2 * L), lambda i, b: (i, 0)),
                pl.BlockSpec((1, 2 * L, BRANCH_W), lambda i, b: (b, 0, 0)),
                pl.BlockSpec((tm, BRANCH_W), lambda i, b: (b * nt + i, gate_col)),
                pl.BlockSpec((tm, BRANCH_W), lambda i, b: (b * nt + i, zprev_col)),
                pl.BlockSpec((1, BRANCH_W), lambda i, b: (0, 0))]
    args, in_specs, alias = _into(prev, args, in_specs)
    return pl.pallas_call(
        _dft_inv_kernel,
        grid=(nt, nb),
        in_specs=in_specs,
        out_specs=pl.BlockSpec((tm, BRANCH_W), lambda i, b: ((out_base + b) * nt + i, 0)),
        out_shape=jax.ShapeDtypeStruct((out_rows or nb * L, BRANCH_W), out_dtype),
        input_output_aliases=alias,
        compiler_params=_cp(("arbitrary", "arbitrary")),
        name="hy_dft_inv",
    )(*args)


def _hyena(u_small, L, nseq, row_base, p, mats, out_rows, prev=None):
    (short_w, short_b, w1, b1, w2, b2, w3, b3, w4, freq, bias) = p
    fwd, inv = mats
    xs = _short_conv(u_small, short_w, short_b, L, nseq, row_base)
    hf, ss = _hyena_filter(L, w1, b1, w2, b2, w3, b3, w4, freq)
    kfs = _dft_fwd(fwd, hf, L, 4, lambda b: (0, b))
    zf = _dft_fwd(fwd, xs, L, nseq, lambda b: (b, 2))
    y = _spec_prod(zf, kfs, ss, 0, L, nseq)
    z1 = _dft_inv(inv, y, xs, 0, xs, 2, bias[0:1], L, nseq, F32)
    zf = _dft_fwd(fwd, z1, L, nseq, lambda b: (b, 0))
    y = _spec_prod(zf, kfs, ss, 1, L, nseq)
    return _dft_inv(inv, y, xs, 1, z1, 0, bias[1:2], L, nseq, BF16, out_rows, row_base, prev)


def _conformer_kernel(a_ref, g_ref, w_ref, b_ref, lg_ref, lb_ref, *rest, L, tc):
    o_ref, pad_scr = rest[-2:]
    pad_scr[0:16, :] = jnp.zeros((16, BRANCH_W), F32)
    pad_scr[16 + L:32 + L, :] = jnp.zeros((16, BRANCH_W), F32)

    def glu(c, carry):
        t0 = pl.multiple_of(c * tc, tc)
        pad_scr[pl.ds(16 + t0, tc), :] = a_ref[pl.ds(t0, tc), :] * _sigmoid(g_ref[pl.ds(t0, tc), :])
        return carry

    lax.fori_loop(0, L // tc, glu, 0)
    b = b_ref[...]
    lg = lg_ref[...]
    lb = lb_ref[...]

    def body(c, carry):
        t0 = pl.multiple_of(c * tc, tc)
        win = pad_scr[pl.ds(t0, tc + 32), :]
        acc = jnp.zeros((tc, BRANCH_W), F32) + b
        for r in range(8):
            sh = win[r:r + tc + 24]
            for a in range(4):
                m = 8 * a + r
                if 1 <= m <= CF_WIDTH:
                    acc = acc + w_ref[m - 1:m, :] * sh[8 * a:8 * a + tc]
        mu = jnp.mean(acc, axis=-1, keepdims=True)
        xc = acc - mu
        var = jnp.mean(xc * xc, axis=-1, keepdims=True)
        y = xc * lax.rsqrt(var + EPS) * lg + lb
        o_ref[pl.ds(t0, tc), :] = (y * _sigmoid(y)).astype(o_ref.dtype)
        return carry

    lax.fori_loop(0, L // tc, body, 0)


def _conformer(u_small, L, nseq, row_base, p, out_rows, prev=None):
    dw_w, dw_b, ln_g, ln_b = p
    tc = 128
    wpad = jnp.pad(dw_w, ((0, 32 - CF_WIDTH), (0, 0)))
    row = lambda a: a.reshape(1, BRANCH_W)
    vec = pl.BlockSpec((1, BRANCH_W), lambda s: (0, 0))
    args = [u_small, u_small, wpad, row(dw_b), row(ln_g), row(ln_b)]
    in_specs = [pl.BlockSpec((L, BRANCH_W), lambda s: (row_base + s, 3)),
                pl.BlockSpec((L, BRANCH_W), lambda s: (row_base + s, 4)),
                pl.BlockSpec((32, BRANCH_W), lambda s: (0, 0)), vec, vec, vec]
    args, in_specs, alias = _into(prev, args, in_specs)
    return pl.pallas_call(
        functools.partial(_conformer_kernel, L=L, tc=tc),
        grid=(nseq,),
        in_specs=in_specs,
        out_specs=pl.BlockSpec((L, BRANCH_W), lambda s: (row_base + s, 0)),
        out_shape=jax.ShapeDtypeStruct((out_rows, BRANCH_W), BF16),
        input_output_aliases=alias,
        scratch_shapes=[pltpu.VMEM((L + 32, BRANCH_W), F32)],
        compiler_params=_cp(("arbitrary",)),
        name="conformer",
    )(*args)


def _rope_tables(S, pad_rows, head, scale_unused=None):
    half = head // 2
    nf = half // 2
    lane = np.arange(LANES)
    inv_lane = (ROPE_BASE ** (-(np.arange(nf)) / nf))[(lane % half) % nf]
    is_row = (lane % head) < half
    pos = jnp.arange(S, dtype=jnp.int32)
    rows = (pos // GRID_W).astype(F32)[:, None]
    cols = (pos % GRID_W).astype(F32)[:, None]
    ang = jnp.where(jnp.asarray(is_row)[None, :], rows, cols) * jnp.asarray(inv_lane, F32)[None, :]
    cos = jnp.concatenate([jnp.cos(ang), jnp.ones((pad_rows, LANES), F32)], axis=0)
    sin = jnp.concatenate([jnp.sin(ang), jnp.zeros((pad_rows, LANES), F32)], axis=0)
    return cos, sin


def _group_ones(width, group):
    idx = np.arange(width)
    return jnp.asarray((idx[:, None] // group) == (idx[None, :] // group), BF16)


def _head_norm_rope(x, ones, group, gain, cos, sin, nf, out_scale):
    w = x.shape[1]
    hi, lo = _split(x * x)
    ms = (_dot(hi, ones) + _dot(lo, ones)) * (1.0 / group)
    xn = x * lax.rsqrt(ms + EPS) * gain
    reps = w // LANES
    c = jnp.concatenate([cos] * reps, axis=1) if reps > 1 else cos
    s = jnp.concatenate([sin] * reps, axis=1) if reps > 1 else sin
    lane = lax.broadcasted_iota(jnp.int32, x.shape, 1)
    first = (lane % (2 * nf)) < nf
    rot = jnp.where(first, -pltpu.roll(xn, w - nf, 1), pltpu.roll(xn, nf, 1))
    return (xn * c + rot * s) * out_scale


def _qkv_kernel(gq_ref, gkv_ref, dq_ref, dk_ref, dv_ref, cg_ref, sg_ref, cd_ref, sd_ref,
                o64_ref, o32_ref, gqn_ref, gkn_ref, dqn_ref, dkn_ref,
                qg_ref, kg_ref, vag_ref, vbg_ref, qd_ref, kd_ref, vad_ref, vbd_ref):
    cg, sg, cd, sd = cg_ref[...], sg_ref[...], cd_ref[...], sd_ref[...]
    o64, o32 = o64_ref[...], o32_ref[...]
    q = _head_norm_rope(gq_ref[...], o64, HEAD_DIM, gqn_ref[...], cg, sg, HEAD_DIM // 4, LOG2E * HEAD_DIM ** -0.5)
    qg_ref[...] = q.astype(BF16)
    kv = gkv_ref[...]
    k = _head_norm_rope(kv[:, :LANES], o64[:LANES, :LANES], HEAD_DIM, gkn_ref[...], cg, sg, HEAD_DIM // 4, 1.0)
    v = kv[:, LANES:]
    kk = jnp.concatenate([k, k], axis=1)
    vv = jnp.concatenate([v, v], axis=1)
    quarter = lax.broadcasted_iota(jnp.int32, kk.shape, 1) // HEAD_DIM
    kr = pltpu.roll(kk, HEAD_DIM, 1)
    vr = pltpu.roll(vv, HEAD_DIM, 1)
    kg_ref[...] = jnp.where((quarter == 0) | (quarter == 3), kk, kr).astype(BF16)
    vag_ref[...] = jnp.where(quarter == 0, vv, jnp.where(quarter == 2, vr, 0.0)).astype(BF16)
    vbg_ref[...] = jnp.where(quarter == 1, vr, jnp.where(quarter == 3, vv, 0.0)).astype(BF16)
    qd = _head_norm_rope(dq_ref[...], o32, DIFF_D, dqn_ref[...], cd, sd, DIFF_D // 4, LOG2E * DIFF_D ** -0.5)
    qd_ref[...] = qd.astype(BF16)
    kd = _head_norm_rope(dk_ref[...], o32, DIFF_D, dkn_ref[...], cd, sd, DIFF_D // 4, 1.0)
    kd_ref[...] = kd.astype(BF16)
    vd = dv_ref[...]
    even = (lax.broadcasted_iota(jnp.int32, vd.shape, 1) // HEAD_DIM) % 2 == 0
    vad_ref[...] = jnp.where(even, vd, 0.0).astype(BF16)
    vbd_ref[...] = jnp.where(even, 0.0, vd).astype(BF16)


def _qkv_prep(u_small, gqn, gkn, dqn, dkn, tables, dims):
    n = u_small.shape[0]
    tm = dims["tm_small"]
    S = dims["S"]
    n_lat_tiles = dims["n_lat"] // tm
    per_seq = S // tm
    tmap = lambda i: (jnp.where(i < n_lat_tiles, i % per_seq, per_seq), 0)
    col = lambda c: pl.BlockSpec((tm, BRANCH_W), lambda i: (i, c))
    tab = pl.BlockSpec((tm, LANES), tmap)
    full = lambda a: pl.BlockSpec(a.shape, lambda i: (0,) * a.ndim)
    o64, o32 = _group_ones(BRANCH_W, HEAD_DIM), _group_ones(BRANCH_W, DIFF_D)
    gains = [jnp.tile(gqn, 4).reshape(1, 256), jnp.tile(gkn, 2).reshape(1, 128),
             jnp.tile(dqn.reshape(-1), 4).reshape(1, 256), jnp.tile(dkn.reshape(-1), 4).reshape(1, 256)]
    out = pl.BlockSpec((tm, BRANCH_W), lambda i: (i, 0))
    return pl.pallas_call(
        _qkv_kernel,
        grid=(n // tm,),
        in_specs=[col(5), col(6), col(7), col(8), col(9), tab, tab, tab, tab, full(o64), full(o32)]
        + [full(g) for g in gains],
        out_specs=[out] * 8,
        out_shape=[jax.ShapeDtypeStruct((n, BRANCH_W), BF16)] * 8,
        compiler_params=_cp(("arbitrary",)),
        name="qkv_prep",
    )(u_small, u_small, u_small, u_small, u_small, *tables, o64, o32, *gains)


def _lane_pick(lane_lo, a, b):
    return jnp.where(lane_lo, a, b)


def _flash_kernel(*refs, segs, tk, tq, diff, lam_init):
    q_ref, o_ref = refs[0], refs[-1]
    seg_refs = [refs[1 + 3 * i:4 + 3 * i] for i in range(len(segs))]
    aux_ref = refs[1 + 3 * len(segs)]
    lane = lax.broadcasted_iota(jnp.int32, (1, LANES), 1)
    lane_lo = lane < HEAD_DIM
    if diff:
        masks = [(lane >= g * DIFF_D) & (lane < (g + 1) * DIFF_D) for g in range(4)]
        acc_of = [0, 1, 0, 1]
        use_a = [True, True, False, False]
        n_acc = 2
    else:
        masks = [lane_lo, ~lane_lo]
        acc_of = [0, 0]
        use_a = [True, False]
        n_acc = 1
    R = len(masks)
    pairs = [slice(p * LANES, (p + 1) * LANES) for p in range(2)]
    qsts = [jnp.concatenate([jnp.where(m, q_ref[:, ps], jnp.zeros((tq, LANES), BF16)) for m in masks], axis=0)
            for ps in pairs]
    lo_hi = []
    for a in range(n_acc):
        rs = [r for r in range(R) if acc_of[r] == a]
        lo_hi.append(([r for r in rs if use_a[r]][0], [r for r in rs if not use_a[r]][0]))

    def rows(x, r):
        return x[r * tq:(r + 1) * tq]

    def chunk(kv_refs, t0, size, carry):
        k_ref, va_ref, vb_ref = kv_refs
        new = []
        for p, ps in enumerate(pairs):
            m_run, l_run, accs = carry[p]
            k = k_ref[pl.ds(t0, size), ps]
            s = lax.dot_general(qsts[p], k, (((1,), (1,)), ((), ())), preferred_element_type=F32)
            m_new = jnp.maximum(m_run, jnp.max(s, axis=-1, keepdims=True))
            alpha = jnp.exp2(m_run - m_new)
            pr = jnp.exp2(s - m_new)
            l_new = alpha * l_run + jnp.sum(pr, axis=-1, keepdims=True)
            prb = pr.astype(BF16)
            va = va_ref[pl.ds(t0, size), ps]
            vb = vb_ref[pl.ds(t0, size), ps]
            new_accs = []
            for a, (r_lo, r_hi) in enumerate(lo_hi):
                al = _lane_pick(lane_lo, rows(alpha, r_lo), rows(alpha, r_hi))
                upd = _dot(rows(prb, r_lo), va) + _dot(rows(prb, r_hi), vb)
                new_accs.append(accs[a] * al + upd)
            new.append((m_new, l_new, tuple(new_accs)))
        return tuple(new)

    one = (jnp.full((R * tq, 1), -jnp.inf, F32), jnp.zeros((R * tq, 1), F32),
           tuple(jnp.zeros((tq, LANES), F32) for _ in range(n_acc)))
    carry = (one, one)
    for kv_refs, T in zip(seg_refs, segs):
        n_main = T // tk
        if n_main:
            carry = lax.fori_loop(0, n_main,
                                  lambda c, cr, kv_refs=kv_refs: chunk(kv_refs, pl.multiple_of(c * tk, tk), tk, cr),
                                  carry, unroll=2 if n_main % 2 == 0 else 1)
        if T - n_main * tk:
            carry = chunk(kv_refs, n_main * tk, T - n_main * tk, carry)

    outs = []
    for p in range(2):
        m_fin, l_fin, accs = carry[p]
        inv_l = 1.0 / l_fin
        norm = [accs[a] * _lane_pick(lane_lo, rows(inv_l, r_lo), rows(inv_l, r_hi))
                for a, (r_lo, r_hi) in enumerate(lo_hi)]
        if diff:
            aux = aux_ref[...]
            lam = (jnp.exp(jnp.sum(aux[0:1] * aux[1:2], axis=-1, keepdims=True))
                   - jnp.exp(jnp.sum(aux[2:3] * aux[3:4], axis=-1, keepdims=True)) + lam_init)
            o = norm[0] - lam * norm[1]
            sq = o * o
            s_lo = jnp.sum(jnp.where(lane_lo, sq, 0.0), axis=-1, keepdims=True)
            s_hi = jnp.sum(jnp.where(lane_lo, 0.0, sq), axis=-1, keepdims=True)
            ms = _lane_pick(lane_lo, s_lo, s_hi) * (1.0 / HEAD_DIM)
            o = o * lax.rsqrt(ms + EPS) * aux[4:5] * (1.0 - lam_init)
        else:
            o = norm[0]
        outs.append(o)
    o_ref[...] = jnp.concatenate(outs, axis=1).astype(o_ref.dtype)


def _into(prev, args, in_specs):
    if prev is None:
        return args, in_specs, {}
    return args + [prev], in_specs + [pl.BlockSpec(memory_space=pl.ANY)], {len(args): 0}


def _flash(q, k, va, vb, aux, nb, Lq, q_base, segs, diff, lam_init, out_rows, out_base, prev=None):
    tq = 256
    tk = 512
    nt = Lq // tq
    args = [q]
    in_specs = [pl.BlockSpec((tq, BRANCH_W), lambda b, i: (q_base + b * nt + i, 0))]
    for length, base in segs:
        spec = pl.BlockSpec((length, BRANCH_W), lambda b, i, base=base: (base + b, 0))
        args += [k, va, vb]
        in_specs += [spec, spec, spec]
    args.append(aux)
    in_specs.append(pl.BlockSpec((8, LANES), lambda b, i: (0, 0)))
    args, in_specs, alias = _into(prev, args, in_specs)
    return pl.pallas_call(
        functools.partial(_flash_kernel, segs=tuple(s[0] for s in segs), tk=tk, tq=tq, diff=diff,
                          lam_init=lam_init),
        grid=(nb, nt),
        in_specs=in_specs,
        out_specs=pl.BlockSpec((tq, BRANCH_W), lambda b, i: (out_base + b * nt + i, 0)),
        out_shape=jax.ShapeDtypeStruct((out_rows, BRANCH_W), BF16),
        input_output_aliases=alias,
        compiler_params=_cp(("arbitrary", "arbitrary")),
        name="flash_diff" if diff else "flash_gqa",
    )(*args)


def _merge_kernel(y0, y1, y2, y3, g_ref, wb_ref, wo_ref, x_ref, mod_ref, o_ref):
    d = D_MODEL
    acc = g_ref[:, 0:d].astype(F32) * _dot(y0[...], wb_ref[0])
    acc = acc + g_ref[:, d:2 * d].astype(F32) * _dot(y1[...], wb_ref[1])
    acc = acc + g_ref[:, 2 * d:3 * d].astype(F32) * _dot(y2[...], wb_ref[2])
    acc = acc + g_ref[:, 3 * d:4 * d].astype(F32) * _dot(y3[...], wb_ref[3])
    mix = _dot(acc.astype(BF16), wo_ref[...])
    o_ref[...] = x_ref[...] + mod_ref[0][2:3, :] * mix


def _merge(ys, gate, wb, wo, x, mods_l, dims):
    n, d = ys[0].shape[0], x.shape[1]
    tm = dims["tm_small"]
    mrow = dims["mod_row"](tm)
    yspec = pl.BlockSpec((tm, BRANCH_W), lambda i: (i, 0))
    return pl.pallas_call(
        _merge_kernel,
        grid=(n // tm,),
        in_specs=[yspec] * 4 + [pl.BlockSpec((tm, GATE_COLS), lambda i: (i, 0)),
                                pl.BlockSpec(wb.shape, lambda i: (0, 0, 0)),
                                pl.BlockSpec(wo.shape, lambda i: (0, 0)),
                                pl.BlockSpec((tm, d), lambda i: (i, 0)),
                                pl.BlockSpec((1, 6, d), lambda i: (mrow(i), 0, 0))],
        out_specs=pl.BlockSpec((tm, d), lambda i: (i, 0)),
        out_shape=jax.ShapeDtypeStruct((n, d), F32),
        compiler_params=_cp(("arbitrary",)),
        name="merge",
    )(*ys, gate, wb, wo, x, mods_l)


def _ffn_kernel(x_ref, mod_ref, g_ref, w1_ref, w3_ref, w2_ref, o_ref, h_scr, acc_scr):
    j = pl.program_id(1)

    @pl.when(j == 0)
    def _():
        m = mod_ref[0]
        h_scr[...] = _norm_mod(x_ref[...], g_ref[...], m[3:4, :], m[4:5, :]).astype(BF16)
        acc_scr[...] = jnp.zeros_like(acc_scr)

    h = h_scr[...]
    a = _dot(h, w1_ref[...])
    b = _dot(h, w3_ref[...])
    t = (a * _sigmoid(a) * b).astype(BF16)
    acc_scr[...] += _dot(t, w2_ref[...])

    @pl.when(j == pl.num_programs(1) - 1)
    def _():
        o_ref[...] = x_ref[...] + mod_ref[0][5:6, :] * acc_scr[...]


def _ffn(x, mods_l, gain, w1, w3, w2, dims):
    n, d = x.shape
    tm, tf = dims["tm_big"], 256
    mrow = dims["mod_row"](tm)
    return pl.pallas_call(
        _ffn_kernel,
        grid=(n // tm, w1.shape[1] // tf),
        in_specs=[pl.BlockSpec((tm, d), lambda i, j: (i, 0)),
                  pl.BlockSpec((1, 6, d), lambda i, j: (mrow(i), 0, 0)),
                  pl.BlockSpec((1, d), lambda i, j: (0, 0)),
                  pl.BlockSpec((d, tf), lambda i, j: (0, j)),
                  pl.BlockSpec((d, tf), lambda i, j: (0, j)),
                  pl.BlockSpec((tf, d), lambda i, j: (j, 0))],
        out_specs=pl.BlockSpec((tm, d), lambda i, j: (i, 0)),
        out_shape=jax.ShapeDtypeStruct((n, d), F32),
        scratch_shapes=[pltpu.VMEM((tm, d), BF16), pltpu.VMEM((tm, d), F32)],
        compiler_params=_cp(("arbitrary", "arbitrary")),
        name="ffn_dense",
    )(x, mods_l, gain, w1, w3, w2)


def _route_kernel(x_ref, mod_ref, g_ref, wr_ref, tok_ref, gate_ref):
    m = mod_ref[0]
    h = _norm_mod(x_ref[...], g_ref[...], m[3:4, :], m[4:5, :])
    tok_ref[...] = h.astype(BF16)
    logits = _dot3(h, wr_ref[...])
    lane = lax.broadcasted_iota(jnp.int32, logits.shape, 1)
    lg = jnp.where(lane < N_EXPERTS, logits, -jnp.inf)
    m1 = jnp.max(lg, axis=-1, keepdims=True)
    i1 = jnp.min(jnp.where(lg == m1, lane, LANES), axis=-1, keepdims=True)
    lg2 = jnp.where(lane == i1, -jnp.inf, lg)
    m2 = jnp.max(lg2, axis=-1, keepdims=True)
    i2 = jnp.min(jnp.where(lg2 == m2, lane, LANES), axis=-1, keepdims=True)
    e2 = jnp.exp(m2 - m1)
    g1 = 1.0 / (1.0 + e2)
    gate_ref[...] = jnp.where(lane == i1, g1, jnp.where(lane == i2, e2 * g1, 0.0))


def _route(x, mods_l, gain, w_router, dims):
    n, d = x.shape
    tm = dims["tm_small"]
    mrow = dims["mod_row"](tm)
    wr = jnp.pad(w_router, ((0, 0), (0, LANES - N_EXPERTS)))
    return pl.pallas_call(
        _route_kernel,
        grid=(n // tm,),
        in_specs=[pl.BlockSpec((tm, d), lambda i: (i, 0)),
                  pl.BlockSpec((1, 6, d), lambda i: (mrow(i), 0, 0)),
                  pl.BlockSpec((1, d), lambda i: (0, 0)),
                  pl.BlockSpec((d, LANES), lambda i: (0, 0))],
        out_specs=[pl.BlockSpec((tm, d), lambda i: (i, 0)), pl.BlockSpec((tm, LANES), lambda i: (i, 0))],
        out_shape=[jax.ShapeDtypeStruct((n, d), BF16), jax.ShapeDtypeStruct((n, LANES), F32)],
        compiler_params=_cp(("arbitrary",)),
        name="moe_route",
    )(x, mods_l, gain, wr)


MOE_SUB = 128
MOE_GRAN = 64


def _moe_kernel(tok_ref, gate_ref, tri_ref, w1_ref, w3_ref, w2_ref, x_ref, mod_ref, o_ref,
                rank_scr, rank_t_scr, mask_t_scr, xe_scr, ye_scr, cnt_smem, *, tm):
    e = pl.program_id(1)
    j = pl.program_id(2)
    ne = pl.num_programs(1)
    nj = pl.num_programs(2)
    nsub = tm // MOE_SUB
    lane = lax.broadcasted_iota(jnp.int32, (1, LANES), 1)

    @pl.when((e == 0) & (j == 0))
    def _():
        tri = tri_ref[...]
        carry = jnp.zeros((1, LANES), F32)
        for blk in range(tm // 256):
            rs = slice(blk * 256, (blk + 1) * 256)
            msk = (gate_ref[rs, :] > 0.0).astype(BF16)
            rank_scr[rs, :] = _dot(tri, msk) + carry
            carry = carry + jnp.sum(msk.astype(F32), axis=0, keepdims=True)
        rank_t_scr[...] = jnp.transpose(rank_scr[...])
        mask_t_scr[...] = jnp.transpose((gate_ref[...] > 0.0).astype(F32))
        for ee in range(N_EXPERTS):
            cnt_smem[ee] = jnp.sum(jnp.where(lane == ee, carry, 0.0)).astype(jnp.int32)
        o_ref[...] = jnp.zeros_like(o_ref)
        ye_scr[...] = jnp.zeros_like(ye_scr)

    cnt = cnt_smem[e]

    @pl.when(j == 0)
    def _():
        rk = rank_t_scr[pl.ds(e, 1), :]
        mk = mask_t_scr[pl.ds(e, 1), :]
        for sb in range(nsub):
            @pl.when(sb * MOE_SUB < cnt)
            def _():
                r_iota = lax.broadcasted_iota(jnp.int32, (MOE_SUB, tm), 0) + sb * MOE_SUB
                sel = jnp.where((rk == r_iota.astype(F32)) & (mk > 0.0), 1.0, 0.0).astype(BF16)
                xe_scr[sb * MOE_SUB:(sb + 1) * MOE_SUB, :] = _dot(sel, tok_ref[...]).astype(BF16)

    for nb_ in range(1, tm // MOE_GRAN + 1):
        @pl.when((cnt > (nb_ - 1) * MOE_GRAN) & (cnt <= nb_ * MOE_GRAN))
        def _():
            rows = slice(0, nb_ * MOE_GRAN)
            xe = xe_scr[rows, :]
            a = _dot(xe, w1_ref[0])
            b = _dot(xe, w3_ref[0])
            t = (a * _sigmoid(a) * b).astype(BF16)
            y = _dot(t, w2_ref[0])

            @pl.when(j == 0)
            def _():
                ye_scr[rows, :] = y

            @pl.when(j > 0)
            def _():
                ye_scr[rows, :] += y

    @pl.when(j == nj - 1)
    def _():
        def scatter(rc, carry):
            r0 = pl.multiple_of(rc * 256, 256)
            gt = gate_ref[pl.ds(r0, 256), :]
            g_e = jnp.sum(jnp.where(lane == e, gt, 0.0), axis=-1, keepdims=True)
            r_e = jnp.sum(jnp.where(lane == e, rank_scr[pl.ds(r0, 256), :], 0.0), axis=-1, keepdims=True)
            lo = jnp.min(jnp.where(g_e > 0.0, r_e, 1e9))
            hi = jnp.max(jnp.where(g_e > 0.0, r_e, -1.0))
            for sb in range(nsub):
                @pl.when((hi >= sb * MOE_SUB) & (lo < (sb + 1) * MOE_SUB))
                def _():
                    c_iota = lax.broadcasted_iota(jnp.int32, (256, MOE_SUB), 1) + sb * MOE_SUB
                    sel_t = jnp.where((r_e == c_iota.astype(F32)) & (g_e > 0.0), 1.0, 0.0).astype(BF16)
                    ye = ye_scr[sb * MOE_SUB:(sb + 1) * MOE_SUB, :].astype(BF16)
                    o_ref[pl.ds(r0, 256), :] += g_e * _dot(sel_t, ye)

            @pl.when(e == ne - 1)
            def _():
                o_ref[pl.ds(r0, 256), :] = (x_ref[pl.ds(r0, 256), :]
                                            + mod_ref[0][5:6, :] * o_ref[pl.ds(r0, 256), :])
            return carry

        lax.fori_loop(0, tm // 256, scatter, 0)


def _moe(tok, gates, w1, w3, w2, x, mods_l, dims):
    n, d = tok.shape
    tm = dims["tm_big"]
    mrow = dims["mod_row"](tm)
    ne, _, ff = w1.shape
    tf = 896
    tri = jnp.asarray(np.tril(np.ones((256, 256)), -1), BF16)
    return pl.pallas_call(
        functools.partial(_moe_kernel, tm=tm),
        grid=(n // tm, ne, ff // tf),
        in_specs=[pl.BlockSpec((tm, d), lambda i, e, j: (i, 0)),
                  pl.BlockSpec((tm, LANES), lambda i, e, j: (i, 0)),
                  pl.BlockSpec((256, 256), lambda i, e, j: (0, 0)),
                  pl.BlockSpec((1, d, tf), lambda i, e, j: (e, 0, j)),
                  pl.BlockSpec((1, d, tf), lambda i, e, j: (e, 0, j)),
                  pl.BlockSpec((1, tf, d), lambda i, e, j: (e, j, 0)),
                  pl.BlockSpec((tm, d), lambda i, e, j: (i, 0)),
                  pl.BlockSpec((1, 6, d), lambda i, e, j: (mrow(i), 0, 0))],
        out_specs=pl.BlockSpec((tm, d), lambda i, e, j: (i, 0)),
        out_shape=jax.ShapeDtypeStruct((n, d), F32),
        scratch_shapes=[pltpu.VMEM((tm, LANES), F32), pltpu.VMEM((LANES, tm), F32), pltpu.VMEM((LANES, tm), F32),
                        pltpu.VMEM((tm, d), BF16), pltpu.VMEM((tm, d), F32), pltpu.SMEM((N_EXPERTS,), jnp.int32)],
        compiler_params=_cp(("arbitrary", "arbitrary", "arbitrary")),
        name="moe_experts",
    )(tok, gates, tri, w1, w3, w2, x, mods_l)


def _make_dims(B, S, n_ctx):
    n_lat, n = B * S, B * (S + n_ctx)

    def pick(prefs):
        for t in prefs:
            if S % t == 0 and n_lat % t == 0 and (n - n_lat) % t == 0:
                return t
        raise ValueError("no row tile divides the latent and context token counts")

    def mod_row(tm):
        return lambda i: jnp.minimum((i * tm) // S, B)

    return {"B": B, "S": S, "n_ctx": n_ctx, "n_lat": n_lat, "n": n,
            "tm_big": pick((1024, 512, 256)), "tm_small": pick((512, 256)), "mod_row": mod_row}


def kernel(x, c, ctx, c_ctx, w_ada, b_ada, norm_mix, norm_ffn, w_in, hy_short_w, hy_short_b, hy_w1, hy_b1, hy_w2, hy_b2, hy_w3, hy_b3, hy_w4, hy_freq, hy_bias, cf_dw_w, cf_dw_b, cf_ln_g, cf_ln_b, gqa_qn, gqa_kn, diff_qn, diff_kn, diff_lq1, diff_lk1, diff_lq2, diff_lk2, diff_subln, w_branch, w_out, ffn_w1, ffn_w3, ffn_w2, moe_router, moe_w1, moe_w3, moe_w2):
    B, S, D = x.shape
    n_ctx = ctx.shape[1]
    depth = w_ada.shape[0]
    dims = _make_dims(B, S, n_ctx)
    n_lat = dims["n_lat"]
    tm_s = dims["tm_small"]

    c16 = jnp.concatenate([c, c_ctx[None, :], jnp.zeros((16 - B - 1, D), F32)], axis=0)
    mods = _ada_mods(c16, w_ada, b_ada).reshape(depth, 16, 6, D)

    xa = jnp.concatenate([x.reshape(n_lat, D), ctx.reshape(B * n_ctx, D)], axis=0)
    tables = _rope_tables(S, tm_s, HEAD_DIM) + _rope_tables(S, tm_s, DIFF_D)
    mats_lat = _dft_mats(S)
    mats_ctx = _dft_mats(n_ctx)

    for l in range(depth):
        lam_init = 0.8 - 0.6 * math.exp(-0.3 * l)
        u_small, gate = _in_proj(xa, mods[l], norm_mix[l].reshape(1, D), w_in[l].astype(BF16), dims)

        hy_p = (hy_short_w[l], hy_short_b[l], hy_w1[l], hy_b1[l], hy_w2[l], hy_b2[l], hy_w3[l], hy_b3[l],
                hy_w4[l], hy_freq[l], hy_bias[l])
        cf_p = (cf_dw_w[l], cf_dw_b[l], cf_ln_g[l], cf_ln_b[l])
        last = l == depth - 1
        n_out = n_lat if last else dims["n"]
        ctx_base = n_lat // n_ctx
        y_hy = _hyena(u_small, S, B, 0, hy_p, mats_lat, n_out)
        y_cf = _conformer(u_small, S, B, 0, cf_p, n_out)
        if not last:
            y_hy = _hyena(u_small, n_ctx, B, ctx_base, hy_p, mats_ctx, n_out, y_hy)
            y_cf = _conformer(u_small, n_ctx, B, ctx_base, cf_p, n_out, y_cf)

        qg, kg, vag, vbg, qd, kd, vad, vbd = _qkv_prep(u_small, gqa_qn[l], gqa_kn[l], diff_qn[l], diff_kn[l],
                                                      tables, dims)

        aux = jnp.zeros((8, LANES), F32)
        aux = aux.at[0, :DIFF_D].set(diff_lq1[l]).at[1, :DIFF_D].set(diff_lk1[l])
        aux = aux.at[2, :DIFF_D].set(diff_lq2[l]).at[3, :DIFF_D].set(diff_lk2[l])
        aux = aux.at[4, :].set(jnp.tile(diff_subln[l], 2))
        y_att = []
        for (q, k, va, vb, is_diff) in ((qg, kg, vag, vbg, False), (qd, kd, vad, vbd, True)):
            y = _flash(q, k, va, vb, aux, B, S, 0, ((S, 0), (n_ctx, ctx_base)), is_diff, lam_init, n_out, 0)
            if not last:
                y = _flash(q, k, va, vb, aux, B, n_ctx, n_lat // 256, ((n_ctx, ctx_base),), is_diff, lam_init,
                           n_out, n_lat // 256, y)
            y_att.append(y)

        xa = _merge((y_hy, y_cf, y_att[0], y_att[1]), gate, w_branch[l].astype(BF16), w_out[l].astype(BF16),
                    xa, mods[l], dims)

        i = l // 2
        if l % 2 == 0:
            xa = _ffn(xa, mods[l], norm_ffn[l].reshape(1, D), ffn_w1[i].astype(BF16), ffn_w3[i].astype(BF16),
                      ffn_w2[i].astype(BF16), dims)
        else:
            tok, gates = _route(xa, mods[l], norm_ffn[l].reshape(1, D), moe_router[i], dims)
            xa = _moe(tok, gates, moe_w1[i].astype(BF16), moe_w3[i].astype(BF16), moe_w2[i].astype(BF16),
                      xa, mods[l], dims)
    return xa[:n_lat].reshape(B, S, D)
```

```python
import functools
import math

import numpy as np
import jax
import jax.numpy as jnp
from jax import lax
from jax.experimental import pallas as pl
from jax.experimental.pallas import tpu as pltpu

F32 = jnp.float32
BF16 = jnp.bfloat16

D_MODEL = 1024
GRID_W = 64
BRANCH_W = 256
EPS = 1e-6
HY_EMB = 33
HY_EMB_PAD = 128
HY_FFN = 64
HY_TARGET = 1e-2
HY_FAST = 0.3
HY_SLOW = 1.5
CF_WIDTH = 31
HEAD_DIM = 64
DIFF_D = 32
ROPE_BASE = 10000.0
N_EXPERTS = 8
LANES = 128
LOG2E = 1.4426950408889634
FIXED_SHIFT_LIMIT = 60.0
SMALL_COLS = 2560
GATE_COLS = 4 * D_MODEL
VMEM_LIMIT = 48 * 1024 * 1024


def _cp(sem, vmem=VMEM_LIMIT):
    return pltpu.CompilerParams(dimension_semantics=sem, vmem_limit_bytes=vmem)


def _sigmoid(x):
    return 1.0 / (1.0 + jnp.exp(-x))


def _dot(a, b):
    return jnp.dot(a, b, preferred_element_type=F32)


def _split(a):
    hi = a.astype(BF16)
    lo = (a - hi.astype(F32)).astype(BF16)
    return hi, lo


def _dot3(a, b):
    ah, al = _split(a)
    bh, bl = _split(b)
    return _dot(ah, bh) + (_dot(al, bh) + _dot(ah, bl))


def _norm_mod(x, gain, shift, scale):
    ms = jnp.mean(x * x, axis=-1, keepdims=True)
    return (x * lax.rsqrt(ms + EPS) * gain) * (1.0 + scale) + shift


def _ada_kernel(c_ref, w_ref, b_ref, o_ref):
    c = c_ref[...]
    s = c * _sigmoid(c)
    o_ref[0] = _dot3(s, w_ref[0]) + b_ref[0]


def _ada_mods(c16, w_ada, b_ada):
    depth, d, n6 = w_ada.shape
    tn = 512
    return pl.pallas_call(
        _ada_kernel,
        grid=(depth, n6 // tn),
        in_specs=[pl.BlockSpec((16, d), lambda l, j: (0, 0)),
                  pl.BlockSpec((1, d, tn), lambda l, j: (l, 0, j)),
                  pl.BlockSpec((1, 1, tn), lambda l, j: (l, 0, j))],
        out_specs=pl.BlockSpec((1, 16, tn), lambda l, j: (l, 0, j)),
        out_shape=jax.ShapeDtypeStruct((depth, 16, n6), F32),
        compiler_params=_cp(("arbitrary", "arbitrary")),
        name="ada_mods",
    )(c16, w_ada, b_ada.reshape(depth, 1, n6))


def _in_proj_kernel(x_ref, mod_ref, g_ref, w_ref, u_ref, gate_ref, h_scr, *, n_small):
    j = pl.program_id(1)

    @pl.when(j == 0)
    def _():
        m = mod_ref[0]
        h_scr[...] = _norm_mod(x_ref[...], g_ref[...], m[0:1, :], m[1:2, :]).astype(BF16)

    r = _dot(h_scr[...], w_ref[...])

    @pl.when(j < n_small)
    def _():
        u_ref[...] = r

    @pl.when(j >= n_small)
    def _():
        gate_ref[...] = _sigmoid(r).astype(BF16)


def _in_proj(x, mods_l, gain, w_bf, dims):
    n, d = x.shape
    tm, tn = dims["tm_big"], 512
    n_small = SMALL_COLS // tn
    n_cols = w_bf.shape[1] // tn
    mrow = dims["mod_row"](tm)
    return pl.pallas_call(
        functools.partial(_in_proj_kernel, n_small=n_small),
        grid=(n // tm, n_cols),
        in_specs=[pl.BlockSpec((tm, d), lambda i, j: (i, 0)),
                  pl.BlockSpec((1, 6, d), lambda i, j: (mrow(i), 0, 0)),
                  pl.BlockSpec((1, d), lambda i, j: (0, 0)),
                  pl.BlockSpec((d, tn), lambda i, j: (0, j))],
        out_specs=[pl.BlockSpec((tm, tn), lambda i, j: (i, jnp.minimum(j, n_small - 1))),
                   pl.BlockSpec((tm, tn), lambda i, j: (i, jnp.maximum(j - n_small, 0)))],
        out_shape=[jax.ShapeDtypeStruct((n, SMALL_COLS), F32),
                   jax.ShapeDtypeStruct((n, GATE_COLS), BF16)],
        scratch_shapes=[pltpu.VMEM((tm, d), BF16)],
        compiler_params=_cp(("arbitrary", "arbitrary")),
        name="in_proj",
    )(x, mods_l, gain, w_bf)


def _short_conv_kernel(u_ref, w_ref, b_ref, o_ref, pad_scr, *, L, tc):
    pad_scr[0:8, :] = jnp.zeros((8, BRANCH_W), F32)
    pad_scr[8 + L:16 + L, :] = jnp.zeros((8, BRANCH_W), F32)
    pad_scr[8:8 + L, :] = u_ref[...]
    w = w_ref[0]
    b = b_ref[0]

    def body(c, carry):
        t0 = pl.multiple_of(c * tc, tc)
        win = pad_scr[pl.ds(t0, tc + 16), :]
        acc = b + w[0:1, :] * win[7:7 + tc]
        acc = acc + w[1:2, :] * win[8:8 + tc]
        acc = acc + w[2:3, :] * win[9:9 + tc]
        o_ref[pl.ds(t0, tc), :] = acc
        return carry

    lax.fori_loop(0, L // tc, body, 0)


def _short_conv(u_small, w, b, L, nseq, row_base):
    w3 = jnp.transpose(w.reshape(3, 3, BRANCH_W), (1, 0, 2))
    w3 = jnp.pad(w3, ((0, 0), (0, 5), (0, 0)))
    b3 = b.reshape(3, 1, BRANCH_W)
    tc = min(L, 256)
    return pl.pallas_call(
        functools.partial(_short_conv_kernel, L=L, tc=tc),
        grid=(nseq, 3),
        in_specs=[pl.BlockSpec((L, BRANCH_W), lambda s, j: (row_base + s, j)),
                  pl.BlockSpec((1, 8, BRANCH_W), lambda s, j: (j, 0, 0)),
                  pl.BlockSpec((1, 1, BRANCH_W), lambda s, j: (j, 0, 0))],
        out_specs=pl.BlockSpec((L, BRANCH_W), lambda s, j: (s, j)),
        out_shape=jax.ShapeDtypeStruct((nseq * L, 3 * BRANCH_W), F32),
        scratch_shapes=[pltpu.VMEM((L + 16, BRANCH_W), F32)],
        compiler_params=_cp(("arbitrary", "arbitrary")),
        name="hy_short_conv",
    )(u_small, w3, b3)


def _filter_consts(L):
    t = np.linspace(0.0, 1.0, L)[:, None]
    bands = (HY_EMB - 1) // 2
    fr = np.linspace(1e-4, bands - 1, bands)[None, :]
    wpos = 2.0 * math.pi * np.arange(L)[:, None] / L
    z = np.concatenate([t, np.cos(fr * wpos), -np.sin(fr * wpos)], axis=-1)
    z = np.pad(z, ((0, 0), (0, HY_EMB_PAD - HY_EMB)))
    deltas = np.abs(np.linspace(math.log(HY_TARGET) / HY_SLOW, math.log(HY_TARGET) / HY_FAST, BRANCH_W))
    win = np.exp(-t * deltas[None, :])
    return jnp.asarray(z, F32), jnp.asarray(win, F32)


def _filter_kernel(z_ref, w1, b1, w2, b2, w3, b3, w4, fq, win_ref, hf_ref, ss_ref, *, tr):
    i = pl.program_id(0)
    f = fq[...]
    a = jnp.sin(f * (_dot3(z_ref[...], w1[...]) + b1[...]))
    a = jnp.sin(f * (_dot3(a, w2[...]) + b2[...]))
    a = jnp.sin(f * (_dot3(a, w3[...]) + b3[...]))
    h = _dot3(a, w4[...])
    win = win_ref[...]
    h = h * jnp.concatenate([win, win, win, win], axis=1)
    row = lax.broadcasted_iota(jnp.int32, h.shape, 0) + i * tr
    col = lax.broadcasted_iota(jnp.int32, h.shape, 1)
    h = jnp.where((row == 0) & (col >= 2 * BRANCH_W), 0.0, h)
    hf_ref[...] = h

    @pl.when(i == 0)
    def _():
        ss_ref[...] = jnp.zeros_like(ss_ref)

    ss_ref[...] += jnp.sum(h * h, axis=0, keepdims=True)


def _hyena_filter(L, w1, b1, w2, b2, w3, b3, w4, freq):
    z, win = _filter_consts(L)
    tr = min(L, 256)
    w1p = jnp.pad(w1, ((0, HY_EMB_PAD - HY_EMB), (0, 0)))
    full = lambda a: pl.BlockSpec(a.shape, lambda i: (0,) * a.ndim)
    args = [w1p, b1.reshape(1, -1), w2, b2.reshape(1, -1), w3, b3.reshape(1, -1), w4, freq.reshape(1, -1)]
    n_out = w4.shape[1]
    return pl.pallas_call(
        functools.partial(_filter_kernel, tr=tr),
        grid=(L // tr,),
        in_specs=[pl.BlockSpec((tr, HY_EMB_PAD), lambda i: (i, 0))] + [full(a) for a in args]
        + [pl.BlockSpec((tr, BRANCH_W), lambda i: (i, 0))],
        out_specs=[pl.BlockSpec((tr, n_out), lambda i: (i, 0)),
                   pl.BlockSpec((1, n_out), lambda i: (0, 0))],
        out_shape=[jax.ShapeDtypeStruct((L, n_out), F32), jax.ShapeDtypeStruct((1, n_out), F32)],
        compiler_params=_cp(("arbitrary",)),
        name="hy_filter",
    )(z, *args, win)


def _dft_mats(L):
    N = 2 * L
    blk = 64
    k = jnp.arange(L, dtype=jnp.int32)[:, None]
    nh = jnp.arange(L // blk, dtype=jnp.int32)[None, :]
    nl = jnp.arange(blk, dtype=jnp.int32)[None, :]
    w = 2.0 * math.pi / N
    a = ((k * (blk * nh)) % N).astype(F32) * w
    b = ((k * nl) % N).astype(F32) * w
    ca, sa, cb, sb = jnp.cos(a), jnp.sin(a), jnp.cos(b), jnp.sin(b)
    cos = (ca[:, :, None] * cb[:, None, :] - sa[:, :, None] * sb[:, None, :]).reshape(L, L)
    sin = (sa[:, :, None] * cb[:, None, :] + ca[:, :, None] * sb[:, None, :]).reshape(L, L)
    alt = jnp.where(jnp.arange(L) % 2 == 0, 1.0, -1.0).astype(F32)
    first = (jnp.arange(L) == 0)
    s_f = jnp.where(first[:, None], alt[None, :], -sin)
    fwd = jnp.concatenate([cos, s_f], axis=0).astype(BF16)
    colscale = jnp.where(first, 1.0 / N, 2.0 / N).astype(F32)
    g_c = cos * colscale[None, :]
    g_s = jnp.where(first[None, :], alt[:, None] / N, -sin * (2.0 / N))
    inv = jnp.concatenate([g_c, g_s], axis=1).astype(BF16)
    return fwd, inv


def _dft_fwd_kernel(f_ref, z_ref, o_ref):
    o_ref[0] = _dot(f_ref[...], z_ref[...].astype(BF16))


def _dft_fwd(fwd, z2d, L, nb, zmap):
    tm = min(2 * L, 1024)
    return pl.pallas_call(
        _dft_fwd_kernel,
        grid=(2 * L // tm, nb),
        in_specs=[pl.BlockSpec((tm, L), lambda i, b: (i, 0)),
                  pl.BlockSpec((L, BRANCH_W), lambda i, b: zmap(b))],
        out_specs=pl.BlockSpec((1, tm, BRANCH_W), lambda i, b: (b, i, 0)),
        out_shape=jax.ShapeDtypeStruct((nb, 2 * L, BRANCH_W), F32),
        compiler_params=_cp(("arbitrary", "arbitrary")),
        name="hy_dft_fwd",
    )(fwd, z2d)


def _spec_prod_kernel(z_ref, kf_ref, kb_ref, sf_ref, sb_ref, y_ref, *, tk):
    i = pl.program_id(1)
    s = lax.rsqrt(sf_ref[...] + sb_ref[...] + EPS)
    zr, zi = z_ref[0, 0], z_ref[0, 1]
    fr, fi = kf_ref[0, 0], kf_ref[0, 1]
    br, bi = kb_ref[0, 0], kb_ref[0, 1]
    row = lax.broadcasted_iota(jnp.int32, (tk, BRANCH_W), 0) + i * tk
    first = row == 0
    kr = (fr + br) * s
    ki = jnp.where(first, fi + bi, fi - bi) * s
    yr = jnp.where(first, zr * kr, zr * kr - zi * ki)
    yi = jnp.where(first, zi * ki, zr * ki + zi * kr)
    y_ref[0, 0] = yr.astype(BF16)
    y_ref[0, 1] = yi.astype(BF16)


def _spec_prod(zf, kfs, ss, order, L, nb):
    tk = min(L, 512)
    zf4 = zf.reshape(nb, 2, L, BRANCH_W)
    kf4 = kfs.reshape(4, 2, L, BRANCH_W)
    y = pl.pallas_call(
        functools.partial(_spec_prod_kernel, tk=tk),
        grid=(nb, L // tk),
        in_specs=[pl.BlockSpec((1, 2, tk, BRANCH_W), lambda b, i: (b, 0, i, 0)),
                  pl.BlockSpec((1, 2, tk, BRANCH_W), lambda b, i: (order, 0, i, 0)),
                  pl.BlockSpec((1, 2, tk, BRANCH_W), lambda b, i: (2 + order, 0, i, 0)),
                  pl.BlockSpec((1, BRANCH_W), lambda b, i: (0, order)),
                  pl.BlockSpec((1, BRANCH_W), lambda b, i: (0, 2 + order))],
        out_specs=pl.BlockSpec((1, 2, tk, BRANCH_W), lambda b, i: (b, 0, i, 0)),
        out_shape=jax.ShapeDtypeStruct((nb, 2, L, BRANCH_W), BF16),
        compiler_params=_cp(("arbitrary", "arbitrary")),
        name="hy_spec_prod",
    )(zf4, kf4, kf4, ss, ss)
    return y.reshape(nb, 2 * L, BRANCH_W)


def _dft_inv_kernel(g_ref, y_ref, gate_ref, zp_ref, bias_ref, *rest):
    o_ref = rest[-1]
    conv = _dot(g_ref[...], y_ref[0])
    o_ref[...] = (gate_ref[...] * (conv + bias_ref[...] * zp_ref[...])).astype(o_ref.dtype)


def _dft_inv(inv, y, xs, gate_col, zprev, zprev_col, bias, L, nb, out_dtype, out_rows=None, out_base=0,
             prev=None):
    tm = min(L, 512)
    nt = L // tm
    args = [inv, y, xs, zprev, bias]
    in_specs = [pl.BlockSpec((tm, 2 * L), lambda i, b: (i, 0)),
                pl.BlockSpec((1, 2 * L, BRANCH_W), lambda i, b: (b, 0, 0)),
                pl.BlockSpec((tm, BRANCH_W), lambda i, b: (b * nt + i, gate_col)),
                pl.BlockSpec((tm, BRANCH_W), lambda i, b: (b * nt + i, zprev_col)),
                pl.BlockSpec((1, BRANCH_W), lambda i, b: (0, 0))]
    args, in_specs, alias = _into(prev, args, in_specs)
    return pl.pallas_call(
        _dft_inv_kernel,
        grid=(nt, nb),
        in_specs=in_specs,
        out_specs=pl.BlockSpec((tm, BRANCH_W), lambda i, b: ((out_base + b) * nt + i, 0)),
        out_shape=jax.ShapeDtypeStruct((out_rows or nb * L, BRANCH_W), out_dtype),
        input_output_aliases=alias,
        compiler_params=_cp(("arbitrary", "arbitrary")),
        name="hy_dft_inv",
    )(*args)


def _hyena(u_small, L, nseq, row_base, p, mats, out_rows, prev=None):
    (short_w, short_b, w1, b1, w2, b2, w3, b3, w4, freq, bias) = p
    fwd, inv = mats
    xs = _short_conv(u_small, short_w, short_b, L, nseq, row_base)
    hf, ss = _hyena_filter(L, w1, b1, w2, b2, w3, b3, w4, freq)
    kfs = _dft_fwd(fwd, hf, L, 4, lambda b: (0, b))
    zf = _dft_fwd(fwd, xs, L, nseq, lambda b: (b, 2))
    y = _spec_prod(zf, kfs, ss, 0, L, nseq)
    z1 = _dft_inv(inv, y, xs, 0, xs, 2, bias[0:1], L, nseq, F32)
    zf = _dft_fwd(fwd, z1, L, nseq, lambda b: (b, 0))
    y = _spec_prod(zf, kfs, ss, 1, L, nseq)
    return _dft_inv(inv, y, xs, 1, z1, 0, bias[1:2], L, nseq, BF16, out_rows, row_base, prev)


def _conformer_kernel(a_ref, g_ref, w_ref, b_ref, lg_ref, lb_ref, *rest, L, tc):
    o_ref, pad_scr = rest[-2:]
    pad_scr[0:16, :] = jnp.zeros((16, BRANCH_W), F32)
    pad_scr[16 + L:32 + L, :] = jnp.zeros((16, BRANCH_W), F32)

    def glu(c, carry):
        t0 = pl.multiple_of(c * tc, tc)
        pad_scr[pl.ds(16 + t0, tc), :] = a_ref[pl.ds(t0, tc), :] * _sigmoid(g_ref[pl.ds(t0, tc), :])
        return carry

    lax.fori_loop(0, L // tc, glu, 0)
    b = b_ref[...]
    lg = lg_ref[...]
    lb = lb_ref[...]

    def body(c, carry):
        t0 = pl.multiple_of(c * tc, tc)
        win = pad_scr[pl.ds(t0, tc + 32), :]
        acc = jnp.zeros((tc, BRANCH_W), F32) + b
        for r in range(8):
            sh = win[r:r + tc + 24]
            for a in range(4):
                m = 8 * a + r
                if 1 <= m <= CF_WIDTH:
                    acc = acc + w_ref[m - 1:m, :] * sh[8 * a:8 * a + tc]
        mu = jnp.mean(acc, axis=-1, keepdims=True)
        xc = acc - mu
        var = jnp.mean(xc * xc, axis=-1, keepdims=True)
        y = xc * lax.rsqrt(var + EPS) * lg + lb
        o_ref[pl.ds(t0, tc), :] = (y * _sigmoid(y)).astype(o_ref.dtype)
        return carry

    lax.fori_loop(0, L // tc, body, 0)


def _conformer(u_small, L, nseq, row_base, p, out_rows, prev=None):
    dw_w, dw_b, ln_g, ln_b = p
    tc = 128
    wpad = jnp.pad(dw_w, ((0, 32 - CF_WIDTH), (0, 0)))
    row = lambda a: a.reshape(1, BRANCH_W)
    vec = pl.BlockSpec((1, BRANCH_W), lambda s: (0, 0))
    args = [u_small, u_small, wpad, row(dw_b), row(ln_g), row(ln_b)]
    in_specs = [pl.BlockSpec((L, BRANCH_W), lambda s: (row_base + s, 3)),
                pl.BlockSpec((L, BRANCH_W), lambda s: (row_base + s, 4)),
                pl.BlockSpec((32, BRANCH_W), lambda s: (0, 0)), vec, vec, vec]
    args, in_specs, alias = _into(prev, args, in_specs)
    return pl.pallas_call(
        functools.partial(_conformer_kernel, L=L, tc=tc),
        grid=(nseq,),
        in_specs=in_specs,
        out_specs=pl.BlockSpec((L, BRANCH_W), lambda s: (row_base + s, 0)),
        out_shape=jax.ShapeDtypeStruct((out_rows, BRANCH_W), BF16),
        input_output_aliases=alias,
        scratch_shapes=[pltpu.VMEM((L + 32, BRANCH_W), F32)],
        compiler_params=_cp(("arbitrary",)),
        name="conformer",
    )(*args)


def _rope_tables(S, pad_rows, head, scale_unused=None):
    half = head // 2
    nf = half // 2
    lane = np.arange(LANES)
    inv_lane = (ROPE_BASE ** (-(np.arange(nf)) / nf))[(lane % half) % nf]
    is_row = (lane % head) < half
    pos = jnp.arange(S, dtype=jnp.int32)
    rows = (pos // GRID_W).astype(F32)[:, None]
    cols = (pos % GRID_W).astype(F32)[:, None]
    ang = jnp.where(jnp.asarray(is_row)[None, :], rows, cols) * jnp.asarray(inv_lane, F32)[None, :]
    cos = jnp.concatenate([jnp.cos(ang), jnp.ones((pad_rows, LANES), F32)], axis=0)
    sin = jnp.concatenate([jnp.sin(ang), jnp.zeros((pad_rows, LANES), F32)], axis=0)
    return cos, sin


def _group_ones(width, group):
    idx = np.arange(width)
    return jnp.asarray((idx[:, None] // group) == (idx[None, :] // group), BF16)


def _head_norm_rope(x, ones, group, gain, cos, sin, nf, out_scale):
    w = x.shape[1]
    hi, lo = _split(x * x)
    ms = (_dot(hi, ones) + _dot(lo, ones)) * (1.0 / group)
    xn = x * lax.rsqrt(ms + EPS) * gain
    reps = w // LANES
    c = jnp.concatenate([cos] * reps, axis=1) if reps > 1 else cos
    s = jnp.concatenate([sin] * reps, axis=1) if reps > 1 else sin
    lane = lax.broadcasted_iota(jnp.int32, x.shape, 1)
    first = (lane % (2 * nf)) < nf
    rot = jnp.where(first, -pltpu.roll(xn, w - nf, 1), pltpu.roll(xn, nf, 1))
    return (xn * c + rot * s) * out_scale


def _qkv_kernel(gq_ref, gkv_ref, dq_ref, dk_ref, dv_ref, cg_ref, sg_ref, cd_ref, sd_ref,
                o64_ref, o32_ref, gqn_ref, gkn_ref, dqn_ref, dkn_ref,
                qg_ref, kg_ref, vag_ref, vbg_ref, qd_ref, kd_ref, vad_ref, vbd_ref):
    cg, sg, cd, sd = cg_ref[...], sg_ref[...], cd_ref[...], sd_ref[...]
    o64, o32 = o64_ref[...], o32_ref[...]
    q = _head_norm_rope(gq_ref[...], o64, HEAD_DIM, gqn_ref[...], cg, sg, HEAD_DIM // 4, LOG2E * HEAD_DIM ** -0.5)
    qg_ref[...] = q.astype(BF16)
    kv = gkv_ref[...]
    k = _head_norm_rope(kv[:, :LANES], o64[:LANES, :LANES], HEAD_DIM, gkn_ref[...], cg, sg, HEAD_DIM // 4, 1.0)
    v = kv[:, LANES:]
    kk = jnp.concatenate([k, k], axis=1)
    vv = jnp.concatenate([v, v], axis=1)
    quarter = lax.broadcasted_iota(jnp.int32, kk.shape, 1) // HEAD_DIM
    kr = pltpu.roll(kk, HEAD_DIM, 1)
    vr = pltpu.roll(vv, HEAD_DIM, 1)
    kg_ref[...] = jnp.where((quarter == 0) | (quarter == 3), kk, kr).astype(BF16)
    vag_ref[...] = jnp.where(quarter == 0, vv, jnp.where(quarter == 2, vr, 0.0)).astype(BF16)
    vbg_ref[...] = jnp.where(quarter == 1, vr, jnp.where(quarter == 3, vv, 0.0)).astype(BF16)
    qd = _head_norm_rope(dq_ref[...], o32, DIFF_D, dqn_ref[...], cd, sd, DIFF_D // 4, LOG2E * DIFF_D ** -0.5)
    qd_ref[...] = qd.astype(BF16)
    kd = _head_norm_rope(dk_ref[...], o32, DIFF_D, dkn_ref[...], cd, sd, DIFF_D // 4, 1.0)
    kd_ref[...] = kd.astype(BF16)
    vd = dv_ref[...]
    even = (lax.broadcasted_iota(jnp.int32, vd.shape, 1) // HEAD_DIM) % 2 == 0
    vad_ref[...] = jnp.where(even, vd, 0.0).astype(BF16)
    vbd_ref[...] = jnp.where(even, 0.0, vd).astype(BF16)


def _qkv_prep(u_small, gqn, gkn, dqn, dkn, tables, dims):
    n = u_small.shape[0]
    tm = dims["tm_small"]
    S = dims["S"]
    n_lat_tiles = dims["n_lat"] // tm
    per_seq = S // tm
    tmap = lambda i: (jnp.where(i < n_lat_tiles, i % per_seq, per_seq), 0)
    col = lambda c: pl.BlockSpec((tm, BRANCH_W), lambda i: (i, c))
    tab = pl.BlockSpec((tm, LANES), tmap)
    full = lambda a: pl.BlockSpec(a.shape, lambda i: (0,) * a.ndim)
    o64, o32 = _group_ones(BRANCH_W, HEAD_DIM), _group_ones(BRANCH_W, DIFF_D)
    gains = [jnp.tile(gqn, 4).reshape(1, 256), jnp.tile(gkn, 2).reshape(1, 128),
             jnp.tile(dqn.reshape(-1), 4).reshape(1, 256), jnp.tile(dkn.reshape(-1), 4).reshape(1, 256)]
    out = pl.BlockSpec((tm, BRANCH_W), lambda i: (i, 0))
    return pl.pallas_call(
        _qkv_kernel,
        grid=(n // tm,),
        in_specs=[col(5), col(6), col(7), col(8), col(9), tab, tab, tab, tab, full(o64), full(o32)]
        + [full(g) for g in gains],
        out_specs=[out] * 8,
        out_shape=[jax.ShapeDtypeStruct((n, BRANCH_W), BF16)] * 8,
        compiler_params=_cp(("arbitrary",)),
        name="qkv_prep",
    )(u_small, u_small, u_small, u_small, u_small, *tables, o64, o32, *gains)


def _lane_pick(lane_lo, a, b):
    return jnp.where(lane_lo, a, b)


def _flash_kernel(*refs, segs, tk, tq, diff, lam_init):
    q_ref, o_ref = refs[0], refs[-1]
    seg_refs = [refs[1 + 3 * i:4 + 3 * i] for i in range(len(segs))]
    aux_ref = refs[1 + 3 * len(segs)]
    lane = lax.broadcasted_iota(jnp.int32, (1, LANES), 1)
    lane_lo = lane < HEAD_DIM
    if diff:
        masks = [(lane >= g * DIFF_D) & (lane < (g + 1) * DIFF_D) for g in range(4)]
        acc_of = [0, 1, 0, 1]
        use_a = [True, True, False, False]
        n_acc = 2
    else:
        masks = [lane_lo, ~lane_lo]
        acc_of = [0, 0]
        use_a = [True, False]
        n_acc = 1
    R = len(masks)
    pairs = [slice(p * LANES, (p + 1) * LANES) for p in range(2)]
    qsts = [jnp.concatenate([jnp.where(m, q_ref[:, ps], jnp.zeros((tq, LANES), BF16)) for m in masks], axis=0)
            for ps in pairs]
    lo_hi = []
    for a in range(n_acc):
        rs = [r for r in range(R) if acc_of[r] == a]
        lo_hi.append(([r for r in rs if use_a[r]][0], [r for r in rs if not use_a[r]][0]))

    def rows(x, r):
        return x[r * tq:(r + 1) * tq]

    shift = jnp.max(aux_ref[5:6, :])

    def run(fixed):
        def chunk(kv_refs, t0, size, carry):
            k_ref, va_ref, vb_ref = kv_refs
            new = []
            for p, ps in enumerate(pairs):
                m_run, l_run, accs = carry[p]
                k = k_ref[pl.ds(t0, size), ps]
                s = lax.dot_general(qsts[p], k, (((1,), (1,)), ((), ())), preferred_element_type=F32)
                if fixed:
                    m_new = m_run
                    pr = jnp.exp2(s - shift)
                    l_new = l_run + jnp.sum(pr, axis=-1, keepdims=True)
                else:
                    m_new = jnp.maximum(m_run, jnp.max(s, axis=-1, keepdims=True))
                    alpha = jnp.exp2(m_run - m_new)
                    pr = jnp.exp2(s - m_new)
                    l_new = alpha * l_run + jnp.sum(pr, axis=-1, keepdims=True)
                prb = pr.astype(BF16)
                va = va_ref[pl.ds(t0, size), ps]
                vb = vb_ref[pl.ds(t0, size), ps]
                new_accs = []
                for a, (r_lo, r_hi) in enumerate(lo_hi):
                    upd = _dot(rows(prb, r_lo), va) + _dot(rows(prb, r_hi), vb)
                    if fixed:
                        new_accs.append(accs[a] + upd)
                    else:
                        al = _lane_pick(lane_lo, rows(alpha, r_lo), rows(alpha, r_hi))
                        new_accs.append(accs[a] * al + upd)
                new.append((m_new, l_new, tuple(new_accs)))
            return tuple(new)

        one = (jnp.full((R * tq, 1), -jnp.inf, F32), jnp.zeros((R * tq, 1), F32),
               tuple(jnp.zeros((tq, LANES), F32) for _ in range(n_acc)))
        carry = (one, one)
        for kv_refs, T in zip(seg_refs, segs):
            n_main = T // tk
            if n_main:
                carry = lax.fori_loop(
                    0, n_main, lambda c, cr, kv_refs=kv_refs: chunk(kv_refs, pl.multiple_of(c * tk, tk), tk, cr),
                    carry, unroll=2 if n_main % 2 == 0 else 1)
            if T - n_main * tk:
                carry = chunk(kv_refs, n_main * tk, T - n_main * tk, carry)

        outs = []
        for p in range(2):
            _, l_fin, accs = carry[p]
            inv_l = 1.0 / l_fin
            norm = [accs[a] * _lane_pick(lane_lo, rows(inv_l, r_lo), rows(inv_l, r_hi))
                    for a, (r_lo, r_hi) in enumerate(lo_hi)]
            if diff:
                aux = aux_ref[...]
                lam = (jnp.exp(jnp.sum(aux[0:1] * aux[1:2], axis=-1, keepdims=True))
                       - jnp.exp(jnp.sum(aux[2:3] * aux[3:4], axis=-1, keepdims=True)) + lam_init)
                o = norm[0] - lam * norm[1]
                sq = o * o
                s_lo = jnp.sum(jnp.where(lane_lo, sq, 0.0), axis=-1, keepdims=True)
                s_hi = jnp.sum(jnp.where(lane_lo, 0.0, sq), axis=-1, keepdims=True)
                ms = _lane_pick(lane_lo, s_lo, s_hi) * (1.0 / HEAD_DIM)
                o = o * lax.rsqrt(ms + EPS) * aux[4:5] * (1.0 - lam_init)
            else:
                o = norm[0]
            outs.append(o)
        o_ref[...] = jnp.concatenate(outs, axis=1).astype(o_ref.dtype)

    @pl.when(shift < FIXED_SHIFT_LIMIT)
    def _():
        run(True)

    @pl.when(jnp.logical_not(shift < FIXED_SHIFT_LIMIT))
    def _():
        run(False)


def _into(prev, args, in_specs):
    if prev is None:
        return args, in_specs, {}
    return args + [prev], in_specs + [pl.BlockSpec(memory_space=pl.ANY)], {len(args): 0}


def _flash(q, k, va, vb, aux, nb, Lq, q_base, segs, diff, lam_init, out_rows, out_base, prev=None):
    tq = 256 if diff else min(512, Lq)
    tk = 512
    nt = Lq // tq
    q_base, out_base = q_base // tq, out_base // tq
    args = [q]
    in_specs = [pl.BlockSpec((tq, BRANCH_W), lambda b, i: (q_base + b * nt + i, 0))]
    for length, base in segs:
        spec = pl.BlockSpec((length, BRANCH_W), lambda b, i, base=base: (base + b, 0))
        args += [k, va, vb]
        in_specs += [spec, spec, spec]
    args.append(aux)
    in_specs.append(pl.BlockSpec((8, LANES), lambda b, i: (0, 0)))
    args, in_specs, alias = _into(prev, args, in_specs)
    return pl.pallas_call(
        functools.partial(_flash_kernel, segs=tuple(s[0] for s in segs), tk=tk, tq=tq, diff=diff,
                          lam_init=lam_init),
        grid=(nb, nt),
        in_specs=in_specs,
        out_specs=pl.BlockSpec((tq, BRANCH_W), lambda b, i: (out_base + b * nt + i, 0)),
        out_shape=jax.ShapeDtypeStruct((out_rows, BRANCH_W), BF16),
        input_output_aliases=alias,
        compiler_params=_cp(("arbitrary", "arbitrary")),
        name="flash_diff" if diff else "flash_gqa",
    )(*args)


def _merge_kernel(y0, y1, y2, y3, g_ref, wb_ref, wo_ref, x_ref, mod_ref, o_ref):
    d = D_MODEL
    acc = g_ref[:, 0:d].astype(F32) * _dot(y0[...], wb_ref[0])
    acc = acc + g_ref[:, d:2 * d].astype(F32) * _dot(y1[...], wb_ref[1])
    acc = acc + g_ref[:, 2 * d:3 * d].astype(F32) * _dot(y2[...], wb_ref[2])
    acc = acc + g_ref[:, 3 * d:4 * d].astype(F32) * _dot(y3[...], wb_ref[3])
    mix = _dot(acc.astype(BF16), wo_ref[...])
    o_ref[...] = x_ref[...] + mod_ref[0][2:3, :] * mix


def _merge(ys, gate, wb, wo, x, mods_l, dims):
    n, d = ys[0].shape[0], x.shape[1]
    tm = dims["tm_small"]
    mrow = dims["mod_row"](tm)
    yspec = pl.BlockSpec((tm, BRANCH_W), lambda i: (i, 0))
    return pl.pallas_call(
        _merge_kernel,
        grid=(n // tm,),
        in_specs=[yspec] * 4 + [pl.BlockSpec((tm, GATE_COLS), lambda i: (i, 0)),
                                pl.BlockSpec(wb.shape, lambda i: (0, 0, 0)),
                                pl.BlockSpec(wo.shape, lambda i: (0, 0)),
                                pl.BlockSpec((tm, d), lambda i: (i, 0)),
                                pl.BlockSpec((1, 6, d), lambda i: (mrow(i), 0, 0))],
        out_specs=pl.BlockSpec((tm, d), lambda i: (i, 0)),
        out_shape=jax.ShapeDtypeStruct((n, d), F32),
        compiler_params=_cp(("arbitrary",)),
        name="merge",
    )(*ys, gate, wb, wo, x, mods_l)


def _ffn_kernel(x_ref, mod_ref, g_ref, w1_ref, w3_ref, w2_ref, o_ref, h_scr, acc_scr):
    j = pl.program_id(1)

    @pl.when(j == 0)
    def _():
        m = mod_ref[0]
        h_scr[...] = _norm_mod(x_ref[...], g_ref[...], m[3:4, :], m[4:5, :]).astype(BF16)
        acc_scr[...] = jnp.zeros_like(acc_scr)

    h = h_scr[...]
    a = _dot(h, w1_ref[...])
    b = _dot(h, w3_ref[...])
    t = (a * _sigmoid(a) * b).astype(BF16)
    acc_scr[...] += _dot(t, w2_ref[...])

    @pl.when(j == pl.num_programs(1) - 1)
    def _():
        o_ref[...] = x_ref[...] + mod_ref[0][5:6, :] * acc_scr[...]


def _ffn(x, mods_l, gain, w1, w3, w2, dims):
    n, d = x.shape
    tm, tf = dims["tm_big"], 256
    mrow = dims["mod_row"](tm)
    return pl.pallas_call(
        _ffn_kernel,
        grid=(n // tm, w1.shape[1] // tf),
        in_specs=[pl.BlockSpec((tm, d), lambda i, j: (i, 0)),
                  pl.BlockSpec((1, 6, d), lambda i, j: (mrow(i), 0, 0)),
                  pl.BlockSpec((1, d), lambda i, j: (0, 0)),
                  pl.BlockSpec((d, tf), lambda i, j: (0, j)),
                  pl.BlockSpec((d, tf), lambda i, j: (0, j)),
                  pl.BlockSpec((tf, d), lambda i, j: (j, 0))],
        out_specs=pl.BlockSpec((tm, d), lambda i, j: (i, 0)),
        out_shape=jax.ShapeDtypeStruct((n, d), F32),
        scratch_shapes=[pltpu.VMEM((tm, d), BF16), pltpu.VMEM((tm, d), F32)],
        compiler_params=_cp(("arbitrary", "arbitrary")),
        name="ffn_dense",
    )(x, mods_l, gain, w1, w3, w2)


def _route_kernel(x_ref, mod_ref, g_ref, wr_ref, tok_ref, gate_ref):
    m = mod_ref[0]
    h = _norm_mod(x_ref[...], g_ref[...], m[3:4, :], m[4:5, :])
    tok_ref[...] = h.astype(BF16)
    logits = _dot3(h, wr_ref[...])
    lane = lax.broadcasted_iota(jnp.int32, logits.shape, 1)
    lg = jnp.where(lane < N_EXPERTS, logits, -jnp.inf)
    m1 = jnp.max(lg, axis=-1, keepdims=True)
    i1 = jnp.min(jnp.where(lg == m1, lane, LANES), axis=-1, keepdims=True)
    lg2 = jnp.where(lane == i1, -jnp.inf, lg)
    m2 = jnp.max(lg2, axis=-1, keepdims=True)
    i2 = jnp.min(jnp.where(lg2 == m2, lane, LANES), axis=-1, keepdims=True)
    e2 = jnp.exp(m2 - m1)
    g1 = 1.0 / (1.0 + e2)
    gate_ref[...] = jnp.where(lane == i1, g1, jnp.where(lane == i2, e2 * g1, 0.0))


def _route(x, mods_l, gain, w_router, dims):
    n, d = x.shape
    tm = dims["tm_small"]
    mrow = dims["mod_row"](tm)
    wr = jnp.pad(w_router, ((0, 0), (0, LANES - N_EXPERTS)))
    return pl.pallas_call(
        _route_kernel,
        grid=(n // tm,),
        in_specs=[pl.BlockSpec((tm, d), lambda i: (i, 0)),
                  pl.BlockSpec((1, 6, d), lambda i: (mrow(i), 0, 0)),
                  pl.BlockSpec((1, d), lambda i: (0, 0)),
                  pl.BlockSpec((d, LANES), lambda i: (0, 0))],
        out_specs=[pl.BlockSpec((tm, d), lambda i: (i, 0)), pl.BlockSpec((tm, LANES), lambda i: (i, 0))],
        out_shape=[jax.ShapeDtypeStruct((n, d), BF16), jax.ShapeDtypeStruct((n, LANES), F32)],
        compiler_params=_cp(("arbitrary",)),
        name="moe_route",
    )(x, mods_l, gain, wr)


MOE_SUB = 128
MOE_GRAN = 64


def _moe_kernel(tok_ref, gate_ref, tri_ref, w1_ref, w3_ref, w2_ref, x_ref, mod_ref, o_ref,
                rank_scr, rank_t_scr, mask_t_scr, xe_scr, ye_scr, cnt_smem, *, tm):
    e = pl.program_id(1)
    j = pl.program_id(2)
    ne = pl.num_programs(1)
    nj = pl.num_programs(2)
    nsub = tm // MOE_SUB
    lane = lax.broadcasted_iota(jnp.int32, (1, LANES), 1)

    @pl.when((e == 0) & (j == 0))
    def _():
        tri = tri_ref[...]
        carry = jnp.zeros((1, LANES), F32)
        for blk in range(tm // 256):
            rs = slice(blk * 256, (blk + 1) * 256)
            msk = (gate_ref[rs, :] > 0.0).astype(BF16)
            rank_scr[rs, :] = _dot(tri, msk) + carry
            carry = carry + jnp.sum(msk.astype(F32), axis=0, keepdims=True)
        rank_t_scr[...] = jnp.transpose(rank_scr[...])
        mask_t_scr[...] = jnp.transpose((gate_ref[...] > 0.0).astype(F32))
        for ee in range(N_EXPERTS):
            cnt_smem[ee] = jnp.sum(jnp.where(lane == ee, carry, 0.0)).astype(jnp.int32)
        o_ref[...] = jnp.zeros_like(o_ref)
        ye_scr[...] = jnp.zeros_like(ye_scr)

    cnt = cnt_smem[e]

    n_sub = (cnt + (MOE_SUB - 1)) // MOE_SUB

    @pl.when(j == 0)
    def _():
        rk = rank_t_scr[pl.ds(e, 1), :]
        mk = mask_t_scr[pl.ds(e, 1), :]

        def gather(sb, carry):
            r0 = pl.multiple_of(sb * MOE_SUB, MOE_SUB)
            r_iota = lax.broadcasted_iota(jnp.int32, (MOE_SUB, tm), 0) + r0
            sel = jnp.where((rk == r_iota.astype(F32)) & (mk > 0.0), 1.0, 0.0).astype(BF16)
            xe_scr[pl.ds(r0, MOE_SUB), :] = _dot(sel, tok_ref[...]).astype(BF16)
            return carry

        lax.fori_loop(0, n_sub, gather, 0)

    def expert_rows(r0, size):
        xe = xe_scr[pl.ds(r0, size), :]
        a = _dot(xe, w1_ref[0])
        b = _dot(xe, w3_ref[0])
        t = (a * _sigmoid(a) * b).astype(BF16)
        y = _dot(t, w2_ref[0])
        ye_scr[pl.ds(r0, size), :] = jnp.where(j == 0, y, ye_scr[pl.ds(r0, size), :] + y)

    n_gran = (cnt + (MOE_GRAN - 1)) // MOE_GRAN
    n_big = n_gran // 4
    rem = n_gran % 4

    def big(i, carry):
        expert_rows(pl.multiple_of(i * 256, 256), 256)
        return carry

    lax.fori_loop(0, n_big, big, 0)

    @pl.when(rem >= 2)
    def _():
        expert_rows(pl.multiple_of(n_big * 256, 128), 128)

    @pl.when(rem % 2 == 1)
    def _():
        expert_rows(pl.multiple_of(n_big * 256 + (rem // 2) * 128, 64), 64)

    @pl.when(j == nj - 1)
    def _():
        def scatter(rc, carry):
            r0 = pl.multiple_of(rc * 256, 256)
            gt = gate_ref[pl.ds(r0, 256), :]
            g_e = jnp.sum(jnp.where(lane == e, gt, 0.0), axis=-1, keepdims=True)
            r_e = jnp.sum(jnp.where(lane == e, rank_scr[pl.ds(r0, 256), :], 0.0), axis=-1, keepdims=True)
            lo = jnp.min(jnp.where(g_e > 0.0, r_e, 1e9)).astype(jnp.int32)
            hi = jnp.max(jnp.where(g_e > 0.0, r_e, -1.0)).astype(jnp.int32)

            def scatter_sub(sb, c2):
                c0 = pl.multiple_of(sb * MOE_SUB, MOE_SUB)
                c_iota = lax.broadcasted_iota(jnp.int32, (256, MOE_SUB), 1) + c0
                sel_t = jnp.where((r_e == c_iota.astype(F32)) & (g_e > 0.0), 1.0, 0.0).astype(BF16)
                ye = ye_scr[pl.ds(c0, MOE_SUB), :].astype(BF16)
                o_ref[pl.ds(r0, 256), :] += g_e * _dot(sel_t, ye)
                return c2

            lax.fori_loop(lo // MOE_SUB, (hi + MOE_SUB) // MOE_SUB, scatter_sub, 0)

            @pl.when(e == ne - 1)
            def _():
                o_ref[pl.ds(r0, 256), :] = (x_ref[pl.ds(r0, 256), :]
                                            + mod_ref[0][5:6, :] * o_ref[pl.ds(r0, 256), :])
            return carry

        lax.fori_loop(0, tm // 256, scatter, 0)


def _moe(tok, gates, w1, w3, w2, x, mods_l, dims):
    n, d = tok.shape
    tm = dims["tm_big"]
    mrow = dims["mod_row"](tm)
    ne, _, ff = w1.shape
    tf = 896
    tri = jnp.asarray(np.tril(np.ones((256, 256)), -1), BF16)
    return pl.pallas_call(
        functools.partial(_moe_kernel, tm=tm),
        grid=(n // tm, ne, ff // tf),
        in_specs=[pl.BlockSpec((tm, d), lambda i, e, j: (i, 0)),
                  pl.BlockSpec((tm, LANES), lambda i, e, j: (i, 0)),
                  pl.BlockSpec((256, 256), lambda i, e, j: (0, 0)),
                  pl.BlockSpec((1, d, tf), lambda i, e, j: (e, 0, j)),
                  pl.BlockSpec((1, d, tf), lambda i, e, j: (e, 0, j)),
                  pl.BlockSpec((1, tf, d), lambda i, e, j: (e, j, 0)),
                  pl.BlockSpec((tm, d), lambda i, e, j: (i, 0)),
                  pl.BlockSpec((1, 6, d), lambda i, e, j: (mrow(i), 0, 0))],
        out_specs=pl.BlockSpec((tm, d), lambda i, e, j: (i, 0)),
        out_shape=jax.ShapeDtypeStruct((n, d), F32),
        scratch_shapes=[pltpu.VMEM((tm, LANES), F32), pltpu.VMEM((LANES, tm), F32), pltpu.VMEM((LANES, tm), F32),
                        pltpu.VMEM((tm, d), BF16), pltpu.VMEM((tm, d), F32), pltpu.SMEM((N_EXPERTS,), jnp.int32)],
        compiler_params=_cp(("arbitrary", "arbitrary", "arbitrary")),
        name="moe_experts",
    )(tok, gates, tri, w1, w3, w2, x, mods_l)


def _make_dims(B, S, n_ctx):
    n_lat, n = B * S, B * (S + n_ctx)

    def pick(prefs):
        for t in prefs:
            if S % t == 0 and n_lat % t == 0 and (n - n_lat) % t == 0:
                return t
        raise ValueError("no row tile divides the latent and context token counts")

    def mod_row(tm):
        return lambda i: jnp.minimum((i * tm) // S, B)

    return {"B": B, "S": S, "n_ctx": n_ctx, "n_lat": n_lat, "n": n,
            "tm_big": pick((1024, 512, 256)), "tm_small": pick((512, 256)), "mod_row": mod_row}


def kernel(x, c, ctx, c_ctx, w_ada, b_ada, norm_mix, norm_ffn, w_in, hy_short_w, hy_short_b, hy_w1, hy_b1, hy_w2, hy_b2, hy_w3, hy_b3, hy_w4, hy_freq, hy_bias, cf_dw_w, cf_dw_b, cf_ln_g, cf_ln_b, gqa_qn, gqa_kn, diff_qn, diff_kn, diff_lq1, diff_lk1, diff_lq2, diff_lk2, diff_subln, w_branch, w_out, ffn_w1, ffn_w3, ffn_w2, moe_router, moe_w1, moe_w3, moe_w2):
    B, S, D = x.shape
    n_ctx = ctx.shape[1]
    depth = w_ada.shape[0]
    dims = _make_dims(B, S, n_ctx)
    n_lat = dims["n_lat"]
    tm_s = dims["tm_small"]

    c16 = jnp.concatenate([c, c_ctx[None, :], jnp.zeros((16 - B - 1, D), F32)], axis=0)
    mods = _ada_mods(c16, w_ada, b_ada).reshape(depth, 16, 6, D)

    xa = jnp.concatenate([x.reshape(n_lat, D), ctx.reshape(B * n_ctx, D)], axis=0)
    tables = _rope_tables(S, tm_s, HEAD_DIM) + _rope_tables(S, tm_s, DIFF_D)
    mats_lat = _dft_mats(S)
    mats_ctx = _dft_mats(n_ctx)

    for l in range(depth):
        lam_init = 0.8 - 0.6 * math.exp(-0.3 * l)
        u_small, gate = _in_proj(xa, mods[l], norm_mix[l].reshape(1, D), w_in[l].astype(BF16), dims)

        hy_p = (hy_short_w[l], hy_short_b[l], hy_w1[l], hy_b1[l], hy_w2[l], hy_b2[l], hy_w3[l], hy_b3[l],
                hy_w4[l], hy_freq[l], hy_bias[l])
        cf_p = (cf_dw_w[l], cf_dw_b[l], cf_ln_g[l], cf_ln_b[l])
        last = l == depth - 1
        n_out = n_lat if last else dims["n"]
        ctx_base = n_lat // n_ctx
        y_hy = _hyena(u_small, S, B, 0, hy_p, mats_lat, n_out)
        y_cf = _conformer(u_small, S, B, 0, cf_p, n_out)
        if not last:
            y_hy = _hyena(u_small, n_ctx, B, ctx_base, hy_p, mats_ctx, n_out, y_hy)
            y_cf = _conformer(u_small, n_ctx, B, ctx_base, cf_p, n_out, y_cf)

        qg, kg, vag, vbg, qd, kd, vad, vbd = _qkv_prep(u_small, gqa_qn[l], gqa_kn[l], diff_qn[l], diff_kn[l],
                                                      tables, dims)

        aux = jnp.zeros((8, LANES), F32)
        aux = aux.at[0, :DIFF_D].set(diff_lq1[l]).at[1, :DIFF_D].set(diff_lk1[l])
        aux = aux.at[2, :DIFF_D].set(diff_lq2[l]).at[3, :DIFF_D].set(diff_lk2[l])
        aux = aux.at[4, :].set(jnp.tile(diff_subln[l], 2))

        def score_bound(qn, kn, d):
            return 1.02 * d * jnp.max(jnp.abs(qn)) * jnp.max(jnp.abs(kn)) * (LOG2E * d ** -0.5) + 0.1

        bounds = (score_bound(gqa_qn[l], gqa_kn[l], HEAD_DIM), score_bound(diff_qn[l], diff_kn[l], DIFF_D))
        y_att = []
        for (q, k, va, vb, is_diff) in ((qg, kg, vag, vbg, False), (qd, kd, vad, vbd, True)):
            aux = aux.at[5, :].set(bounds[int(is_diff)])
            y = _flash(q, k, va, vb, aux, B, S, 0, ((S, 0), (n_ctx, ctx_base)), is_diff, lam_init, n_out, 0)
            if not last:
                y = _flash(q, k, va, vb, aux, B, n_ctx, n_lat, ((n_ctx, ctx_base),), is_diff, lam_init,
                           n_out, n_lat, y)
            y_att.append(y)

        xa = _merge((y_hy, y_cf, y_att[0], y_att[1]), gate, w_branch[l].astype(BF16), w_out[l].astype(BF16),
                    xa, mods[l], dims)

        i = l // 2
        if l % 2 == 0:
            xa = _ffn(xa, mods[l], norm_ffn[l].reshape(1, D), ffn_w1[i].astype(BF16), ffn_w3[i].astype(BF16),
                      ffn_w2[i].astype(BF16), dims)
        else:
            tok, gates = _route(xa, mods[l], norm_ffn[l].reshape(1, D), moe_router[i], dims)
            xa = _moe(tok, gates, moe_w1[i].astype(BF16), moe_w3[i].astype(BF16), moe_w2[i].astype(BF16),
                      xa, mods[l], dims)
    return xa[:n_lat].reshape(B, S, D)
```

```python
import functools
import math

import numpy as np
import jax
import jax.numpy as jnp
from jax import lax
from jax.experimental import pallas as pl
from jax.experimental.pallas import tpu as pltpu

F32 = jnp.float32
BF16 = jnp.bfloat16

D_MODEL = 1024
GRID_W = 64
BRANCH_W = 256
EPS = 1e-6
HY_EMB = 33
HY_EMB_PAD = 128
HY_FFN = 64
HY_TARGET = 1e-2
HY_FAST = 0.3
HY_SLOW = 1.5
CF_WIDTH = 31
HEAD_DIM = 64
DIFF_D = 32
ROPE_BASE = 10000.0
N_EXPERTS = 8
LANES = 128
LOG2E = 1.4426950408889634
FIXED_SHIFT_LIMIT = 60.0
SMALL_COLS = 2560
GATE_COLS = 4 * D_MODEL
VMEM_LIMIT = 48 * 1024 * 1024
VMEM_LIMIT_MOE = 56 * 1024 * 1024


def _cp(sem, vmem=VMEM_LIMIT):
    return pltpu.CompilerParams(dimension_semantics=sem, vmem_limit_bytes=vmem)


def _sigmoid(x):
    return 1.0 / (1.0 + jnp.exp(-x))


def _dot(a, b):
    return jnp.dot(a, b, preferred_element_type=F32)


def _split(a):
    hi = a.astype(BF16)
    lo = (a - hi.astype(F32)).astype(BF16)
    return hi, lo


def _dot3(a, b):
    ah, al = _split(a)
    bh, bl = _split(b)
    return _dot(ah, bh) + (_dot(al, bh) + _dot(ah, bl))


def _norm_mod(x, gain, shift, scale):
    ms = jnp.mean(x * x, axis=-1, keepdims=True)
    return (x * lax.rsqrt(ms + EPS) * gain) * (1.0 + scale) + shift


def _ada_kernel(c_ref, w_ref, b_ref, o_ref):
    c = c_ref[...]
    s = c * _sigmoid(c)
    o_ref[0] = _dot3(s, w_ref[0]) + b_ref[0]


def _ada_mods(c16, w_ada, b_ada):
    depth, d, n6 = w_ada.shape
    tn = 512
    return pl.pallas_call(
        _ada_kernel,
        grid=(depth, n6 // tn),
        in_specs=[pl.BlockSpec((16, d), lambda l, j: (0, 0)),
                  pl.BlockSpec((1, d, tn), lambda l, j: (l, 0, j)),
                  pl.BlockSpec((1, 1, tn), lambda l, j: (l, 0, j))],
        out_specs=pl.BlockSpec((1, 16, tn), lambda l, j: (l, 0, j)),
        out_shape=jax.ShapeDtypeStruct((depth, 16, n6), F32),
        compiler_params=_cp(("arbitrary", "arbitrary")),
        name="ada_mods",
    )(c16, w_ada, b_ada.reshape(depth, 1, n6))


def _in_proj_kernel(x_ref, mod_ref, g_ref, w_ref, o_ref, h_scr, *, gate):
    @pl.when(pl.program_id(1) == 0)
    def _():
        m = mod_ref[0]
        h_scr[...] = _norm_mod(x_ref[...], g_ref[...], m[0:1, :], m[1:2, :]).astype(BF16)

    r = _dot(h_scr[...], w_ref[...])
    o_ref[...] = _sigmoid(r).astype(o_ref.dtype) if gate else r


def _in_proj(x, mods_l, gain, w_bf, dims):
    n, d = x.shape
    tm, tn = dims["tm_big"], 512
    mrow = dims["mod_row"](tm)

    def call(col0, n_cols, gate, dtype):
        return pl.pallas_call(
            functools.partial(_in_proj_kernel, gate=gate),
            grid=(n // tm, n_cols // tn),
            in_specs=[pl.BlockSpec((tm, d), lambda i, j: (i, 0)),
                      pl.BlockSpec((1, 6, d), lambda i, j: (mrow(i), 0, 0)),
                      pl.BlockSpec((1, d), lambda i, j: (0, 0)),
                      pl.BlockSpec((d, tn), lambda i, j: (0, col0 // tn + j))],
            out_specs=pl.BlockSpec((tm, tn), lambda i, j: (i, j)),
            out_shape=jax.ShapeDtypeStruct((n, n_cols), dtype),
            scratch_shapes=[pltpu.VMEM((tm, d), BF16)],
            compiler_params=_cp(("arbitrary", "arbitrary")),
            name="in_proj_gate" if gate else "in_proj_mix",
        )(x, mods_l, gain, w_bf)

    return call(0, SMALL_COLS, False, F32), call(SMALL_COLS, GATE_COLS, True, BF16)


def _short_conv_kernel(u_ref, w_ref, b_ref, o_ref, pad_scr, *, L, tc):
    pad_scr[0:8, :] = jnp.zeros((8, BRANCH_W), F32)
    pad_scr[8 + L:16 + L, :] = jnp.zeros((8, BRANCH_W), F32)
    pad_scr[8:8 + L, :] = u_ref[...]
    w = w_ref[0]
    b = b_ref[0]

    def body(c, carry):
        t0 = pl.multiple_of(c * tc, tc)
        win = pad_scr[pl.ds(t0, tc + 16), :]
        acc = b + w[0:1, :] * win[7:7 + tc]
        acc = acc + w[1:2, :] * win[8:8 + tc]
        acc = acc + w[2:3, :] * win[9:9 + tc]
        o_ref[pl.ds(t0, tc), :] = acc
        return carry

    lax.fori_loop(0, L // tc, body, 0)


def _short_conv(u_small, w, b, L, nseq, row_base):
    w3 = jnp.transpose(w.reshape(3, 3, BRANCH_W), (1, 0, 2))
    w3 = jnp.pad(w3, ((0, 0), (0, 5), (0, 0)))
    b3 = b.reshape(3, 1, BRANCH_W)
    tc = min(L, 256)
    return pl.pallas_call(
        functools.partial(_short_conv_kernel, L=L, tc=tc),
        grid=(nseq, 3),
        in_specs=[pl.BlockSpec((L, BRANCH_W), lambda s, j: (row_base + s, j)),
                  pl.BlockSpec((1, 8, BRANCH_W), lambda s, j: (j, 0, 0)),
                  pl.BlockSpec((1, 1, BRANCH_W), lambda s, j: (j, 0, 0))],
        out_specs=pl.BlockSpec((L, BRANCH_W), lambda s, j: (s, j)),
        out_shape=jax.ShapeDtypeStruct((nseq * L, 3 * BRANCH_W), F32),
        scratch_shapes=[pltpu.VMEM((L + 16, BRANCH_W), F32)],
        compiler_params=_cp(("arbitrary", "arbitrary")),
        name="hy_short_conv",
    )(u_small, w3, b3)


def _filter_consts(L):
    t = np.linspace(0.0, 1.0, L)[:, None]
    bands = (HY_EMB - 1) // 2
    fr = np.linspace(1e-4, bands - 1, bands)[None, :]
    wpos = 2.0 * math.pi * np.arange(L)[:, None] / L
    z = np.concatenate([t, np.cos(fr * wpos), -np.sin(fr * wpos)], axis=-1)
    z = np.pad(z, ((0, 0), (0, HY_EMB_PAD - HY_EMB)))
    deltas = np.abs(np.linspace(math.log(HY_TARGET) / HY_SLOW, math.log(HY_TARGET) / HY_FAST, BRANCH_W))
    win = np.exp(-t * deltas[None, :])
    return jnp.asarray(z, F32), jnp.asarray(win, F32)


def _filter_kernel(z_ref, w1, b1, w2, b2, w3, b3, w4, fq, win_ref, hf_ref, ss_ref, *, tr):
    i = pl.program_id(0)
    f = fq[...]
    a = jnp.sin(f * (_dot3(z_ref[...], w1[...]) + b1[...]))
    a = jnp.sin(f * (_dot3(a, w2[...]) + b2[...]))
    a = jnp.sin(f * (_dot3(a, w3[...]) + b3[...]))
    h = _dot3(a, w4[...])
    win = win_ref[...]
    h = h * jnp.concatenate([win, win, win, win], axis=1)
    row = lax.broadcasted_iota(jnp.int32, h.shape, 0) + i * tr
    col = lax.broadcasted_iota(jnp.int32, h.shape, 1)
    h = jnp.where((row == 0) & (col >= 2 * BRANCH_W), 0.0, h)
    hf_ref[...] = h

    @pl.when(i == 0)
    def _():
        ss_ref[...] = jnp.zeros_like(ss_ref)

    ss_ref[...] += jnp.sum(h * h, axis=0, keepdims=True)


def _hyena_filter(L, w1, b1, w2, b2, w3, b3, w4, freq):
    z, win = _filter_consts(L)
    tr = min(L, 256)
    w1p = jnp.pad(w1, ((0, HY_EMB_PAD - HY_EMB), (0, 0)))
    full = lambda a: pl.BlockSpec(a.shape, lambda i: (0,) * a.ndim)
    args = [w1p, b1.reshape(1, -1), w2, b2.reshape(1, -1), w3, b3.reshape(1, -1), w4, freq.reshape(1, -1)]
    n_out = w4.shape[1]
    return pl.pallas_call(
        functools.partial(_filter_kernel, tr=tr),
        grid=(L // tr,),
        in_specs=[pl.BlockSpec((tr, HY_EMB_PAD), lambda i: (i, 0))] + [full(a) for a in args]
        + [pl.BlockSpec((tr, BRANCH_W), lambda i: (i, 0))],
        out_specs=[pl.BlockSpec((tr, n_out), lambda i: (i, 0)),
                   pl.BlockSpec((1, n_out), lambda i: (0, 0))],
        out_shape=[jax.ShapeDtypeStruct((L, n_out), F32), jax.ShapeDtypeStruct((1, n_out), F32)],
        compiler_params=_cp(("arbitrary",)),
        name="hy_filter",
    )(z, *args, win)


def _dft_mats(L):
    N = 2 * L
    blk = 64
    k = jnp.arange(L, dtype=jnp.int32)[:, None]
    nh = jnp.arange(L // blk, dtype=jnp.int32)[None, :]
    nl = jnp.arange(blk, dtype=jnp.int32)[None, :]
    w = 2.0 * math.pi / N
    a = ((k * (blk * nh)) % N).astype(F32) * w
    b = ((k * nl) % N).astype(F32) * w
    ca, sa, cb, sb = jnp.cos(a), jnp.sin(a), jnp.cos(b), jnp.sin(b)
    cos = (ca[:, :, None] * cb[:, None, :] - sa[:, :, None] * sb[:, None, :]).reshape(L, L)
    sin = (sa[:, :, None] * cb[:, None, :] + ca[:, :, None] * sb[:, None, :]).reshape(L, L)
    alt = jnp.where(jnp.arange(L) % 2 == 0, 1.0, -1.0).astype(F32)
    first = (jnp.arange(L) == 0)
    s_f = jnp.where(first[:, None], alt[None, :], -sin)
    fwd = jnp.concatenate([cos, s_f], axis=0).astype(BF16)
    colscale = jnp.where(first, 1.0 / N, 2.0 / N).astype(F32)
    g_c = cos * colscale[None, :]
    g_s = jnp.where(first[None, :], alt[:, None] / N, -sin * (2.0 / N))
    inv = jnp.concatenate([g_c, g_s], axis=1).astype(BF16)
    return fwd, inv


def _dft_fwd_kernel(f_ref, z_ref, o_ref):
    o_ref[0] = _dot(f_ref[...], z_ref[...].astype(BF16))


def _dft_fwd(fwd, z2d, L, nb, zmap):
    tm = min(2 * L, 1024)
    return pl.pallas_call(
        _dft_fwd_kernel,
        grid=(2 * L // tm, nb),
        in_specs=[pl.BlockSpec((tm, L), lambda i, b: (i, 0)),
                  pl.BlockSpec((L, BRANCH_W), lambda i, b: zmap(b))],
        out_specs=pl.BlockSpec((1, tm, BRANCH_W), lambda i, b: (b, i, 0)),
        out_shape=jax.ShapeDtypeStruct((nb, 2 * L, BRANCH_W), F32),
        compiler_params=_cp(("arbitrary", "arbitrary")),
        name="hy_dft_fwd",
    )(fwd, z2d)


def _spec_prod_kernel(z_ref, kf_ref, kb_ref, sf_ref, sb_ref, y_ref, *, tk):
    i = pl.program_id(1)
    s = lax.rsqrt(sf_ref[...] + sb_ref[...] + EPS)
    zr, zi = z_ref[0, 0], z_ref[0, 1]
    fr, fi = kf_ref[0, 0], kf_ref[0, 1]
    br, bi = kb_ref[0, 0], kb_ref[0, 1]
    row = lax.broadcasted_iota(jnp.int32, (tk, BRANCH_W), 0) + i * tk
    first = row == 0
    kr = (fr + br) * s
    ki = jnp.where(first, fi + bi, fi - bi) * s
    yr = jnp.where(first, zr * kr, zr * kr - zi * ki)
    yi = jnp.where(first, zi * ki, zr * ki + zi * kr)
    y_ref[0, 0] = yr.astype(BF16)
    y_ref[0, 1] = yi.astype(BF16)


def _spec_prod(zf, kfs, ss, order, L, nb):
    tk = min(L, 512)
    zf4 = zf.reshape(nb, 2, L, BRANCH_W)
    kf4 = kfs.reshape(4, 2, L, BRANCH_W)
    y = pl.pallas_call(
        functools.partial(_spec_prod_kernel, tk=tk),
        grid=(nb, L // tk),
        in_specs=[pl.BlockSpec((1, 2, tk, BRANCH_W), lambda b, i: (b, 0, i, 0)),
                  pl.BlockSpec((1, 2, tk, BRANCH_W), lambda b, i: (order, 0, i, 0)),
                  pl.BlockSpec((1, 2, tk, BRANCH_W), lambda b, i: (2 + order, 0, i, 0)),
                  pl.BlockSpec((1, BRANCH_W), lambda b, i: (0, order)),
                  pl.BlockSpec((1, BRANCH_W), lambda b, i: (0, 2 + order))],
        out_specs=pl.BlockSpec((1, 2, tk, BRANCH_W), lambda b, i: (b, 0, i, 0)),
        out_shape=jax.ShapeDtypeStruct((nb, 2, L, BRANCH_W), BF16),
        compiler_params=_cp(("arbitrary", "arbitrary")),
        name="hy_spec_prod",
    )(zf4, kf4, kf4, ss, ss)
    return y.reshape(nb, 2 * L, BRANCH_W)


def _dft_inv_kernel(g_ref, y_ref, gate_ref, zp_ref, bias_ref, *rest):
    o_ref = rest[-1]
    conv = _dot(g_ref[...], y_ref[0])
    o_ref[...] = (gate_ref[...] * (conv + bias_ref[...] * zp_ref[...])).astype(o_ref.dtype)


def _dft_inv(inv, y, xs, gate_col, zprev, zprev_col, bias, L, nb, out_dtype, out_rows=None, out_base=0,
             prev=None):
    tm = min(L, 512)
    nt = L // tm
    args = [inv, y, xs, zprev, bias]
    in_specs = [pl.BlockSpec((tm, 2 * L), lambda i, b: (i, 0)),
                pl.BlockSpec((1, 2 * L, BRANCH_W), lambda i, b: (b, 0, 0)),
                pl.BlockSpec((tm, BRANCH_W), lambda i, b: (b * nt + i, gate_col)),
                pl.BlockSpec((tm, BRANCH_W), lambda i, b: (b * nt + i, zprev_col)),
                pl.BlockSpec((1, BRANCH_W), lambda i, b: (0, 0))]
    args, in_specs, alias = _into(prev, args, in_specs)
    return pl.pallas_call(
        _dft_inv_kernel,
        grid=(nt, nb),
        in_specs=in_specs,
        out_specs=pl.BlockSpec((tm, BRANCH_W), lambda i, b: ((out_base + b) * nt + i, 0)),
        out_shape=jax.ShapeDtypeStruct((out_rows or nb * L, BRANCH_W), out_dtype),
        input_output_aliases=alias,
        compiler_params=_cp(("arbitrary", "arbitrary")),
        name="hy_dft_inv",
    )(*args)


def _hyena(u_small, L, nseq, row_base, p, mats, out_rows, prev=None):
    (short_w, short_b, w1, b1, w2, b2, w3, b3, w4, freq, bias) = p
    fwd, inv = mats
    xs = _short_conv(u_small, short_w, short_b, L, nseq, row_base)
    hf, ss = _hyena_filter(L, w1, b1, w2, b2, w3, b3, w4, freq)
    kfs = _dft_fwd(fwd, hf, L, 4, lambda b: (0, b))
    zf = _dft_fwd(fwd, xs, L, nseq, lambda b: (b, 2))
    y = _spec_prod(zf, kfs, ss, 0, L, nseq)
    z1 = _dft_inv(inv, y, xs, 0, xs, 2, bias[0:1], L, nseq, F32)
    zf = _dft_fwd(fwd, z1, L, nseq, lambda b: (b, 0))
    y = _spec_prod(zf, kfs, ss, 1, L, nseq)
    return _dft_inv(inv, y, xs, 1, z1, 0, bias[1:2], L, nseq, BF16, out_rows, row_base, prev)


def _conformer_kernel(a_ref, g_ref, w_ref, b_ref, lg_ref, lb_ref, *rest, L, tc):
    o_ref, pad_scr = rest[-2:]
    pad_scr[0:16, :] = jnp.zeros((16, BRANCH_W), F32)
    pad_scr[16 + L:32 + L, :] = jnp.zeros((16, BRANCH_W), F32)

    def glu(c, carry):
        t0 = pl.multiple_of(c * tc, tc)
        pad_scr[pl.ds(16 + t0, tc), :] = a_ref[pl.ds(t0, tc), :] * _sigmoid(g_ref[pl.ds(t0, tc), :])
        return carry

    lax.fori_loop(0, L // tc, glu, 0)
    b = b_ref[...]
    lg = lg_ref[...]
    lb = lb_ref[...]

    def body(c, carry):
        t0 = pl.multiple_of(c * tc, tc)
        win = pad_scr[pl.ds(t0, tc + 32), :]
        acc = jnp.zeros((tc, BRANCH_W), F32) + b
        for r in range(8):
            sh = win[r:r + tc + 24]
            for a in range(4):
                m = 8 * a + r
                if 1 <= m <= CF_WIDTH:
                    acc = acc + w_ref[m - 1:m, :] * sh[8 * a:8 * a + tc]
        mu = jnp.mean(acc, axis=-1, keepdims=True)
        xc = acc - mu
        var = jnp.mean(xc * xc, axis=-1, keepdims=True)
        y = xc * lax.rsqrt(var + EPS) * lg + lb
        o_ref[pl.ds(t0, tc), :] = (y * _sigmoid(y)).astype(o_ref.dtype)
        return carry

    lax.fori_loop(0, L // tc, body, 0)


def _conformer(u_small, L, nseq, row_base, p, out_rows, prev=None):
    dw_w, dw_b, ln_g, ln_b = p
    tc = 128
    wpad = jnp.pad(dw_w, ((0, 32 - CF_WIDTH), (0, 0)))
    row = lambda a: a.reshape(1, BRANCH_W)
    vec = pl.BlockSpec((1, BRANCH_W), lambda s: (0, 0))
    args = [u_small, u_small, wpad, row(dw_b), row(ln_g), row(ln_b)]
    in_specs = [pl.BlockSpec((L, BRANCH_W), lambda s: (row_base + s, 3)),
                pl.BlockSpec((L, BRANCH_W), lambda s: (row_base + s, 4)),
                pl.BlockSpec((32, BRANCH_W), lambda s: (0, 0)), vec, vec, vec]
    args, in_specs, alias = _into(prev, args, in_specs)
    return pl.pallas_call(
        functools.partial(_conformer_kernel, L=L, tc=tc),
        grid=(nseq,),
        in_specs=in_specs,
        out_specs=pl.BlockSpec((L, BRANCH_W), lambda s: (row_base + s, 0)),
        out_shape=jax.ShapeDtypeStruct((out_rows, BRANCH_W), BF16),
        input_output_aliases=alias,
        scratch_shapes=[pltpu.VMEM((L + 32, BRANCH_W), F32)],
        compiler_params=_cp(("arbitrary",)),
        name="conformer",
    )(*args)


def _rope_tables(S, pad_rows, head, scale_unused=None):
    half = head // 2
    nf = half // 2
    lane = np.arange(LANES)
    inv_lane = (ROPE_BASE ** (-(np.arange(nf)) / nf))[(lane % half) % nf]
    is_row = (lane % head) < half
    pos = jnp.arange(S, dtype=jnp.int32)
    rows = (pos // GRID_W).astype(F32)[:, None]
    cols = (pos % GRID_W).astype(F32)[:, None]
    ang = jnp.where(jnp.asarray(is_row)[None, :], rows, cols) * jnp.asarray(inv_lane, F32)[None, :]
    cos = jnp.concatenate([jnp.cos(ang), jnp.ones((pad_rows, LANES), F32)], axis=0)
    sin = jnp.concatenate([jnp.sin(ang), jnp.zeros((pad_rows, LANES), F32)], axis=0)
    return cos, sin


def _group_ones(width, group):
    idx = np.arange(width)
    return jnp.asarray((idx[:, None] // group) == (idx[None, :] // group), BF16)


def _head_norm_rope(x, ones, group, gain, cos, sin, nf, out_scale):
    w = x.shape[1]
    hi, lo = _split(x * x)
    ms = (_dot(hi, ones) + _dot(lo, ones)) * (1.0 / group)
    xn = x * lax.rsqrt(ms + EPS) * gain
    reps = w // LANES
    c = jnp.concatenate([cos] * reps, axis=1) if reps > 1 else cos
    s = jnp.concatenate([sin] * reps, axis=1) if reps > 1 else sin
    lane = lax.broadcasted_iota(jnp.int32, x.shape, 1)
    first = (lane % (2 * nf)) < nf
    rot = jnp.where(first, -pltpu.roll(xn, w - nf, 1), pltpu.roll(xn, nf, 1))
    return (xn * c + rot * s) * out_scale


def _qkv_kernel(gq_ref, gkv_ref, dq_ref, dk_ref, dv_ref, cg_ref, sg_ref, cd_ref, sd_ref,
                o64_ref, o32_ref, gqn_ref, gkn_ref, dqn_ref, dkn_ref,
                qg_ref, kg_ref, vag_ref, vbg_ref, qd_ref, kd_ref, vad_ref, vbd_ref):
    cg, sg, cd, sd = cg_ref[...], sg_ref[...], cd_ref[...], sd_ref[...]
    o64, o32 = o64_ref[...], o32_ref[...]
    q = _head_norm_rope(gq_ref[...], o64, HEAD_DIM, gqn_ref[...], cg, sg, HEAD_DIM // 4, LOG2E * HEAD_DIM ** -0.5)
    qg_ref[...] = q.astype(BF16)
    kv = gkv_ref[...]
    k = _head_norm_rope(kv[:, :LANES], o64[:LANES, :LANES], HEAD_DIM, gkn_ref[...], cg, sg, HEAD_DIM // 4, 1.0)
    v = kv[:, LANES:]
    kk = jnp.concatenate([k, k], axis=1)
    vv = jnp.concatenate([v, v], axis=1)
    quarter = lax.broadcasted_iota(jnp.int32, kk.shape, 1) // HEAD_DIM
    kr = pltpu.roll(kk, HEAD_DIM, 1)
    vr = pltpu.roll(vv, HEAD_DIM, 1)
    kg_ref[...] = jnp.where((quarter == 0) | (quarter == 3), kk, kr).astype(BF16)
    vag_ref[...] = jnp.where(quarter == 0, vv, jnp.where(quarter == 2, vr, 0.0)).astype(BF16)
    vbg_ref[...] = jnp.where(quarter == 1, vr, jnp.where(quarter == 3, vv, 0.0)).astype(BF16)
    qd = _head_norm_rope(dq_ref[...], o32, DIFF_D, dqn_ref[...], cd, sd, DIFF_D // 4, LOG2E * DIFF_D ** -0.5)
    qd_ref[...] = qd.astype(BF16)
    kd = _head_norm_rope(dk_ref[...], o32, DIFF_D, dkn_ref[...], cd, sd, DIFF_D // 4, 1.0)
    kd_ref[...] = kd.astype(BF16)
    vd = dv_ref[...]
    even = (lax.broadcasted_iota(jnp.int32, vd.shape, 1) // HEAD_DIM) % 2 == 0
    vad_ref[...] = jnp.where(even, vd, 0.0).astype(BF16)
    vbd_ref[...] = jnp.where(even, 0.0, vd).astype(BF16)


def _qkv_prep(u_small, gqn, gkn, dqn, dkn, tables, dims):
    n = u_small.shape[0]
    tm = dims["tm_small"]
    S = dims["S"]
    n_lat_tiles = dims["n_lat"] // tm
    per_seq = S // tm
    tmap = lambda i: (jnp.where(i < n_lat_tiles, i % per_seq, per_seq), 0)
    col = lambda c: pl.BlockSpec((tm, BRANCH_W), lambda i: (i, c))
    tab = pl.BlockSpec((tm, LANES), tmap)
    full = lambda a: pl.BlockSpec(a.shape, lambda i: (0,) * a.ndim)
    o64, o32 = _group_ones(BRANCH_W, HEAD_DIM), _group_ones(BRANCH_W, DIFF_D)
    gains = [jnp.tile(gqn, 4).reshape(1, 256), jnp.tile(gkn, 2).reshape(1, 128),
             jnp.tile(dqn.reshape(-1), 4).reshape(1, 256), jnp.tile(dkn.reshape(-1), 4).reshape(1, 256)]
    out = pl.BlockSpec((tm, BRANCH_W), lambda i: (i, 0))
    return pl.pallas_call(
        _qkv_kernel,
        grid=(n // tm,),
        in_specs=[col(5), col(6), col(7), col(8), col(9), tab, tab, tab, tab, full(o64), full(o32)]
        + [full(g) for g in gains],
        out_specs=[out] * 8,
        out_shape=[jax.ShapeDtypeStruct((n, BRANCH_W), BF16)] * 8,
        compiler_params=_cp(("arbitrary",)),
        name="qkv_prep",
    )(u_small, u_small, u_small, u_small, u_small, *tables, o64, o32, *gains)


def _lane_pick(lane_lo, a, b):
    return jnp.where(lane_lo, a, b)


def _flash_kernel(*refs, segs, tk, tq, diff, lam_init):
    q_ref, o_ref = refs[0], refs[-1]
    seg_refs = [refs[1 + 3 * i:4 + 3 * i] for i in range(len(segs))]
    aux_ref = refs[1 + 3 * len(segs)]
    lane = lax.broadcasted_iota(jnp.int32, (1, LANES), 1)
    lane_lo = lane < HEAD_DIM
    if diff:
        masks = [(lane >= g * DIFF_D) & (lane < (g + 1) * DIFF_D) for g in range(4)]
        acc_of = [0, 1, 0, 1]
        use_a = [True, True, False, False]
        n_acc = 2
    else:
        masks = [lane_lo, ~lane_lo]
        acc_of = [0, 0]
        use_a = [True, False]
        n_acc = 1
    R = len(masks)
    pairs = [slice(p * LANES, (p + 1) * LANES) for p in range(2)]
    qsts = [jnp.concatenate([jnp.where(m, q_ref[:, ps], jnp.zeros((tq, LANES), BF16)) for m in masks], axis=0)
            for ps in pairs]
    lo_hi = []
    for a in range(n_acc):
        rs = [r for r in range(R) if acc_of[r] == a]
        lo_hi.append(([r for r in rs if use_a[r]][0], [r for r in rs if not use_a[r]][0]))

    def rows(x, r):
        return x[r * tq:(r + 1) * tq]

    shift = jnp.max(aux_ref[5:6, :])

    def run(fixed):
        def chunk(kv_refs, t0, size, carry):
            k_ref, va_ref, vb_ref = kv_refs
            new = []
            for p, ps in enumerate(pairs):
                m_run, l_run, accs = carry[p]
                k = k_ref[pl.ds(t0, size), ps]
                s = lax.dot_general(qsts[p], k, (((1,), (1,)), ((), ())), preferred_element_type=F32)
                if fixed:
                    m_new = m_run
                    pr = jnp.exp2(s - shift)
                    l_new = l_run + jnp.sum(pr, axis=-1, keepdims=True)
                else:
                    m_new = jnp.maximum(m_run, jnp.max(s, axis=-1, keepdims=True))
                    alpha = jnp.exp2(m_run - m_new)
                    pr = jnp.exp2(s - m_new)
                    l_new = alpha * l_run + jnp.sum(pr, axis=-1, keepdims=True)
                prb = pr.astype(BF16)
                va = va_ref[pl.ds(t0, size), ps]
                vb = vb_ref[pl.ds(t0, size), ps]
                new_accs = []
                for a, (r_lo, r_hi) in enumerate(lo_hi):
                    upd = _dot(rows(prb, r_lo), va) + _dot(rows(prb, r_hi), vb)
                    if fixed:
                        new_accs.append(accs[a] + upd)
                    else:
                        al = _lane_pick(lane_lo, rows(alpha, r_lo), rows(alpha, r_hi))
                        new_accs.append(accs[a] * al + upd)
                new.append((m_new, l_new, tuple(new_accs)))
            return tuple(new)

        one = (jnp.full((R * tq, 1), -jnp.inf, F32), jnp.zeros((R * tq, 1), F32),
               tuple(jnp.zeros((tq, LANES), F32) for _ in range(n_acc)))
        carry = (one, one)
        for kv_refs, T in zip(seg_refs, segs):
            n_main = T // tk
            if n_main:
                carry = lax.fori_loop(
                    0, n_main, lambda c, cr, kv_refs=kv_refs: chunk(kv_refs, pl.multiple_of(c * tk, tk), tk, cr),
                    carry, unroll=2 if n_main % 2 == 0 else 1)
            if T - n_main * tk:
                carry = chunk(kv_refs, n_main * tk, T - n_main * tk, carry)

        outs = []
        for p in range(2):
            _, l_fin, accs = carry[p]
            inv_l = 1.0 / l_fin
            norm = [accs[a] * _lane_pick(lane_lo, rows(inv_l, r_lo), rows(inv_l, r_hi))
                    for a, (r_lo, r_hi) in enumerate(lo_hi)]
            if diff:
                aux = aux_ref[...]
                lam = (jnp.exp(jnp.sum(aux[0:1] * aux[1:2], axis=-1, keepdims=True))
                       - jnp.exp(jnp.sum(aux[2:3] * aux[3:4], axis=-1, keepdims=True)) + lam_init)
                o = norm[0] - lam * norm[1]
                sq = o * o
                s_lo = jnp.sum(jnp.where(lane_lo, sq, 0.0), axis=-1, keepdims=True)
                s_hi = jnp.sum(jnp.where(lane_lo, 0.0, sq), axis=-1, keepdims=True)
                ms = _lane_pick(lane_lo, s_lo, s_hi) * (1.0 / HEAD_DIM)
                o = o * lax.rsqrt(ms + EPS) * aux[4:5] * (1.0 - lam_init)
            else:
                o = norm[0]
            outs.append(o)
        o_ref[...] = jnp.concatenate(outs, axis=1).astype(o_ref.dtype)

    @pl.when(shift < FIXED_SHIFT_LIMIT)
    def _():
        run(True)

    @pl.when(jnp.logical_not(shift < FIXED_SHIFT_LIMIT))
    def _():
        run(False)


def _into(prev, args, in_specs):
    if prev is None:
        return args, in_specs, {}
    return args + [prev], in_specs + [pl.BlockSpec(memory_space=pl.ANY)], {len(args): 0}


def _flash(q, k, va, vb, aux, nb, Lq, q_base, segs, diff, lam_init, out_rows, out_base, prev=None):
    tq = 256 if diff else min(512, Lq)
    tk = 512
    nt = Lq // tq
    q_base, out_base = q_base // tq, out_base // tq
    args = [q]
    in_specs = [pl.BlockSpec((tq, BRANCH_W), lambda b, i: (q_base + b * nt + i, 0))]
    for length, base in segs:
        spec = pl.BlockSpec((length, BRANCH_W), lambda b, i, base=base: (base + b, 0))
        args += [k, va, vb]
        in_specs += [spec, spec, spec]
    args.append(aux)
    in_specs.append(pl.BlockSpec((8, LANES), lambda b, i: (0, 0)))
    args, in_specs, alias = _into(prev, args, in_specs)
    return pl.pallas_call(
        functools.partial(_flash_kernel, segs=tuple(s[0] for s in segs), tk=tk, tq=tq, diff=diff,
                          lam_init=lam_init),
        grid=(nb, nt),
        in_specs=in_specs,
        out_specs=pl.BlockSpec((tq, BRANCH_W), lambda b, i: (out_base + b * nt + i, 0)),
        out_shape=jax.ShapeDtypeStruct((out_rows, BRANCH_W), BF16),
        input_output_aliases=alias,
        compiler_params=_cp(("arbitrary", "arbitrary")),
        name="flash_diff" if diff else "flash_gqa",
    )(*args)


def _merge_kernel(y0, y1, y2, y3, g_ref, wb_ref, wo_ref, x_ref, mod_ref, o_ref):
    d = D_MODEL
    acc = g_ref[:, 0:d].astype(F32) * _dot(y0[...], wb_ref[0])
    acc = acc + g_ref[:, d:2 * d].astype(F32) * _dot(y1[...], wb_ref[1])
    acc = acc + g_ref[:, 2 * d:3 * d].astype(F32) * _dot(y2[...], wb_ref[2])
    acc = acc + g_ref[:, 3 * d:4 * d].astype(F32) * _dot(y3[...], wb_ref[3])
    mix = _dot(acc.astype(BF16), wo_ref[...])
    o_ref[...] = x_ref[...] + mod_ref[0][2:3, :] * mix


def _merge(ys, gate, wb, wo, x, mods_l, dims):
    n, d = ys[0].shape[0], x.shape[1]
    tm = dims["tm_small"]
    mrow = dims["mod_row"](tm)
    yspec = pl.BlockSpec((tm, BRANCH_W), lambda i: (i, 0))
    return pl.pallas_call(
        _merge_kernel,
        grid=(n // tm,),
        in_specs=[yspec] * 4 + [pl.BlockSpec((tm, GATE_COLS), lambda i: (i, 0)),
                                pl.BlockSpec(wb.shape, lambda i: (0, 0, 0)),
                                pl.BlockSpec(wo.shape, lambda i: (0, 0)),
                                pl.BlockSpec((tm, d), lambda i: (i, 0)),
                                pl.BlockSpec((1, 6, d), lambda i: (mrow(i), 0, 0))],
        out_specs=pl.BlockSpec((tm, d), lambda i: (i, 0)),
        out_shape=jax.ShapeDtypeStruct((n, d), F32),
        compiler_params=_cp(("arbitrary",)),
        name="merge",
    )(*ys, gate, wb, wo, x, mods_l)


def _ffn_kernel(x_ref, mod_ref, g_ref, w1_ref, w3_ref, w2_ref, o_ref, h_scr, acc_scr):
    j = pl.program_id(1)

    @pl.when(j == 0)
    def _():
        m = mod_ref[0]
        h_scr[...] = _norm_mod(x_ref[...], g_ref[...], m[3:4, :], m[4:5, :]).astype(BF16)
        acc_scr[...] = jnp.zeros_like(acc_scr)

    h = h_scr[...]
    a = _dot(h, w1_ref[...])
    b = _dot(h, w3_ref[...])
    t = (a * _sigmoid(a) * b).astype(BF16)
    acc_scr[...] += _dot(t, w2_ref[...])

    @pl.when(j == pl.num_programs(1) - 1)
    def _():
        o_ref[...] = x_ref[...] + mod_ref[0][5:6, :] * acc_scr[...]


def _ffn(x, mods_l, gain, w1, w3, w2, dims):
    n, d = x.shape
    tm, tf = dims["tm_big"], 256
    mrow = dims["mod_row"](tm)
    return pl.pallas_call(
        _ffn_kernel,
        grid=(n // tm, w1.shape[1] // tf),
        in_specs=[pl.BlockSpec((tm, d), lambda i, j: (i, 0)),
                  pl.BlockSpec((1, 6, d), lambda i, j: (mrow(i), 0, 0)),
                  pl.BlockSpec((1, d), lambda i, j: (0, 0)),
                  pl.BlockSpec((d, tf), lambda i, j: (0, j)),
                  pl.BlockSpec((d, tf), lambda i, j: (0, j)),
                  pl.BlockSpec((tf, d), lambda i, j: (j, 0))],
        out_specs=pl.BlockSpec((tm, d), lambda i, j: (i, 0)),
        out_shape=jax.ShapeDtypeStruct((n, d), F32),
        scratch_shapes=[pltpu.VMEM((tm, d), BF16), pltpu.VMEM((tm, d), F32)],
        compiler_params=_cp(("arbitrary", "arbitrary")),
        name="ffn_dense",
    )(x, mods_l, gain, w1, w3, w2)


def _route_kernel(x_ref, mod_ref, g_ref, wr_ref, tok_ref, gate_ref):
    m = mod_ref[0]
    h = _norm_mod(x_ref[...], g_ref[...], m[3:4, :], m[4:5, :])
    tok_ref[...] = h.astype(BF16)
    logits = _dot3(h, wr_ref[...])
    lane = lax.broadcasted_iota(jnp.int32, logits.shape, 1)
    lg = jnp.where(lane < N_EXPERTS, logits, -jnp.inf)
    m1 = jnp.max(lg, axis=-1, keepdims=True)
    i1 = jnp.min(jnp.where(lg == m1, lane, LANES), axis=-1, keepdims=True)
    lg2 = jnp.where(lane == i1, -jnp.inf, lg)
    m2 = jnp.max(lg2, axis=-1, keepdims=True)
    i2 = jnp.min(jnp.where(lg2 == m2, lane, LANES), axis=-1, keepdims=True)
    e2 = jnp.exp(m2 - m1)
    g1 = 1.0 / (1.0 + e2)
    gate_ref[...] = jnp.where(lane == i1, g1, jnp.where(lane == i2, e2 * g1, 0.0))


def _route(x, mods_l, gain, w_router, dims):
    n, d = x.shape
    tm = dims["tm_small"]
    mrow = dims["mod_row"](tm)
    wr = jnp.pad(w_router, ((0, 0), (0, LANES - N_EXPERTS)))
    return pl.pallas_call(
        _route_kernel,
        grid=(n // tm,),
        in_specs=[pl.BlockSpec((tm, d), lambda i: (i, 0)),
                  pl.BlockSpec((1, 6, d), lambda i: (mrow(i), 0, 0)),
                  pl.BlockSpec((1, d), lambda i: (0, 0)),
                  pl.BlockSpec((d, LANES), lambda i: (0, 0))],
        out_specs=[pl.BlockSpec((tm, d), lambda i: (i, 0)), pl.BlockSpec((tm, LANES), lambda i: (i, 0))],
        out_shape=[jax.ShapeDtypeStruct((n, d), BF16), jax.ShapeDtypeStruct((n, LANES), F32)],
        compiler_params=_cp(("arbitrary",)),
        name="moe_route",
    )(x, mods_l, gain, wr)


MOE_SUB = 128
MOE_GRAN = 64


def _moe_kernel(tok_ref, gate_ref, tri_ref, w1_ref, w3_ref, w2_ref, x_ref, mod_ref, o_ref,
                rank_scr, rank_t_scr, mask_t_scr, xe_scr, ye_scr, cnt_smem, *, tm):
    e = pl.program_id(1)
    j = pl.program_id(2)
    ne = pl.num_programs(1)
    nj = pl.num_programs(2)
    nsub = tm // MOE_SUB
    lane = lax.broadcasted_iota(jnp.int32, (1, LANES), 1)

    @pl.when((e == 0) & (j == 0))
    def _():
        tri = tri_ref[...]
        carry = jnp.zeros((1, LANES), F32)
        for blk in range(tm // 256):
            rs = slice(blk * 256, (blk + 1) * 256)
            msk = (gate_ref[rs, :] > 0.0).astype(BF16)
            rank_scr[rs, :] = _dot(tri, msk) + carry
            carry = carry + jnp.sum(msk.astype(F32), axis=0, keepdims=True)
        rank_t_scr[...] = jnp.transpose(rank_scr[...])
        mask_t_scr[...] = jnp.transpose((gate_ref[...] > 0.0).astype(F32))
        for ee in range(N_EXPERTS):
            cnt_smem[ee] = jnp.sum(jnp.where(lane == ee, carry, 0.0)).astype(jnp.int32)
        o_ref[...] = jnp.zeros_like(o_ref)
        ye_scr[...] = jnp.zeros_like(ye_scr)

    cnt = cnt_smem[e]

    n_sub = (cnt + (MOE_SUB - 1)) // MOE_SUB

    @pl.when(j == 0)
    def _():
        rk = rank_t_scr[pl.ds(e, 1), :]
        mk = mask_t_scr[pl.ds(e, 1), :]

        def gather(sb, carry):
            r0 = pl.multiple_of(sb * MOE_SUB, MOE_SUB)
            r_iota = lax.broadcasted_iota(jnp.int32, (MOE_SUB, tm), 0) + r0
            sel = jnp.where((rk == r_iota.astype(F32)) & (mk > 0.0), 1.0, 0.0).astype(BF16)
            xe_scr[pl.ds(r0, MOE_SUB), :] = _dot(sel, tok_ref[...]).astype(BF16)
            return carry

        lax.fori_loop(0, n_sub, gather, 0)

    def expert_rows(r0, size):
        xe = xe_scr[pl.ds(r0, size), :]
        a = _dot(xe, w1_ref[0])
        b = _dot(xe, w3_ref[0])
        t = (a * _sigmoid(a) * b).astype(BF16)
        y = _dot(t, w2_ref[0])
        ye_scr[pl.ds(r0, size), :] = jnp.where(j == 0, y, ye_scr[pl.ds(r0, size), :] + y)

    n_gran = (cnt + (MOE_GRAN - 1)) // MOE_GRAN
    n_big = n_gran // 4
    rem = n_gran % 4

    def big(i, carry):
        expert_rows(pl.multiple_of(i * 256, 256), 256)
        return carry

    lax.fori_loop(0, n_big, big, 0)

    @pl.when(rem >= 2)
    def _():
        expert_rows(pl.multiple_of(n_big * 256, 128), 128)

    @pl.when(rem % 2 == 1)
    def _():
        expert_rows(pl.multiple_of(n_big * 256 + (rem // 2) * 128, 64), 64)

    @pl.when(j == nj - 1)
    def _():
        def scatter(rc, carry):
            r0 = pl.multiple_of(rc * 256, 256)
            gt = gate_ref[pl.ds(r0, 256), :]
            g_e = jnp.sum(jnp.where(lane == e, gt, 0.0), axis=-1, keepdims=True)
            r_e = jnp.sum(jnp.where(lane == e, rank_scr[pl.ds(r0, 256), :], 0.0), axis=-1, keepdims=True)
            lo = jnp.min(jnp.where(g_e > 0.0, r_e, 1e9)).astype(jnp.int32)
            hi = jnp.max(jnp.where(g_e > 0.0, r_e, -1.0)).astype(jnp.int32)

            def scatter_sub(sb, c2):
                c0 = pl.multiple_of(sb * MOE_SUB, MOE_SUB)
                c_iota = lax.broadcasted_iota(jnp.int32, (256, MOE_SUB), 1) + c0
                sel_t = jnp.where((r_e == c_iota.astype(F32)) & (g_e > 0.0), 1.0, 0.0).astype(BF16)
                ye = ye_scr[pl.ds(c0, MOE_SUB), :].astype(BF16)
                o_ref[pl.ds(r0, 256), :] += g_e * _dot(sel_t, ye)
                return c2

            lax.fori_loop(lo // MOE_SUB, (hi + MOE_SUB) // MOE_SUB, scatter_sub, 0)

            @pl.when(e == ne - 1)
            def _():
                o_ref[pl.ds(r0, 256), :] = (x_ref[pl.ds(r0, 256), :]
                                            + mod_ref[0][5:6, :] * o_ref[pl.ds(r0, 256), :])
            return carry

        lax.fori_loop(0, tm // 256, scatter, 0)


def _moe(tok, gates, w1, w3, w2, x, mods_l, dims):
    n, d = tok.shape
    tm = dims["tm_moe"] if n % dims["tm_moe"] == 0 else dims["tm_big"]
    mrow = dims["mod_row"](tm)
    ne, _, ff = w1.shape
    tf = 512
    tri = jnp.asarray(np.tril(np.ones((256, 256)), -1), BF16)
    once = pl.Buffered(1)
    return pl.pallas_call(
        functools.partial(_moe_kernel, tm=tm),
        grid=(n // tm, ne, ff // tf),
        in_specs=[pl.BlockSpec((tm, d), lambda i, e, j: (i, 0), pipeline_mode=once),
                  pl.BlockSpec((tm, LANES), lambda i, e, j: (i, 0), pipeline_mode=once),
                  pl.BlockSpec((256, 256), lambda i, e, j: (0, 0), pipeline_mode=once),
                  pl.BlockSpec((1, d, tf), lambda i, e, j: (e, 0, j)),
                  pl.BlockSpec((1, d, tf), lambda i, e, j: (e, 0, j)),
                  pl.BlockSpec((1, tf, d), lambda i, e, j: (e, j, 0)),
                  pl.BlockSpec((tm, d), lambda i, e, j: (i, 0), pipeline_mode=once),
                  pl.BlockSpec((1, 6, d), lambda i, e, j: (mrow(i), 0, 0))],
        out_specs=pl.BlockSpec((tm, d), lambda i, e, j: (i, 0), pipeline_mode=once),
        out_shape=jax.ShapeDtypeStruct((n, d), F32),
        scratch_shapes=[pltpu.VMEM((tm, LANES), F32), pltpu.VMEM((LANES, tm), F32), pltpu.VMEM((LANES, tm), F32),
                        pltpu.VMEM((tm, d), BF16), pltpu.VMEM((tm, d), F32), pltpu.SMEM((N_EXPERTS,), jnp.int32)],
        compiler_params=_cp(("arbitrary", "arbitrary", "arbitrary"), VMEM_LIMIT_MOE),
        name="moe_experts",
    )(tok, gates, tri, w1, w3, w2, x, mods_l)


def _make_dims(B, S, n_ctx):
    n_lat, n = B * S, B * (S + n_ctx)

    def pick(prefs):
        for t in prefs:
            if S % t == 0 and n_lat % t == 0 and (n - n_lat) % t == 0:
                return t
        raise ValueError("no row tile divides the latent and context token counts")

    def mod_row(tm):
        return lambda i: jnp.minimum((i * tm) // S, B)

    return {"B": B, "S": S, "n_ctx": n_ctx, "n_lat": n_lat, "n": n,
            "tm_big": pick((1024, 512, 256)), "tm_small": pick((512, 256)), "tm_moe": pick((2048, 1024, 512, 256)),
            "mod_row": mod_row}


def kernel(x, c, ctx, c_ctx, w_ada, b_ada, norm_mix, norm_ffn, w_in, hy_short_w, hy_short_b, hy_w1, hy_b1, hy_w2, hy_b2, hy_w3, hy_b3, hy_w4, hy_freq, hy_bias, cf_dw_w, cf_dw_b, cf_ln_g, cf_ln_b, gqa_qn, gqa_kn, diff_qn, diff_kn, diff_lq1, diff_lk1, diff_lq2, diff_lk2, diff_subln, w_branch, w_out, ffn_w1, ffn_w3, ffn_w2, moe_router, moe_w1, moe_w3, moe_w2):
    B, S, D = x.shape
    n_ctx = ctx.shape[1]
    depth = w_ada.shape[0]
    dims = _make_dims(B, S, n_ctx)
    n_lat = dims["n_lat"]
    tm_s = dims["tm_small"]

    c16 = jnp.concatenate([c, c_ctx[None, :], jnp.zeros((16 - B - 1, D), F32)], axis=0)
    mods = _ada_mods(c16, w_ada, b_ada).reshape(depth, 16, 6, D)

    xa = jnp.concatenate([x.reshape(n_lat, D), ctx.reshape(B * n_ctx, D)], axis=0)
    tables = _rope_tables(S, tm_s, HEAD_DIM) + _rope_tables(S, tm_s, DIFF_D)
    mats_lat = _dft_mats(S)
    mats_ctx = _dft_mats(n_ctx)

    for l in range(depth):
        lam_init = 0.8 - 0.6 * math.exp(-0.3 * l)
        u_small, gate = _in_proj(xa, mods[l], norm_mix[l].reshape(1, D), w_in[l].astype(BF16), dims)

        hy_p = (hy_short_w[l], hy_short_b[l], hy_w1[l], hy_b1[l], hy_w2[l], hy_b2[l], hy_w3[l], hy_b3[l],
                hy_w4[l], hy_freq[l], hy_bias[l])
        cf_p = (cf_dw_w[l], cf_dw_b[l], cf_ln_g[l], cf_ln_b[l])
        last = l == depth - 1
        n_out = n_lat if last else dims["n"]
        ctx_base = n_lat // n_ctx
        y_hy = _hyena(u_small, S, B, 0, hy_p, mats_lat, n_out)
        y_cf = _conformer(u_small, S, B, 0, cf_p, n_out)
        if not last:
            y_hy = _hyena(u_small, n_ctx, B, ctx_base, hy_p, mats_ctx, n_out, y_hy)
            y_cf = _conformer(u_small, n_ctx, B, ctx_base, cf_p, n_out, y_cf)

        qg, kg, vag, vbg, qd, kd, vad, vbd = _qkv_prep(u_small, gqa_qn[l], gqa_kn[l], diff_qn[l], diff_kn[l],
                                                      tables, dims)

        aux = jnp.zeros((8, LANES), F32)
        aux = aux.at[0, :DIFF_D].set(diff_lq1[l]).at[1, :DIFF_D].set(diff_lk1[l])
        aux = aux.at[2, :DIFF_D].set(diff_lq2[l]).at[3, :DIFF_D].set(diff_lk2[l])
        aux = aux.at[4, :].set(jnp.tile(diff_subln[l], 2))

        def score_bound(qn, kn, d):
            return 1.02 * d * jnp.max(jnp.abs(qn)) * jnp.max(jnp.abs(kn)) * (LOG2E * d ** -0.5) + 0.1

        bounds = (score_bound(gqa_qn[l], gqa_kn[l], HEAD_DIM), score_bound(diff_qn[l], diff_kn[l], DIFF_D))
        y_att = []
        for (q, k, va, vb, is_diff) in ((qg, kg, vag, vbg, False), (qd, kd, vad, vbd, True)):
            aux = aux.at[5, :].set(bounds[int(is_diff)])
            y = _flash(q, k, va, vb, aux, B, S, 0, ((S, 0), (n_ctx, ctx_base)), is_diff, lam_init, n_out, 0)
            if not last:
                y = _flash(q, k, va, vb, aux, B, n_ctx, n_lat, ((n_ctx, ctx_base),), is_diff, lam_init,
                           n_out, n_lat, y)
            y_att.append(y)

        xa = _merge((y_hy, y_cf, y_att[0], y_att[1]), gate, w_branch[l].astype(BF16), w_out[l].astype(BF16),
                    xa, mods[l], dims)

        i = l // 2
        if l % 2 == 0:
            xa = _ffn(xa, mods[l], norm_ffn[l].reshape(1, D), ffn_w1[i].astype(BF16), ffn_w3[i].astype(BF16),
                      ffn_w2[i].astype(BF16), dims)
        else:
            tok, gates = _route(xa, mods[l], norm_ffn[l].reshape(1, D), moe_router[i], dims)
            xa = _moe(tok, gates, moe_w1[i].astype(BF16), moe_w3[i].astype(BF16), moe_w2[i].astype(BF16),
                      xa, mods[l], dims)
    return xa[:n_lat].reshape(B, S, D)
```

```python
import functools
import math

import numpy as np
import jax
import jax.numpy as jnp
from jax import lax
from jax.experimental import pallas as pl
from jax.experimental.pallas import tpu as pltpu

F32 = jnp.float32
BF16 = jnp.bfloat16

D_MODEL = 1024
GRID_W = 64
BRANCH_W = 256
EPS = 1e-6
HY_EMB = 33
HY_EMB_PAD = 128
HY_FFN = 64
HY_TARGET = 1e-2
HY_FAST = 0.3
HY_SLOW = 1.5
CF_WIDTH = 31
HEAD_DIM = 64
DIFF_D = 32
ROPE_BASE = 10000.0
N_EXPERTS = 8
LANES = 128
LOG2E = 1.4426950408889634
FIXED_SHIFT_LIMIT = 60.0
SMALL_COLS = 2560
GATE_COLS = 4 * D_MODEL
VMEM_LIMIT = 48 * 1024 * 1024
VMEM_LIMIT_MOE = 56 * 1024 * 1024


def _cp(sem, vmem=VMEM_LIMIT):
    return pltpu.CompilerParams(dimension_semantics=sem, vmem_limit_bytes=vmem)


def _sigmoid(x):
    return 1.0 / (1.0 + jnp.exp(-x))


def _dot(a, b):
    return jnp.dot(a, b, preferred_element_type=F32)


def _split(a):
    hi = a.astype(BF16)
    lo = (a - hi.astype(F32)).astype(BF16)
    return hi, lo


def _dot3(a, b):
    ah, al = _split(a)
    bh, bl = _split(b)
    return _dot(ah, bh) + (_dot(al, bh) + _dot(ah, bl))


def _norm_mod(x, gain, shift, scale):
    ms = jnp.mean(x * x, axis=-1, keepdims=True)
    return (x * lax.rsqrt(ms + EPS) * gain) * (1.0 + scale) + shift


def _ada_kernel(c_ref, w_ref, b_ref, o_ref):
    c = c_ref[...]
    s = c * _sigmoid(c)
    o_ref[0] = _dot3(s, w_ref[0]) + b_ref[0]


def _ada_mods(c16, w_ada, b_ada):
    depth, d, n6 = w_ada.shape
    tn = 512
    return pl.pallas_call(
        _ada_kernel,
        grid=(depth, n6 // tn),
        in_specs=[pl.BlockSpec((16, d), lambda l, j: (0, 0)),
                  pl.BlockSpec((1, d, tn), lambda l, j: (l, 0, j)),
                  pl.BlockSpec((1, 1, tn), lambda l, j: (l, 0, j))],
        out_specs=pl.BlockSpec((1, 16, tn), lambda l, j: (l, 0, j)),
        out_shape=jax.ShapeDtypeStruct((depth, 16, n6), F32),
        compiler_params=_cp(("arbitrary", "arbitrary")),
        name="ada_mods",
    )(c16, w_ada, b_ada.reshape(depth, 1, n6))


def _in_proj_kernel(x_ref, mod_ref, g_ref, w_ref, o_ref, h_scr, *, gate):
    @pl.when(pl.program_id(1) == 0)
    def _():
        m = mod_ref[0]
        h_scr[...] = _norm_mod(x_ref[...], g_ref[...], m[0:1, :], m[1:2, :]).astype(BF16)

    r = _dot(h_scr[...], w_ref[...])
    o_ref[...] = _sigmoid(r).astype(o_ref.dtype) if gate else r


def _in_proj(x, mods_l, gain, w_bf, dims):
    n, d = x.shape
    tm = dims["tm_big"]
    mrow = dims["mod_row"](tm)

    def call(w, tn, gate, dtype):
        n_cols = w.shape[1]
        return pl.pallas_call(
            functools.partial(_in_proj_kernel, gate=gate),
            grid=(n // tm, n_cols // tn),
            in_specs=[pl.BlockSpec((tm, d), lambda i, j: (i, 0)),
                      pl.BlockSpec((1, 6, d), lambda i, j: (mrow(i), 0, 0)),
                      pl.BlockSpec((1, d), lambda i, j: (0, 0)),
                      pl.BlockSpec((d, tn), lambda i, j: (0, j))],
            out_specs=pl.BlockSpec((tm, tn), lambda i, j: (i, j)),
            out_shape=jax.ShapeDtypeStruct((n, n_cols), dtype),
            scratch_shapes=[pltpu.VMEM((tm, d), BF16)],
            compiler_params=_cp(("arbitrary", "arbitrary")),
            name="in_proj_gate" if gate else "in_proj_mix",
        )(x, mods_l, gain, w)

    return (call(w_bf[:, :SMALL_COLS], SMALL_COLS // 2, False, F32),
            call(w_bf[:, SMALL_COLS:], GATE_COLS // 4, True, BF16))


def _short_conv_kernel(u_ref, w_ref, b_ref, o_ref, pad_scr, *, L, tc, planes):
    if planes:
        o_ref = o_ref.at[0]
    pad_scr[0:8, :] = jnp.zeros((8, BRANCH_W), F32)
    pad_scr[8 + L:16 + L, :] = jnp.zeros((8, BRANCH_W), F32)
    pad_scr[8:8 + L, :] = u_ref[...]
    w = w_ref[0]
    b = b_ref[0]

    def body(c, carry):
        t0 = pl.multiple_of(c * tc, tc)
        win = pad_scr[pl.ds(t0, tc + 16), :]
        acc = b + w[0:1, :] * win[7:7 + tc]
        acc = acc + w[1:2, :] * win[8:8 + tc]
        acc = acc + w[2:3, :] * win[9:9 + tc]
        o_ref[pl.ds(t0, tc), :] = acc
        return carry

    lax.fori_loop(0, L // tc, body, 0)


def _short_conv(u_small, w, b, L, nseq, row_base, planes=False):
    w3 = jnp.transpose(w.reshape(3, 3, BRANCH_W), (1, 0, 2))
    w3 = jnp.pad(w3, ((0, 0), (0, 5), (0, 0)))
    b3 = b.reshape(3, 1, BRANCH_W)
    tc = min(L, 256)
    return pl.pallas_call(
        functools.partial(_short_conv_kernel, L=L, tc=tc, planes=planes),
        grid=(nseq, 3),
        in_specs=[pl.BlockSpec((L, BRANCH_W), lambda s, j: (row_base + s, j)),
                  pl.BlockSpec((1, 8, BRANCH_W), lambda s, j: (j, 0, 0)),
                  pl.BlockSpec((1, 1, BRANCH_W), lambda s, j: (j, 0, 0))],
        out_specs=(pl.BlockSpec((1, L, BRANCH_W), lambda s, j: (j, s, 0)) if planes
                   else pl.BlockSpec((L, BRANCH_W), lambda s, j: (s, j))),
        out_shape=jax.ShapeDtypeStruct((3, nseq * L, BRANCH_W) if planes else (nseq * L, 3 * BRANCH_W), F32),
        scratch_shapes=[pltpu.VMEM((L + 16, BRANCH_W), F32)],
        compiler_params=_cp(("arbitrary", "arbitrary")),
        name="hy_short_conv",
    )(u_small, w3, b3)


def _filter_consts(L):
    t = np.linspace(0.0, 1.0, L)[:, None]
    bands = (HY_EMB - 1) // 2
    fr = np.linspace(1e-4, bands - 1, bands)[None, :]
    wpos = 2.0 * math.pi * np.arange(L)[:, None] / L
    z = np.concatenate([t, np.cos(fr * wpos), -np.sin(fr * wpos)], axis=-1)
    z = np.pad(z, ((0, 0), (0, HY_EMB_PAD - HY_EMB)))
    deltas = np.abs(np.linspace(math.log(HY_TARGET) / HY_SLOW, math.log(HY_TARGET) / HY_FAST, BRANCH_W))
    win = np.exp(-t * deltas[None, :])
    return jnp.asarray(z, F32), jnp.asarray(win, F32)


def _filter_kernel(z_ref, w1, b1, w2, b2, w3, b3, w4, fq, win_ref, hf_ref, ss_ref, *, tr, planes):
    i = pl.program_id(0)
    f = fq[...]
    a = jnp.sin(f * (_dot3(z_ref[...], w1[...]) + b1[...]))
    a = jnp.sin(f * (_dot3(a, w2[...]) + b2[...]))
    a = jnp.sin(f * (_dot3(a, w3[...]) + b3[...]))
    h = _dot3(a, w4[...])
    win = win_ref[...]
    h = h * jnp.concatenate([win, win, win, win], axis=1)
    row = lax.broadcasted_iota(jnp.int32, h.shape, 0) + i * tr
    col = lax.broadcasted_iota(jnp.int32, h.shape, 1)
    h = jnp.where((row == 0) & (col >= 2 * BRANCH_W), 0.0, h)
    if planes:
        for d in range(4):
            hf_ref[d] = h[:, d * BRANCH_W:(d + 1) * BRANCH_W]
    else:
        hf_ref[...] = h

    @pl.when(i == 0)
    def _():
        ss_ref[...] = jnp.zeros_like(ss_ref)

    ss_ref[...] += jnp.sum(h * h, axis=0, keepdims=True)


def _hyena_filter(L, w1, b1, w2, b2, w3, b3, w4, freq, planes=False):
    z, win = _filter_consts(L)
    tr = min(L, 256)
    w1p = jnp.pad(w1, ((0, HY_EMB_PAD - HY_EMB), (0, 0)))
    full = lambda a: pl.BlockSpec(a.shape, lambda i: (0,) * a.ndim)
    args = [w1p, b1.reshape(1, -1), w2, b2.reshape(1, -1), w3, b3.reshape(1, -1), w4, freq.reshape(1, -1)]
    n_out = w4.shape[1]
    return pl.pallas_call(
        functools.partial(_filter_kernel, tr=tr, planes=planes),
        grid=(L // tr,),
        in_specs=[pl.BlockSpec((tr, HY_EMB_PAD), lambda i: (i, 0))] + [full(a) for a in args]
        + [pl.BlockSpec((tr, BRANCH_W), lambda i: (i, 0))],
        out_specs=[pl.BlockSpec((4, tr, BRANCH_W), lambda i: (0, i, 0)) if planes
                   else pl.BlockSpec((tr, n_out), lambda i: (i, 0)),
                   pl.BlockSpec((1, n_out), lambda i: (0, 0))],
        out_shape=[jax.ShapeDtypeStruct((4, L, BRANCH_W) if planes else (L, n_out), F32),
                   jax.ShapeDtypeStruct((1, n_out), F32)],
        compiler_params=_cp(("arbitrary",)),
        name="hy_filter",
    )(z, *args, win)


def _dft_mats(L):
    N = 2 * L
    blk = 64
    k = jnp.arange(L, dtype=jnp.int32)[:, None]
    nh = jnp.arange(L // blk, dtype=jnp.int32)[None, :]
    nl = jnp.arange(blk, dtype=jnp.int32)[None, :]
    w = 2.0 * math.pi / N
    a = ((k * (blk * nh)) % N).astype(F32) * w
    b = ((k * nl) % N).astype(F32) * w
    ca, sa, cb, sb = jnp.cos(a), jnp.sin(a), jnp.cos(b), jnp.sin(b)
    cos = (ca[:, :, None] * cb[:, None, :] - sa[:, :, None] * sb[:, None, :]).reshape(L, L)
    sin = (sa[:, :, None] * cb[:, None, :] + ca[:, :, None] * sb[:, None, :]).reshape(L, L)
    alt = jnp.where(jnp.arange(L) % 2 == 0, 1.0, -1.0).astype(F32)
    first = (jnp.arange(L) == 0)
    s_f = jnp.where(first[:, None], alt[None, :], -sin)
    fwd = jnp.concatenate([cos, s_f], axis=0).astype(BF16)
    colscale = jnp.where(first, 1.0 / N, 2.0 / N).astype(F32)
    g_c = cos * colscale[None, :]
    g_s = jnp.where(first[None, :], alt[:, None] / N, -sin * (2.0 / N))
    inv = jnp.concatenate([g_c, g_s], axis=1).astype(BF16)
    return fwd, inv


def _dft_fwd_kernel(f_ref, z_ref, o_ref):
    o_ref[0] = _dot(f_ref[...], z_ref[...].astype(BF16))


def _dft_fwd(fwd, z2d, L, nb, zmap):
    tm = min(2 * L, 1024)
    return pl.pallas_call(
        _dft_fwd_kernel,
        grid=(2 * L // tm, nb),
        in_specs=[pl.BlockSpec((tm, L), lambda i, b: (i, 0)),
                  pl.BlockSpec((L, BRANCH_W), lambda i, b: zmap(b))],
        out_specs=pl.BlockSpec((1, tm, BRANCH_W), lambda i, b: (b, i, 0)),
        out_shape=jax.ShapeDtypeStruct((nb, 2 * L, BRANCH_W), F32),
        compiler_params=_cp(("arbitrary", "arbitrary")),
        name="hy_dft_fwd",
    )(fwd, z2d)


def _spec_prod_kernel(z_ref, kf_ref, kb_ref, sf_ref, sb_ref, y_ref, *, tk):
    i = pl.program_id(1)
    s = lax.rsqrt(sf_ref[...] + sb_ref[...] + EPS)
    zr, zi = z_ref[0, 0], z_ref[0, 1]
    fr, fi = kf_ref[0, 0], kf_ref[0, 1]
    br, bi = kb_ref[0, 0], kb_ref[0, 1]
    row = lax.broadcasted_iota(jnp.int32, (tk, BRANCH_W), 0) + i * tk
    first = row == 0
    kr = (fr + br) * s
    ki = jnp.where(first, fi + bi, fi - bi) * s
    yr = jnp.where(first, zr * kr, zr * kr - zi * ki)
    yi = jnp.where(first, zi * ki, zr * ki + zi * kr)
    y_ref[0, 0] = yr.astype(BF16)
    y_ref[0, 1] = yi.astype(BF16)


def _spec_prod(zf, kfs, ss, order, L, nb):
    tk = min(L, 512)
    zf4 = zf.reshape(nb, 2, L, BRANCH_W)
    kf4 = kfs.reshape(4, 2, L, BRANCH_W)
    y = pl.pallas_call(
        functools.partial(_spec_prod_kernel, tk=tk),
        grid=(nb, L // tk),
        in_specs=[pl.BlockSpec((1, 2, tk, BRANCH_W), lambda b, i: (b, 0, i, 0)),
                  pl.BlockSpec((1, 2, tk, BRANCH_W), lambda b, i: (order, 0, i, 0)),
                  pl.BlockSpec((1, 2, tk, BRANCH_W), lambda b, i: (2 + order, 0, i, 0)),
                  pl.BlockSpec((1, BRANCH_W), lambda b, i: (0, order)),
                  pl.BlockSpec((1, BRANCH_W), lambda b, i: (0, 2 + order))],
        out_specs=pl.BlockSpec((1, 2, tk, BRANCH_W), lambda b, i: (b, 0, i, 0)),
        out_shape=jax.ShapeDtypeStruct((nb, 2, L, BRANCH_W), BF16),
        compiler_params=_cp(("arbitrary", "arbitrary")),
        name="hy_spec_prod",
    )(zf4, kf4, kf4, ss, ss)
    return y.reshape(nb, 2 * L, BRANCH_W)


def _dft_inv_kernel(g_ref, y_ref, gate_ref, zp_ref, bias_ref, *rest):
    o_ref = rest[-1]
    conv = _dot(g_ref[...], y_ref[0])
    o_ref[...] = (gate_ref[...] * (conv + bias_ref[...] * zp_ref[...])).astype(o_ref.dtype)


def _dft_inv(inv, y, xs, gate_col, zprev, zprev_col, bias, L, nb, out_dtype, out_rows=None, out_base=0,
             prev=None):
    tm = min(L, 512)
    nt = L // tm
    args = [inv, y, xs, zprev, bias]
    in_specs = [pl.BlockSpec((tm, 2 * L), lambda i, b: (i, 0)),
                pl.BlockSpec((1, 2 * L, BRANCH_W), lambda i, b: (b, 0, 0)),
                pl.BlockSpec((tm, BRANCH_W), lambda i, b: (b * nt + i, gate_col)),
                pl.BlockSpec((tm, BRANCH_W), lambda i, b: (b * nt + i, zprev_col)),
                pl.BlockSpec((1, BRANCH_W), lambda i, b: (0, 0))]
    args, in_specs, alias = _into(prev, args, in_specs)
    return pl.pallas_call(
        _dft_inv_kernel,
        grid=(nt, nb),
        in_specs=in_specs,
        out_specs=pl.BlockSpec((tm, BRANCH_W), lambda i, b: ((out_base + b) * nt + i, 0)),
        out_shape=jax.ShapeDtypeStruct((out_rows or nb * L, BRANCH_W), out_dtype),
        input_output_aliases=alias,
        compiler_params=_cp(("arbitrary", "arbitrary")),
        name="hy_dft_inv",
    )(*args)


def _hyena(u_small, L, nseq, row_base, p, mats, out_rows, prev=None):
    (short_w, short_b, w1, b1, w2, b2, w3, b3, w4, freq, bias) = p
    fwd, inv = mats
    xs = _short_conv(u_small, short_w, short_b, L, nseq, row_base)
    hf, ss = _hyena_filter(L, w1, b1, w2, b2, w3, b3, w4, freq)
    kfs = _dft_fwd(fwd, hf, L, 4, lambda b: (0, b))
    zf = _dft_fwd(fwd, xs, L, nseq, lambda b: (b, 2))
    y = _spec_prod(zf, kfs, ss, 0, L, nseq)
    z1 = _dft_inv(inv, y, xs, 0, xs, 2, bias[0:1], L, nseq, F32)
    zf = _dft_fwd(fwd, z1, L, nseq, lambda b: (b, 0))
    y = _spec_prod(zf, kfs, ss, 1, L, nseq)
    return _dft_inv(inv, y, xs, 1, z1, 0, bias[1:2], L, nseq, BF16, out_rows, row_base, prev)


FFT_N1 = 64
FFT_KB = 8


def _fft_consts(L):
    N = 2 * L
    n2s = N // FFT_N1
    h = FFT_N1 // 2
    k1 = jnp.arange(FFT_N1, dtype=jnp.int32)
    ang_a = ((k1[:, None] * jnp.arange(h, dtype=jnp.int32)[None, :]) % FFT_N1).astype(F32) * (2.0 * math.pi / FFT_N1)
    ca, sa = jnp.cos(ang_a), jnp.sin(ang_a)
    fa = jnp.concatenate([ca, -sa], axis=0).astype(BF16)
    fai = (jnp.concatenate([ca.T, -sa.T], axis=1) / N).astype(BF16)
    k = k1[:, None, None] + FFT_N1 * jnp.arange(n2s, dtype=jnp.int32)[None, :, None]
    n2 = jnp.arange(n2s, dtype=jnp.int32)[None, None, :]
    ang = ((k * n2) % N).astype(F32) * (2.0 * math.pi / N)
    c, s = jnp.cos(ang), jnp.sin(ang)
    g = jnp.concatenate([jnp.concatenate([c, s], axis=2), jnp.concatenate([-s, c], axis=2)], axis=1)
    ct, st = jnp.swapaxes(c, 1, 2), jnp.swapaxes(s, 1, 2)
    gi = jnp.concatenate([jnp.concatenate([ct, -st], axis=2), jnp.concatenate([st, ct], axis=2)], axis=1)
    return fa, fai, g.astype(BF16), gi.astype(BF16)


def _fft_a_kernel(f_ref, z_ref, o_ref):
    o_ref[...] = _dot(f_ref[...], z_ref[0].astype(BF16)).astype(o_ref.dtype)


def _fft_a(fa, z3, zmap, nb, L):
    h = FFT_N1 // 2
    cols = (L // h) * BRANCH_W
    tn = min(cols, 4096)
    return pl.pallas_call(
        _fft_a_kernel,
        grid=(nb, cols // tn),
        in_specs=[pl.BlockSpec(fa.shape, lambda b, j: (0, 0)),
                  pl.BlockSpec((1, h, tn), lambda b, j: zmap(b) + (j,))],
        out_specs=pl.BlockSpec((2 * FFT_N1, tn), lambda b, j: (b, j)),
        out_shape=jax.ShapeDtypeStruct((nb * 2 * FFT_N1, cols), BF16),
        compiler_params=_cp(("arbitrary", "arbitrary")),
        name="hy_fft_a",
    )(fa, z3)


def _fft_spec_kernel(a_ref, g_ref, o_ref):
    for j in range(FFT_KB):
        s = _dot(g_ref[j], jnp.concatenate([a_ref[0, 0, j], a_ref[0, 1, j]], axis=0))
        half = s.shape[0] // 2
        o_ref[0, j, 0] = s[:half]
        o_ref[0, j, 1] = s[half:]


def _fft_spec(g, a5):
    nb, _, n1s, n2s, w = a5.shape
    return pl.pallas_call(
        _fft_spec_kernel,
        grid=(n1s // FFT_KB, nb),
        in_specs=[pl.BlockSpec((1, 2, FFT_KB, n2s, w), lambda i, b: (b, 0, i, 0, 0)),
                  pl.BlockSpec((FFT_KB, 2 * n2s, 2 * n2s), lambda i, b: (i, 0, 0))],
        out_specs=pl.BlockSpec((1, FFT_KB, 2, n2s, w), lambda i, b: (b, i, 0, 0, 0)),
        out_shape=jax.ShapeDtypeStruct((nb, n1s, 2, n2s, w), F32),
        compiler_params=_cp(("arbitrary", "arbitrary")),
        name="hy_fft_spec",
    )(a5, g)


def _fft_conv_kernel(a_ref, g_ref, gi_ref, sf_ref, sb_ref, ssf_ref, ssb_ref, o_ref):
    scale = lax.rsqrt(ssf_ref[...] + ssb_ref[...] + EPS)
    for j in range(FFT_KB):
        s = _dot(g_ref[j], jnp.concatenate([a_ref[0, 0, j], a_ref[0, 1, j]], axis=0))
        half = s.shape[0] // 2
        sr, si = s[:half], s[half:]
        kr = (sf_ref[0, j, 0] + sb_ref[0, j, 0]) * scale
        ki = (sf_ref[0, j, 1] - sb_ref[0, j, 1]) * scale
        y = jnp.concatenate([sr * kr - si * ki, sr * ki + si * kr], axis=0).astype(BF16)
        t = _dot(gi_ref[j], y)
        o_ref[0, 0, j] = t[:half].astype(BF16)
        o_ref[0, 1, j] = t[half:].astype(BF16)


def _fft_conv(g, gi, a5, kspec, ss, order):
    nb, _, n1s, n2s, w = a5.shape
    kblk = lambda d: pl.BlockSpec((1, FFT_KB, 2, n2s, w), lambda i, b: (d, i, 0, 0, 0))
    gblk = pl.BlockSpec((FFT_KB, 2 * n2s, 2 * n2s), lambda i, b: (i, 0, 0))
    ablk = pl.BlockSpec((1, 2, FFT_KB, n2s, w), lambda i, b: (b, 0, i, 0, 0))
    return pl.pallas_call(
        _fft_conv_kernel,
        grid=(n1s // FFT_KB, nb),
        in_specs=[ablk, gblk, gblk, kblk(order), kblk(2 + order),
                  pl.BlockSpec((1, w), lambda i, b: (0, order)),
                  pl.BlockSpec((1, w), lambda i, b: (0, 2 + order))],
        out_specs=ablk,
        out_shape=jax.ShapeDtypeStruct(a5.shape, BF16),
        compiler_params=_cp(("arbitrary", "arbitrary")),
        name="hy_fft_conv",
    )(a5, g, gi, kspec, kspec, ss, ss)


def _fft_ainv_kernel(f_ref, a_ref, gate_ref, zp_ref, bias_ref, o_ref):
    conv = _dot(f_ref[...], a_ref[...])
    o_ref[...] = (gate_ref[0] * (conv + bias_ref[...] * zp_ref[0])).astype(o_ref.dtype)


def _fft_ainv(fai, a2, gate3, gate_plane, zprev3, zprev_plane, bias, L, nb, out_dtype, out_view_rows):
    h = FFT_N1 // 2
    cols = (L // h) * BRANCH_W
    tn = min(cols, 4096)
    return pl.pallas_call(
        _fft_ainv_kernel,
        grid=(nb, cols // tn),
        in_specs=[pl.BlockSpec(fai.shape, lambda b, j: (0, 0)),
                  pl.BlockSpec((2 * FFT_N1, tn), lambda b, j: (b, j)),
                  pl.BlockSpec((1, h, tn), lambda b, j: (gate_plane, b, j)),
                  pl.BlockSpec((1, h, tn), lambda b, j: (zprev_plane, b, j)),
                  pl.BlockSpec((1, tn), lambda b, j: (0, 0))],
        out_specs=pl.BlockSpec((h, tn), lambda b, j: (b, j)),
        out_shape=jax.ShapeDtypeStruct((out_view_rows, cols), out_dtype),
        compiler_params=_cp(("arbitrary", "arbitrary")),
        name="hy_fft_ainv",
    )(fai, a2, gate3, zprev3, jnp.tile(bias, (1, tn // BRANCH_W)))


def _hyena_fft(u_small, L, nseq, p, consts, out_rows):
    (short_w, short_b, w1, b1, w2, b2, w3, b3, w4, freq, bias) = p
    fa, fai, g, gi = consts
    h = FFT_N1 // 2
    n2s = L // h
    cols = n2s * BRANCH_W
    xs3 = _short_conv(u_small, short_w, short_b, L, nseq, 0, planes=True).reshape(3, nseq * h, cols)
    hf4, ss = _hyena_filter(L, w1, b1, w2, b2, w3, b3, w4, freq, planes=True)
    ka = _fft_a(fa, hf4.reshape(4, h, cols), lambda b: (b, 0), 4, L)
    kspec = _fft_spec(g, ka.reshape(4, 2, FFT_N1, n2s, BRANCH_W))

    def conv(z3, zmap, order, gate_plane, zprev_plane, out_dtype, out_view_rows):
        a = _fft_a(fa, z3, zmap, nseq, L)
        a = _fft_conv(g, gi, a.reshape(nseq, 2, FFT_N1, n2s, BRANCH_W), kspec, ss, order)
        return _fft_ainv(fai, a.reshape(nseq * 2 * FFT_N1, cols), xs3, gate_plane, z3, zprev_plane,
                         bias[order:order + 1], L, nseq, out_dtype, out_view_rows)

    z1 = conv(xs3, lambda b: (2, b), 0, 0, 2, F32, nseq * h).reshape(1, nseq * h, cols)
    y = conv(z1, lambda b: (0, b), 1, 1, 0, BF16, out_rows // n2s)
    return y.reshape(out_rows, BRANCH_W)


def _conformer_kernel(a_ref, g_ref, w_ref, b_ref, lg_ref, lb_ref, *rest, L, tc):
    o_ref, pad_scr = rest[-2:]
    pad_scr[0:16, :] = jnp.zeros((16, BRANCH_W), F32)
    pad_scr[16 + L:32 + L, :] = jnp.zeros((16, BRANCH_W), F32)

    def glu(c, carry):
        t0 = pl.multiple_of(c * tc, tc)
        pad_scr[pl.ds(16 + t0, tc), :] = a_ref[pl.ds(t0, tc), :] * _sigmoid(g_ref[pl.ds(t0, tc), :])
        return carry

    lax.fori_loop(0, L // tc, glu, 0)
    b = b_ref[...]
    lg = lg_ref[...]
    lb = lb_ref[...]

    def body(c, carry):
        t0 = pl.multiple_of(c * tc, tc)
        win = pad_scr[pl.ds(t0, tc + 32), :]
        acc = jnp.zeros((tc, BRANCH_W), F32) + b
        for r in range(8):
            sh = win[r:r + tc + 24]
            for a in range(4):
                m = 8 * a + r
                if 1 <= m <= CF_WIDTH:
                    acc = acc + w_ref[m - 1:m, :] * sh[8 * a:8 * a + tc]
        mu = jnp.mean(acc, axis=-1, keepdims=True)
        xc = acc - mu
        var = jnp.mean(xc * xc, axis=-1, keepdims=True)
        y = xc * lax.rsqrt(var + EPS) * lg + lb
        o_ref[pl.ds(t0, tc), :] = (y * _sigmoid(y)).astype(o_ref.dtype)
        return carry

    lax.fori_loop(0, L // tc, body, 0)


def _conformer(u_small, L, nseq, row_base, p, out_rows, prev=None):
    dw_w, dw_b, ln_g, ln_b = p
    tc = 128
    wpad = jnp.pad(dw_w, ((0, 32 - CF_WIDTH), (0, 0)))
    row = lambda a: a.reshape(1, BRANCH_W)
    vec = pl.BlockSpec((1, BRANCH_W), lambda s: (0, 0))
    args = [u_small, u_small, wpad, row(dw_b), row(ln_g), row(ln_b)]
    in_specs = [pl.BlockSpec((L, BRANCH_W), lambda s: (row_base + s, 3)),
                pl.BlockSpec((L, BRANCH_W), lambda s: (row_base + s, 4)),
                pl.BlockSpec((32, BRANCH_W), lambda s: (0, 0)), vec, vec, vec]
    args, in_specs, alias = _into(prev, args, in_specs)
    return pl.pallas_call(
        functools.partial(_conformer_kernel, L=L, tc=tc),
        grid=(nseq,),
        in_specs=in_specs,
        out_specs=pl.BlockSpec((L, BRANCH_W), lambda s: (row_base + s, 0)),
        out_shape=jax.ShapeDtypeStruct((out_rows, BRANCH_W), BF16),
        input_output_aliases=alias,
        scratch_shapes=[pltpu.VMEM((L + 32, BRANCH_W), F32)],
        compiler_params=_cp(("arbitrary",)),
        name="conformer",
    )(*args)


def _rope_tables(S, pad_rows, head, scale_unused=None):
    half = head // 2
    nf = half // 2
    lane = np.arange(LANES)
    inv_lane = (ROPE_BASE ** (-(np.arange(nf)) / nf))[(lane % half) % nf]
    is_row = (lane % head) < half
    pos = jnp.arange(S, dtype=jnp.int32)
    rows = (pos // GRID_W).astype(F32)[:, None]
    cols = (pos % GRID_W).astype(F32)[:, None]
    ang = jnp.where(jnp.asarray(is_row)[None, :], rows, cols) * jnp.asarray(inv_lane, F32)[None, :]
    cos = jnp.concatenate([jnp.cos(ang), jnp.ones((pad_rows, LANES), F32)], axis=0)
    sin = jnp.concatenate([jnp.sin(ang), jnp.zeros((pad_rows, LANES), F32)], axis=0)
    return cos, sin


def _group_ones(width, group):
    idx = np.arange(width)
    return jnp.asarray((idx[:, None] // group) == (idx[None, :] // group), BF16)


def _head_norm_rope(x, ones, group, gain, cos, sin, nf, out_scale):
    w = x.shape[1]
    hi, lo = _split(x * x)
    ms = (_dot(hi, ones) + _dot(lo, ones)) * (1.0 / group)
    xn = x * lax.rsqrt(ms + EPS) * gain
    reps = w // LANES
    c = jnp.concatenate([cos] * reps, axis=1) if reps > 1 else cos
    s = jnp.concatenate([sin] * reps, axis=1) if reps > 1 else sin
    lane = lax.broadcasted_iota(jnp.int32, x.shape, 1)
    first = (lane % (2 * nf)) < nf
    rot = jnp.where(first, -pltpu.roll(xn, w - nf, 1), pltpu.roll(xn, nf, 1))
    return (xn * c + rot * s) * out_scale


def _qkv_kernel(gq_ref, gkv_ref, dq_ref, dk_ref, dv_ref, cg_ref, sg_ref, cd_ref, sd_ref,
                o64_ref, o32_ref, gqn_ref, gkn_ref, dqn_ref, dkn_ref,
                qg_ref, kg_ref, vag_ref, vbg_ref, qd_ref, kd_ref, vad_ref, vbd_ref):
    cg, sg, cd, sd = cg_ref[...], sg_ref[...], cd_ref[...], sd_ref[...]
    o64, o32 = o64_ref[...], o32_ref[...]
    q = _head_norm_rope(gq_ref[...], o64, HEAD_DIM, gqn_ref[...], cg, sg, HEAD_DIM // 4, LOG2E * HEAD_DIM ** -0.5)
    qg_ref[...] = q.astype(BF16)
    kv = gkv_ref[...]
    k = _head_norm_rope(kv[:, :LANES], o64[:LANES, :LANES], HEAD_DIM, gkn_ref[...], cg, sg, HEAD_DIM // 4, 1.0)
    v = kv[:, LANES:]
    kk = jnp.concatenate([k, k], axis=1)
    vv = jnp.concatenate([v, v], axis=1)
    quarter = lax.broadcasted_iota(jnp.int32, kk.shape, 1) // HEAD_DIM
    kr = pltpu.roll(kk, HEAD_DIM, 1)
    vr = pltpu.roll(vv, HEAD_DIM, 1)
    kg_ref[...] = jnp.where((quarter == 0) | (quarter == 3), kk, kr).astype(BF16)
    vag_ref[...] = jnp.where(quarter == 0, vv, jnp.where(quarter == 2, vr, 0.0)).astype(BF16)
    vbg_ref[...] = jnp.where(quarter == 1, vr, jnp.where(quarter == 3, vv, 0.0)).astype(BF16)
    qd = _head_norm_rope(dq_ref[...], o32, DIFF_D, dqn_ref[...], cd, sd, DIFF_D // 4, LOG2E * DIFF_D ** -0.5)
    qd_ref[...] = qd.astype(BF16)
    kd = _head_norm_rope(dk_ref[...], o32, DIFF_D, dkn_ref[...], cd, sd, DIFF_D // 4, 1.0)
    kd_ref[...] = kd.astype(BF16)
    vd = dv_ref[...]
    even = (lax.broadcasted_iota(jnp.int32, vd.shape, 1) // HEAD_DIM) % 2 == 0
    vad_ref[...] = jnp.where(even, vd, 0.0).astype(BF16)
    vbd_ref[...] = jnp.where(even, 0.0, vd).astype(BF16)


def _qkv_prep(u_small, gqn, gkn, dqn, dkn, tables, dims):
    n = u_small.shape[0]
    tm = dims["tm_small"]
    S = dims["S"]
    n_lat_tiles = dims["n_lat"] // tm
    per_seq = S // tm
    tmap = lambda i: (jnp.where(i < n_lat_tiles, i % per_seq, per_seq), 0)
    col = lambda c: pl.BlockSpec((tm, BRANCH_W), lambda i: (i, c))
    tab = pl.BlockSpec((tm, LANES), tmap)
    full = lambda a: pl.BlockSpec(a.shape, lambda i: (0,) * a.ndim)
    o64, o32 = _group_ones(BRANCH_W, HEAD_DIM), _group_ones(BRANCH_W, DIFF_D)
    gains = [jnp.tile(gqn, 4).reshape(1, 256), jnp.tile(gkn, 2).reshape(1, 128),
             jnp.tile(dqn.reshape(-1), 4).reshape(1, 256), jnp.tile(dkn.reshape(-1), 4).reshape(1, 256)]
    out = pl.BlockSpec((tm, BRANCH_W), lambda i: (i, 0))
    return pl.pallas_call(
        _qkv_kernel,
        grid=(n // tm,),
        in_specs=[col(5), col(6), col(7), col(8), col(9), tab, tab, tab, tab, full(o64), full(o32)]
        + [full(g) for g in gains],
        out_specs=[out] * 8,
        out_shape=[jax.ShapeDtypeStruct((n, BRANCH_W), BF16)] * 8,
        compiler_params=_cp(("arbitrary",)),
        name="qkv_prep",
    )(u_small, u_small, u_small, u_small, u_small, *tables, o64, o32, *gains)


def _lane_pick(lane_lo, a, b):
    return jnp.where(lane_lo, a, b)


def _flash_kernel(*refs, segs, tk, tq, diff, lam_init):
    q_ref, o_ref = refs[0], refs[-1]
    seg_refs = [refs[1 + 3 * i:4 + 3 * i] for i in range(len(segs))]
    aux_ref = refs[1 + 3 * len(segs)]
    lane = lax.broadcasted_iota(jnp.int32, (1, LANES), 1)
    lane_lo = lane < HEAD_DIM
    if diff:
        masks = [(lane >= g * DIFF_D) & (lane < (g + 1) * DIFF_D) for g in range(4)]
        acc_of = [0, 1, 0, 1]
        use_a = [True, True, False, False]
        n_acc = 2
    else:
        masks = [lane_lo, ~lane_lo]
        acc_of = [0, 0]
        use_a = [True, False]
        n_acc = 1
    R = len(masks)
    pairs = [slice(p * LANES, (p + 1) * LANES) for p in range(2)]
    qsts = [jnp.concatenate([jnp.where(m, q_ref[:, ps], jnp.zeros((tq, LANES), BF16)) for m in masks], axis=0)
            for ps in pairs]
    lo_hi = []
    for a in range(n_acc):
        rs = [r for r in range(R) if acc_of[r] == a]
        lo_hi.append(([r for r in rs if use_a[r]][0], [r for r in rs if not use_a[r]][0]))

    def rows(x, r):
        return x[r * tq:(r + 1) * tq]

    shift = jnp.max(aux_ref[5:6, :])

    def run(fixed):
        def chunk(kv_refs, t0, size, carry):
            k_ref, va_ref, vb_ref = kv_refs
            new = []
            for p, ps in enumerate(pairs):
                m_run, l_run, accs = carry[p]
                k = k_ref[pl.ds(t0, size), ps]
                s = lax.dot_general(qsts[p], k, (((1,), (1,)), ((), ())), preferred_element_type=F32)
                if fixed:
                    m_new = m_run
                    pr = jnp.exp2(s - shift)
                    l_new = l_run + jnp.sum(pr, axis=-1, keepdims=True)
                else:
                    m_new = jnp.maximum(m_run, jnp.max(s, axis=-1, keepdims=True))
                    alpha = jnp.exp2(m_run - m_new)
                    pr = jnp.exp2(s - m_new)
                    l_new = alpha * l_run + jnp.sum(pr, axis=-1, keepdims=True)
                prb = pr.astype(BF16)
                va = va_ref[pl.ds(t0, size), ps]
                vb = vb_ref[pl.ds(t0, size), ps]
                new_accs = []
                for a, (r_lo, r_hi) in enumerate(lo_hi):
                    upd = _dot(rows(prb, r_lo), va) + _dot(rows(prb, r_hi), vb)
                    if fixed:
                        new_accs.append(accs[a] + upd)
                    else:
                        al = _lane_pick(lane_lo, rows(alpha, r_lo), rows(alpha, r_hi))
                        new_accs.append(accs[a] * al + upd)
                new.append((m_new, l_new, tuple(new_accs)))
            return tuple(new)

        one = (jnp.full((R * tq, 1), -jnp.inf, F32), jnp.zeros((R * tq, 1), F32),
               tuple(jnp.zeros((tq, LANES), F32) for _ in range(n_acc)))
        carry = (one, one)
        for kv_refs, T in zip(seg_refs, segs):
            n_main = T // tk
            if n_main:
                carry = lax.fori_loop(
                    0, n_main, lambda c, cr, kv_refs=kv_refs: chunk(kv_refs, pl.multiple_of(c * tk, tk), tk, cr),
                    carry, unroll=2 if n_main % 2 == 0 else 1)
            if T - n_main * tk:
                carry = chunk(kv_refs, n_main * tk, T - n_main * tk, carry)

        outs = []
        for p in range(2):
            _, l_fin, accs = carry[p]
            inv_l = 1.0 / l_fin
            norm = [accs[a] * _lane_pick(lane_lo, rows(inv_l, r_lo), rows(inv_l, r_hi))
                    for a, (r_lo, r_hi) in enumerate(lo_hi)]
            if diff:
                aux = aux_ref[...]
                lam = (jnp.exp(jnp.sum(aux[0:1] * aux[1:2], axis=-1, keepdims=True))
                       - jnp.exp(jnp.sum(aux[2:3] * aux[3:4], axis=-1, keepdims=True)) + lam_init)
                o = norm[0] - lam * norm[1]
                sq = o * o
                s_lo = jnp.sum(jnp.where(lane_lo, sq, 0.0), axis=-1, keepdims=True)
                s_hi = jnp.sum(jnp.where(lane_lo, 0.0, sq), axis=-1, keepdims=True)
                ms = _lane_pick(lane_lo, s_lo, s_hi) * (1.0 / HEAD_DIM)
                o = o * lax.rsqrt(ms + EPS) * aux[4:5] * (1.0 - lam_init)
            else:
                o = norm[0]
            outs.append(o)
        o_ref[...] = jnp.concatenate(outs, axis=1).astype(o_ref.dtype)

    @pl.when(shift < FIXED_SHIFT_LIMIT)
    def _():
        run(True)

    @pl.when(jnp.logical_not(shift < FIXED_SHIFT_LIMIT))
    def _():
        run(False)


def _into(prev, args, in_specs):
    if prev is None:
        return args, in_specs, {}
    return args + [prev], in_specs + [pl.BlockSpec(memory_space=pl.ANY)], {len(args): 0}


def _flash(q, k, va, vb, aux, nb, Lq, q_base, segs, diff, lam_init, out_rows, out_base, prev=None):
    tq = 256 if diff else min(512, Lq)
    tk = 512
    nt = Lq // tq
    q_base, out_base = q_base // tq, out_base // tq
    args = [q]
    in_specs = [pl.BlockSpec((tq, BRANCH_W), lambda b, i: (q_base + b * nt + i, 0))]
    for length, base in segs:
        spec = pl.BlockSpec((length, BRANCH_W), lambda b, i, base=base: (base + b, 0))
        args += [k, va, vb]
        in_specs += [spec, spec, spec]
    args.append(aux)
    in_specs.append(pl.BlockSpec((8, LANES), lambda b, i: (0, 0)))
    args, in_specs, alias = _into(prev, args, in_specs)
    return pl.pallas_call(
        functools.partial(_flash_kernel, segs=tuple(s[0] for s in segs), tk=tk, tq=tq, diff=diff,
                          lam_init=lam_init),
        grid=(nb, nt),
        in_specs=in_specs,
        out_specs=pl.BlockSpec((tq, BRANCH_W), lambda b, i: (out_base + b * nt + i, 0)),
        out_shape=jax.ShapeDtypeStruct((out_rows, BRANCH_W), BF16),
        input_output_aliases=alias,
        compiler_params=_cp(("arbitrary", "arbitrary")),
        name="flash_diff" if diff else "flash_gqa",
    )(*args)


def _merge_kernel(y0, y1, y2, y3, g_ref, wb_ref, wo_ref, x_ref, mod_ref, o_ref):
    d = D_MODEL
    acc = g_ref[:, 0:d].astype(F32) * _dot(y0[...], wb_ref[0])
    acc = acc + g_ref[:, d:2 * d].astype(F32) * _dot(y1[...], wb_ref[1])
    acc = acc + g_ref[:, 2 * d:3 * d].astype(F32) * _dot(y2[...], wb_ref[2])
    acc = acc + g_ref[:, 3 * d:4 * d].astype(F32) * _dot(y3[...], wb_ref[3])
    mix = _dot(acc.astype(BF16), wo_ref[...])
    o_ref[...] = x_ref[...] + mod_ref[0][2:3, :] * mix


def _merge(ys, gate, wb, wo, x, mods_l, dims):
    n, d = ys[0].shape[0], x.shape[1]
    tm = dims["tm_small"]
    mrow = dims["mod_row"](tm)
    yspec = pl.BlockSpec((tm, BRANCH_W), lambda i: (i, 0))
    return pl.pallas_call(
        _merge_kernel,
        grid=(n // tm,),
        in_specs=[yspec] * 4 + [pl.BlockSpec((tm, GATE_COLS), lambda i: (i, 0)),
                                pl.BlockSpec(wb.shape, lambda i: (0, 0, 0)),
                                pl.BlockSpec(wo.shape, lambda i: (0, 0)),
                                pl.BlockSpec((tm, d), lambda i: (i, 0)),
                                pl.BlockSpec((1, 6, d), lambda i: (mrow(i), 0, 0))],
        out_specs=pl.BlockSpec((tm, d), lambda i: (i, 0)),
        out_shape=jax.ShapeDtypeStruct((n, d), F32),
        compiler_params=_cp(("arbitrary",)),
        name="merge",
    )(*ys, gate, wb, wo, x, mods_l)


def _ffn_kernel(x_ref, mod_ref, g_ref, w1_ref, w3_ref, w2_ref, o_ref, h_scr, acc_scr):
    j = pl.program_id(1)

    @pl.when(j == 0)
    def _():
        m = mod_ref[0]
        h_scr[...] = _norm_mod(x_ref[...], g_ref[...], m[3:4, :], m[4:5, :]).astype(BF16)
        acc_scr[...] = jnp.zeros_like(acc_scr)

    h = h_scr[...]
    a = _dot(h, w1_ref[...])
    b = _dot(h, w3_ref[...])
    t = (a * _sigmoid(a) * b).astype(BF16)
    acc_scr[...] += _dot(t, w2_ref[...])

    @pl.when(j == pl.num_programs(1) - 1)
    def _():
        o_ref[...] = x_ref[...] + mod_ref[0][5:6, :] * acc_scr[...]


def _ffn(x, mods_l, gain, w1, w3, w2, dims):
    n, d = x.shape
    tm, tf = dims["tm_big"], 256
    mrow = dims["mod_row"](tm)
    return pl.pallas_call(
        _ffn_kernel,
        grid=(n // tm, w1.shape[1] // tf),
        in_specs=[pl.BlockSpec((tm, d), lambda i, j: (i, 0)),
                  pl.BlockSpec((1, 6, d), lambda i, j: (mrow(i), 0, 0)),
                  pl.BlockSpec((1, d), lambda i, j: (0, 0)),
                  pl.BlockSpec((d, tf), lambda i, j: (0, j)),
                  pl.BlockSpec((d, tf), lambda i, j: (0, j)),
                  pl.BlockSpec((tf, d), lambda i, j: (j, 0))],
        out_specs=pl.BlockSpec((tm, d), lambda i, j: (i, 0)),
        out_shape=jax.ShapeDtypeStruct((n, d), F32),
        scratch_shapes=[pltpu.VMEM((tm, d), BF16), pltpu.VMEM((tm, d), F32)],
        compiler_params=_cp(("arbitrary", "arbitrary")),
        name="ffn_dense",
    )(x, mods_l, gain, w1, w3, w2)


def _route_kernel(x_ref, mod_ref, g_ref, wr_ref, tok_ref, gate_ref):
    m = mod_ref[0]
    h = _norm_mod(x_ref[...], g_ref[...], m[3:4, :], m[4:5, :])
    tok_ref[...] = h.astype(BF16)
    logits = _dot3(h, wr_ref[...])
    lane = lax.broadcasted_iota(jnp.int32, logits.shape, 1)
    lg = jnp.where(lane < N_EXPERTS, logits, -jnp.inf)
    m1 = jnp.max(lg, axis=-1, keepdims=True)
    i1 = jnp.min(jnp.where(lg == m1, lane, LANES), axis=-1, keepdims=True)
    lg2 = jnp.where(lane == i1, -jnp.inf, lg)
    m2 = jnp.max(lg2, axis=-1, keepdims=True)
    i2 = jnp.min(jnp.where(lg2 == m2, lane, LANES), axis=-1, keepdims=True)
    e2 = jnp.exp(m2 - m1)
    g1 = 1.0 / (1.0 + e2)
    gate_ref[...] = jnp.where(lane == i1, g1, jnp.where(lane == i2, e2 * g1, 0.0))


def _route(x, mods_l, gain, w_router, dims):
    n, d = x.shape
    tm = dims["tm_small"]
    mrow = dims["mod_row"](tm)
    wr = jnp.pad(w_router, ((0, 0), (0, LANES - N_EXPERTS)))
    return pl.pallas_call(
        _route_kernel,
        grid=(n // tm,),
        in_specs=[pl.BlockSpec((tm, d), lambda i: (i, 0)),
                  pl.BlockSpec((1, 6, d), lambda i: (mrow(i), 0, 0)),
                  pl.BlockSpec((1, d), lambda i: (0, 0)),
                  pl.BlockSpec((d, LANES), lambda i: (0, 0))],
        out_specs=[pl.BlockSpec((tm, d), lambda i: (i, 0)), pl.BlockSpec((tm, LANES), lambda i: (i, 0))],
        out_shape=[jax.ShapeDtypeStruct((n, d), BF16), jax.ShapeDtypeStruct((n, LANES), F32)],
        compiler_params=_cp(("arbitrary",)),
        name="moe_route",
    )(x, mods_l, gain, wr)


MOE_SUB = 128
MOE_GRAN = 64


def _moe_kernel(tok_ref, gate_ref, tri_ref, w1_ref, w3_ref, w2_ref, x_ref, mod_ref, o_ref,
                rank_scr, rank_t_scr, mask_t_scr, xe_scr, ye_scr, cnt_smem, *, tm):
    e = pl.program_id(1)
    j = pl.program_id(2)
    ne = pl.num_programs(1)
    nj = pl.num_programs(2)
    nsub = tm // MOE_SUB
    lane = lax.broadcasted_iota(jnp.int32, (1, LANES), 1)

    @pl.when((e == 0) & (j == 0))
    def _():
        tri = tri_ref[...]
        carry = jnp.zeros((1, LANES), F32)
        for blk in range(tm // 256):
            rs = slice(blk * 256, (blk + 1) * 256)
            msk = (gate_ref[rs, :] > 0.0).astype(BF16)
            rank_scr[rs, :] = _dot(tri, msk) + carry
            carry = carry + jnp.sum(msk.astype(F32), axis=0, keepdims=True)
        rank_t_scr[...] = jnp.transpose(rank_scr[...])
        mask_t_scr[...] = jnp.transpose((gate_ref[...] > 0.0).astype(F32))
        for ee in range(N_EXPERTS):
            cnt_smem[ee] = jnp.sum(jnp.where(lane == ee, carry, 0.0)).astype(jnp.int32)
        o_ref[...] = jnp.zeros_like(o_ref)
        ye_scr[...] = jnp.zeros_like(ye_scr)

    cnt = cnt_smem[e]

    n_sub = (cnt + (MOE_SUB - 1)) // MOE_SUB

    @pl.when(j == 0)
    def _():
        rk = rank_t_scr[pl.ds(e, 1), :]
        mk = mask_t_scr[pl.ds(e, 1), :]

        def gather(sb, carry):
            r0 = pl.multiple_of(sb * MOE_SUB, MOE_SUB)
            r_iota = lax.broadcasted_iota(jnp.int32, (MOE_SUB, tm), 0) + r0
            sel = jnp.where((rk == r_iota.astype(F32)) & (mk > 0.0), 1.0, 0.0).astype(BF16)
            xe_scr[pl.ds(r0, MOE_SUB), :] = _dot(sel, tok_ref[...]).astype(BF16)
            return carry

        lax.fori_loop(0, n_sub, gather, 0)

    def expert_rows(r0, size):
        xe = xe_scr[pl.ds(r0, size), :]
        a = _dot(xe, w1_ref[0])
        b = _dot(xe, w3_ref[0])
        t = (a * _sigmoid(a) * b).astype(BF16)
        y = _dot(t, w2_ref[0])
        ye_scr[pl.ds(r0, size), :] = jnp.where(j == 0, y, ye_scr[pl.ds(r0, size), :] + y)

    n_gran = (cnt + (MOE_GRAN - 1)) // MOE_GRAN
    n_big = n_gran // 4
    rem = n_gran % 4

    def big(i, carry):
        expert_rows(pl.multiple_of(i * 256, 256), 256)
        return carry

    lax.fori_loop(0, n_big, big, 0)

    @pl.when(rem >= 2)
    def _():
        expert_rows(pl.multiple_of(n_big * 256, 128), 128)

    @pl.when(rem % 2 == 1)
    def _():
        expert_rows(pl.multiple_of(n_big * 256 + (rem // 2) * 128, 64), 64)

    @pl.when(j == nj - 1)
    def _():
        def scatter(rc, carry):
            r0 = pl.multiple_of(rc * 256, 256)
            gt = gate_ref[pl.ds(r0, 256), :]
            g_e = jnp.sum(jnp.where(lane == e, gt, 0.0), axis=-1, keepdims=True)
            r_e = jnp.sum(jnp.where(lane == e, rank_scr[pl.ds(r0, 256), :], 0.0), axis=-1, keepdims=True)
            lo = jnp.min(jnp.where(g_e > 0.0, r_e, 1e9)).astype(jnp.int32)
            hi = jnp.max(jnp.where(g_e > 0.0, r_e, -1.0)).astype(jnp.int32)

            def scatter_sub(sb, c2):
                c0 = pl.multiple_of(sb * MOE_SUB, MOE_SUB)
                c_iota = lax.broadcasted_iota(jnp.int32, (256, MOE_SUB), 1) + c0
                sel_t = jnp.where((r_e == c_iota.astype(F32)) & (g_e > 0.0), 1.0, 0.0).astype(BF16)
                ye = ye_scr[pl.ds(c0, MOE_SUB), :].astype(BF16)
                o_ref[pl.ds(r0, 256), :] += g_e * _dot(sel_t, ye)
                return c2

            lax.fori_loop(lo // MOE_SUB, (hi + MOE_SUB) // MOE_SUB, scatter_sub, 0)

            @pl.when(e == ne - 1)
            def _():
                o_ref[pl.ds(r0, 256), :] = (x_ref[pl.ds(r0, 256), :]
                                            + mod_ref[0][5:6, :] * o_ref[pl.ds(r0, 256), :])
            return carry

        lax.fori_loop(0, tm // 256, scatter, 0)


def _moe(tok, gates, w1, w3, w2, x, mods_l, dims):
    n, d = tok.shape
    tm = dims["tm_moe"] if n % dims["tm_moe"] == 0 else dims["tm_big"]
    mrow = dims["mod_row"](tm)
    ne, _, ff = w1.shape
    tf = 512
    tri = jnp.asarray(np.tril(np.ones((256, 256)), -1), BF16)
    once = pl.Buffered(1)
    return pl.pallas_call(
        functools.partial(_moe_kernel, tm=tm),
        grid=(n // tm, ne, ff // tf),
        in_specs=[pl.BlockSpec((tm, d), lambda i, e, j: (i, 0), pipeline_mode=once),
                  pl.BlockSpec((tm, LANES), lambda i, e, j: (i, 0), pipeline_mode=once),
                  pl.BlockSpec((256, 256), lambda i, e, j: (0, 0), pipeline_mode=once),
                  pl.BlockSpec((1, d, tf), lambda i, e, j: (e, 0, j)),
                  pl.BlockSpec((1, d, tf), lambda i, e, j: (e, 0, j)),
                  pl.BlockSpec((1, tf, d), lambda i, e, j: (e, j, 0)),
                  pl.BlockSpec((tm, d), lambda i, e, j: (i, 0), pipeline_mode=once),
                  pl.BlockSpec((1, 6, d), lambda i, e, j: (mrow(i), 0, 0))],
        out_specs=pl.BlockSpec((tm, d), lambda i, e, j: (i, 0), pipeline_mode=once),
        out_shape=jax.ShapeDtypeStruct((n, d), F32),
        scratch_shapes=[pltpu.VMEM((tm, LANES), F32), pltpu.VMEM((LANES, tm), F32), pltpu.VMEM((LANES, tm), F32),
                        pltpu.VMEM((tm, d), BF16), pltpu.VMEM((tm, d), F32), pltpu.SMEM((N_EXPERTS,), jnp.int32)],
        compiler_params=_cp(("arbitrary", "arbitrary", "arbitrary"), VMEM_LIMIT_MOE),
        name="moe_experts",
    )(tok, gates, tri, w1, w3, w2, x, mods_l)


def _make_dims(B, S, n_ctx):
    n_lat, n = B * S, B * (S + n_ctx)

    def pick(prefs):
        for t in prefs:
            if S % t == 0 and n_lat % t == 0 and (n - n_lat) % t == 0:
                return t
        raise ValueError("no row tile divides the latent and context token counts")

    def mod_row(tm):
        return lambda i: jnp.minimum((i * tm) // S, B)

    return {"B": B, "S": S, "n_ctx": n_ctx, "n_lat": n_lat, "n": n,
            "tm_big": pick((1024, 512, 256)), "tm_small": pick((512, 256)), "tm_moe": pick((2048, 1024, 512, 256)),
            "mod_row": mod_row}


def kernel(x, c, ctx, c_ctx, w_ada, b_ada, norm_mix, norm_ffn, w_in, hy_short_w, hy_short_b, hy_w1, hy_b1, hy_w2, hy_b2, hy_w3, hy_b3, hy_w4, hy_freq, hy_bias, cf_dw_w, cf_dw_b, cf_ln_g, cf_ln_b, gqa_qn, gqa_kn, diff_qn, diff_kn, diff_lq1, diff_lk1, diff_lq2, diff_lk2, diff_subln, w_branch, w_out, ffn_w1, ffn_w3, ffn_w2, moe_router, moe_w1, moe_w3, moe_w2):
    B, S, D = x.shape
    n_ctx = ctx.shape[1]
    depth = w_ada.shape[0]
    dims = _make_dims(B, S, n_ctx)
    n_lat = dims["n_lat"]
    tm_s = dims["tm_small"]

    c16 = jnp.concatenate([c, c_ctx[None, :], jnp.zeros((16 - B - 1, D), F32)], axis=0)
    mods = _ada_mods(c16, w_ada, b_ada).reshape(depth, 16, 6, D)

    xa = jnp.concatenate([x.reshape(n_lat, D), ctx.reshape(B * n_ctx, D)], axis=0)
    tables = _rope_tables(S, tm_s, HEAD_DIM) + _rope_tables(S, tm_s, DIFF_D)
    fft_lat = _fft_consts(S)
    mats_ctx = _dft_mats(n_ctx)

    for l in range(depth):
        lam_init = 0.8 - 0.6 * math.exp(-0.3 * l)
        u_small, gate = _in_proj(xa, mods[l], norm_mix[l].reshape(1, D), w_in[l].astype(BF16), dims)

        hy_p = (hy_short_w[l], hy_short_b[l], hy_w1[l], hy_b1[l], hy_w2[l], hy_b2[l], hy_w3[l], hy_b3[l],
                hy_w4[l], hy_freq[l], hy_bias[l])
        cf_p = (cf_dw_w[l], cf_dw_b[l], cf_ln_g[l], cf_ln_b[l])
        last = l == depth - 1
        n_out = n_lat if last else dims["n"]
        ctx_base = n_lat // n_ctx
        y_hy = _hyena_fft(u_small, S, B, hy_p, fft_lat, n_out)
        y_cf = _conformer(u_small, S, B, 0, cf_p, n_out)
        if not last:
            y_hy = _hyena(u_small, n_ctx, B, ctx_base, hy_p, mats_ctx, n_out, y_hy)
            y_cf = _conformer(u_small, n_ctx, B, ctx_base, cf_p, n_out, y_cf)

        qg, kg, vag, vbg, qd, kd, vad, vbd = _qkv_prep(u_small, gqa_qn[l], gqa_kn[l], diff_qn[l], diff_kn[l],
                                                      tables, dims)

        aux = jnp.zeros((8, LANES), F32)
        aux = aux.at[0, :DIFF_D].set(diff_lq1[l]).at[1, :DIFF_D].set(diff_lk1[l])
        aux = aux.at[2, :DIFF_D].set(diff_lq2[l]).at[3, :DIFF_D].set(diff_lk2[l])
        aux = aux.at[4, :].set(jnp.tile(diff_subln[l], 2))

        def score_bound(qn, kn, d):
            return 1.02 * d * jnp.max(jnp.abs(qn)) * jnp.max(jnp.abs(kn)) * (LOG2E * d ** -0.5) + 0.1

        bounds = (score_bound(gqa_qn[l], gqa_kn[l], HEAD_DIM), score_bound(diff_qn[l], diff_kn[l], DIFF_D))
        y_att = []
        for (q, k, va, vb, is_diff) in ((qg, kg, vag, vbg, False), (qd, kd, vad, vbd, True)):
            aux = aux.at[5, :].set(bounds[int(is_diff)])
            y = _flash(q, k, va, vb, aux, B, S, 0, ((S, 0), (n_ctx, ctx_base)), is_diff, lam_init, n_out, 0)
            if not last:
                y = _flash(q, k, va, vb, aux, B, n_ctx, n_lat, ((n_ctx, ctx_base),), is_diff, lam_init,
                           n_out, n_lat, y)
            y_att.append(y)

        xa = _merge((y_hy, y_cf, y_att[0], y_att[1]), gate, w_branch[l].astype(BF16), w_out[l].astype(BF16),
                    xa, mods[l], dims)

        i = l // 2
        if l % 2 == 0:
            xa = _ffn(xa, mods[l], norm_ffn[l].reshape(1, D), ffn_w1[i].astype(BF16), ffn_w3[i].astype(BF16),
                      ffn_w2[i].astype(BF16), dims)
        else:
            tok, gates = _route(xa, mods[l], norm_ffn[l].reshape(1, D), moe_router[i], dims)
            xa = _moe(tok, gates, moe_w1[i].astype(BF16), moe_w3[i].astype(BF16), moe_w2[i].astype(BF16),
                      xa, mods[l], dims)
    return xa[:n_lat].reshape(B, S, D)
```

```python
import functools
import math

import numpy as np
import jax
import jax.numpy as jnp
from jax import lax
from jax.experimental import pallas as pl
from jax.experimental.pallas import tpu as pltpu

F32 = jnp.float32
BF16 = jnp.bfloat16

D_MODEL = 1024
GRID_W = 64
BRANCH_W = 256
EPS = 1e-6
HY_EMB = 33
HY_EMB_PAD = 128
HY_FFN = 64
HY_TARGET = 1e-2
HY_FAST = 0.3
HY_SLOW = 1.5
CF_WIDTH = 31
HEAD_DIM = 64
DIFF_D = 32
ROPE_BASE = 10000.0
N_EXPERTS = 8
LANES = 128
LOG2E = 1.4426950408889634
FIXED_SHIFT_LIMIT = 60.0
SMALL_COLS = 2560
GATE_COLS = 4 * D_MODEL
VMEM_LIMIT = 48 * 1024 * 1024
VMEM_LIMIT_MOE = 56 * 1024 * 1024


def _cp(sem, vmem=VMEM_LIMIT):
    return pltpu.CompilerParams(dimension_semantics=sem, vmem_limit_bytes=vmem)


def _sigmoid(x):
    return 1.0 / (1.0 + jnp.exp(-x))


def _dot(a, b):
    return jnp.dot(a, b, preferred_element_type=F32)


def _split(a):
    hi = a.astype(BF16)
    lo = (a - hi.astype(F32)).astype(BF16)
    return hi, lo


def _dot3(a, b):
    ah, al = _split(a)
    bh, bl = _split(b)
    return _dot(ah, bh) + (_dot(al, bh) + _dot(ah, bl))


def _norm_mod(x, gain, shift, scale):
    ms = jnp.mean(x * x, axis=-1, keepdims=True)
    return (x * lax.rsqrt(ms + EPS) * gain) * (1.0 + scale) + shift


def _ada_kernel(c_ref, w_ref, b_ref, o_ref):
    c = c_ref[...]
    s = c * _sigmoid(c)
    o_ref[0] = _dot3(s, w_ref[0]) + b_ref[0]


def _ada_mods(c16, w_ada, b_ada):
    depth, d, n6 = w_ada.shape
    tn = 512
    return pl.pallas_call(
        _ada_kernel,
        grid=(depth, n6 // tn),
        in_specs=[pl.BlockSpec((16, d), lambda l, j: (0, 0)),
                  pl.BlockSpec((1, d, tn), lambda l, j: (l, 0, j)),
                  pl.BlockSpec((1, 1, tn), lambda l, j: (l, 0, j))],
        out_specs=pl.BlockSpec((1, 16, tn), lambda l, j: (l, 0, j)),
        out_shape=jax.ShapeDtypeStruct((depth, 16, n6), F32),
        compiler_params=_cp(("arbitrary", "arbitrary")),
        name="ada_mods",
    )(c16, w_ada, b_ada.reshape(depth, 1, n6))


def _in_proj_kernel(x_ref, mod_ref, g_ref, w_ref, o_ref, h_scr, *, gate):
    @pl.when(pl.program_id(1) == 0)
    def _():
        m = mod_ref[0]
        h_scr[...] = _norm_mod(x_ref[...], g_ref[...], m[0:1, :], m[1:2, :]).astype(BF16)

    r = _dot(h_scr[...], w_ref[...])
    o_ref[...] = _sigmoid(r).astype(o_ref.dtype) if gate else r


def _in_proj(x, mods_l, gain, w_bf, dims):
    n, d = x.shape
    tm = dims["tm_big"]
    mrow = dims["mod_row"](tm)

    def call(w, tn, gate, dtype):
        n_cols = w.shape[1]
        return pl.pallas_call(
            functools.partial(_in_proj_kernel, gate=gate),
            grid=(n // tm, n_cols // tn),
            in_specs=[pl.BlockSpec((tm, d), lambda i, j: (i, 0)),
                      pl.BlockSpec((1, 6, d), lambda i, j: (mrow(i), 0, 0)),
                      pl.BlockSpec((1, d), lambda i, j: (0, 0)),
                      pl.BlockSpec((d, tn), lambda i, j: (0, j))],
            out_specs=pl.BlockSpec((tm, tn), lambda i, j: (i, j)),
            out_shape=jax.ShapeDtypeStruct((n, n_cols), dtype),
            scratch_shapes=[pltpu.VMEM((tm, d), BF16)],
            compiler_params=_cp(("arbitrary", "arbitrary")),
            name="in_proj_gate" if gate else "in_proj_mix",
        )(x, mods_l, gain, w)

    return (call(w_bf[:, :SMALL_COLS], SMALL_COLS // 2, False, F32),
            call(w_bf[:, SMALL_COLS:], GATE_COLS // 4, True, BF16))


def _short_conv_kernel(u_ref, w_ref, b_ref, o_ref, pad_scr, *, L, tc, planes):
    if planes:
        o_ref = o_ref.at[0]
    pad_scr[0:8, :] = jnp.zeros((8, BRANCH_W), F32)
    pad_scr[8 + L:16 + L, :] = jnp.zeros((8, BRANCH_W), F32)
    pad_scr[8:8 + L, :] = u_ref[...]
    w = w_ref[0]
    b = b_ref[0]

    def body(c, carry):
        t0 = pl.multiple_of(c * tc, tc)
        win = pad_scr[pl.ds(t0, tc + 16), :]
        acc = b + w[0:1, :] * win[7:7 + tc]
        acc = acc + w[1:2, :] * win[8:8 + tc]
        acc = acc + w[2:3, :] * win[9:9 + tc]
        o_ref[pl.ds(t0, tc), :] = acc
        return carry

    lax.fori_loop(0, L // tc, body, 0)


def _short_conv(u_small, w, b, L, nseq, row_base, planes=False):
    w3 = jnp.transpose(w.reshape(3, 3, BRANCH_W), (1, 0, 2))
    w3 = jnp.pad(w3, ((0, 0), (0, 5), (0, 0)))
    b3 = b.reshape(3, 1, BRANCH_W)
    tc = min(L, 256)
    return pl.pallas_call(
        functools.partial(_short_conv_kernel, L=L, tc=tc, planes=planes),
        grid=(nseq, 3),
        in_specs=[pl.BlockSpec((L, BRANCH_W), lambda s, j: (row_base + s, j)),
                  pl.BlockSpec((1, 8, BRANCH_W), lambda s, j: (j, 0, 0)),
                  pl.BlockSpec((1, 1, BRANCH_W), lambda s, j: (j, 0, 0))],
        out_specs=(pl.BlockSpec((1, L, BRANCH_W), lambda s, j: (j, s, 0)) if planes
                   else pl.BlockSpec((L, BRANCH_W), lambda s, j: (s, j))),
        out_shape=jax.ShapeDtypeStruct((3, nseq * L, BRANCH_W) if planes else (nseq * L, 3 * BRANCH_W), F32),
        scratch_shapes=[pltpu.VMEM((L + 16, BRANCH_W), F32)],
        compiler_params=_cp(("arbitrary", "arbitrary")),
        name="hy_short_conv",
    )(u_small, w3, b3)


def _filter_consts(L):
    t = np.linspace(0.0, 1.0, L)[:, None]
    bands = (HY_EMB - 1) // 2
    fr = np.linspace(1e-4, bands - 1, bands)[None, :]
    wpos = 2.0 * math.pi * np.arange(L)[:, None] / L
    z = np.concatenate([t, np.cos(fr * wpos), -np.sin(fr * wpos)], axis=-1)
    z = np.pad(z, ((0, 0), (0, HY_EMB_PAD - HY_EMB)))
    deltas = np.abs(np.linspace(math.log(HY_TARGET) / HY_SLOW, math.log(HY_TARGET) / HY_FAST, BRANCH_W))
    win = np.exp(-t * deltas[None, :])
    return jnp.asarray(z, F32), jnp.asarray(win, F32)


def _filter_kernel(z_ref, w1, b1, w2, b2, w3, b3, w4, fq, win_ref, hf_ref, ss_ref, *, tr, planes):
    i = pl.program_id(0)
    f = fq[...]
    a = jnp.sin(f * (_dot3(z_ref[...], w1[...]) + b1[...]))
    a = jnp.sin(f * (_dot3(a, w2[...]) + b2[...]))
    a = jnp.sin(f * (_dot3(a, w3[...]) + b3[...]))
    h = _dot3(a, w4[...])
    win = win_ref[...]
    h = h * jnp.concatenate([win, win, win, win], axis=1)
    row = lax.broadcasted_iota(jnp.int32, h.shape, 0) + i * tr
    col = lax.broadcasted_iota(jnp.int32, h.shape, 1)
    h = jnp.where((row == 0) & (col >= 2 * BRANCH_W), 0.0, h)
    if planes:
        for d in range(4):
            hf_ref[d] = h[:, d * BRANCH_W:(d + 1) * BRANCH_W]
    else:
        hf_ref[...] = h

    @pl.when(i == 0)
    def _():
        ss_ref[...] = jnp.zeros_like(ss_ref)

    ss_ref[...] += jnp.sum(h * h, axis=0, keepdims=True)


def _hyena_filter(L, w1, b1, w2, b2, w3, b3, w4, freq, planes=False):
    z, win = _filter_consts(L)
    tr = min(L, 256)
    w1p = jnp.pad(w1, ((0, HY_EMB_PAD - HY_EMB), (0, 0)))
    full = lambda a: pl.BlockSpec(a.shape, lambda i: (0,) * a.ndim)
    args = [w1p, b1.reshape(1, -1), w2, b2.reshape(1, -1), w3, b3.reshape(1, -1), w4, freq.reshape(1, -1)]
    n_out = w4.shape[1]
    return pl.pallas_call(
        functools.partial(_filter_kernel, tr=tr, planes=planes),
        grid=(L // tr,),
        in_specs=[pl.BlockSpec((tr, HY_EMB_PAD), lambda i: (i, 0))] + [full(a) for a in args]
        + [pl.BlockSpec((tr, BRANCH_W), lambda i: (i, 0))],
        out_specs=[pl.BlockSpec((4, tr, BRANCH_W), lambda i: (0, i, 0)) if planes
                   else pl.BlockSpec((tr, n_out), lambda i: (i, 0)),
                   pl.BlockSpec((1, n_out), lambda i: (0, 0))],
        out_shape=[jax.ShapeDtypeStruct((4, L, BRANCH_W) if planes else (L, n_out), F32),
                   jax.ShapeDtypeStruct((1, n_out), F32)],
        compiler_params=_cp(("arbitrary",)),
        name="hy_filter",
    )(z, *args, win)


def _dft_mats(L):
    N = 2 * L
    blk = 64
    k = jnp.arange(L, dtype=jnp.int32)[:, None]
    nh = jnp.arange(L // blk, dtype=jnp.int32)[None, :]
    nl = jnp.arange(blk, dtype=jnp.int32)[None, :]
    w = 2.0 * math.pi / N
    a = ((k * (blk * nh)) % N).astype(F32) * w
    b = ((k * nl) % N).astype(F32) * w
    ca, sa, cb, sb = jnp.cos(a), jnp.sin(a), jnp.cos(b), jnp.sin(b)
    cos = (ca[:, :, None] * cb[:, None, :] - sa[:, :, None] * sb[:, None, :]).reshape(L, L)
    sin = (sa[:, :, None] * cb[:, None, :] + ca[:, :, None] * sb[:, None, :]).reshape(L, L)
    alt = jnp.where(jnp.arange(L) % 2 == 0, 1.0, -1.0).astype(F32)
    first = (jnp.arange(L) == 0)
    s_f = jnp.where(first[:, None], alt[None, :], -sin)
    fwd = jnp.concatenate([cos, s_f], axis=0).astype(BF16)
    colscale = jnp.where(first, 1.0 / N, 2.0 / N).astype(F32)
    g_c = cos * colscale[None, :]
    g_s = jnp.where(first[None, :], alt[:, None] / N, -sin * (2.0 / N))
    inv = jnp.concatenate([g_c, g_s], axis=1).astype(BF16)
    return fwd, inv


def _dft_fwd_kernel(f_ref, z_ref, o_ref):
    o_ref[0] = _dot(f_ref[...], z_ref[...].astype(BF16))


def _dft_fwd(fwd, z2d, L, nb, zmap):
    tm = min(2 * L, 1024)
    return pl.pallas_call(
        _dft_fwd_kernel,
        grid=(2 * L // tm, nb),
        in_specs=[pl.BlockSpec((tm, L), lambda i, b: (i, 0)),
                  pl.BlockSpec((L, BRANCH_W), lambda i, b: zmap(b))],
        out_specs=pl.BlockSpec((1, tm, BRANCH_W), lambda i, b: (b, i, 0)),
        out_shape=jax.ShapeDtypeStruct((nb, 2 * L, BRANCH_W), F32),
        compiler_params=_cp(("arbitrary", "arbitrary")),
        name="hy_dft_fwd",
    )(fwd, z2d)


def _spec_prod_kernel(z_ref, kf_ref, kb_ref, sf_ref, sb_ref, y_ref, *, tk):
    i = pl.program_id(1)
    s = lax.rsqrt(sf_ref[...] + sb_ref[...] + EPS)
    zr, zi = z_ref[0, 0], z_ref[0, 1]
    fr, fi = kf_ref[0, 0], kf_ref[0, 1]
    br, bi = kb_ref[0, 0], kb_ref[0, 1]
    row = lax.broadcasted_iota(jnp.int32, (tk, BRANCH_W), 0) + i * tk
    first = row == 0
    kr = (fr + br) * s
    ki = jnp.where(first, fi + bi, fi - bi) * s
    yr = jnp.where(first, zr * kr, zr * kr - zi * ki)
    yi = jnp.where(first, zi * ki, zr * ki + zi * kr)
    y_ref[0, 0] = yr.astype(BF16)
    y_ref[0, 1] = yi.astype(BF16)


def _spec_prod(zf, kfs, ss, order, L, nb):
    tk = min(L, 512)
    zf4 = zf.reshape(nb, 2, L, BRANCH_W)
    kf4 = kfs.reshape(4, 2, L, BRANCH_W)
    y = pl.pallas_call(
        functools.partial(_spec_prod_kernel, tk=tk),
        grid=(nb, L // tk),
        in_specs=[pl.BlockSpec((1, 2, tk, BRANCH_W), lambda b, i: (b, 0, i, 0)),
                  pl.BlockSpec((1, 2, tk, BRANCH_W), lambda b, i: (order, 0, i, 0)),
                  pl.BlockSpec((1, 2, tk, BRANCH_W), lambda b, i: (2 + order, 0, i, 0)),
                  pl.BlockSpec((1, BRANCH_W), lambda b, i: (0, order)),
                  pl.BlockSpec((1, BRANCH_W), lambda b, i: (0, 2 + order))],
        out_specs=pl.BlockSpec((1, 2, tk, BRANCH_W), lambda b, i: (b, 0, i, 0)),
        out_shape=jax.ShapeDtypeStruct((nb, 2, L, BRANCH_W), BF16),
        compiler_params=_cp(("arbitrary", "arbitrary")),
        name="hy_spec_prod",
    )(zf4, kf4, kf4, ss, ss)
    return y.reshape(nb, 2 * L, BRANCH_W)


def _dft_inv_kernel(g_ref, y_ref, gate_ref, zp_ref, bias_ref, *rest):
    o_ref = rest[-1]
    conv = _dot(g_ref[...], y_ref[0])
    o_ref[...] = (gate_ref[...] * (conv + bias_ref[...] * zp_ref[...])).astype(o_ref.dtype)


def _dft_inv(inv, y, xs, gate_col, zprev, zprev_col, bias, L, nb, out_dtype, out_rows=None, out_base=0,
             prev=None):
    tm = min(L, 512)
    nt = L // tm
    args = [inv, y, xs, zprev, bias]
    in_specs = [pl.BlockSpec((tm, 2 * L), lambda i, b: (i, 0)),
                pl.BlockSpec((1, 2 * L, BRANCH_W), lambda i, b: (b, 0, 0)),
                pl.BlockSpec((tm, BRANCH_W), lambda i, b: (b * nt + i, gate_col)),
                pl.BlockSpec((tm, BRANCH_W), lambda i, b: (b * nt + i, zprev_col)),
                pl.BlockSpec((1, BRANCH_W), lambda i, b: (0, 0))]
    args, in_specs, alias = _into(prev, args, in_specs)
    return pl.pallas_call(
        _dft_inv_kernel,
        grid=(nt, nb),
        in_specs=in_specs,
        out_specs=pl.BlockSpec((tm, BRANCH_W), lambda i, b: ((out_base + b) * nt + i, 0)),
        out_shape=jax.ShapeDtypeStruct((out_rows or nb * L, BRANCH_W), out_dtype),
        input_output_aliases=alias,
        compiler_params=_cp(("arbitrary", "arbitrary")),
        name="hy_dft_inv",
    )(*args)


def _hyena(u_small, L, nseq, row_base, p, mats, out_rows, prev=None):
    (short_w, short_b, w1, b1, w2, b2, w3, b3, w4, freq, bias) = p
    fwd, inv = mats
    xs = _short_conv(u_small, short_w, short_b, L, nseq, row_base)
    hf, ss = _hyena_filter(L, w1, b1, w2, b2, w3, b3, w4, freq)
    kfs = _dft_fwd(fwd, hf, L, 4, lambda b: (0, b))
    zf = _dft_fwd(fwd, xs, L, nseq, lambda b: (b, 2))
    y = _spec_prod(zf, kfs, ss, 0, L, nseq)
    z1 = _dft_inv(inv, y, xs, 0, xs, 2, bias[0:1], L, nseq, F32)
    zf = _dft_fwd(fwd, z1, L, nseq, lambda b: (b, 0))
    y = _spec_prod(zf, kfs, ss, 1, L, nseq)
    return _dft_inv(inv, y, xs, 1, z1, 0, bias[1:2], L, nseq, BF16, out_rows, row_base, prev)


FFT_N1 = 64
FFT_KB = 8


def _fft_consts(L):
    N = 2 * L
    n2s = N // FFT_N1
    h = FFT_N1 // 2
    k1 = jnp.arange(FFT_N1, dtype=jnp.int32)
    ang_a = ((k1[:, None] * jnp.arange(h, dtype=jnp.int32)[None, :]) % FFT_N1).astype(F32) * (2.0 * math.pi / FFT_N1)
    ca, sa = jnp.cos(ang_a), jnp.sin(ang_a)
    fa = jnp.concatenate([ca, -sa], axis=0).astype(BF16)
    fai = (jnp.concatenate([ca.T, -sa.T], axis=1) / N).astype(BF16)
    k = k1[:, None, None] + FFT_N1 * jnp.arange(n2s, dtype=jnp.int32)[None, :, None]
    n2 = jnp.arange(n2s, dtype=jnp.int32)[None, None, :]
    ang = ((k * n2) % N).astype(F32) * (2.0 * math.pi / N)
    c, s = jnp.cos(ang), jnp.sin(ang)
    g = jnp.concatenate([jnp.concatenate([c, s], axis=2), jnp.concatenate([-s, c], axis=2)], axis=1)
    ct, st = jnp.swapaxes(c, 1, 2), jnp.swapaxes(s, 1, 2)
    gi = jnp.concatenate([jnp.concatenate([ct, -st], axis=2), jnp.concatenate([st, ct], axis=2)], axis=1)
    return fa, fai, g.astype(BF16), gi.astype(BF16)


def _fft_a_kernel(f_ref, z_ref, o_ref):
    o_ref[...] = _dot(f_ref[...], z_ref[0].astype(BF16)).astype(o_ref.dtype)


def _fft_a(fa, z3, zmap, nb, L):
    h = FFT_N1 // 2
    cols = (L // h) * BRANCH_W
    tn = min(cols, 4096)
    return pl.pallas_call(
        _fft_a_kernel,
        grid=(nb, cols // tn),
        in_specs=[pl.BlockSpec(fa.shape, lambda b, j: (0, 0)),
                  pl.BlockSpec((1, h, tn), lambda b, j: zmap(b) + (j,))],
        out_specs=pl.BlockSpec((2 * FFT_N1, tn), lambda b, j: (b, j)),
        out_shape=jax.ShapeDtypeStruct((nb * 2 * FFT_N1, cols), BF16),
        compiler_params=_cp(("arbitrary", "arbitrary")),
        name="hy_fft_a",
    )(fa, z3)


def _fft_spec_kernel(a_ref, g_ref, o_ref):
    for j in range(FFT_KB):
        s = _dot(g_ref[j], jnp.concatenate([a_ref[0, 0, j], a_ref[0, 1, j]], axis=0))
        half = s.shape[0] // 2
        o_ref[0, j, 0] = s[:half]
        o_ref[0, j, 1] = s[half:]


def _fft_spec(g, a5):
    nb, _, n1s, n2s, w = a5.shape
    return pl.pallas_call(
        _fft_spec_kernel,
        grid=(n1s // FFT_KB, nb),
        in_specs=[pl.BlockSpec((1, 2, FFT_KB, n2s, w), lambda i, b: (b, 0, i, 0, 0)),
                  pl.BlockSpec((FFT_KB, 2 * n2s, 2 * n2s), lambda i, b: (i, 0, 0))],
        out_specs=pl.BlockSpec((1, FFT_KB, 2, n2s, w), lambda i, b: (b, i, 0, 0, 0)),
        out_shape=jax.ShapeDtypeStruct((nb, n1s, 2, n2s, w), F32),
        compiler_params=_cp(("arbitrary", "arbitrary")),
        name="hy_fft_spec",
    )(a5, g)


def _fft_conv_kernel(a_ref, g_ref, gi_ref, sf_ref, sb_ref, ssf_ref, ssb_ref, o_ref):
    scale = lax.rsqrt(ssf_ref[...] + ssb_ref[...] + EPS)
    for j in range(FFT_KB):
        s = _dot(g_ref[j], jnp.concatenate([a_ref[0, 0, j], a_ref[0, 1, j]], axis=0))
        half = s.shape[0] // 2
        sr, si = s[:half], s[half:]
        kr = (sf_ref[0, j, 0] + sb_ref[0, j, 0]) * scale
        ki = (sf_ref[0, j, 1] - sb_ref[0, j, 1]) * scale
        y = jnp.concatenate([sr * kr - si * ki, sr * ki + si * kr], axis=0).astype(BF16)
        t = _dot(gi_ref[j], y)
        o_ref[0, 0, j] = t[:half].astype(BF16)
        o_ref[0, 1, j] = t[half:].astype(BF16)


def _fft_conv(g, gi, a5, kspec, ss, order):
    nb, _, n1s, n2s, w = a5.shape
    kblk = lambda d: pl.BlockSpec((1, FFT_KB, 2, n2s, w), lambda i, b: (d, i, 0, 0, 0))
    gblk = pl.BlockSpec((FFT_KB, 2 * n2s, 2 * n2s), lambda i, b: (i, 0, 0))
    ablk = pl.BlockSpec((1, 2, FFT_KB, n2s, w), lambda i, b: (b, 0, i, 0, 0))
    return pl.pallas_call(
        _fft_conv_kernel,
        grid=(n1s // FFT_KB, nb),
        in_specs=[ablk, gblk, gblk, kblk(order), kblk(2 + order),
                  pl.BlockSpec((1, w), lambda i, b: (0, order)),
                  pl.BlockSpec((1, w), lambda i, b: (0, 2 + order))],
        out_specs=ablk,
        out_shape=jax.ShapeDtypeStruct(a5.shape, BF16),
        compiler_params=_cp(("arbitrary", "arbitrary")),
        name="hy_fft_conv",
    )(a5, g, gi, kspec, kspec, ss, ss)


def _fft_ainv_kernel(f_ref, a_ref, gate_ref, zp_ref, bias_ref, o_ref):
    conv = _dot(f_ref[...], a_ref[...])
    o_ref[...] = (gate_ref[0] * (conv + bias_ref[...] * zp_ref[0])).astype(o_ref.dtype)


def _fft_ainv(fai, a2, gate3, gate_plane, zprev3, zprev_plane, bias, L, nb, out_dtype, out_view_rows):
    h = FFT_N1 // 2
    cols = (L // h) * BRANCH_W
    tn = min(cols, 4096)
    return pl.pallas_call(
        _fft_ainv_kernel,
        grid=(nb, cols // tn),
        in_specs=[pl.BlockSpec(fai.shape, lambda b, j: (0, 0)),
                  pl.BlockSpec((2 * FFT_N1, tn), lambda b, j: (b, j)),
                  pl.BlockSpec((1, h, tn), lambda b, j: (gate_plane, b, j)),
                  pl.BlockSpec((1, h, tn), lambda b, j: (zprev_plane, b, j)),
                  pl.BlockSpec((1, tn), lambda b, j: (0, 0))],
        out_specs=pl.BlockSpec((h, tn), lambda b, j: (b, j)),
        out_shape=jax.ShapeDtypeStruct((out_view_rows, cols), out_dtype),
        compiler_params=_cp(("arbitrary", "arbitrary")),
        name="hy_fft_ainv",
    )(fai, a2, gate3, zprev3, jnp.tile(bias, (1, tn // BRANCH_W)))


def _hyena_fft(u_small, L, nseq, p, consts, out_rows):
    (short_w, short_b, w1, b1, w2, b2, w3, b3, w4, freq, bias) = p
    fa, fai, g, gi = consts
    h = FFT_N1 // 2
    n2s = L // h
    cols = n2s * BRANCH_W
    xs3 = _short_conv(u_small, short_w, short_b, L, nseq, 0, planes=True).reshape(3, nseq * h, cols)
    hf4, ss = _hyena_filter(L, w1, b1, w2, b2, w3, b3, w4, freq, planes=True)
    ka = _fft_a(fa, hf4.reshape(4, h, cols), lambda b: (b, 0), 4, L)
    kspec = _fft_spec(g, ka.reshape(4, 2, FFT_N1, n2s, BRANCH_W))

    def conv(z3, zmap, order, gate_plane, zprev_plane, out_dtype, out_view_rows):
        a = _fft_a(fa, z3, zmap, nseq, L)
        a = _fft_conv(g, gi, a.reshape(nseq, 2, FFT_N1, n2s, BRANCH_W), kspec, ss, order)
        return _fft_ainv(fai, a.reshape(nseq * 2 * FFT_N1, cols), xs3, gate_plane, z3, zprev_plane,
                         bias[order:order + 1], L, nseq, out_dtype, out_view_rows)

    z1 = conv(xs3, lambda b: (2, b), 0, 0, 2, F32, nseq * h).reshape(1, nseq * h, cols)
    y = conv(z1, lambda b: (0, b), 1, 1, 0, BF16, out_rows // n2s)
    return y.reshape(out_rows, BRANCH_W)


def _conformer_kernel(a_ref, g_ref, w_ref, b_ref, lg_ref, lb_ref, *rest, L, tc):
    o_ref, pad_scr = rest[-2:]
    pad_scr[0:16, :] = jnp.zeros((16, BRANCH_W), F32)
    pad_scr[16 + L:32 + L, :] = jnp.zeros((16, BRANCH_W), F32)

    def glu(c, carry):
        t0 = pl.multiple_of(c * tc, tc)
        pad_scr[pl.ds(16 + t0, tc), :] = a_ref[pl.ds(t0, tc), :] * _sigmoid(g_ref[pl.ds(t0, tc), :])
        return carry

    lax.fori_loop(0, L // tc, glu, 0)
    b = b_ref[...]
    lg = lg_ref[...]
    lb = lb_ref[...]

    def body(c, carry):
        t0 = pl.multiple_of(c * tc, tc)
        win = pad_scr[pl.ds(t0, tc + 32), :]
        acc = jnp.zeros((tc, BRANCH_W), F32) + b
        for r in range(8):
            sh = win[r:r + tc + 24]
            for a in range(4):
                m = 8 * a + r
                if 1 <= m <= CF_WIDTH:
                    acc = acc + w_ref[m - 1:m, :] * sh[8 * a:8 * a + tc]
        mu = jnp.mean(acc, axis=-1, keepdims=True)
        xc = acc - mu
        var = jnp.mean(xc * xc, axis=-1, keepdims=True)
        y = xc * lax.rsqrt(var + EPS) * lg + lb
        o_ref[pl.ds(t0, tc), :] = (y * _sigmoid(y)).astype(o_ref.dtype)
        return carry

    lax.fori_loop(0, L // tc, body, 0)


def _conformer(u_small, L, nseq, row_base, p, out_rows, prev=None):
    dw_w, dw_b, ln_g, ln_b = p
    tc = 128
    wpad = jnp.pad(dw_w, ((0, 32 - CF_WIDTH), (0, 0)))
    row = lambda a: a.reshape(1, BRANCH_W)
    vec = pl.BlockSpec((1, BRANCH_W), lambda s: (0, 0))
    args = [u_small, u_small, wpad, row(dw_b), row(ln_g), row(ln_b)]
    in_specs = [pl.BlockSpec((L, BRANCH_W), lambda s: (row_base + s, 3)),
                pl.BlockSpec((L, BRANCH_W), lambda s: (row_base + s, 4)),
                pl.BlockSpec((32, BRANCH_W), lambda s: (0, 0)), vec, vec, vec]
    args, in_specs, alias = _into(prev, args, in_specs)
    return pl.pallas_call(
        functools.partial(_conformer_kernel, L=L, tc=tc),
        grid=(nseq,),
        in_specs=in_specs,
        out_specs=pl.BlockSpec((L, BRANCH_W), lambda s: (row_base + s, 0)),
        out_shape=jax.ShapeDtypeStruct((out_rows, BRANCH_W), BF16),
        input_output_aliases=alias,
        scratch_shapes=[pltpu.VMEM((L + 32, BRANCH_W), F32)],
        compiler_params=_cp(("arbitrary",)),
        name="conformer",
    )(*args)


def _rope_tables(S, pad_rows, head, scale_unused=None):
    half = head // 2
    nf = half // 2
    lane = np.arange(LANES)
    inv_lane = (ROPE_BASE ** (-(np.arange(nf)) / nf))[(lane % half) % nf]
    is_row = (lane % head) < half
    pos = jnp.arange(S, dtype=jnp.int32)
    rows = (pos // GRID_W).astype(F32)[:, None]
    cols = (pos % GRID_W).astype(F32)[:, None]
    ang = jnp.where(jnp.asarray(is_row)[None, :], rows, cols) * jnp.asarray(inv_lane, F32)[None, :]
    cos = jnp.concatenate([jnp.cos(ang), jnp.ones((pad_rows, LANES), F32)], axis=0)
    sin = jnp.concatenate([jnp.sin(ang), jnp.zeros((pad_rows, LANES), F32)], axis=0)
    return cos, sin


def _group_ones(width, group):
    idx = np.arange(width)
    return jnp.asarray((idx[:, None] // group) == (idx[None, :] // group), BF16)


def _head_norm_rope(x, ones, group, gain, cos, sin, nf, out_scale):
    w = x.shape[1]
    hi, lo = _split(x * x)
    ms = (_dot(hi, ones) + _dot(lo, ones)) * (1.0 / group)
    xn = x * lax.rsqrt(ms + EPS) * gain
    reps = w // LANES
    c = jnp.concatenate([cos] * reps, axis=1) if reps > 1 else cos
    s = jnp.concatenate([sin] * reps, axis=1) if reps > 1 else sin
    lane = lax.broadcasted_iota(jnp.int32, x.shape, 1)
    first = (lane % (2 * nf)) < nf
    rot = jnp.where(first, -pltpu.roll(xn, w - nf, 1), pltpu.roll(xn, nf, 1))
    return (xn * c + rot * s) * out_scale


def _qkv_kernel(gq_ref, gkv_ref, dq_ref, dk_ref, dv_ref, cg_ref, sg_ref, cd_ref, sd_ref,
                o64_ref, o32_ref, gqn_ref, gkn_ref, dqn_ref, dkn_ref,
                qg_ref, kg_ref, vag_ref, vbg_ref, qd_ref, kd_ref, vad_ref, vbd_ref):
    cg, sg, cd, sd = cg_ref[...], sg_ref[...], cd_ref[...], sd_ref[...]
    o64, o32 = o64_ref[...], o32_ref[...]
    q = _head_norm_rope(gq_ref[...], o64, HEAD_DIM, gqn_ref[...], cg, sg, HEAD_DIM // 4, LOG2E * HEAD_DIM ** -0.5)
    qg_ref[...] = q.astype(BF16)
    kv = gkv_ref[...]
    k = _head_norm_rope(kv[:, :LANES], o64[:LANES, :LANES], HEAD_DIM, gkn_ref[...], cg, sg, HEAD_DIM // 4, 1.0)
    v = kv[:, LANES:]
    kk = jnp.concatenate([k, k], axis=1)
    vv = jnp.concatenate([v, v], axis=1)
    quarter = lax.broadcasted_iota(jnp.int32, kk.shape, 1) // HEAD_DIM
    kr = pltpu.roll(kk, HEAD_DIM, 1)
    vr = pltpu.roll(vv, HEAD_DIM, 1)
    kg_ref[...] = jnp.where((quarter == 0) | (quarter == 3), kk, kr).astype(BF16)
    vag_ref[...] = jnp.where(quarter == 0, vv, jnp.where(quarter == 2, vr, 0.0)).astype(BF16)
    vbg_ref[...] = jnp.where(quarter == 1, vr, jnp.where(quarter == 3, vv, 0.0)).astype(BF16)
    qd = _head_norm_rope(dq_ref[...], o32, DIFF_D, dqn_ref[...], cd, sd, DIFF_D // 4, LOG2E * DIFF_D ** -0.5)
    qd_ref[...] = qd.astype(BF16)
    kd = _head_norm_rope(dk_ref[...], o32, DIFF_D, dkn_ref[...], cd, sd, DIFF_D // 4, 1.0)
    kd_ref[...] = kd.astype(BF16)
    vd = dv_ref[...]
    even = (lax.broadcasted_iota(jnp.int32, vd.shape, 1) // HEAD_DIM) % 2 == 0
    vad_ref[...] = jnp.where(even, vd, 0.0).astype(BF16)
    vbd_ref[...] = jnp.where(even, 0.0, vd).astype(BF16)


def _qkv_prep(u_small, gqn, gkn, dqn, dkn, tables, dims):
    n = u_small.shape[0]
    tm = dims["tm_small"]
    S = dims["S"]
    n_lat_tiles = dims["n_lat"] // tm
    per_seq = S // tm
    tmap = lambda i: (jnp.where(i < n_lat_tiles, i % per_seq, per_seq), 0)
    col = lambda c: pl.BlockSpec((tm, BRANCH_W), lambda i: (i, c))
    tab = pl.BlockSpec((tm, LANES), tmap)
    full = lambda a: pl.BlockSpec(a.shape, lambda i: (0,) * a.ndim)
    o64, o32 = _group_ones(BRANCH_W, HEAD_DIM), _group_ones(BRANCH_W, DIFF_D)
    gains = [jnp.tile(gqn, 4).reshape(1, 256), jnp.tile(gkn, 2).reshape(1, 128),
             jnp.tile(dqn.reshape(-1), 4).reshape(1, 256), jnp.tile(dkn.reshape(-1), 4).reshape(1, 256)]
    out = pl.BlockSpec((tm, BRANCH_W), lambda i: (i, 0))
    return pl.pallas_call(
        _qkv_kernel,
        grid=(n // tm,),
        in_specs=[col(5), col(6), col(7), col(8), col(9), tab, tab, tab, tab, full(o64), full(o32)]
        + [full(g) for g in gains],
        out_specs=[out] * 8,
        out_shape=[jax.ShapeDtypeStruct((n, BRANCH_W), BF16)] * 8,
        compiler_params=_cp(("arbitrary",)),
        name="qkv_prep",
    )(u_small, u_small, u_small, u_small, u_small, *tables, o64, o32, *gains)


def _lane_pick(lane_lo, a, b):
    return jnp.where(lane_lo, a, b)


def _flash_kernel(*refs, segs, tk, tq, diff, lam_init):
    q_ref, o_ref = refs[0], refs[-1]
    seg_refs = [refs[1 + 3 * i:4 + 3 * i] for i in range(len(segs))]
    aux_ref = refs[1 + 3 * len(segs)]
    lane = lax.broadcasted_iota(jnp.int32, (1, LANES), 1)
    lane_lo = lane < HEAD_DIM
    if diff:
        masks = [(lane >= g * DIFF_D) & (lane < (g + 1) * DIFF_D) for g in range(4)]
        acc_of = [0, 1, 0, 1]
        use_a = [True, True, False, False]
        n_acc = 2
    else:
        masks = [lane_lo, ~lane_lo]
        acc_of = [0, 0]
        use_a = [True, False]
        n_acc = 1
    R = len(masks)
    pairs = [slice(p * LANES, (p + 1) * LANES) for p in range(2)]
    qsts = [jnp.concatenate([jnp.where(m, q_ref[:, ps], jnp.zeros((tq, LANES), BF16)) for m in masks], axis=0)
            for ps in pairs]
    lo_hi = []
    for a in range(n_acc):
        rs = [r for r in range(R) if acc_of[r] == a]
        lo_hi.append(([r for r in rs if use_a[r]][0], [r for r in rs if not use_a[r]][0]))

    def rows(x, r):
        return x[r * tq:(r + 1) * tq]

    shift = jnp.max(aux_ref[5:6, :])

    def run(fixed):
        def chunk(kv_refs, t0, size, carry):
            k_ref, va_ref, vb_ref = kv_refs
            new = []
            for p, ps in enumerate(pairs):
                m_run, l_run, accs = carry[p]
                k = k_ref[pl.ds(t0, size), ps]
                s = lax.dot_general(qsts[p], k, (((1,), (1,)), ((), ())), preferred_element_type=F32)
                if fixed:
                    m_new = m_run
                    pr = jnp.exp2(s - shift)
                    l_new = l_run + jnp.sum(pr, axis=-1, keepdims=True)
                else:
                    m_new = jnp.maximum(m_run, jnp.max(s, axis=-1, keepdims=True))
                    alpha = jnp.exp2(m_run - m_new)
                    pr = jnp.exp2(s - m_new)
                    l_new = alpha * l_run + jnp.sum(pr, axis=-1, keepdims=True)
                prb = pr.astype(BF16)
                va = va_ref[pl.ds(t0, size), ps]
                vb = vb_ref[pl.ds(t0, size), ps]
                new_accs = []
                for a, (r_lo, r_hi) in enumerate(lo_hi):
                    upd = _dot(rows(prb, r_lo), va) + _dot(rows(prb, r_hi), vb)
                    if fixed:
                        new_accs.append(accs[a] + upd)
                    else:
                        al = _lane_pick(lane_lo, rows(alpha, r_lo), rows(alpha, r_hi))
                        new_accs.append(accs[a] * al + upd)
                new.append((m_new, l_new, tuple(new_accs)))
            return tuple(new)

        one = (jnp.full((R * tq, 1), -jnp.inf, F32), jnp.zeros((R * tq, 1), F32),
               tuple(jnp.zeros((tq, LANES), F32) for _ in range(n_acc)))
        carry = (one, one)
        for kv_refs, T in zip(seg_refs, segs):
            n_main = T // tk
            if n_main:
                carry = lax.fori_loop(
                    0, n_main, lambda c, cr, kv_refs=kv_refs: chunk(kv_refs, pl.multiple_of(c * tk, tk), tk, cr),
                    carry, unroll=2 if n_main % 2 == 0 else 1)
            if T - n_main * tk:
                carry = chunk(kv_refs, n_main * tk, T - n_main * tk, carry)

        outs = []
        for p in range(2):
            _, l_fin, accs = carry[p]
            inv_l = 1.0 / l_fin
            norm = [accs[a] * _lane_pick(lane_lo, rows(inv_l, r_lo), rows(inv_l, r_hi))
                    for a, (r_lo, r_hi) in enumerate(lo_hi)]
            if diff:
                aux = aux_ref[...]
                lam = (jnp.exp(jnp.sum(aux[0:1] * aux[1:2], axis=-1, keepdims=True))
                       - jnp.exp(jnp.sum(aux[2:3] * aux[3:4], axis=-1, keepdims=True)) + lam_init)
                o = norm[0] - lam * norm[1]
                sq = o * o
                s_lo = jnp.sum(jnp.where(lane_lo, sq, 0.0), axis=-1, keepdims=True)
                s_hi = jnp.sum(jnp.where(lane_lo, 0.0, sq), axis=-1, keepdims=True)
                ms = _lane_pick(lane_lo, s_lo, s_hi) * (1.0 / HEAD_DIM)
                o = o * lax.rsqrt(ms + EPS) * aux[4:5] * (1.0 - lam_init)
            else:
                o = norm[0]
            outs.append(o)
        o_ref[...] = jnp.concatenate(outs, axis=1).astype(o_ref.dtype)

    @pl.when(shift < FIXED_SHIFT_LIMIT)
    def _():
        run(True)

    @pl.when(jnp.logical_not(shift < FIXED_SHIFT_LIMIT))
    def _():
        run(False)


def _into(prev, args, in_specs):
    if prev is None:
        return args, in_specs, {}
    return args + [prev], in_specs + [pl.BlockSpec(memory_space=pl.ANY)], {len(args): 0}


def _flash(q, k, va, vb, aux, nb, Lq, q_base, segs, diff, lam_init, out_rows, out_base, prev=None):
    tq = min(512, Lq)
    tk = 512
    nt = Lq // tq
    q_base, out_base = q_base // tq, out_base // tq
    args = [q]
    in_specs = [pl.BlockSpec((tq, BRANCH_W), lambda b, i: (q_base + b * nt + i, 0))]
    for length, base in segs:
        spec = pl.BlockSpec((length, BRANCH_W), lambda b, i, base=base: (base + b, 0))
        args += [k, va, vb]
        in_specs += [spec, spec, spec]
    args.append(aux)
    in_specs.append(pl.BlockSpec((8, LANES), lambda b, i: (0, 0)))
    args, in_specs, alias = _into(prev, args, in_specs)
    return pl.pallas_call(
        functools.partial(_flash_kernel, segs=tuple(s[0] for s in segs), tk=tk, tq=tq, diff=diff,
                          lam_init=lam_init),
        grid=(nb, nt),
        in_specs=in_specs,
        out_specs=pl.BlockSpec((tq, BRANCH_W), lambda b, i: (out_base + b * nt + i, 0)),
        out_shape=jax.ShapeDtypeStruct((out_rows, BRANCH_W), BF16),
        input_output_aliases=alias,
        compiler_params=_cp(("arbitrary", "arbitrary")),
        name="flash_diff" if diff else "flash_gqa",
    )(*args)


def _merge_kernel(y0, y1, y2, y3, g_ref, wb_ref, wo_ref, x_ref, mod_ref, o_ref):
    d = D_MODEL
    acc = g_ref[:, 0:d].astype(F32) * _dot(y0[...], wb_ref[0])
    acc = acc + g_ref[:, d:2 * d].astype(F32) * _dot(y1[...], wb_ref[1])
    acc = acc + g_ref[:, 2 * d:3 * d].astype(F32) * _dot(y2[...], wb_ref[2])
    acc = acc + g_ref[:, 3 * d:4 * d].astype(F32) * _dot(y3[...], wb_ref[3])
    mix = _dot(acc.astype(BF16), wo_ref[...])
    o_ref[...] = x_ref[...] + mod_ref[0][2:3, :] * mix


def _merge(ys, gate, wb, wo, x, mods_l, dims):
    n, d = ys[0].shape[0], x.shape[1]
    tm = dims["tm_small"]
    mrow = dims["mod_row"](tm)
    yspec = pl.BlockSpec((tm, BRANCH_W), lambda i: (i, 0))
    return pl.pallas_call(
        _merge_kernel,
        grid=(n // tm,),
        in_specs=[yspec] * 4 + [pl.BlockSpec((tm, GATE_COLS), lambda i: (i, 0)),
                                pl.BlockSpec(wb.shape, lambda i: (0, 0, 0)),
                                pl.BlockSpec(wo.shape, lambda i: (0, 0)),
                                pl.BlockSpec((tm, d), lambda i: (i, 0)),
                                pl.BlockSpec((1, 6, d), lambda i: (mrow(i), 0, 0))],
        out_specs=pl.BlockSpec((tm, d), lambda i: (i, 0)),
        out_shape=jax.ShapeDtypeStruct((n, d), F32),
        compiler_params=_cp(("arbitrary",)),
        name="merge",
    )(*ys, gate, wb, wo, x, mods_l)


def _ffn_kernel(x_ref, mod_ref, g_ref, w1_ref, w3_ref, w2_ref, o_ref, h_scr, acc_scr):
    j = pl.program_id(1)

    @pl.when(j == 0)
    def _():
        m = mod_ref[0]
        h_scr[...] = _norm_mod(x_ref[...], g_ref[...], m[3:4, :], m[4:5, :]).astype(BF16)
        acc_scr[...] = jnp.zeros_like(acc_scr)

    h = h_scr[...]
    a = _dot(h, w1_ref[...])
    b = _dot(h, w3_ref[...])
    t = (a * _sigmoid(a) * b).astype(BF16)
    acc_scr[...] += _dot(t, w2_ref[...])

    @pl.when(j == pl.num_programs(1) - 1)
    def _():
        o_ref[...] = x_ref[...] + mod_ref[0][5:6, :] * acc_scr[...]


def _ffn(x, mods_l, gain, w1, w3, w2, dims):
    n, d = x.shape
    tm, tf = dims["tm_big"], 256
    mrow = dims["mod_row"](tm)
    return pl.pallas_call(
        _ffn_kernel,
        grid=(n // tm, w1.shape[1] // tf),
        in_specs=[pl.BlockSpec((tm, d), lambda i, j: (i, 0)),
                  pl.BlockSpec((1, 6, d), lambda i, j: (mrow(i), 0, 0)),
                  pl.BlockSpec((1, d), lambda i, j: (0, 0)),
                  pl.BlockSpec((d, tf), lambda i, j: (0, j)),
                  pl.BlockSpec((d, tf), lambda i, j: (0, j)),
                  pl.BlockSpec((tf, d), lambda i, j: (j, 0))],
        out_specs=pl.BlockSpec((tm, d), lambda i, j: (i, 0)),
        out_shape=jax.ShapeDtypeStruct((n, d), F32),
        scratch_shapes=[pltpu.VMEM((tm, d), BF16), pltpu.VMEM((tm, d), F32)],
        compiler_params=_cp(("arbitrary", "arbitrary")),
        name="ffn_dense",
    )(x, mods_l, gain, w1, w3, w2)


def _route_kernel(x_ref, mod_ref, g_ref, wr_ref, tok_ref, gate_ref):
    m = mod_ref[0]
    h = _norm_mod(x_ref[...], g_ref[...], m[3:4, :], m[4:5, :])
    tok_ref[...] = h.astype(BF16)
    logits = _dot3(h, wr_ref[...])
    lane = lax.broadcasted_iota(jnp.int32, logits.shape, 1)
    lg = jnp.where(lane < N_EXPERTS, logits, -jnp.inf)
    m1 = jnp.max(lg, axis=-1, keepdims=True)
    i1 = jnp.min(jnp.where(lg == m1, lane, LANES), axis=-1, keepdims=True)
    lg2 = jnp.where(lane == i1, -jnp.inf, lg)
    m2 = jnp.max(lg2, axis=-1, keepdims=True)
    i2 = jnp.min(jnp.where(lg2 == m2, lane, LANES), axis=-1, keepdims=True)
    e2 = jnp.exp(m2 - m1)
    g1 = 1.0 / (1.0 + e2)
    gate_ref[...] = jnp.where(lane == i1, g1, jnp.where(lane == i2, e2 * g1, 0.0))


def _route(x, mods_l, gain, w_router, dims):
    n, d = x.shape
    tm = dims["tm_small"]
    mrow = dims["mod_row"](tm)
    wr = jnp.pad(w_router, ((0, 0), (0, LANES - N_EXPERTS)))
    return pl.pallas_call(
        _route_kernel,
        grid=(n // tm,),
        in_specs=[pl.BlockSpec((tm, d), lambda i: (i, 0)),
                  pl.BlockSpec((1, 6, d), lambda i: (mrow(i), 0, 0)),
                  pl.BlockSpec((1, d), lambda i: (0, 0)),
                  pl.BlockSpec((d, LANES), lambda i: (0, 0))],
        out_specs=[pl.BlockSpec((tm, d), lambda i: (i, 0)), pl.BlockSpec((tm, LANES), lambda i: (i, 0))],
        out_shape=[jax.ShapeDtypeStruct((n, d), BF16), jax.ShapeDtypeStruct((n, LANES), F32)],
        compiler_params=_cp(("arbitrary",)),
        name="moe_route",
    )(x, mods_l, gain, wr)


MOE_SUB = 128
MOE_GATHER = 256
MOE_GRAN = 64


def _moe_kernel(tok_ref, gate_ref, tri_ref, w1_ref, w3_ref, w2_ref, x_ref, mod_ref, o_ref,
                rank_scr, rank_t_scr, mask_t_scr, xe_scr, ye_scr, cnt_smem, *, tm):
    e = pl.program_id(1)
    j = pl.program_id(2)
    ne = pl.num_programs(1)
    nj = pl.num_programs(2)
    lane = lax.broadcasted_iota(jnp.int32, (1, LANES), 1)

    @pl.when((e == 0) & (j == 0))
    def _():
        tri = tri_ref[...]
        carry = jnp.zeros((1, LANES), F32)
        for blk in range(tm // 256):
            rs = slice(blk * 256, (blk + 1) * 256)
            msk = (gate_ref[rs, :] > 0.0).astype(BF16)
            rank_scr[rs, :] = _dot(tri, msk) + carry
            carry = carry + jnp.sum(msk.astype(F32), axis=0, keepdims=True)
        rank_t_scr[...] = jnp.transpose(rank_scr[...])
        mask_t_scr[...] = jnp.transpose((gate_ref[...] > 0.0).astype(F32))
        for ee in range(N_EXPERTS):
            cnt_smem[ee] = jnp.sum(jnp.where(lane == ee, carry, 0.0)).astype(jnp.int32)
        o_ref[...] = jnp.zeros_like(o_ref)
        ye_scr[...] = jnp.zeros_like(ye_scr)

    cnt = cnt_smem[e]

    n_sub = (cnt + (MOE_GATHER - 1)) // MOE_GATHER

    @pl.when(j == 0)
    def _():
        rk = rank_t_scr[pl.ds(e, 1), :]
        mk = mask_t_scr[pl.ds(e, 1), :]

        def gather(sb, carry):
            r0 = pl.multiple_of(sb * MOE_GATHER, MOE_GATHER)
            r_iota = lax.broadcasted_iota(jnp.int32, (MOE_GATHER, tm), 0) + r0
            sel = jnp.where((rk == r_iota.astype(F32)) & (mk > 0.0), 1.0, 0.0).astype(BF16)
            xe_scr[pl.ds(r0, MOE_GATHER), :] = _dot(sel, tok_ref[...]).astype(BF16)
            return carry

        lax.fori_loop(0, n_sub, gather, 0)

    def expert_rows(r0, size):
        xe = xe_scr[pl.ds(r0, size), :]
        a = _dot(xe, w1_ref[0])
        b = _dot(xe, w3_ref[0])
        t = (a * _sigmoid(a) * b).astype(BF16)
        y = _dot(t, w2_ref[0])
        ye_scr[pl.ds(r0, size), :] = jnp.where(j == 0, y, ye_scr[pl.ds(r0, size), :] + y)

    n_gran = (cnt + (MOE_GRAN - 1)) // MOE_GRAN
    n_big = n_gran // 4
    rem = n_gran % 4

    def big(i, carry):
        expert_rows(pl.multiple_of(i * 256, 256), 256)
        return carry

    lax.fori_loop(0, n_big, big, 0)

    @pl.when(rem >= 2)
    def _():
        expert_rows(pl.multiple_of(n_big * 256, 128), 128)

    @pl.when(rem % 2 == 1)
    def _():
        expert_rows(pl.multiple_of(n_big * 256 + (rem // 2) * 128, 64), 64)

    @pl.when(j == nj - 1)
    def _():
        def scatter(rc, carry):
            r0 = pl.multiple_of(rc * 256, 256)
            gt = gate_ref[pl.ds(r0, 256), :]
            g_e = jnp.sum(jnp.where(lane == e, gt, 0.0), axis=-1, keepdims=True)
            r_e = jnp.sum(jnp.where(lane == e, rank_scr[pl.ds(r0, 256), :], 0.0), axis=-1, keepdims=True)
            lo = jnp.min(jnp.where(g_e > 0.0, r_e, 1e9)).astype(jnp.int32)
            hi = jnp.max(jnp.where(g_e > 0.0, r_e, -1.0)).astype(jnp.int32)

            def scatter_sub(sb, c2):
                c0 = pl.multiple_of(sb * MOE_SUB, MOE_SUB)
                c_iota = lax.broadcasted_iota(jnp.int32, (256, MOE_SUB), 1) + c0
                sel_t = jnp.where((r_e == c_iota.astype(F32)) & (g_e > 0.0), 1.0, 0.0).astype(BF16)
                ye = ye_scr[pl.ds(c0, MOE_SUB), :].astype(BF16)
                o_ref[pl.ds(r0, 256), :] += g_e * _dot(sel_t, ye)
                return c2

            lax.fori_loop(lo // MOE_SUB, (hi + MOE_SUB) // MOE_SUB, scatter_sub, 0)

            @pl.when(e == ne - 1)
            def _():
                o_ref[pl.ds(r0, 256), :] = (x_ref[pl.ds(r0, 256), :]
                                            + mod_ref[0][5:6, :] * o_ref[pl.ds(r0, 256), :])
            return carry

        lax.fori_loop(0, tm // 256, scatter, 0)


def _moe(tok, gates, w1, w3, w2, x, mods_l, dims):
    n, d = tok.shape
    tm = dims["tm_moe"] if n % dims["tm_moe"] == 0 else dims["tm_big"]
    mrow = dims["mod_row"](tm)
    ne, _, ff = w1.shape
    tf = 896
    tri =jnp.asarray(np.tril(np.ones((256, 256)), -1), BF16)
    once = pl.Buffered(1)
    return pl.pallas_call(
        functools.partial(_moe_kernel, tm=tm),
        grid=(n // tm, ne, ff // tf),
        in_specs=[pl.BlockSpec((tm, d), lambda i, e, j: (i, 0), pipeline_mode=once),
                  pl.BlockSpec((tm, LANES), lambda i, e, j: (i, 0), pipeline_mode=once),
                  pl.BlockSpec((256, 256), lambda i, e, j: (0, 0), pipeline_mode=once),
                  pl.BlockSpec((1, d, tf), lambda i, e, j: (e, 0, j)),
                  pl.BlockSpec((1, d, tf), lambda i, e, j: (e, 0, j)),
                  pl.BlockSpec((1, tf, d), lambda i, e, j: (e, j, 0)),
                  pl.BlockSpec((tm, d), lambda i, e, j: (i, 0), pipeline_mode=once),
                  pl.BlockSpec((1, 6, d), lambda i, e, j: (mrow(i), 0, 0))],
        out_specs=pl.BlockSpec((tm, d), lambda i, e, j: (i, 0), pipeline_mode=once),
        out_shape=jax.ShapeDtypeStruct((n, d), F32),
        scratch_shapes=[pltpu.VMEM((tm, LANES), F32), pltpu.VMEM((LANES, tm), F32), pltpu.VMEM((LANES, tm), F32),
                        pltpu.VMEM((tm, d), BF16), pltpu.VMEM((tm, d), F32), pltpu.SMEM((N_EXPERTS,), jnp.int32)],
        compiler_params=_cp(("arbitrary", "arbitrary", "arbitrary"), VMEM_LIMIT_MOE),
        name="moe_experts",
    )(tok, gates, tri, w1, w3, w2, x, mods_l)


def _make_dims(B, S, n_ctx):
    n_lat, n = B * S, B * (S + n_ctx)

    def pick(prefs):
        for t in prefs:
            if S % t == 0 and n_lat % t == 0 and (n - n_lat) % t == 0:
                return t
        raise ValueError("no row tile divides the latent and context token counts")

    def mod_row(tm):
        return lambda i: jnp.minimum((i * tm) // S, B)

    return {"B": B, "S": S, "n_ctx": n_ctx, "n_lat": n_lat, "n": n,
            "tm_big": pick((1024, 512, 256)), "tm_small": pick((512, 256)), "tm_moe": pick((2048, 1024, 512, 256)),
            "mod_row": mod_row}


def kernel(x, c, ctx, c_ctx, w_ada, b_ada, norm_mix, norm_ffn, w_in, hy_short_w, hy_short_b, hy_w1, hy_b1, hy_w2, hy_b2, hy_w3, hy_b3, hy_w4, hy_freq, hy_bias, cf_dw_w, cf_dw_b, cf_ln_g, cf_ln_b, gqa_qn, gqa_kn, diff_qn, diff_kn, diff_lq1, diff_lk1, diff_lq2, diff_lk2, diff_subln, w_branch, w_out, ffn_w1, ffn_w3, ffn_w2, moe_router, moe_w1, moe_w3, moe_w2):
    B, S, D = x.shape
    n_ctx = ctx.shape[1]
    depth = w_ada.shape[0]
    dims = _make_dims(B, S, n_ctx)
    n_lat = dims["n_lat"]
    tm_s = dims["tm_small"]

    c16 = jnp.concatenate([c, c_ctx[None, :], jnp.zeros((16 - B - 1, D), F32)], axis=0)
    mods = _ada_mods(c16, w_ada, b_ada).reshape(depth, 16, 6, D)

    xa = jnp.concatenate([x.reshape(n_lat, D), ctx.reshape(B * n_ctx, D)], axis=0)
    tables = _rope_tables(S, tm_s, HEAD_DIM) + _rope_tables(S, tm_s, DIFF_D)
    fft_lat = _fft_consts(S)
    mats_ctx = _dft_mats(n_ctx)

    for l in range(depth):
        lam_init = 0.8 - 0.6 * math.exp(-0.3 * l)
        u_small, gate = _in_proj(xa, mods[l], norm_mix[l].reshape(1, D), w_in[l].astype(BF16), dims)

        hy_p = (hy_short_w[l], hy_short_b[l], hy_w1[l], hy_b1[l], hy_w2[l], hy_b2[l], hy_w3[l], hy_b3[l],
                hy_w4[l], hy_freq[l], hy_bias[l])
        cf_p = (cf_dw_w[l], cf_dw_b[l], cf_ln_g[l], cf_ln_b[l])
        last = l == depth - 1
        n_out = n_lat if last else dims["n"]
        ctx_base = n_lat // n_ctx
        y_hy = _hyena_fft(u_small, S, B, hy_p, fft_lat, n_out)
        y_cf = _conformer(u_small, S, B, 0, cf_p, n_out)
        if not last:
            y_hy = _hyena(u_small, n_ctx, B, ctx_base, hy_p, mats_ctx, n_out, y_hy)
            y_cf = _conformer(u_small, n_ctx, B, ctx_base, cf_p, n_out, y_cf)

        qg, kg, vag, vbg, qd, kd, vad, vbd = _qkv_prep(u_small, gqa_qn[l], gqa_kn[l], diff_qn[l], diff_kn[l],
                                                      tables, dims)

        aux = jnp.zeros((8, LANES), F32)
        aux = aux.at[0, :DIFF_D].set(diff_lq1[l]).at[1, :DIFF_D].set(diff_lk1[l])
        aux = aux.at[2, :DIFF_D].set(diff_lq2[l]).at[3, :DIFF_D].set(diff_lk2[l])
        aux = aux.at[4, :].set(jnp.tile(diff_subln[l], 2))

        def score_bound(qn, kn, d):
            return 1.02 * d * jnp.max(jnp.abs(qn)) * jnp.max(jnp.abs(kn)) * (LOG2E * d ** -0.5) + 0.1

        bounds = (score_bound(gqa_qn[l], gqa_kn[l], HEAD_DIM), score_bound(diff_qn[l], diff_kn[l], DIFF_D))
        y_att = []
        for (q, k, va, vb, is_diff) in ((qg, kg, vag, vbg, False), (qd, kd, vad, vbd, True)):
            aux = aux.at[5, :].set(bounds[int(is_diff)])
            y = _flash(q, k, va, vb, aux, B, S, 0, ((S, 0), (n_ctx, ctx_base)), is_diff, lam_init, n_out, 0)
            if not last:
                y = _flash(q, k, va, vb, aux, B, n_ctx, n_lat, ((n_ctx, ctx_base),), is_diff, lam_init,
                           n_out, n_lat, y)
            y_att.append(y)

        xa = _merge((y_hy, y_cf, y_att[0], y_att[1]), gate, w_branch[l].astype(BF16), w_out[l].astype(BF16),
                    xa, mods[l], dims)

        i = l // 2
        if l % 2 == 0:
            xa = _ffn(xa, mods[l], norm_ffn[l].reshape(1, D), ffn_w1[i].astype(BF16), ffn_w3[i].astype(BF16),
                      ffn_w2[i].astype(BF16), dims)
        else:
            tok, gates = _route(xa, mods[l], norm_ffn[l].reshape(1, D), moe_router[i], dims)
            xa = _moe(tok, gates, moe_w1[i].astype(BF16), moe_w3[i].astype(BF16), moe_w2[i].astype(BF16),
                      xa, mods[l], dims)
    return xa[:n_lat].reshape(B, S, D)
```

```python
import functools
import math

import numpy as np
import jax
import jax.numpy as jnp
from jax import lax
from jax.experimental import pallas as pl
from jax.experimental.pallas import tpu as pltpu

F32 = jnp.float32
BF16 = jnp.bfloat16

D_MODEL = 1024
GRID_W = 64
BRANCH_W = 256
EPS = 1e-6
HY_EMB = 33
HY_EMB_PAD = 128
HY_FFN = 64
HY_TARGET = 1e-2
HY_FAST = 0.3
HY_SLOW = 1.5
CF_WIDTH = 31
HEAD_DIM = 64
DIFF_D = 32
ROPE_BASE = 10000.0
N_EXPERTS = 8
LANES = 128
LOG2E = 1.4426950408889634
FIXED_SHIFT_LIMIT = 60.0
SMALL_COLS = 2560
GATE_COLS = 4 * D_MODEL
VMEM_LIMIT = 48 * 1024 * 1024
VMEM_LIMIT_MOE = 56 * 1024 * 1024


def _cp(sem, vmem=VMEM_LIMIT):
    return pltpu.CompilerParams(dimension_semantics=sem, vmem_limit_bytes=vmem)


def _sigmoid(x):
    return 1.0 / (1.0 + jnp.exp(-x))


def _dot(a, b):
    return jnp.dot(a, b, preferred_element_type=F32)


def _split(a):
    hi = a.astype(BF16)
    lo = (a - hi.astype(F32)).astype(BF16)
    return hi, lo


def _dot3(a, b):
    ah, al = _split(a)
    bh, bl = _split(b)
    return _dot(ah, bh) + (_dot(al, bh) + _dot(ah, bl))


def _norm_mod(x, gain, shift, scale):
    ms = jnp.mean(x * x, axis=-1, keepdims=True)
    return (x * lax.rsqrt(ms + EPS) * gain) * (1.0 + scale) + shift


def _ada_kernel(c_ref, w_ref, b_ref, o_ref):
    c = c_ref[...]
    s = c * _sigmoid(c)
    o_ref[0] = _dot3(s, w_ref[0]) + b_ref[0]


def _ada_mods(c16, w_ada, b_ada):
    depth, d, n6 = w_ada.shape
    tn = 512
    return pl.pallas_call(
        _ada_kernel,
        grid=(depth, n6 // tn),
        in_specs=[pl.BlockSpec((16, d), lambda l, j: (0, 0)),
                  pl.BlockSpec((1, d, tn), lambda l, j: (l, 0, j)),
                  pl.BlockSpec((1, 1, tn), lambda l, j: (l, 0, j))],
        out_specs=pl.BlockSpec((1, 16, tn), lambda l, j: (l, 0, j)),
        out_shape=jax.ShapeDtypeStruct((depth, 16, n6), F32),
        compiler_params=_cp(("arbitrary", "arbitrary")),
        name="ada_mods",
    )(c16, w_ada, b_ada.reshape(depth, 1, n6))


def _in_proj_kernel(x_ref, mod_ref, g_ref, w_ref, o_ref, h_scr, *, gate):
    @pl.when(pl.program_id(1) == 0)
    def _():
        m = mod_ref[0]
        h_scr[...] = _norm_mod(x_ref[...], g_ref[...], m[0:1, :], m[1:2, :]).astype(BF16)

    r = _dot(h_scr[...], w_ref[...])
    o_ref[...] = _sigmoid(r).astype(o_ref.dtype) if gate else r


def _in_proj(x, mods_l, gain, w_bf, dims):
    n, d = x.shape
    tm = dims["tm_big"]
    mrow = dims["mod_row"](tm)

    def call(w, tn, gate, dtype):
        n_cols = w.shape[1]
        return pl.pallas_call(
            functools.partial(_in_proj_kernel, gate=gate),
            grid=(n // tm, n_cols // tn),
            in_specs=[pl.BlockSpec((tm, d), lambda i, j: (i, 0)),
                      pl.BlockSpec((1, 6, d), lambda i, j: (mrow(i), 0, 0)),
                      pl.BlockSpec((1, d), lambda i, j: (0, 0)),
                      pl.BlockSpec((d, tn), lambda i, j: (0, j))],
            out_specs=pl.BlockSpec((tm, tn), lambda i, j: (i, j)),
            out_shape=jax.ShapeDtypeStruct((n, n_cols), dtype),
            scratch_shapes=[pltpu.VMEM((tm, d), BF16)],
            compiler_params=_cp(("arbitrary", "arbitrary")),
            name="in_proj_gate" if gate else "in_proj_mix",
        )(x, mods_l, gain, w)

    return (call(w_bf[:, :SMALL_COLS], SMALL_COLS // 2, False, F32),
            call(w_bf[:, SMALL_COLS:], GATE_COLS // 4, True, BF16))


def _short_conv_kernel(u_ref, w_ref, b_ref, o_ref, pad_scr, *, L, tc, planes):
    if planes:
        o_ref = o_ref.at[0]
    pad_scr[0:8, :] = jnp.zeros((8, BRANCH_W), F32)
    pad_scr[8 + L:16 + L, :] = jnp.zeros((8, BRANCH_W), F32)
    pad_scr[8:8 + L, :] = u_ref[...]
    w = w_ref[0]
    b = b_ref[0]

    def body(c, carry):
        t0 = pl.multiple_of(c * tc, tc)
        win = pad_scr[pl.ds(t0, tc + 16), :]
        acc = b + w[0:1, :] * win[7:7 + tc]
        acc = acc + w[1:2, :] * win[8:8 + tc]
        acc = acc + w[2:3, :] * win[9:9 + tc]
        o_ref[pl.ds(t0, tc), :] = acc
        return carry

    lax.fori_loop(0, L // tc, body, 0)


def _short_conv(u_small, w, b, L, nseq, row_base, planes=False):
    w3 = jnp.transpose(w.reshape(3, 3, BRANCH_W), (1, 0, 2))
    w3 = jnp.pad(w3, ((0, 0), (0, 5), (0, 0)))
    b3 = b.reshape(3, 1, BRANCH_W)
    tc = min(L, 256)
    return pl.pallas_call(
        functools.partial(_short_conv_kernel, L=L, tc=tc, planes=planes),
        grid=(nseq, 3),
        in_specs=[pl.BlockSpec((L, BRANCH_W), lambda s, j: (row_base + s, j)),
                  pl.BlockSpec((1, 8, BRANCH_W), lambda s, j: (j, 0, 0)),
                  pl.BlockSpec((1, 1, BRANCH_W), lambda s, j: (j, 0, 0))],
        out_specs=(pl.BlockSpec((1, L, BRANCH_W), lambda s, j: (j, s, 0)) if planes
                   else pl.BlockSpec((L, BRANCH_W), lambda s, j: (s, j))),
        out_shape=jax.ShapeDtypeStruct((3, nseq * L, BRANCH_W) if planes else (nseq * L, 3 * BRANCH_W), F32),
        scratch_shapes=[pltpu.VMEM((L + 16, BRANCH_W), F32)],
        compiler_params=_cp(("arbitrary", "arbitrary")),
        name="hy_short_conv",
    )(u_small, w3, b3)


def _filter_consts(L):
    t = np.linspace(0.0, 1.0, L)[:, None]
    bands = (HY_EMB - 1) // 2
    fr = np.linspace(1e-4, bands - 1, bands)[None, :]
    wpos = 2.0 * math.pi * np.arange(L)[:, None] / L
    z = np.concatenate([t, np.cos(fr * wpos), -np.sin(fr * wpos)], axis=-1)
    z = np.pad(z, ((0, 0), (0, HY_EMB_PAD - HY_EMB)))
    deltas = np.abs(np.linspace(math.log(HY_TARGET) / HY_SLOW, math.log(HY_TARGET) / HY_FAST, BRANCH_W))
    win = np.exp(-t * deltas[None, :])
    return jnp.asarray(z, F32), jnp.asarray(win, F32)


def _filter_kernel(z_ref, w1, b1, w2, b2, w3, b3, w4, fq, win_ref, hf_ref, ss_ref, *, tr, planes):
    i = pl.program_id(0)
    f = fq[...]
    a = jnp.sin(f * (_dot3(z_ref[...], w1[...]) + b1[...]))
    a = jnp.sin(f * (_dot3(a, w2[...]) + b2[...]))
    a = jnp.sin(f * (_dot3(a, w3[...]) + b3[...]))
    h = _dot3(a, w4[...])
    win = win_ref[...]
    h = h * jnp.concatenate([win, win, win, win], axis=1)
    row = lax.broadcasted_iota(jnp.int32, h.shape, 0) + i * tr
    col = lax.broadcasted_iota(jnp.int32, h.shape, 1)
    h = jnp.where((row == 0) & (col >= 2 * BRANCH_W), 0.0, h)
    if planes:
        for d in range(4):
            hf_ref[d] = h[:, d * BRANCH_W:(d + 1) * BRANCH_W]
    else:
        hf_ref[...] = h

    @pl.when(i == 0)
    def _():
        ss_ref[...] = jnp.zeros_like(ss_ref)

    ss_ref[...] += jnp.sum(h * h, axis=0, keepdims=True)


def _hyena_filter(L, w1, b1, w2, b2, w3, b3, w4, freq, planes=False):
    z, win = _filter_consts(L)
    tr = min(L, 256)
    w1p = jnp.pad(w1, ((0, HY_EMB_PAD - HY_EMB), (0, 0)))
    full = lambda a: pl.BlockSpec(a.shape, lambda i: (0,) * a.ndim)
    args = [w1p, b1.reshape(1, -1), w2, b2.reshape(1, -1), w3, b3.reshape(1, -1), w4, freq.reshape(1, -1)]
    n_out = w4.shape[1]
    return pl.pallas_call(
        functools.partial(_filter_kernel, tr=tr, planes=planes),
        grid=(L // tr,),
        in_specs=[pl.BlockSpec((tr, HY_EMB_PAD), lambda i: (i, 0))] + [full(a) for a in args]
        + [pl.BlockSpec((tr, BRANCH_W), lambda i: (i, 0))],
        out_specs=[pl.BlockSpec((4, tr, BRANCH_W), lambda i: (0, i, 0)) if planes
                   else pl.BlockSpec((tr, n_out), lambda i: (i, 0)),
                   pl.BlockSpec((1, n_out), lambda i: (0, 0))],
        out_shape=[jax.ShapeDtypeStruct((4, L, BRANCH_W) if planes else (L, n_out), F32),
                   jax.ShapeDtypeStruct((1, n_out), F32)],
        compiler_params=_cp(("arbitrary",)),
        name="hy_filter",
    )(z, *args, win)


def _dft_mats(L):
    N = 2 * L
    blk = 64
    k = jnp.arange(L, dtype=jnp.int32)[:, None]
    nh = jnp.arange(L // blk, dtype=jnp.int32)[None, :]
    nl = jnp.arange(blk, dtype=jnp.int32)[None, :]
    w = 2.0 * math.pi / N
    a = ((k * (blk * nh)) % N).astype(F32) * w
    b = ((k * nl) % N).astype(F32) * w
    ca, sa, cb, sb = jnp.cos(a), jnp.sin(a), jnp.cos(b), jnp.sin(b)
    cos = (ca[:, :, None] * cb[:, None, :] - sa[:, :, None] * sb[:, None, :]).reshape(L, L)
    sin = (sa[:, :, None] * cb[:, None, :] + ca[:, :, None] * sb[:, None, :]).reshape(L, L)
    alt = jnp.where(jnp.arange(L) % 2 == 0, 1.0, -1.0).astype(F32)
    first = (jnp.arange(L) == 0)
    s_f = jnp.where(first[:, None], alt[None, :], -sin)
    fwd = jnp.concatenate([cos, s_f], axis=0).astype(BF16)
    colscale = jnp.where(first, 1.0 / N, 2.0 / N).astype(F32)
    g_c = cos * colscale[None, :]
    g_s = jnp.where(first[None, :], alt[:, None] / N, -sin * (2.0 / N))
    inv = jnp.concatenate([g_c, g_s], axis=1).astype(BF16)
    return fwd, inv


def _dft_fwd_kernel(f_ref, z_ref, o_ref):
    o_ref[0] = _dot(f_ref[...], z_ref[...].astype(BF16))


def _dft_fwd(fwd, z2d, L, nb, zmap):
    tm = min(2 * L, 1024)
    return pl.pallas_call(
        _dft_fwd_kernel,
        grid=(2 * L // tm, nb),
        in_specs=[pl.BlockSpec((tm, L), lambda i, b: (i, 0)),
                  pl.BlockSpec((L, BRANCH_W), lambda i, b: zmap(b))],
        out_specs=pl.BlockSpec((1, tm, BRANCH_W), lambda i, b: (b, i, 0)),
        out_shape=jax.ShapeDtypeStruct((nb, 2 * L, BRANCH_W), F32),
        compiler_params=_cp(("arbitrary", "arbitrary")),
        name="hy_dft_fwd",
    )(fwd, z2d)


def _spec_prod_kernel(z_ref, kf_ref, kb_ref, sf_ref, sb_ref, y_ref, *, tk):
    i = pl.program_id(1)
    s = lax.rsqrt(sf_ref[...] + sb_ref[...] + EPS)
    zr, zi = z_ref[0, 0], z_ref[0, 1]
    fr, fi = kf_ref[0, 0], kf_ref[0, 1]
    br, bi = kb_ref[0, 0], kb_ref[0, 1]
    row = lax.broadcasted_iota(jnp.int32, (tk, BRANCH_W), 0) + i * tk
    first = row == 0
    kr = (fr + br) * s
    ki = jnp.where(first, fi + bi, fi - bi) * s
    yr = jnp.where(first, zr * kr, zr * kr - zi * ki)
    yi = jnp.where(first, zi * ki, zr * ki + zi * kr)
    y_ref[0, 0] = yr.astype(BF16)
    y_ref[0, 1] = yi.astype(BF16)


def _spec_prod(zf, kfs, ss, order, L, nb):
    tk = min(L, 512)
    zf4 = zf.reshape(nb, 2, L, BRANCH_W)
    kf4 = kfs.reshape(4, 2, L, BRANCH_W)
    y = pl.pallas_call(
        functools.partial(_spec_prod_kernel, tk=tk),
        grid=(nb, L // tk),
        in_specs=[pl.BlockSpec((1, 2, tk, BRANCH_W), lambda b, i: (b, 0, i, 0)),
                  pl.BlockSpec((1, 2, tk, BRANCH_W), lambda b, i: (order, 0, i, 0)),
                  pl.BlockSpec((1, 2, tk, BRANCH_W), lambda b, i: (2 + order, 0, i, 0)),
                  pl.BlockSpec((1, BRANCH_W), lambda b, i: (0, order)),
                  pl.BlockSpec((1, BRANCH_W), lambda b, i: (0, 2 + order))],
        out_specs=pl.BlockSpec((1, 2, tk, BRANCH_W), lambda b, i: (b, 0, i, 0)),
        out_shape=jax.ShapeDtypeStruct((nb, 2, L, BRANCH_W), BF16),
        compiler_params=_cp(("arbitrary", "arbitrary")),
        name="hy_spec_prod",
    )(zf4, kf4, kf4, ss, ss)
    return y.reshape(nb, 2 * L, BRANCH_W)


def _dft_inv_kernel(g_ref, y_ref, gate_ref, zp_ref, bias_ref, *rest):
    o_ref = rest[-1]
    conv = _dot(g_ref[...], y_ref[0])
    o_ref[...] = (gate_ref[...] * (conv + bias_ref[...] * zp_ref[...])).astype(o_ref.dtype)


def _dft_inv(inv, y, xs, gate_col, zprev, zprev_col, bias, L, nb, out_dtype, out_rows=None, out_base=0,
             prev=None):
    tm = min(L, 512)
    nt = L // tm
    args = [inv, y, xs, zprev, bias]
    in_specs = [pl.BlockSpec((tm, 2 * L), lambda i, b: (i, 0)),
                pl.BlockSpec((1, 2 * L, BRANCH_W), lambda i, b: (b, 0, 0)),
                pl.BlockSpec((tm, BRANCH_W), lambda i, b: (b * nt + i, gate_col)),
                pl.BlockSpec((tm, BRANCH_W), lambda i, b: (b * nt + i, zprev_col)),
                pl.BlockSpec((1, BRANCH_W), lambda i, b: (0, 0))]
    args, in_specs, alias = _into(prev, args, in_specs)
    return pl.pallas_call(
        _dft_inv_kernel,
        grid=(nt, nb),
        in_specs=in_specs,
        out_specs=pl.BlockSpec((tm, BRANCH_W), lambda i, b: ((out_base + b) * nt + i, 0)),
        out_shape=jax.ShapeDtypeStruct((out_rows or nb * L, BRANCH_W), out_dtype),
        input_output_aliases=alias,
        compiler_params=_cp(("arbitrary", "arbitrary")),
        name="hy_dft_inv",
    )(*args)


def _hyena(u_small, L, nseq, row_base, p, mats, out_rows, prev=None):
    (short_w, short_b, w1, b1, w2, b2, w3, b3, w4, freq, bias) = p
    fwd, inv = mats
    xs = _short_conv(u_small, short_w, short_b, L, nseq, row_base)
    hf, ss = _hyena_filter(L, w1, b1, w2, b2, w3, b3, w4, freq)
    kfs = _dft_fwd(fwd, hf, L, 4, lambda b: (0, b))
    zf = _dft_fwd(fwd, xs, L, nseq, lambda b: (b, 2))
    y = _spec_prod(zf, kfs, ss, 0, L, nseq)
    z1 = _dft_inv(inv, y, xs, 0, xs, 2, bias[0:1], L, nseq, F32)
    zf = _dft_fwd(fwd, z1, L, nseq, lambda b: (b, 0))
    y = _spec_prod(zf, kfs, ss, 1, L, nseq)
    return _dft_inv(inv, y, xs, 1, z1, 0, bias[1:2], L, nseq, BF16, out_rows, row_base, prev)


FFT_N1 = 64
FFT_KB = 8


def _fft_consts(L):
    N = 2 * L
    n2s = N // FFT_N1
    h = FFT_N1 // 2
    k1 = jnp.arange(FFT_N1, dtype=jnp.int32)
    ang_a = ((k1[:, None] * jnp.arange(h, dtype=jnp.int32)[None, :]) % FFT_N1).astype(F32) * (2.0 * math.pi / FFT_N1)
    ca, sa = jnp.cos(ang_a), jnp.sin(ang_a)
    fa = jnp.concatenate([ca, -sa], axis=0).astype(BF16)
    fai = (jnp.concatenate([ca.T, -sa.T], axis=1) / N).astype(BF16)
    k = k1[:, None, None] + FFT_N1 * jnp.arange(n2s, dtype=jnp.int32)[None, :, None]
    n2 = jnp.arange(n2s, dtype=jnp.int32)[None, None, :]
    ang = ((k * n2) % N).astype(F32) * (2.0 * math.pi / N)
    c, s = jnp.cos(ang), jnp.sin(ang)
    g = jnp.concatenate([jnp.concatenate([c, s], axis=2), jnp.concatenate([-s, c], axis=2)], axis=1)
    ct, st = jnp.swapaxes(c, 1, 2), jnp.swapaxes(s, 1, 2)
    gi = jnp.concatenate([jnp.concatenate([ct, -st], axis=2), jnp.concatenate([st, ct], axis=2)], axis=1)
    return fa, fai, g.astype(BF16), gi.astype(BF16)


def _fft_a_kernel(f_ref, z_ref, o_ref):
    o_ref[...] = _dot(f_ref[...], z_ref[0].astype(BF16)).astype(o_ref.dtype)


def _fft_a(fa, z3, zmap, nb, L):
    h = FFT_N1 // 2
    cols = (L // h) * BRANCH_W
    tn = min(cols, 4096)
    return pl.pallas_call(
        _fft_a_kernel,
        grid=(nb, cols // tn),
        in_specs=[pl.BlockSpec(fa.shape, lambda b, j: (0, 0)),
                  pl.BlockSpec((1, h, tn), lambda b, j: zmap(b) + (j,))],
        out_specs=pl.BlockSpec((2 * FFT_N1, tn), lambda b, j: (b, j)),
        out_shape=jax.ShapeDtypeStruct((nb * 2 * FFT_N1, cols), BF16),
        compiler_params=_cp(("arbitrary", "arbitrary")),
        name="hy_fft_a",
    )(fa, z3)


def _fft_spec_kernel(a_ref, g_ref, o_ref):
    for j in range(FFT_KB):
        s = _dot(g_ref[j], jnp.concatenate([a_ref[0, 0, j], a_ref[0, 1, j]], axis=0))
        half = s.shape[0] // 2
        o_ref[0, j, 0] = s[:half]
        o_ref[0, j, 1] = s[half:]


def _fft_spec(g, a5):
    nb, _, n1s, n2s, w = a5.shape
    return pl.pallas_call(
        _fft_spec_kernel,
        grid=(n1s // FFT_KB, nb),
        in_specs=[pl.BlockSpec((1, 2, FFT_KB, n2s, w), lambda i, b: (b, 0, i, 0, 0)),
                  pl.BlockSpec((FFT_KB, 2 * n2s, 2 * n2s), lambda i, b: (i, 0, 0))],
        out_specs=pl.BlockSpec((1, FFT_KB, 2, n2s, w), lambda i, b: (b, i, 0, 0, 0)),
        out_shape=jax.ShapeDtypeStruct((nb, n1s, 2, n2s, w), F32),
        compiler_params=_cp(("arbitrary", "arbitrary")),
        name="hy_fft_spec",
    )(a5, g)


def _fft_conv_kernel(a_ref, g_ref, gi_ref, sf_ref, sb_ref, ssf_ref, ssb_ref, o_ref):
    scale = lax.rsqrt(ssf_ref[...] + ssb_ref[...] + EPS)
    for j in range(FFT_KB):
        s = _dot(g_ref[j], jnp.concatenate([a_ref[0, 0, j], a_ref[0, 1, j]], axis=0))
        half = s.shape[0] // 2
        sr, si = s[:half], s[half:]
        kr = (sf_ref[0, j, 0] + sb_ref[0, j, 0]) * scale
        ki = (sf_ref[0, j, 1] - sb_ref[0, j, 1]) * scale
        y = jnp.concatenate([sr * kr - si * ki, sr * ki + si * kr], axis=0).astype(BF16)
        t = _dot(gi_ref[j], y)
        o_ref[0, 0, j] = t[:half].astype(BF16)
        o_ref[0, 1, j] = t[half:].astype(BF16)


def _fft_conv(g, gi, a5, kspec, ss, order):
    nb, _, n1s, n2s, w = a5.shape
    kblk = lambda d: pl.BlockSpec((1, FFT_KB, 2, n2s, w), lambda i, b: (d, i, 0, 0, 0))
    gblk = pl.BlockSpec((FFT_KB, 2 * n2s, 2 * n2s), lambda i, b: (i, 0, 0))
    ablk = pl.BlockSpec((1, 2, FFT_KB, n2s, w), lambda i, b: (b, 0, i, 0, 0))
    return pl.pallas_call(
        _fft_conv_kernel,
        grid=(n1s // FFT_KB, nb),
        in_specs=[ablk, gblk, gblk, kblk(order), kblk(2 + order),
                  pl.BlockSpec((1, w), lambda i, b: (0, order)),
                  pl.BlockSpec((1, w), lambda i, b: (0, 2 + order))],
        out_specs=ablk,
        out_shape=jax.ShapeDtypeStruct(a5.shape, BF16),
        compiler_params=_cp(("arbitrary", "arbitrary")),
        name="hy_fft_conv",
    )(a5, g, gi, kspec, kspec, ss, ss)


def _fft_ainv_kernel(f_ref, a_ref, gate_ref, zp_ref, bias_ref, o_ref):
    conv = _dot(f_ref[...], a_ref[...])
    o_ref[...] = (gate_ref[0] * (conv + bias_ref[...] * zp_ref[0])).astype(o_ref.dtype)


def _fft_ainv(fai, a2, gate3, gate_plane, zprev3, zprev_plane, bias, L, nb, out_dtype, out_view_rows):
    h = FFT_N1 // 2
    cols = (L // h) * BRANCH_W
    tn = min(cols, 4096)
    return pl.pallas_call(
        _fft_ainv_kernel,
        grid=(nb, cols // tn),
        in_specs=[pl.BlockSpec(fai.shape, lambda b, j: (0, 0)),
                  pl.BlockSpec((2 * FFT_N1, tn), lambda b, j: (b, j)),
                  pl.BlockSpec((1, h, tn), lambda b, j: (gate_plane, b, j)),
                  pl.BlockSpec((1, h, tn), lambda b, j: (zprev_plane, b, j)),
                  pl.BlockSpec((1, tn), lambda b, j: (0, 0))],
        out_specs=pl.BlockSpec((h, tn), lambda b, j: (b, j)),
        out_shape=jax.ShapeDtypeStruct((out_view_rows, cols), out_dtype),
        compiler_params=_cp(("arbitrary", "arbitrary")),
        name="hy_fft_ainv",
    )(fai, a2, gate3, zprev3, jnp.tile(bias, (1, tn // BRANCH_W)))


def _hyena_fft(u_small, L, nseq, p, consts, out_rows):
    (short_w, short_b, w1, b1, w2, b2, w3, b3, w4, freq, bias) = p
    fa, fai, g, gi = consts
    h = FFT_N1 // 2
    n2s = L // h
    cols = n2s * BRANCH_W
    xs3 = _short_conv(u_small, short_w, short_b, L, nseq, 0, planes=True).reshape(3, nseq * h, cols)
    hf4, ss = _hyena_filter(L, w1, b1, w2, b2, w3, b3, w4, freq, planes=True)
    ka = _fft_a(fa, hf4.reshape(4, h, cols), lambda b: (b, 0), 4, L)
    kspec = _fft_spec(g, ka.reshape(4, 2, FFT_N1, n2s, BRANCH_W))

    def conv(z3, zmap, order, gate_plane, zprev_plane, out_dtype, out_view_rows):
        a = _fft_a(fa, z3, zmap, nseq, L)
        a = _fft_conv(g, gi, a.reshape(nseq, 2, FFT_N1, n2s, BRANCH_W), kspec, ss, order)
        return _fft_ainv(fai, a.reshape(nseq * 2 * FFT_N1, cols), xs3, gate_plane, z3, zprev_plane,
                         bias[order:order + 1], L, nseq, out_dtype, out_view_rows)

    z1 = conv(xs3, lambda b: (2, b), 0, 0, 2, F32, nseq * h).reshape(1, nseq * h, cols)
    y = conv(z1, lambda b: (0, b), 1, 1, 0, BF16, out_rows // n2s)
    return y.reshape(out_rows, BRANCH_W)


def _conformer_kernel(a_ref, g_ref, w_ref, b_ref, lg_ref, lb_ref, *rest, L, tc):
    o_ref, pad_scr = rest[-2:]
    pad_scr[0:16, :] = jnp.zeros((16, BRANCH_W), F32)
    pad_scr[16 + L:32 + L, :] = jnp.zeros((16, BRANCH_W), F32)

    def glu(c, carry):
        t0 = pl.multiple_of(c * tc, tc)
        pad_scr[pl.ds(16 + t0, tc), :] = a_ref[pl.ds(t0, tc), :] * _sigmoid(g_ref[pl.ds(t0, tc), :])
        return carry

    lax.fori_loop(0, L // tc, glu, 0)
    b = b_ref[...]
    lg = lg_ref[...]
    lb = lb_ref[...]

    def body(c, carry):
        t0 = pl.multiple_of(c * tc, tc)
        win = pad_scr[pl.ds(t0, tc + 32), :]
        acc = jnp.zeros((tc, BRANCH_W), F32) + b
        for r in range(8):
            sh = win[r:r + tc + 24]
            for a in range(4):
                m = 8 * a + r
                if 1 <= m <= CF_WIDTH:
                    acc = acc + w_ref[m - 1:m, :] * sh[8 * a:8 * a + tc]
        mu = jnp.mean(acc, axis=-1, keepdims=True)
        xc = acc - mu
        var = jnp.mean(xc * xc, axis=-1, keepdims=True)
        y = xc * lax.rsqrt(var + EPS) * lg + lb
        o_ref[pl.ds(t0, tc), :] = (y * _sigmoid(y)).astype(o_ref.dtype)
        return carry

    lax.fori_loop(0, L // tc, body, 0)


def _conformer(u_small, L, nseq, row_base, p, out_rows, prev=None):
    dw_w, dw_b, ln_g, ln_b = p
    tc = 128
    wpad = jnp.pad(dw_w, ((0, 32 - CF_WIDTH), (0, 0)))
    row = lambda a: a.reshape(1, BRANCH_W)
    vec = pl.BlockSpec((1, BRANCH_W), lambda s: (0, 0))
    args = [u_small, u_small, wpad, row(dw_b), row(ln_g), row(ln_b)]
    in_specs = [pl.BlockSpec((L, BRANCH_W), lambda s: (row_base + s, 3)),
                pl.BlockSpec((L, BRANCH_W), lambda s: (row_base + s, 4)),
                pl.BlockSpec((32, BRANCH_W), lambda s: (0, 0)), vec, vec, vec]
    args, in_specs, alias = _into(prev, args, in_specs)
    return pl.pallas_call(
        functools.partial(_conformer_kernel, L=L, tc=tc),
        grid=(nseq,),
        in_specs=in_specs,
        out_specs=pl.BlockSpec((L, BRANCH_W), lambda s: (row_base + s, 0)),
        out_shape=jax.ShapeDtypeStruct((out_rows, BRANCH_W), BF16),
        input_output_aliases=alias,
        scratch_shapes=[pltpu.VMEM((L + 32, BRANCH_W), F32)],
        compiler_params=_cp(("arbitrary",)),
        name="conformer",
    )(*args)


def _rope_tables(S, pad_rows, head, scale_unused=None):
    half = head // 2
    nf = half // 2
    lane = np.arange(LANES)
    inv_lane = (ROPE_BASE ** (-(np.arange(nf)) / nf))[(lane % half) % nf]
    is_row = (lane % head) < half
    pos = jnp.arange(S, dtype=jnp.int32)
    rows = (pos // GRID_W).astype(F32)[:, None]
    cols = (pos % GRID_W).astype(F32)[:, None]
    ang = jnp.where(jnp.asarray(is_row)[None, :], rows, cols) * jnp.asarray(inv_lane, F32)[None, :]
    cos = jnp.concatenate([jnp.cos(ang), jnp.ones((pad_rows, LANES), F32)], axis=0)
    sin = jnp.concatenate([jnp.sin(ang), jnp.zeros((pad_rows, LANES), F32)], axis=0)
    return cos, sin


def _group_ones(width, group):
    idx = np.arange(width)
    return jnp.asarray((idx[:, None] // group) == (idx[None, :] // group), BF16)


def _head_norm_rope(x, ones, group, gain, cos, sin, nf, out_scale):
    w = x.shape[1]
    hi, lo = _split(x * x)
    ms = (_dot(hi, ones) + _dot(lo, ones)) * (1.0 / group)
    xn = x * lax.rsqrt(ms + EPS) * gain
    reps = w // LANES
    c = jnp.concatenate([cos] * reps, axis=1) if reps > 1 else cos
    s = jnp.concatenate([sin] * reps, axis=1) if reps > 1 else sin
    lane = lax.broadcasted_iota(jnp.int32, x.shape, 1)
    first = (lane % (2 * nf)) < nf
    rot = jnp.where(first, -pltpu.roll(xn, w - nf, 1), pltpu.roll(xn, nf, 1))
    return (xn * c + rot * s) * out_scale


def _qkv_kernel(gq_ref, gkv_ref, dq_ref, dk_ref, dv_ref, cg_ref, sg_ref, cd_ref, sd_ref,
                o64_ref, o32_ref, gqn_ref, gkn_ref, dqn_ref, dkn_ref,
                qg_ref, kg_ref, vag_ref, vbg_ref, qd_ref, kd_ref, vad_ref, vbd_ref):
    cg, sg, cd, sd = cg_ref[...], sg_ref[...], cd_ref[...], sd_ref[...]
    o64, o32 = o64_ref[...], o32_ref[...]
    q = _head_norm_rope(gq_ref[...], o64, HEAD_DIM, gqn_ref[...], cg, sg, HEAD_DIM // 4, LOG2E * HEAD_DIM ** -0.5)
    qg_ref[...] = q.astype(BF16)
    kv = gkv_ref[...]
    k = _head_norm_rope(kv[:, :LANES], o64[:LANES, :LANES], HEAD_DIM, gkn_ref[...], cg, sg, HEAD_DIM // 4, 1.0)
    v = kv[:, LANES:]
    kk = jnp.concatenate([k, k], axis=1)
    vv = jnp.concatenate([v, v], axis=1)
    quarter = lax.broadcasted_iota(jnp.int32, kk.shape, 1) // HEAD_DIM
    kr = pltpu.roll(kk, HEAD_DIM, 1)
    vr = pltpu.roll(vv, HEAD_DIM, 1)
    kg_ref[...] = jnp.where((quarter == 0) | (quarter == 3), kk, kr).astype(BF16)
    vag_ref[...] = jnp.where(quarter == 0, vv, jnp.where(quarter == 2, vr, 0.0)).astype(BF16)
    vbg_ref[...] = jnp.where(quarter == 1, vr, jnp.where(quarter == 3, vv, 0.0)).astype(BF16)
    qd = _head_norm_rope(dq_ref[...], o32, DIFF_D, dqn_ref[...], cd, sd, DIFF_D // 4, LOG2E * DIFF_D ** -0.5)
    qd_ref[...] = qd.astype(BF16)
    kd = _head_norm_rope(dk_ref[...], o32, DIFF_D, dkn_ref[...], cd, sd, DIFF_D // 4, 1.0)
    kd_ref[...] = kd.astype(BF16)
    vd = dv_ref[...]
    even = (lax.broadcasted_iota(jnp.int32, vd.shape, 1) // HEAD_DIM) % 2 == 0
    vad_ref[...] = jnp.where(even, vd, 0.0).astype(BF16)
    vbd_ref[...] = jnp.where(even, 0.0, vd).astype(BF16)


def _qkv_prep(u_small, gqn, gkn, dqn, dkn, tables, dims):
    n = u_small.shape[0]
    tm = dims["tm_small"]
    S = dims["S"]
    n_lat_tiles = dims["n_lat"] // tm
    per_seq = S // tm
    tmap = lambda i: (jnp.where(i < n_lat_tiles, i % per_seq, per_seq), 0)
    col = lambda c: pl.BlockSpec((tm, BRANCH_W), lambda i: (i, c))
    tab = pl.BlockSpec((tm, LANES), tmap)
    full = lambda a: pl.BlockSpec(a.shape, lambda i: (0,) * a.ndim)
    o64, o32 = _group_ones(BRANCH_W, HEAD_DIM), _group_ones(BRANCH_W, DIFF_D)
    gains = [jnp.tile(gqn, 4).reshape(1, 256), jnp.tile(gkn, 2).reshape(1, 128),
             jnp.tile(dqn.reshape(-1), 4).reshape(1, 256), jnp.tile(dkn.reshape(-1), 4).reshape(1, 256)]
    out = pl.BlockSpec((tm, BRANCH_W), lambda i: (i, 0))
    return pl.pallas_call(
        _qkv_kernel,
        grid=(n // tm,),
        in_specs=[col(5), col(6), col(7), col(8), col(9), tab, tab, tab, tab, full(o64), full(o32)]
        + [full(g) for g in gains],
        out_specs=[out] * 8,
        out_shape=[jax.ShapeDtypeStruct((n, BRANCH_W), BF16)] * 8,
        compiler_params=_cp(("arbitrary",)),
        name="qkv_prep",
    )(u_small, u_small, u_small, u_small, u_small, *tables, o64, o32, *gains)


def _lane_pick(lane_lo, a, b):
    return jnp.where(lane_lo, a, b)


def _flash_kernel(*refs, segs, tk, tq, diff, lam_init):
    q_ref, o_ref = refs[0], refs[-1]
    seg_refs = [refs[1 + 3 * i:4 + 3 * i] for i in range(len(segs))]
    aux_ref = refs[1 + 3 * len(segs)]
    lane = lax.broadcasted_iota(jnp.int32, (1, LANES), 1)
    lane_lo = lane < HEAD_DIM
    if diff:
        masks = [(lane >= g * DIFF_D) & (lane < (g + 1) * DIFF_D) for g in range(4)]
        acc_of = [0, 1, 0, 1]
        use_a = [True, True, False, False]
        n_acc = 2
    else:
        masks = [lane_lo, ~lane_lo]
        acc_of = [0, 0]
        use_a = [True, False]
        n_acc = 1
    R = len(masks)
    pairs = [slice(p * LANES, (p + 1) * LANES) for p in range(2)]
    qsts = [jnp.concatenate([jnp.where(m, q_ref[:, ps], jnp.zeros((tq, LANES), BF16)) for m in masks], axis=0)
            for ps in pairs]
    lo_hi = []
    for a in range(n_acc):
        rs = [r for r in range(R) if acc_of[r] == a]
        lo_hi.append(([r for r in rs if use_a[r]][0], [r for r in rs if not use_a[r]][0]))

    def rows(x, r):
        return x[r * tq:(r + 1) * tq]

    shift = jnp.max(aux_ref[5:6, :])

    def run(fixed):
        def chunk(kv_refs, t0, size, carry):
            k_ref, va_ref, vb_ref = kv_refs
            new = []
            for p, ps in enumerate(pairs):
                m_run, l_run, accs = carry[p]
                k = k_ref[pl.ds(t0, size), ps]
                s = lax.dot_general(qsts[p], k, (((1,), (1,)), ((), ())), preferred_element_type=F32)
                if fixed:
                    m_new = m_run
                    pr = jnp.exp2(s - shift)
                    l_new = l_run + jnp.sum(pr, axis=-1, keepdims=True)
                else:
                    m_new = jnp.maximum(m_run, jnp.max(s, axis=-1, keepdims=True))
                    alpha = jnp.exp2(m_run - m_new)
                    pr = jnp.exp2(s - m_new)
                    l_new = alpha * l_run + jnp.sum(pr, axis=-1, keepdims=True)
                prb = pr.astype(BF16)
                va = va_ref[pl.ds(t0, size), ps]
                vb = vb_ref[pl.ds(t0, size), ps]
                new_accs = []
                for a, (r_lo, r_hi) in enumerate(lo_hi):
                    upd = _dot(rows(prb, r_lo), va) + _dot(rows(prb, r_hi), vb)
                    if fixed:
                        new_accs.append(accs[a] + upd)
                    else:
                        al = _lane_pick(lane_lo, rows(alpha, r_lo), rows(alpha, r_hi))
                        new_accs.append(accs[a] * al + upd)
                new.append((m_new, l_new, tuple(new_accs)))
            return tuple(new)

        one = (jnp.full((R * tq, 1), -jnp.inf, F32), jnp.zeros((R * tq, 1), F32),
               tuple(jnp.zeros((tq, LANES), F32) for _ in range(n_acc)))
        carry = (one, one)
        for kv_refs, T in zip(seg_refs, segs):
            n_main = T // tk
            if n_main:
                carry = lax.fori_loop(
                    0, n_main, lambda c, cr, kv_refs=kv_refs: chunk(kv_refs, pl.multiple_of(c * tk, tk), tk, cr),
                    carry, unroll=2 if n_main % 2 == 0 else 1)
            if T - n_main * tk:
                carry = chunk(kv_refs, n_main * tk, T - n_main * tk, carry)

        outs = []
        for p in range(2):
            _, l_fin, accs = carry[p]
            inv_l = 1.0 / l_fin
            norm = [accs[a] * _lane_pick(lane_lo, rows(inv_l, r_lo), rows(inv_l, r_hi))
                    for a, (r_lo, r_hi) in enumerate(lo_hi)]
            if diff:
                aux = aux_ref[...]
                lam = (jnp.exp(jnp.sum(aux[0:1] * aux[1:2], axis=-1, keepdims=True))
                       - jnp.exp(jnp.sum(aux[2:3] * aux[3:4], axis=-1, keepdims=True)) + lam_init)
                o = norm[0] - lam * norm[1]
                sq = o * o
                s_lo = jnp.sum(jnp.where(lane_lo, sq, 0.0), axis=-1, keepdims=True)
                s_hi = jnp.sum(jnp.where(lane_lo, 0.0, sq), axis=-1, keepdims=True)
                ms = _lane_pick(lane_lo, s_lo, s_hi) * (1.0 / HEAD_DIM)
                o = o * lax.rsqrt(ms + EPS) * aux[4:5] * (1.0 - lam_init)
            else:
                o = norm[0]
            outs.append(o)
        o_ref[...] = jnp.concatenate(outs, axis=1).astype(o_ref.dtype)

    @pl.when(shift < FIXED_SHIFT_LIMIT)
    def _():
        run(True)

    @pl.when(jnp.logical_not(shift < FIXED_SHIFT_LIMIT))
    def _():
        run(False)


def _into(prev, args, in_specs):
    if prev is None:
        return args, in_specs, {}
    return args + [prev], in_specs + [pl.BlockSpec(memory_space=pl.ANY)], {len(args): 0}


def _flash(q, k, va, vb, aux, nb, Lq, q_base, segs, diff, lam_init, out_rows, out_base, prev=None):
    tq = min(512, Lq)
    tk = 512
    nt = Lq // tq
    q_base, out_base = q_base // tq, out_base // tq
    args = [q]
    in_specs = [pl.BlockSpec((tq, BRANCH_W), lambda b, i: (q_base + b * nt + i, 0))]
    for length, base in segs:
        spec = pl.BlockSpec((length, BRANCH_W), lambda b, i, base=base: (base + b, 0))
        args += [k, va, vb]
        in_specs += [spec, spec, spec]
    args.append(aux)
    in_specs.append(pl.BlockSpec((8, LANES), lambda b, i: (0, 0)))
    args, in_specs, alias = _into(prev, args, in_specs)
    return pl.pallas_call(
        functools.partial(_flash_kernel, segs=tuple(s[0] for s in segs), tk=tk, tq=tq, diff=diff,
                          lam_init=lam_init),
        grid=(nb, nt),
        in_specs=in_specs,
        out_specs=pl.BlockSpec((tq, BRANCH_W), lambda b, i: (out_base + b * nt + i, 0)),
        out_shape=jax.ShapeDtypeStruct((out_rows, BRANCH_W), BF16),
        input_output_aliases=alias,
        compiler_params=_cp(("arbitrary", "arbitrary")),
        name="flash_diff" if diff else "flash_gqa",
    )(*args)


def _merge_kernel(y0, y1, y2, y3, g_ref, wb_ref, wo_ref, x_ref, mod_ref, o_ref):
    d = D_MODEL
    acc = g_ref[:, 0:d].astype(F32) * _dot(y0[...], wb_ref[0])
    acc = acc + g_ref[:, d:2 * d].astype(F32) * _dot(y1[...], wb_ref[1])
    acc = acc + g_ref[:, 2 * d:3 * d].astype(F32) * _dot(y2[...], wb_ref[2])
    acc = acc + g_ref[:, 3 * d:4 * d].astype(F32) * _dot(y3[...], wb_ref[3])
    mix = _dot(acc.astype(BF16), wo_ref[...])
    o_ref[...] = x_ref[...] + mod_ref[0][2:3, :] * mix


def _merge(ys, gate, wb, wo, x, mods_l, dims):
    n, d = ys[0].shape[0], x.shape[1]
    tm = dims["tm_small"]
    mrow = dims["mod_row"](tm)
    yspec = pl.BlockSpec((tm, BRANCH_W), lambda i: (i, 0))
    return pl.pallas_call(
        _merge_kernel,
        grid=(n // tm,),
        in_specs=[yspec] * 4 + [pl.BlockSpec((tm, GATE_COLS), lambda i: (i, 0)),
                                pl.BlockSpec(wb.shape, lambda i: (0, 0, 0)),
                                pl.BlockSpec(wo.shape, lambda i: (0, 0)),
                                pl.BlockSpec((tm, d), lambda i: (i, 0)),
                                pl.BlockSpec((1, 6, d), lambda i: (mrow(i), 0, 0))],
        out_specs=pl.BlockSpec((tm, d), lambda i: (i, 0)),
        out_shape=jax.ShapeDtypeStruct((n, d), F32),
        compiler_params=_cp(("arbitrary",)),
        name="merge",
    )(*ys, gate, wb, wo, x, mods_l)


def _ffn_kernel(x_ref, mod_ref, g_ref, w1_ref, w3_ref, w2_ref, o_ref, h_scr, acc_scr):
    j = pl.program_id(1)

    @pl.when(j == 0)
    def _():
        m = mod_ref[0]
        h_scr[...] = _norm_mod(x_ref[...], g_ref[...], m[3:4, :], m[4:5, :]).astype(BF16)
        acc_scr[...] = jnp.zeros_like(acc_scr)

    h = h_scr[...]
    a = _dot(h, w1_ref[...])
    b = _dot(h, w3_ref[...])
    t = (a * _sigmoid(a) * b).astype(BF16)
    acc_scr[...] += _dot(t, w2_ref[...])

    @pl.when(j == pl.num_programs(1) - 1)
    def _():
        o_ref[...] = x_ref[...] + mod_ref[0][5:6, :] * acc_scr[...]


def _ffn(x, mods_l, gain, w1, w3, w2, dims):
    n, d = x.shape
    tm, tf = dims["tm_big"], 256
    mrow = dims["mod_row"](tm)
    return pl.pallas_call(
        _ffn_kernel,
        grid=(n // tm, w1.shape[1] // tf),
        in_specs=[pl.BlockSpec((tm, d), lambda i, j: (i, 0)),
                  pl.BlockSpec((1, 6, d), lambda i, j: (mrow(i), 0, 0)),
                  pl.BlockSpec((1, d), lambda i, j: (0, 0)),
                  pl.BlockSpec((d, tf), lambda i, j: (0, j)),
                  pl.BlockSpec((d, tf), lambda i, j: (0, j)),
                  pl.BlockSpec((tf, d), lambda i, j: (j, 0))],
        out_specs=pl.BlockSpec((tm, d), lambda i, j: (i, 0)),
        out_shape=jax.ShapeDtypeStruct((n, d), F32),
        scratch_shapes=[pltpu.VMEM((tm, d), BF16), pltpu.VMEM((tm, d), F32)],
        compiler_params=_cp(("arbitrary", "arbitrary")),
        name="ffn_dense",
    )(x, mods_l, gain, w1, w3, w2)


def _route_kernel(x_ref, mod_ref, g_ref, wr_ref, tok_ref, gate_ref):
    m = mod_ref[0]
    h = _norm_mod(x_ref[...], g_ref[...], m[3:4, :], m[4:5, :])
    tok_ref[...] = h.astype(BF16)
    logits = _dot3(h, wr_ref[...])
    lane = lax.broadcasted_iota(jnp.int32, logits.shape, 1)
    lg = jnp.where(lane < N_EXPERTS, logits, -jnp.inf)
    m1 = jnp.max(lg, axis=-1, keepdims=True)
    i1 = jnp.min(jnp.where(lg == m1, lane, LANES), axis=-1, keepdims=True)
    lg2 = jnp.where(lane == i1, -jnp.inf, lg)
    m2 = jnp.max(lg2, axis=-1, keepdims=True)
    i2 = jnp.min(jnp.where(lg2 == m2, lane, LANES), axis=-1, keepdims=True)
    e2 = jnp.exp(m2 - m1)
    g1 = 1.0 / (1.0 + e2)
    gate_ref[...] = jnp.where(lane == i1, g1, jnp.where(lane == i2, e2 * g1, 0.0))


def _route(x, mods_l, gain, w_router, dims):
    n, d = x.shape
    tm = dims["tm_small"]
    mrow = dims["mod_row"](tm)
    wr = jnp.pad(w_router, ((0, 0), (0, LANES - N_EXPERTS)))
    return pl.pallas_call(
        _route_kernel,
        grid=(n // tm,),
        in_specs=[pl.BlockSpec((tm, d), lambda i: (i, 0)),
                  pl.BlockSpec((1, 6, d), lambda i: (mrow(i), 0, 0)),
                  pl.BlockSpec((1, d), lambda i: (0, 0)),
                  pl.BlockSpec((d, LANES), lambda i: (0, 0))],
        out_specs=[pl.BlockSpec((tm, d), lambda i: (i, 0)), pl.BlockSpec((tm, LANES), lambda i: (i, 0))],
        out_shape=[jax.ShapeDtypeStruct((n, d), BF16), jax.ShapeDtypeStruct((n, LANES), F32)],
        compiler_params=_cp(("arbitrary",)),
        name="moe_route",
    )(x, mods_l, gain, wr)


MOE_SUB = 128
MOE_GATHER = 256
MOE_GRAN = 64


def _moe_kernel(tok_ref, gate_ref, tri_ref, w1_ref, w3_ref, w2_ref, x_ref, mod_ref, o_ref,
                rank_scr, rank_t_scr, mask_t_scr, xe_scr, ye_scr, cnt_smem, *, tm):
    e = pl.program_id(1)
    j = pl.program_id(2)
    ne = pl.num_programs(1)
    nj = pl.num_programs(2)
    lane = lax.broadcasted_iota(jnp.int32, (1, LANES), 1)

    @pl.when((e == 0) & (j == 0))
    def _():
        tri = tri_ref[...]
        carry = jnp.zeros((1, LANES), F32)
        def put_counts(blk, counts):
            for ee in range(N_EXPERTS):
                cnt_smem[blk * N_EXPERTS + ee] = jnp.sum(jnp.where(lane == ee, counts, 0.0)).astype(jnp.int32)

        for blk in range(tm // 256):
            put_counts(blk, carry)
            rs = slice(blk * 256, (blk + 1) * 256)
            msk = (gate_ref[rs, :] > 0.0).astype(BF16)
            rank_scr[rs, :] = _dot(tri, msk) + carry
            carry = carry + jnp.sum(msk.astype(F32), axis=0, keepdims=True)
        rank_t_scr[...] = jnp.transpose(rank_scr[...])
        mask_t_scr[...] = jnp.transpose((gate_ref[...] > 0.0).astype(F32))
        put_counts(tm // 256, carry)
        o_ref[...] = jnp.zeros_like(o_ref)
        ye_scr[...] = jnp.zeros_like(ye_scr)

    cnt = cnt_smem[(tm // 256) * N_EXPERTS + e]

    n_sub = (cnt + (MOE_GATHER - 1)) // MOE_GATHER

    @pl.when(j == 0)
    def _():
        rk = rank_t_scr[pl.ds(e, 1), :]
        mk = mask_t_scr[pl.ds(e, 1), :]

        def gather(sb, carry):
            r0 = pl.multiple_of(sb * MOE_GATHER, MOE_GATHER)
            r_iota = lax.broadcasted_iota(jnp.int32, (MOE_GATHER, tm), 0) + r0
            sel = jnp.where((rk == r_iota.astype(F32)) & (mk > 0.0), 1.0, 0.0).astype(BF16)
            xe_scr[pl.ds(r0, MOE_GATHER), :] = _dot(sel, tok_ref[...]).astype(BF16)
            return carry

        lax.fori_loop(0, n_sub, gather, 0)

    def expert_rows(r0, size):
        xe = xe_scr[pl.ds(r0, size), :]
        a = _dot(xe, w1_ref[0])
        b = _dot(xe, w3_ref[0])
        t = (a * _sigmoid(a) * b).astype(BF16)
        y = _dot(t, w2_ref[0])
        ye_scr[pl.ds(r0, size), :] = jnp.where(j == 0, y, ye_scr[pl.ds(r0, size), :] + y)

    n_gran = (cnt + (MOE_GRAN - 1)) // MOE_GRAN
    n_big = n_gran // 4
    rem = n_gran % 4

    def big(i, carry):
        expert_rows(pl.multiple_of(i * 256, 256), 256)
        return carry

    lax.fori_loop(0, n_big - 1, big, 0)
    for r in range(min(4, (tm - 256) // MOE_GRAN + 1)):
        @pl.when((n_big >= 1) & (rem == r))
        def _():
            expert_rows(pl.multiple_of((n_big - 1) * 256, 256), 256 + r * MOE_GRAN)

    @pl.when((n_big == 0) & (rem >= 2))
    def _():
        expert_rows(0, 128)

    @pl.when((n_big == 0) & (rem % 2 == 1))
    def _():
        expert_rows(pl.multiple_of((rem // 2) * 128, 128), 64)

    @pl.when(j == nj - 1)
    def _():
        def scatter(rc, carry):
            r0 = pl.multiple_of(rc * 256, 256)
            gt = gate_ref[pl.ds(r0, 256), :]
            g_e = jnp.sum(jnp.where(lane == e, gt, 0.0), axis=-1, keepdims=True)
            r_e = jnp.sum(jnp.where(lane == e, rank_scr[pl.ds(r0, 256), :], 0.0), axis=-1, keepdims=True)
            lo = cnt_smem[rc * N_EXPERTS + e]
            hi = cnt_smem[(rc + 1) * N_EXPERTS + e]

            def scatter_sub(sb, c2):
                c0 = pl.multiple_of(sb * MOE_SUB, MOE_SUB)
                c_iota = lax.broadcasted_iota(jnp.int32, (256, MOE_SUB), 1) + c0
                sel_t = jnp.where((r_e == c_iota.astype(F32)) & (g_e > 0.0), 1.0, 0.0).astype(BF16)
                ye = ye_scr[pl.ds(c0, MOE_SUB), :].astype(BF16)
                o_ref[pl.ds(r0, 256), :] += g_e * _dot(sel_t, ye)
                return c2

            lax.fori_loop(lo // MOE_SUB, (hi + (MOE_SUB - 1)) // MOE_SUB, scatter_sub, 0)

            @pl.when(e == ne - 1)
            def _():
                o_ref[pl.ds(r0, 256), :] = (x_ref[pl.ds(r0, 256), :]
                                            + mod_ref[0][5:6, :] * o_ref[pl.ds(r0, 256), :])
            return carry

        lax.fori_loop(0, tm // 256, scatter, 0)


def _moe(tok, gates, w1, w3, w2, x, mods_l, dims):
    n, d = tok.shape
    tm = dims["tm_moe"] if n % dims["tm_moe"] == 0 else dims["tm_big"]
    mrow = dims["mod_row"](tm)
    ne, _, ff = w1.shape
    tf = 896
    tri =jnp.asarray(np.tril(np.ones((256, 256)), -1), BF16)
    once = pl.Buffered(1)
    return pl.pallas_call(
        functools.partial(_moe_kernel, tm=tm),
        grid=(n // tm, ne, ff // tf),
        in_specs=[pl.BlockSpec((tm, d), lambda i, e, j: (i, 0), pipeline_mode=once),
                  pl.BlockSpec((tm, LANES), lambda i, e, j: (i, 0), pipeline_mode=once),
                  pl.BlockSpec((256, 256), lambda i, e, j: (0, 0), pipeline_mode=once),
                  pl.BlockSpec((1, d, tf), lambda i, e, j: (e, 0, j)),
                  pl.BlockSpec((1, d, tf), lambda i, e, j: (e, 0, j)),
                  pl.BlockSpec((1, tf, d), lambda i, e, j: (e, j, 0)),
                  pl.BlockSpec((tm, d), lambda i, e, j: (i, 0), pipeline_mode=once),
                  pl.BlockSpec((1, 6, d), lambda i, e, j: (mrow(i), 0, 0))],
        out_specs=pl.BlockSpec((tm, d), lambda i, e, j: (i, 0), pipeline_mode=once),
        out_shape=jax.ShapeDtypeStruct((n, d), F32),
        scratch_shapes=[pltpu.VMEM((tm, LANES), F32), pltpu.VMEM((LANES, tm), F32), pltpu.VMEM((LANES, tm), F32),
                        pltpu.VMEM((tm, d), BF16), pltpu.VMEM((tm, d), F32), pltpu.SMEM(((tm // 256 + 1) * N_EXPERTS,), jnp.int32)],
        compiler_params=_cp(("arbitrary", "arbitrary", "arbitrary"), VMEM_LIMIT_MOE),
        name="moe_experts",
    )(tok, gates, tri, w1, w3, w2, x, mods_l)


def _make_dims(B, S, n_ctx):
    n_lat, n = B * S, B * (S + n_ctx)

    def pick(prefs):
        for t in prefs:
            if S % t == 0 and n_lat % t == 0 and (n - n_lat) % t == 0:
                return t
        raise ValueError("no row tile divides the latent and context token counts")

    def mod_row(tm):
        return lambda i: jnp.minimum((i * tm) // S, B)

    return {"B": B, "S": S, "n_ctx": n_ctx, "n_lat": n_lat, "n": n,
            "tm_big": pick((1024, 512, 256)), "tm_small": pick((512, 256)), "tm_moe": pick((2048, 1024, 512, 256)),
            "mod_row": mod_row}


def kernel(x, c, ctx, c_ctx, w_ada, b_ada, norm_mix, norm_ffn, w_in, hy_short_w, hy_short_b, hy_w1, hy_b1, hy_w2, hy_b2, hy_w3, hy_b3, hy_w4, hy_freq, hy_bias, cf_dw_w, cf_dw_b, cf_ln_g, cf_ln_b, gqa_qn, gqa_kn, diff_qn, diff_kn, diff_lq1, diff_lk1, diff_lq2, diff_lk2, diff_subln, w_branch, w_out, ffn_w1, ffn_w3, ffn_w2, moe_router, moe_w1, moe_w3, moe_w2):
    B, S, D = x.shape
    n_ctx = ctx.shape[1]
    depth = w_ada.shape[0]
    dims = _make_dims(B, S, n_ctx)
    n_lat = dims["n_lat"]
    tm_s = dims["tm_small"]

    c16 = jnp.concatenate([c, c_ctx[None, :], jnp.zeros((16 - B - 1, D), F32)], axis=0)
    mods = _ada_mods(c16, w_ada, b_ada).reshape(depth, 16, 6, D)

    xa = jnp.concatenate([x.reshape(n_lat, D), ctx.reshape(B * n_ctx, D)], axis=0)
    tables = _rope_tables(S, tm_s, HEAD_DIM) + _rope_tables(S, tm_s, DIFF_D)
    fft_lat = _fft_consts(S)
    mats_ctx = _dft_mats(n_ctx)

    for l in range(depth):
        lam_init = 0.8 - 0.6 * math.exp(-0.3 * l)
        u_small, gate = _in_proj(xa, mods[l], norm_mix[l].reshape(1, D), w_in[l].astype(BF16), dims)

        hy_p = (hy_short_w[l], hy_short_b[l], hy_w1[l], hy_b1[l], hy_w2[l], hy_b2[l], hy_w3[l], hy_b3[l],
                hy_w4[l], hy_freq[l], hy_bias[l])
        cf_p = (cf_dw_w[l], cf_dw_b[l], cf_ln_g[l], cf_ln_b[l])
        last = l == depth - 1
        n_out = n_lat if last else dims["n"]
        ctx_base = n_lat // n_ctx
        y_hy = _hyena_fft(u_small, S, B, hy_p, fft_lat, n_out)
        y_cf = _conformer(u_small, S, B, 0, cf_p, n_out)
        if not last:
            y_hy = _hyena(u_small, n_ctx, B, ctx_base, hy_p, mats_ctx, n_out, y_hy)
            y_cf = _conformer(u_small, n_ctx, B, ctx_base, cf_p, n_out, y_cf)

        qg, kg, vag, vbg, qd, kd, vad, vbd = _qkv_prep(u_small, gqa_qn[l], gqa_kn[l], diff_qn[l], diff_kn[l],
                                                      tables, dims)

        aux = jnp.zeros((8, LANES), F32)
        aux = aux.at[0, :DIFF_D].set(diff_lq1[l]).at[1, :DIFF_D].set(diff_lk1[l])
        aux = aux.at[2, :DIFF_D].set(diff_lq2[l]).at[3, :DIFF_D].set(diff_lk2[l])
        aux = aux.at[4, :].set(jnp.tile(diff_subln[l], 2))

        def score_bound(qn, kn, d):
            return 1.02 * d * jnp.max(jnp.abs(qn)) * jnp.max(jnp.abs(kn)) * (LOG2E * d ** -0.5) + 0.1

        bounds = (score_bound(gqa_qn[l], gqa_kn[l], HEAD_DIM), score_bound(diff_qn[l], diff_kn[l], DIFF_D))
        y_att = []
        for (q, k, va, vb, is_diff) in ((qg, kg, vag, vbg, False), (qd, kd, vad, vbd, True)):
            aux = aux.at[5, :].set(bounds[int(is_diff)])
            y = _flash(q, k, va, vb, aux, B, S, 0, ((S, 0), (n_ctx, ctx_base)), is_diff, lam_init, n_out, 0)
            if not last:
                y = _flash(q, k, va, vb, aux, B, n_ctx, n_lat, ((n_ctx, ctx_base),), is_diff, lam_init,
                           n_out, n_lat, y)
            y_att.append(y)

        xa = _merge((y_hy, y_cf, y_att[0], y_att[1]), gate, w_branch[l].astype(BF16), w_out[l].astype(BF16),
                    xa, mods[l], dims)

        i = l // 2
        if l % 2 == 0:
            xa = _ffn(xa, mods[l], norm_ffn[l].reshape(1, D), ffn_w1[i].astype(BF16), ffn_w3[i].astype(BF16),
                      ffn_w2[i].astype(BF16), dims)
        else:
            tok, gates = _route(xa, mods[l], norm_ffn[l].reshape(1, D), moe_router[i], dims)
            xa = _moe(tok, gates, moe_w1[i].astype(BF16), moe_w3[i].astype(BF16), moe_w2[i].astype(BF16),
                      xa, mods[l], dims)
    return xa[:n_lat].reshape(B, S, D)
```

```python
import functools
import math

import numpy as np
import jax
import jax.numpy as jnp
from jax import lax
from jax.experimental import pallas as pl
from jax.experimental.pallas import tpu as pltpu

F32 = jnp.float32
BF16 = jnp.bfloat16

D_MODEL = 1024
GRID_W = 64
BRANCH_W = 256
EPS = 1e-6
HY_EMB = 33
HY_EMB_PAD = 128
HY_FFN = 64
HY_TARGET = 1e-2
HY_FAST = 0.3
HY_SLOW = 1.5
CF_WIDTH = 31
HEAD_DIM = 64
DIFF_D = 32
ROPE_BASE = 10000.0
N_EXPERTS = 8
LANES = 128
LOG2E = 1.4426950408889634
FIXED_SHIFT_LIMIT = 60.0
SMALL_COLS = 2560
GATE_COLS = 4 * D_MODEL
VMEM_LIMIT = 48 * 1024 * 1024
VMEM_LIMIT_MOE = 56 * 1024 * 1024


def _cp(sem, vmem=VMEM_LIMIT):
    return pltpu.CompilerParams(dimension_semantics=sem, vmem_limit_bytes=vmem)


def _sigmoid(x):
    return 1.0 / (1.0 + jnp.exp(-x))


def _dot(a, b):
    return jnp.dot(a, b, preferred_element_type=F32)


def _split(a):
    hi = a.astype(BF16)
    lo = (a - hi.astype(F32)).astype(BF16)
    return hi, lo


def _dot3(a, b):
    ah, al = _split(a)
    bh, bl = _split(b)
    return _dot(ah, bh) + (_dot(al, bh) + _dot(ah, bl))


def _norm_mod(x, gain, shift, scale):
    ms = jnp.mean(x * x, axis=-1, keepdims=True)
    return (x * lax.rsqrt(ms + EPS) * gain) * (1.0 + scale) + shift


def _ada_kernel(c_ref, w_ref, b_ref, o_ref):
    c = c_ref[...]
    s = c * _sigmoid(c)
    o_ref[0] = _dot3(s, w_ref[0]) + b_ref[0]


def _ada_mods(c16, w_ada, b_ada):
    depth, d, n6 = w_ada.shape
    tn = 512
    return pl.pallas_call(
        _ada_kernel,
        grid=(depth, n6 // tn),
        in_specs=[pl.BlockSpec((16, d), lambda l, j: (0, 0)),
                  pl.BlockSpec((1, d, tn), lambda l, j: (l, 0, j)),
                  pl.BlockSpec((1, 1, tn), lambda l, j: (l, 0, j))],
        out_specs=pl.BlockSpec((1, 16, tn), lambda l, j: (l, 0, j)),
        out_shape=jax.ShapeDtypeStruct((depth, 16, n6), F32),
        compiler_params=_cp(("arbitrary", "arbitrary")),
        name="ada_mods",
    )(c16, w_ada, b_ada.reshape(depth, 1, n6))


def _in_proj_kernel(x_ref, mod_ref, g_ref, w_ref, o_ref, h_scr, *, gate):
    @pl.when(pl.program_id(1) == 0)
    def _():
        m = mod_ref[0]
        h_scr[...] = _norm_mod(x_ref[...], g_ref[...], m[0:1, :], m[1:2, :]).astype(BF16)

    r = _dot(h_scr[...], w_ref[...])
    o_ref[...] = _sigmoid(r).astype(o_ref.dtype) if gate else r


def _in_proj(x, mods_l, gain, w_bf, dims):
    n, d = x.shape
    tm = dims["tm_big"]
    mrow = dims["mod_row"](tm)

    def call(w, tn, gate, dtype):
        n_cols = w.shape[1]
        return pl.pallas_call(
            functools.partial(_in_proj_kernel, gate=gate),
            grid=(n // tm, n_cols // tn),
            in_specs=[pl.BlockSpec((tm, d), lambda i, j: (i, 0)),
                      pl.BlockSpec((1, 6, d), lambda i, j: (mrow(i), 0, 0)),
                      pl.BlockSpec((1, d), lambda i, j: (0, 0)),
                      pl.BlockSpec((d, tn), lambda i, j: (0, j))],
            out_specs=pl.BlockSpec((tm, tn), lambda i, j: (i, j)),
            out_shape=jax.ShapeDtypeStruct((n, n_cols), dtype),
            scratch_shapes=[pltpu.VMEM((tm, d), BF16)],
            compiler_params=_cp(("arbitrary", "arbitrary")),
            name="in_proj_gate" if gate else "in_proj_mix",
        )(x, mods_l, gain, w)

    return (call(w_bf[:, :SMALL_COLS], SMALL_COLS // 2, False, F32),
            call(w_bf[:, SMALL_COLS:], GATE_COLS // 4, True, BF16))


def _short_conv_kernel(u_ref, w_ref, b_ref, o_ref, pad_scr, *, L, tc, planes):
    if planes:
        o_ref = o_ref.at[0]
    pad_scr[0:8, :] = jnp.zeros((8, BRANCH_W), F32)
    pad_scr[8 + L:16 + L, :] = jnp.zeros((8, BRANCH_W), F32)
    pad_scr[8:8 + L, :] = u_ref[...]
    w = w_ref[0]
    b = b_ref[0]

    def body(c, carry):
        t0 = pl.multiple_of(c * tc, tc)
        win = pad_scr[pl.ds(t0, tc + 16), :]
        acc = b + w[0:1, :] * win[7:7 + tc]
        acc = acc + w[1:2, :] * win[8:8 + tc]
        acc = acc + w[2:3, :] * win[9:9 + tc]
        o_ref[pl.ds(t0, tc), :] = acc
        return carry

    lax.fori_loop(0, L // tc, body, 0)


def _short_conv(u_small, w, b, L, nseq, row_base, planes=False):
    w3 = jnp.transpose(w.reshape(3, 3, BRANCH_W), (1, 0, 2))
    w3 = jnp.pad(w3, ((0, 0), (0, 5), (0, 0)))
    b3 = b.reshape(3, 1, BRANCH_W)
    tc = min(L, 256)
    return pl.pallas_call(
        functools.partial(_short_conv_kernel, L=L, tc=tc, planes=planes),
        grid=(nseq, 3),
        in_specs=[pl.BlockSpec((L, BRANCH_W), lambda s, j: (row_base + s, j)),
                  pl.BlockSpec((1, 8, BRANCH_W), lambda s, j: (j, 0, 0)),
                  pl.BlockSpec((1, 1, BRANCH_W), lambda s, j: (j, 0, 0))],
        out_specs=(pl.BlockSpec((1, L, BRANCH_W), lambda s, j: (j, s, 0)) if planes
                   else pl.BlockSpec((L, BRANCH_W), lambda s, j: (s, j))),
        out_shape=jax.ShapeDtypeStruct((3, nseq * L, BRANCH_W) if planes else (nseq * L, 3 * BRANCH_W), F32),
        scratch_shapes=[pltpu.VMEM((L + 16, BRANCH_W), F32)],
        compiler_params=_cp(("arbitrary", "arbitrary")),
        name="hy_short_conv",
    )(u_small, w3, b3)


def _filter_consts(L):
    t = np.linspace(0.0, 1.0, L)[:, None]
    bands = (HY_EMB - 1) // 2
    fr = np.linspace(1e-4, bands - 1, bands)[None, :]
    wpos = 2.0 * math.pi * np.arange(L)[:, None] / L
    z = np.concatenate([t, np.cos(fr * wpos), -np.sin(fr * wpos)], axis=-1)
    z = np.pad(z, ((0, 0), (0, HY_EMB_PAD - HY_EMB)))
    deltas = np.abs(np.linspace(math.log(HY_TARGET) / HY_SLOW, math.log(HY_TARGET) / HY_FAST, BRANCH_W))
    win = np.exp(-t * deltas[None, :])
    return jnp.asarray(z, F32), jnp.asarray(win, F32)


def _filter_kernel(z_ref, w1, b1, w2, b2, w3, b3, w4, fq, win_ref, hf_ref, ss_ref, *, tr, planes):
    i = pl.program_id(0)
    f = fq[...]
    a = jnp.sin(f * (_dot3(z_ref[...], w1[...]) + b1[...]))
    a = jnp.sin(f * (_dot3(a, w2[...]) + b2[...]))
    a = jnp.sin(f * (_dot3(a, w3[...]) + b3[...]))
    h = _dot3(a, w4[...])
    win = win_ref[...]
    h = h * jnp.concatenate([win, win, win, win], axis=1)
    row = lax.broadcasted_iota(jnp.int32, h.shape, 0) + i * tr
    col = lax.broadcasted_iota(jnp.int32, h.shape, 1)
    h = jnp.where((row == 0) & (col >= 2 * BRANCH_W), 0.0, h)
    if planes:
        for d in range(4):
            hf_ref[d] = h[:, d * BRANCH_W:(d + 1) * BRANCH_W]
    else:
        hf_ref[...] = h

    @pl.when(i == 0)
    def _():
        ss_ref[...] = jnp.zeros_like(ss_ref)

    ss_ref[...] += jnp.sum(h * h, axis=0, keepdims=True)


def _hyena_filter(L, w1, b1, w2, b2, w3, b3, w4, freq, planes=False):
    z, win = _filter_consts(L)
    tr = min(L, 256)
    w1p = jnp.pad(w1, ((0, HY_EMB_PAD - HY_EMB), (0, 0)))
    full = lambda a: pl.BlockSpec(a.shape, lambda i: (0,) * a.ndim)
    args = [w1p, b1.reshape(1, -1), w2, b2.reshape(1, -1), w3, b3.reshape(1, -1), w4, freq.reshape(1, -1)]
    n_out = w4.shape[1]
    return pl.pallas_call(
        functools.partial(_filter_kernel, tr=tr, planes=planes),
        grid=(L // tr,),
        in_specs=[pl.BlockSpec((tr, HY_EMB_PAD), lambda i: (i, 0))] + [full(a) for a in args]
        + [pl.BlockSpec((tr, BRANCH_W), lambda i: (i, 0))],
        out_specs=[pl.BlockSpec((4, tr, BRANCH_W), lambda i: (0, i, 0)) if planes
                   else pl.BlockSpec((tr, n_out), lambda i: (i, 0)),
                   pl.BlockSpec((1, n_out), lambda i: (0, 0))],
        out_shape=[jax.ShapeDtypeStruct((4, L, BRANCH_W) if planes else (L, n_out), F32),
                   jax.ShapeDtypeStruct((1, n_out), F32)],
        compiler_params=_cp(("arbitrary",)),
        name="hy_filter",
    )(z, *args, win)


def _dft_mats(L):
    N = 2 * L
    blk = 64
    k = jnp.arange(L, dtype=jnp.int32)[:, None]
    nh = jnp.arange(L // blk, dtype=jnp.int32)[None, :]
    nl = jnp.arange(blk, dtype=jnp.int32)[None, :]
    w = 2.0 * math.pi / N
    a = ((k * (blk * nh)) % N).astype(F32) * w
    b = ((k * nl) % N).astype(F32) * w
    ca, sa, cb, sb = jnp.cos(a), jnp.sin(a), jnp.cos(b), jnp.sin(b)
    cos = (ca[:, :, None] * cb[:, None, :] - sa[:, :, None] * sb[:, None, :]).reshape(L, L)
    sin = (sa[:, :, None] * cb[:, None, :] + ca[:, :, None] * sb[:, None, :]).reshape(L, L)
    alt = jnp.where(jnp.arange(L) % 2 == 0, 1.0, -1.0).astype(F32)
    first = (jnp.arange(L) == 0)
    s_f = jnp.where(first[:, None], alt[None, :], -sin)
    fwd = jnp.concatenate([cos, s_f], axis=0).astype(BF16)
    colscale = jnp.where(first, 1.0 / N, 2.0 / N).astype(F32)
    g_c = cos * colscale[None, :]
    g_s = jnp.where(first[None, :], alt[:, None] / N, -sin * (2.0 / N))
    inv = jnp.concatenate([g_c, g_s], axis=1).astype(BF16)
    return fwd, inv


def _dft_fwd_kernel(f_ref, z_ref, o_ref):
    o_ref[0] = _dot(f_ref[...], z_ref[...].astype(BF16))


def _dft_fwd(fwd, z2d, L, nb, zmap):
    tm = min(2 * L, 1024)
    return pl.pallas_call(
        _dft_fwd_kernel,
        grid=(2 * L // tm, nb),
        in_specs=[pl.BlockSpec((tm, L), lambda i, b: (i, 0)),
                  pl.BlockSpec((L, BRANCH_W), lambda i, b: zmap(b))],
        out_specs=pl.BlockSpec((1, tm, BRANCH_W), lambda i, b: (b, i, 0)),
        out_shape=jax.ShapeDtypeStruct((nb, 2 * L, BRANCH_W), F32),
        compiler_params=_cp(("arbitrary", "arbitrary")),
        name="hy_dft_fwd",
    )(fwd, z2d)


def _spec_prod_kernel(z_ref, kf_ref, kb_ref, sf_ref, sb_ref, y_ref, *, tk):
    i = pl.program_id(1)
    s = lax.rsqrt(sf_ref[...] + sb_ref[...] + EPS)
    zr, zi = z_ref[0, 0], z_ref[0, 1]
    fr, fi = kf_ref[0, 0], kf_ref[0, 1]
    br, bi = kb_ref[0, 0], kb_ref[0, 1]
    row = lax.broadcasted_iota(jnp.int32, (tk, BRANCH_W), 0) + i * tk
    first = row == 0
    kr = (fr + br) * s
    ki = jnp.where(first, fi + bi, fi - bi) * s
    yr = jnp.where(first, zr * kr, zr * kr - zi * ki)
    yi = jnp.where(first, zi * ki, zr * ki + zi * kr)
    y_ref[0, 0] = yr.astype(BF16)
    y_ref[0, 1] = yi.astype(BF16)


def _spec_prod(zf, kfs, ss, order, L, nb):
    tk = min(L, 512)
    zf4 = zf.reshape(nb, 2, L, BRANCH_W)
    kf4 = kfs.reshape(4, 2, L, BRANCH_W)
    y = pl.pallas_call(
        functools.partial(_spec_prod_kernel, tk=tk),
        grid=(nb, L // tk),
        in_specs=[pl.BlockSpec((1, 2, tk, BRANCH_W), lambda b, i: (b, 0, i, 0)),
                  pl.BlockSpec((1, 2, tk, BRANCH_W), lambda b, i: (order, 0, i, 0)),
                  pl.BlockSpec((1, 2, tk, BRANCH_W), lambda b, i: (2 + order, 0, i, 0)),
                  pl.BlockSpec((1, BRANCH_W), lambda b, i: (0, order)),
                  pl.BlockSpec((1, BRANCH_W), lambda b, i: (0, 2 + order))],
        out_specs=pl.BlockSpec((1, 2, tk, BRANCH_W), lambda b, i: (b, 0, i, 0)),
        out_shape=jax.ShapeDtypeStruct((nb, 2, L, BRANCH_W), BF16),
        compiler_params=_cp(("arbitrary", "arbitrary")),
        name="hy_spec_prod",
    )(zf4, kf4, kf4, ss, ss)
    return y.reshape(nb, 2 * L, BRANCH_W)


def _dft_inv_kernel(g_ref, y_ref, gate_ref, zp_ref, bias_ref, *rest):
    o_ref = rest[-1]
    conv = _dot(g_ref[...], y_ref[0])
    o_ref[...] = (gate_ref[...] * (conv + bias_ref[...] * zp_ref[...])).astype(o_ref.dtype)


def _dft_inv(inv, y, xs, gate_col, zprev, zprev_col, bias, L, nb, out_dtype, out_rows=None, out_base=0,
             prev=None):
    tm = min(L, 512)
    nt = L // tm
    args = [inv, y, xs, zprev, bias]
    in_specs = [pl.BlockSpec((tm, 2 * L), lambda i, b: (i, 0)),
                pl.BlockSpec((1, 2 * L, BRANCH_W), lambda i, b: (b, 0, 0)),
                pl.BlockSpec((tm, BRANCH_W), lambda i, b: (b * nt + i, gate_col)),
                pl.BlockSpec((tm, BRANCH_W), lambda i, b: (b * nt + i, zprev_col)),
                pl.BlockSpec((1, BRANCH_W), lambda i, b: (0, 0))]
    args, in_specs, alias = _into(prev, args, in_specs)
    return pl.pallas_call(
        _dft_inv_kernel,
        grid=(nt, nb),
        in_specs=in_specs,
        out_specs=pl.BlockSpec((tm, BRANCH_W), lambda i, b: ((out_base + b) * nt + i, 0)),
        out_shape=jax.ShapeDtypeStruct((out_rows or nb * L, BRANCH_W), out_dtype),
        input_output_aliases=alias,
        compiler_params=_cp(("arbitrary", "arbitrary")),
        name="hy_dft_inv",
    )(*args)


def _hyena(u_small, L, nseq, row_base, p, mats, out_rows, prev=None):
    (short_w, short_b, w1, b1, w2, b2, w3, b3, w4, freq, bias) = p
    fwd, inv = mats
    xs = _short_conv(u_small, short_w, short_b, L, nseq, row_base)
    hf, ss = _hyena_filter(L, w1, b1, w2, b2, w3, b3, w4, freq)
    kfs = _dft_fwd(fwd, hf, L, 4, lambda b: (0, b))
    zf = _dft_fwd(fwd, xs, L, nseq, lambda b: (b, 2))
    y = _spec_prod(zf, kfs, ss, 0, L, nseq)
    z1 = _dft_inv(inv, y, xs, 0, xs, 2, bias[0:1], L, nseq, F32)
    zf = _dft_fwd(fwd, z1, L, nseq, lambda b: (b, 0))
    y = _spec_prod(zf, kfs, ss, 1, L, nseq)
    return _dft_inv(inv, y, xs, 1, z1, 0, bias[1:2], L, nseq, BF16, out_rows, row_base, prev)


FFT_N1 = 64
FFT_KB = 16


def _fft_consts(L):
    N = 2 * L
    n2s = N // FFT_N1
    h = FFT_N1 // 2
    k1 = jnp.arange(FFT_N1, dtype=jnp.int32)
    ang_a = ((k1[:, None] * jnp.arange(h, dtype=jnp.int32)[None, :]) % FFT_N1).astype(F32) * (2.0 * math.pi / FFT_N1)
    ca, sa = jnp.cos(ang_a), jnp.sin(ang_a)
    fa = jnp.concatenate([ca, -sa], axis=0).astype(BF16)
    fai = (jnp.concatenate([ca.T, -sa.T], axis=1) / N).astype(BF16)
    k = k1[:, None, None] + FFT_N1 * jnp.arange(n2s, dtype=jnp.int32)[None, :, None]
    n2 = jnp.arange(n2s, dtype=jnp.int32)[None, None, :]
    ang = ((k * n2) % N).astype(F32) * (2.0 * math.pi / N)
    c, s = jnp.cos(ang), jnp.sin(ang)
    g = jnp.concatenate([jnp.concatenate([c, s], axis=2), jnp.concatenate([-s, c], axis=2)], axis=1)
    ct, st = jnp.swapaxes(c, 1, 2), jnp.swapaxes(s, 1, 2)
    gi = jnp.concatenate([jnp.concatenate([ct, -st], axis=2), jnp.concatenate([st, ct], axis=2)], axis=1)
    return fa, fai, g.astype(BF16), gi.astype(BF16)


def _fft_a_kernel(f_ref, z_ref, o_ref):
    o_ref[...] = _dot(f_ref[...], z_ref[0].astype(BF16)).astype(o_ref.dtype)


def _fft_a(fa, z3, zmap, nb, L):
    h = FFT_N1 // 2
    cols = (L // h) * BRANCH_W
    tn = min(cols, 4096)
    return pl.pallas_call(
        _fft_a_kernel,
        grid=(nb, cols // tn),
        in_specs=[pl.BlockSpec(fa.shape, lambda b, j: (0, 0)),
                  pl.BlockSpec((1, h, tn), lambda b, j: zmap(b) + (j,))],
        out_specs=pl.BlockSpec((2 * FFT_N1, tn), lambda b, j: (b, j)),
        out_shape=jax.ShapeDtypeStruct((nb * 2 * FFT_N1, cols), BF16),
        compiler_params=_cp(("arbitrary", "arbitrary")),
        name="hy_fft_a",
    )(fa, z3)


def _fft_spec_kernel(a_ref, g_ref, o_ref):
    for j in range(FFT_KB):
        s = _dot(g_ref[j], jnp.concatenate([a_ref[0, 0, j], a_ref[0, 1, j]], axis=0))
        half = s.shape[0] // 2
        o_ref[0, j, 0] = s[:half]
        o_ref[0, j, 1] = s[half:]


def _fft_spec(g, a5):
    nb, _, n1s, n2s, w = a5.shape
    return pl.pallas_call(
        _fft_spec_kernel,
        grid=(n1s // FFT_KB, nb),
        in_specs=[pl.BlockSpec((1, 2, FFT_KB, n2s, w), lambda i, b: (b, 0, i, 0, 0)),
                  pl.BlockSpec((FFT_KB, 2 * n2s, 2 * n2s), lambda i, b: (i, 0, 0))],
        out_specs=pl.BlockSpec((1, FFT_KB, 2, n2s, w), lambda i, b: (b, i, 0, 0, 0)),
        out_shape=jax.ShapeDtypeStruct((nb, n1s, 2, n2s, w), F32),
        compiler_params=_cp(("arbitrary", "arbitrary")),
        name="hy_fft_spec",
    )(a5, g)


def _fft_conv_kernel(a_ref, g_ref, gi_ref, sf_ref, sb_ref, ssf_ref, ssb_ref, o_ref):
    scale = lax.rsqrt(ssf_ref[...] + ssb_ref[...] + EPS)
    for j in range(FFT_KB):
        s = _dot(g_ref[j], jnp.concatenate([a_ref[0, 0, j], a_ref[0, 1, j]], axis=0))
        half = s.shape[0] // 2
        sr, si = s[:half], s[half:]
        kr = (sf_ref[0, j, 0] + sb_ref[0, j, 0]) * scale
        ki = (sf_ref[0, j, 1] - sb_ref[0, j, 1]) * scale
        y = jnp.concatenate([sr * kr - si * ki, sr * ki + si * kr], axis=0).astype(BF16)
        t = _dot(gi_ref[j], y)
        o_ref[0, 0, j] = t[:half].astype(BF16)
        o_ref[0, 1, j] = t[half:].astype(BF16)


def _fft_conv(g, gi, a5, kspec, ss, order):
    nb, _, n1s, n2s, w = a5.shape
    kblk = lambda d: pl.BlockSpec((1, FFT_KB, 2, n2s, w), lambda i, b: (d, i, 0, 0, 0))
    gblk = pl.BlockSpec((FFT_KB, 2 * n2s, 2 * n2s), lambda i, b: (i, 0, 0))
    ablk = pl.BlockSpec((1, 2, FFT_KB, n2s, w), lambda i, b: (b, 0, i, 0, 0))
    return pl.pallas_call(
        _fft_conv_kernel,
        grid=(n1s // FFT_KB, nb),
        in_specs=[ablk, gblk, gblk, kblk(order), kblk(2 + order),
                  pl.BlockSpec((1, w), lambda i, b: (0, order)),
                  pl.BlockSpec((1, w), lambda i, b: (0, 2 + order))],
        out_specs=ablk,
        out_shape=jax.ShapeDtypeStruct(a5.shape, BF16),
        compiler_params=_cp(("arbitrary", "arbitrary")),
        name="hy_fft_conv",
    )(a5, g, gi, kspec, kspec, ss, ss)


def _fft_ainv_kernel(f_ref, a_ref, gate_ref, zp_ref, bias_ref, o_ref):
    conv = _dot(f_ref[...], a_ref[...])
    o_ref[...] = (gate_ref[0] * (conv + bias_ref[...] * zp_ref[0])).astype(o_ref.dtype)


def _fft_ainv(fai, a2, gate3, gate_plane, zprev3, zprev_plane, bias, L, nb, out_dtype, out_view_rows):
    h = FFT_N1 // 2
    cols = (L // h) * BRANCH_W
    tn = min(cols, 4096)
    return pl.pallas_call(
        _fft_ainv_kernel,
        grid=(nb, cols // tn),
        in_specs=[pl.BlockSpec(fai.shape, lambda b, j: (0, 0)),
                  pl.BlockSpec((2 * FFT_N1, tn), lambda b, j: (b, j)),
                  pl.BlockSpec((1, h, tn), lambda b, j: (gate_plane, b, j)),
                  pl.BlockSpec((1, h, tn), lambda b, j: (zprev_plane, b, j)),
                  pl.BlockSpec((1, tn), lambda b, j: (0, 0))],
        out_specs=pl.BlockSpec((h, tn), lambda b, j: (b, j)),
        out_shape=jax.ShapeDtypeStruct((out_view_rows, cols), out_dtype),
        compiler_params=_cp(("arbitrary", "arbitrary")),
        name="hy_fft_ainv",
    )(fai, a2, gate3, zprev3, jnp.tile(bias, (1, tn // BRANCH_W)))


def _hyena_fft(u_small, L, nseq, p, consts, out_rows):
    (short_w, short_b, w1, b1, w2, b2, w3, b3, w4, freq, bias) = p
    fa, fai, g, gi = consts
    h = FFT_N1 // 2
    n2s = L // h
    cols = n2s * BRANCH_W
    xs3 = _short_conv(u_small, short_w, short_b, L, nseq, 0, planes=True).reshape(3, nseq * h, cols)
    hf4, ss = _hyena_filter(L, w1, b1, w2, b2, w3, b3, w4, freq, planes=True)
    ka = _fft_a(fa, hf4.reshape(4, h, cols), lambda b: (b, 0), 4, L)
    kspec = _fft_spec(g, ka.reshape(4, 2, FFT_N1, n2s, BRANCH_W))

    def conv(z3, zmap, order, gate_plane, zprev_plane, out_dtype, out_view_rows):
        a = _fft_a(fa, z3, zmap, nseq, L)
        a = _fft_conv(g, gi, a.reshape(nseq, 2, FFT_N1, n2s, BRANCH_W), kspec, ss, order)
        return _fft_ainv(fai, a.reshape(nseq * 2 * FFT_N1, cols), xs3, gate_plane, z3, zprev_plane,
                         bias[order:order + 1], L, nseq, out_dtype, out_view_rows)

    z1 = conv(xs3, lambda b: (2, b), 0, 0, 2, F32, nseq * h).reshape(1, nseq * h, cols)
    y = conv(z1, lambda b: (0, b), 1, 1, 0, BF16, out_rows // n2s)
    return y.reshape(out_rows, BRANCH_W)


def _conformer_kernel(a_ref, g_ref, w_ref, b_ref, lg_ref, lb_ref, *rest, L, tc):
    o_ref, pad_scr = rest[-2:]
    pad_scr[0:16, :] = jnp.zeros((16, BRANCH_W), F32)
    pad_scr[16 + L:32 + L, :] = jnp.zeros((16, BRANCH_W), F32)

    def glu(c, carry):
        t0 = pl.multiple_of(c * tc, tc)
        pad_scr[pl.ds(16 + t0, tc), :] = a_ref[pl.ds(t0, tc), :] * _sigmoid(g_ref[pl.ds(t0, tc), :])
        return carry

    lax.fori_loop(0, L // tc, glu, 0)
    b = b_ref[...]
    lg = lg_ref[...]
    lb = lb_ref[...]

    def body(c, carry):
        t0 = pl.multiple_of(c * tc, tc)
        win = pad_scr[pl.ds(t0, tc + 32), :]
        acc = jnp.zeros((tc, BRANCH_W), F32) + b
        for r in range(8):
            sh = win[r:r + tc + 24]
            for a in range(4):
                m = 8 * a + r
                if 1 <= m <= CF_WIDTH:
                    acc = acc + w_ref[m - 1:m, :] * sh[8 * a:8 * a + tc]
        mu = jnp.mean(acc, axis=-1, keepdims=True)
        xc = acc - mu
        var = jnp.mean(xc * xc, axis=-1, keepdims=True)
        y = xc * lax.rsqrt(var + EPS) * lg + lb
        o_ref[pl.ds(t0, tc), :] = (y * _sigmoid(y)).astype(o_ref.dtype)
        return carry

    lax.fori_loop(0, L // tc, body, 0)


def _conformer(u_small, L, nseq, row_base, p, out_rows, prev=None):
    dw_w, dw_b, ln_g, ln_b = p
    tc = 128
    wpad = jnp.pad(dw_w, ((0, 32 - CF_WIDTH), (0, 0)))
    row = lambda a: a.reshape(1, BRANCH_W)
    vec = pl.BlockSpec((1, BRANCH_W), lambda s: (0, 0))
    args = [u_small, u_small, wpad, row(dw_b), row(ln_g), row(ln_b)]
    in_specs = [pl.BlockSpec((L, BRANCH_W), lambda s: (row_base + s, 3)),
                pl.BlockSpec((L, BRANCH_W), lambda s: (row_base + s, 4)),
                pl.BlockSpec((32, BRANCH_W), lambda s: (0, 0)), vec, vec, vec]
    args, in_specs, alias = _into(prev, args, in_specs)
    return pl.pallas_call(
        functools.partial(_conformer_kernel, L=L, tc=tc),
        grid=(nseq,),
        in_specs=in_specs,
        out_specs=pl.BlockSpec((L, BRANCH_W), lambda s: (row_base + s, 0)),
        out_shape=jax.ShapeDtypeStruct((out_rows, BRANCH_W), BF16),
        input_output_aliases=alias,
        scratch_shapes=[pltpu.VMEM((L + 32, BRANCH_W), F32)],
        compiler_params=_cp(("arbitrary",)),
        name="conformer",
    )(*args)


def _rope_tables(S, pad_rows, head, scale_unused=None):
    half = head // 2
    nf = half // 2
    lane = np.arange(LANES)
    inv_lane = (ROPE_BASE ** (-(np.arange(nf)) / nf))[(lane % half) % nf]
    is_row = (lane % head) < half
    pos = jnp.arange(S, dtype=jnp.int32)
    rows = (pos // GRID_W).astype(F32)[:, None]
    cols = (pos % GRID_W).astype(F32)[:, None]
    ang = jnp.where(jnp.asarray(is_row)[None, :], rows, cols) * jnp.asarray(inv_lane, F32)[None, :]
    cos = jnp.concatenate([jnp.cos(ang), jnp.ones((pad_rows, LANES), F32)], axis=0)
    sin = jnp.concatenate([jnp.sin(ang), jnp.zeros((pad_rows, LANES), F32)], axis=0)
    return cos, sin


def _group_ones(width, group):
    idx = np.arange(width)
    return jnp.asarray((idx[:, None] // group) == (idx[None, :] // group), BF16)


def _head_norm_rope(x, ones, group, gain, cos, sin, nf, out_scale):
    w = x.shape[1]
    hi, lo = _split(x * x)
    ms = (_dot(hi, ones) + _dot(lo, ones)) * (1.0 / group)
    xn = x * lax.rsqrt(ms + EPS) * gain
    reps = w // LANES
    c = jnp.concatenate([cos] * reps, axis=1) if reps > 1 else cos
    s = jnp.concatenate([sin] * reps, axis=1) if reps > 1 else sin
    lane = lax.broadcasted_iota(jnp.int32, x.shape, 1)
    first = (lane % (2 * nf)) < nf
    rot = jnp.where(first, -pltpu.roll(xn, w - nf, 1), pltpu.roll(xn, nf, 1))
    return (xn * c + rot * s) * out_scale


def _qkv_kernel(gq_ref, gkv_ref, dq_ref, dk_ref, dv_ref, cg_ref, sg_ref, cd_ref, sd_ref,
                o64_ref, o32_ref, gqn_ref, gkn_ref, dqn_ref, dkn_ref,
                qg_ref, kg_ref, vag_ref, vbg_ref, qd_ref, kd_ref, vad_ref, vbd_ref):
    cg, sg, cd, sd = cg_ref[...], sg_ref[...], cd_ref[...], sd_ref[...]
    o64, o32 = o64_ref[...], o32_ref[...]
    q = _head_norm_rope(gq_ref[...], o64, HEAD_DIM, gqn_ref[...], cg, sg, HEAD_DIM // 4, LOG2E * HEAD_DIM ** -0.5)
    qg_ref[...] = q.astype(BF16)
    kv = gkv_ref[...]
    k = _head_norm_rope(kv[:, :LANES], o64[:LANES, :LANES], HEAD_DIM, gkn_ref[...], cg, sg, HEAD_DIM // 4, 1.0)
    v = kv[:, LANES:]
    kk = jnp.concatenate([k, k], axis=1)
    vv = jnp.concatenate([v, v], axis=1)
    quarter = lax.broadcasted_iota(jnp.int32, kk.shape, 1) // HEAD_DIM
    kr = pltpu.roll(kk, HEAD_DIM, 1)
    vr = pltpu.roll(vv, HEAD_DIM, 1)
    kg_ref[...] = jnp.where((quarter == 0) | (quarter == 3), kk, kr).astype(BF16)
    vag_ref[...] = jnp.where(quarter == 0, vv, jnp.where(quarter == 2, vr, 0.0)).astype(BF16)
    vbg_ref[...] = jnp.where(quarter == 1, vr, jnp.where(quarter == 3, vv, 0.0)).astype(BF16)
    qd = _head_norm_rope(dq_ref[...], o32, DIFF_D, dqn_ref[...], cd, sd, DIFF_D // 4, LOG2E * DIFF_D ** -0.5)
    qd_ref[...] = qd.astype(BF16)
    kd = _head_norm_rope(dk_ref[...], o32, DIFF_D, dkn_ref[...], cd, sd, DIFF_D // 4, 1.0)
    kd_ref[...] = kd.astype(BF16)
    vd = dv_ref[...]
    even = (lax.broadcasted_iota(jnp.int32, vd.shape, 1) // HEAD_DIM) % 2 == 0
    vad_ref[...] = jnp.where(even, vd, 0.0).astype(BF16)
    vbd_ref[...] = jnp.where(even, 0.0, vd).astype(BF16)


def _qkv_prep(u_small, gqn, gkn, dqn, dkn, tables, dims):
    n = u_small.shape[0]
    tm = dims["tm_small"]
    S = dims["S"]
    n_lat_tiles = dims["n_lat"] // tm
    per_seq = S // tm
    tmap = lambda i: (jnp.where(i < n_lat_tiles, i % per_seq, per_seq), 0)
    col = lambda c: pl.BlockSpec((tm, BRANCH_W), lambda i: (i, c))
    tab = pl.BlockSpec((tm, LANES), tmap)
    full = lambda a: pl.BlockSpec(a.shape, lambda i: (0,) * a.ndim)
    o64, o32 = _group_ones(BRANCH_W, HEAD_DIM), _group_ones(BRANCH_W, DIFF_D)
    gains = [jnp.tile(gqn, 4).reshape(1, 256), jnp.tile(gkn, 2).reshape(1, 128),
             jnp.tile(dqn.reshape(-1), 4).reshape(1, 256), jnp.tile(dkn.reshape(-1), 4).reshape(1, 256)]
    out = pl.BlockSpec((tm, BRANCH_W), lambda i: (i, 0))
    return pl.pallas_call(
        _qkv_kernel,
        grid=(n // tm,),
        in_specs=[col(5), col(6), col(7), col(8), col(9), tab, tab, tab, tab, full(o64), full(o32)]
        + [full(g) for g in gains],
        out_specs=[out] * 8,
        out_shape=[jax.ShapeDtypeStruct((n, BRANCH_W), BF16)] * 8,
        compiler_params=_cp(("arbitrary",)),
        name="qkv_prep",
    )(u_small, u_small, u_small, u_small, u_small, *tables, o64, o32, *gains)


def _lane_pick(lane_lo, a, b):
    return jnp.where(lane_lo, a, b)


def _flash_kernel(*refs, segs, tk, tq, diff, lam_init):
    q_ref, o_ref = refs[0], refs[-1]
    seg_refs = [refs[1 + 3 * i:4 + 3 * i] for i in range(len(segs))]
    aux_ref = refs[1 + 3 * len(segs)]
    lane = lax.broadcasted_iota(jnp.int32, (1, LANES), 1)
    lane_lo = lane < HEAD_DIM
    if diff:
        masks = [(lane >= g * DIFF_D) & (lane < (g + 1) * DIFF_D) for g in range(4)]
        acc_of = [0, 1, 0, 1]
        use_a = [True, True, False, False]
        n_acc = 2
    else:
        masks = [lane_lo, ~lane_lo]
        acc_of = [0, 0]
        use_a = [True, False]
        n_acc = 1
    R = len(masks)
    pairs = [slice(p * LANES, (p + 1) * LANES) for p in range(2)]
    qsts = [jnp.concatenate([jnp.where(m, q_ref[:, ps], jnp.zeros((tq, LANES), BF16)) for m in masks], axis=0)
            for ps in pairs]
    lo_hi = []
    for a in range(n_acc):
        rs = [r for r in range(R) if acc_of[r] == a]
        lo_hi.append(([r for r in rs if use_a[r]][0], [r for r in rs if not use_a[r]][0]))

    def rows(x, r):
        return x[r * tq:(r + 1) * tq]

    shift = jnp.max(aux_ref[5:6, :])

    def run(fixed):
        def chunk(kv_refs, t0, size, carry):
            k_ref, va_ref, vb_ref = kv_refs
            new = []
            for p, ps in enumerate(pairs):
                m_run, l_run, accs = carry[p]
                k = k_ref[pl.ds(t0, size), ps]
                s = lax.dot_general(qsts[p], k, (((1,), (1,)), ((), ())), preferred_element_type=F32)
                if fixed:
                    m_new = m_run
                    pr = jnp.exp2(s - shift)
                    l_new = l_run + jnp.sum(pr, axis=-1, keepdims=True)
                else:
                    m_new = jnp.maximum(m_run, jnp.max(s, axis=-1, keepdims=True))
                    alpha = jnp.exp2(m_run - m_new)
                    pr = jnp.exp2(s - m_new)
                    l_new = alpha * l_run + jnp.sum(pr, axis=-1, keepdims=True)
                prb = pr.astype(BF16)
                va = va_ref[pl.ds(t0, size), ps]
                vb = vb_ref[pl.ds(t0, size), ps]
                new_accs = []
                for a, (r_lo, r_hi) in enumerate(lo_hi):
                    upd = _dot(rows(prb, r_lo), va) + _dot(rows(prb, r_hi), vb)
                    if fixed:
                        new_accs.append(accs[a] + upd)
                    else:
                        al = _lane_pick(lane_lo, rows(alpha, r_lo), rows(alpha, r_hi))
                        new_accs.append(accs[a] * al + upd)
                new.append((m_new, l_new, tuple(new_accs)))
            return tuple(new)

        one = (jnp.full((R * tq, 1), -jnp.inf, F32), jnp.zeros((R * tq, 1), F32),
               tuple(jnp.zeros((tq, LANES), F32) for _ in range(n_acc)))
        carry = (one, one)
        for kv_refs, T in zip(seg_refs, segs):
            n_main = T // tk
            if n_main:
                carry = lax.fori_loop(
                    0, n_main, lambda c, cr, kv_refs=kv_refs: chunk(kv_refs, pl.multiple_of(c * tk, tk), tk, cr),
                    carry, unroll=2 if n_main % 2 == 0 else 1)
            if T - n_main * tk:
                carry = chunk(kv_refs, n_main * tk, T - n_main * tk, carry)

        outs = []
        for p in range(2):
            _, l_fin, accs = carry[p]
            inv_l = 1.0 / l_fin
            norm = [accs[a] * _lane_pick(lane_lo, rows(inv_l, r_lo), rows(inv_l, r_hi))
                    for a, (r_lo, r_hi) in enumerate(lo_hi)]
            if diff:
                aux = aux_ref[...]
                lam = (jnp.exp(jnp.sum(aux[0:1] * aux[1:2], axis=-1, keepdims=True))
                       - jnp.exp(jnp.sum(aux[2:3] * aux[3:4], axis=-1, keepdims=True)) + lam_init)
                o = norm[0] - lam * norm[1]
                sq = o * o
                s_lo = jnp.sum(jnp.where(lane_lo, sq, 0.0), axis=-1, keepdims=True)
                s_hi = jnp.sum(jnp.where(lane_lo, 0.0, sq), axis=-1, keepdims=True)
                ms = _lane_pick(lane_lo, s_lo, s_hi) * (1.0 / HEAD_DIM)
                o = o * lax.rsqrt(ms + EPS) * aux[4:5] * (1.0 - lam_init)
            else:
                o = norm[0]
            outs.append(o)
        o_ref[...] = jnp.concatenate(outs, axis=1).astype(o_ref.dtype)

    @pl.when(shift < FIXED_SHIFT_LIMIT)
    def _():
        run(True)

    @pl.when(jnp.logical_not(shift < FIXED_SHIFT_LIMIT))
    def _():
        run(False)


def _into(prev, args, in_specs):
    if prev is None:
        return args, in_specs, {}
    return args + [prev], in_specs + [pl.BlockSpec(memory_space=pl.ANY)], {len(args): 0}


def _flash(q, k, va, vb, aux, nb, Lq, q_base, segs, diff, lam_init, out_rows, out_base, prev=None):
    tq = min(512, Lq)
    tk = 512 if diff else 1024
    nt = Lq // tq
    q_base, out_base = q_base // tq, out_base // tq
    args = [q]
    in_specs = [pl.BlockSpec((tq, BRANCH_W), lambda b, i: (q_base + b * nt + i, 0))]
    for length, base in segs:
        spec = pl.BlockSpec((length, BRANCH_W), lambda b, i, base=base: (base + b, 0))
        args += [k, va, vb]
        in_specs += [spec, spec, spec]
    args.append(aux)
    in_specs.append(pl.BlockSpec((8, LANES), lambda b, i: (0, 0)))
    args, in_specs, alias = _into(prev, args, in_specs)
    return pl.pallas_call(
        functools.partial(_flash_kernel, segs=tuple(s[0] for s in segs), tk=tk, tq=tq, diff=diff,
                          lam_init=lam_init),
        grid=(nb, nt),
        in_specs=in_specs,
        out_specs=pl.BlockSpec((tq, BRANCH_W), lambda b, i: (out_base + b * nt + i, 0)),
        out_shape=jax.ShapeDtypeStruct((out_rows, BRANCH_W), BF16),
        input_output_aliases=alias,
        compiler_params=_cp(("arbitrary", "arbitrary")),
        name="flash_diff" if diff else "flash_gqa",
    )(*args)


def _merge_kernel(y0, y1, y2, y3, g_ref, wb_ref, wo_ref, x_ref, mod_ref, o_ref):
    d = D_MODEL
    acc = g_ref[:, 0:d].astype(F32) * _dot(y0[...], wb_ref[0])
    acc = acc + g_ref[:, d:2 * d].astype(F32) * _dot(y1[...], wb_ref[1])
    acc = acc + g_ref[:, 2 * d:3 * d].astype(F32) * _dot(y2[...], wb_ref[2])
    acc = acc + g_ref[:, 3 * d:4 * d].astype(F32) * _dot(y3[...], wb_ref[3])
    mix = _dot(acc.astype(BF16), wo_ref[...])
    o_ref[...] = x_ref[...] + mod_ref[0][2:3, :] * mix


def _merge(ys, gate, wb, wo, x, mods_l, dims):
    n, d = ys[0].shape[0], x.shape[1]
    tm = dims["tm_small"]
    mrow = dims["mod_row"](tm)
    yspec = pl.BlockSpec((tm, BRANCH_W), lambda i: (i, 0))
    return pl.pallas_call(
        _merge_kernel,
        grid=(n // tm,),
        in_specs=[yspec] * 4 + [pl.BlockSpec((tm, GATE_COLS), lambda i: (i, 0)),
                                pl.BlockSpec(wb.shape, lambda i: (0, 0, 0)),
                                pl.BlockSpec(wo.shape, lambda i: (0, 0)),
                                pl.BlockSpec((tm, d), lambda i: (i, 0)),
                                pl.BlockSpec((1, 6, d), lambda i: (mrow(i), 0, 0))],
        out_specs=pl.BlockSpec((tm, d), lambda i: (i, 0)),
        out_shape=jax.ShapeDtypeStruct((n, d), F32),
        compiler_params=_cp(("arbitrary",)),
        name="merge",
    )(*ys, gate, wb, wo, x, mods_l)


def _ffn_kernel(x_ref, mod_ref, g_ref, w1_ref, w3_ref, w2_ref, o_ref, h_scr, acc_scr):
    j = pl.program_id(1)

    @pl.when(j == 0)
    def _():
        m = mod_ref[0]
        h_scr[...] = _norm_mod(x_ref[...], g_ref[...], m[3:4, :], m[4:5, :]).astype(BF16)
        acc_scr[...] = jnp.zeros_like(acc_scr)

    h = h_scr[...]
    a = _dot(h, w1_ref[...])
    b = _dot(h, w3_ref[...])
    t = (a * _sigmoid(a) * b).astype(BF16)
    acc_scr[...] += _dot(t, w2_ref[...])

    @pl.when(j == pl.num_programs(1) - 1)
    def _():
        o_ref[...] = x_ref[...] + mod_ref[0][5:6, :] * acc_scr[...]


def _ffn(x, mods_l, gain, w1, w3, w2, dims):
    n, d = x.shape
    tm, tf = dims["tm_big"], 256
    mrow = dims["mod_row"](tm)
    return pl.pallas_call(
        _ffn_kernel,
        grid=(n // tm, w1.shape[1] // tf),
        in_specs=[pl.BlockSpec((tm, d), lambda i, j: (i, 0)),
                  pl.BlockSpec((1, 6, d), lambda i, j: (mrow(i), 0, 0)),
                  pl.BlockSpec((1, d), lambda i, j: (0, 0)),
                  pl.BlockSpec((d, tf), lambda i, j: (0, j)),
                  pl.BlockSpec((d, tf), lambda i, j: (0, j)),
                  pl.BlockSpec((tf, d), lambda i, j: (j, 0))],
        out_specs=pl.BlockSpec((tm, d), lambda i, j: (i, 0)),
        out_shape=jax.ShapeDtypeStruct((n, d), F32),
        scratch_shapes=[pltpu.VMEM((tm, d), BF16), pltpu.VMEM((tm, d), F32)],
        compiler_params=_cp(("arbitrary", "arbitrary")),
        name="ffn_dense",
    )(x, mods_l, gain, w1, w3, w2)


def _route_kernel(x_ref, mod_ref, g_ref, wr_ref, tok_ref, gate_ref):
    m = mod_ref[0]
    h = _norm_mod(x_ref[...], g_ref[...], m[3:4, :], m[4:5, :])
    tok_ref[...] = h.astype(BF16)
    logits = _dot3(h, wr_ref[...])
    lane = lax.broadcasted_iota(jnp.int32, logits.shape, 1)
    lg = jnp.where(lane < N_EXPERTS, logits, -jnp.inf)
    m1 = jnp.max(lg, axis=-1, keepdims=True)
    i1 = jnp.min(jnp.where(lg == m1, lane, LANES), axis=-1, keepdims=True)
    lg2 = jnp.where(lane == i1, -jnp.inf, lg)
    m2 = jnp.max(lg2, axis=-1, keepdims=True)
    i2 = jnp.min(jnp.where(lg2 == m2, lane, LANES), axis=-1, keepdims=True)
    e2 = jnp.exp(m2 - m1)
    g1 = 1.0 / (1.0 + e2)
    gate_ref[...] = jnp.where(lane == i1, g1, jnp.where(lane == i2, e2 * g1, 0.0))


def _route(x, mods_l, gain, w_router, dims):
    n, d = x.shape
    tm = dims["tm_small"]
    mrow = dims["mod_row"](tm)
    wr = jnp.pad(w_router, ((0, 0), (0, LANES - N_EXPERTS)))
    return pl.pallas_call(
        _route_kernel,
        grid=(n // tm,),
        in_specs=[pl.BlockSpec((tm, d), lambda i: (i, 0)),
                  pl.BlockSpec((1, 6, d), lambda i: (mrow(i), 0, 0)),
                  pl.BlockSpec((1, d), lambda i: (0, 0)),
                  pl.BlockSpec((d, LANES), lambda i: (0, 0))],
        out_specs=[pl.BlockSpec((tm, d), lambda i: (i, 0)), pl.BlockSpec((tm, LANES), lambda i: (i, 0))],
        out_shape=[jax.ShapeDtypeStruct((n, d), BF16), jax.ShapeDtypeStruct((n, LANES), F32)],
        compiler_params=_cp(("arbitrary",)),
        name="moe_route",
    )(x, mods_l, gain, wr)


MOE_SUB = 128
MOE_GATHER = 256
MOE_GRAN = 64


def _moe_kernel(tok_ref, gate_ref, tri_ref, w1_ref, w3_ref, w2_ref, x_ref, mod_ref, o_ref,
                rank_scr, rank_t_scr, mask_t_scr, xe_scr, ye_scr, cnt_smem, *, tm):
    e = pl.program_id(1)
    j = pl.program_id(2)
    ne = pl.num_programs(1)
    nj = pl.num_programs(2)
    lane = lax.broadcasted_iota(jnp.int32, (1, LANES), 1)

    @pl.when((e == 0) & (j == 0))
    def _():
        tri = tri_ref[...]
        carry = jnp.zeros((1, LANES), F32)
        def put_counts(blk, counts):
            for ee in range(N_EXPERTS):
                cnt_smem[blk * N_EXPERTS + ee] = jnp.sum(jnp.where(lane == ee, counts, 0.0)).astype(jnp.int32)

        for blk in range(tm // 256):
            put_counts(blk, carry)
            rs = slice(blk * 256, (blk + 1) * 256)
            msk = (gate_ref[rs, :] > 0.0).astype(BF16)
            rank_scr[rs, :] = _dot(tri, msk) + carry
            carry = carry + jnp.sum(msk.astype(F32), axis=0, keepdims=True)
        rank_t_scr[...] = jnp.transpose(rank_scr[...])
        mask_t_scr[...] = jnp.transpose((gate_ref[...] > 0.0).astype(F32))
        put_counts(tm // 256, carry)
        o_ref[...] = jnp.zeros_like(o_ref)
        ye_scr[...] = jnp.zeros_like(ye_scr)

    cnt = cnt_smem[(tm // 256) * N_EXPERTS + e]

    n_sub = (cnt + (MOE_GATHER - 1)) // MOE_GATHER

    @pl.when(j == 0)
    def _():
        rk = rank_t_scr[pl.ds(e, 1), :]
        mk = mask_t_scr[pl.ds(e, 1), :]

        def gather(sb, carry):
            r0 = pl.multiple_of(sb * MOE_GATHER, MOE_GATHER)
            r_iota = lax.broadcasted_iota(jnp.int32, (MOE_GATHER, tm), 0) + r0
            sel = jnp.where((rk == r_iota.astype(F32)) & (mk > 0.0), 1.0, 0.0).astype(BF16)
            xe_scr[pl.ds(r0, MOE_GATHER), :] = _dot(sel, tok_ref[...]).astype(BF16)
            return carry

        lax.fori_loop(0, n_sub, gather, 0)

    def expert_rows(r0, size):
        xe = xe_scr[pl.ds(r0, size), :]
        a = _dot(xe, w1_ref[0])
        b = _dot(xe, w3_ref[0])
        t = (a * _sigmoid(a) * b).astype(BF16)
        y = _dot(t, w2_ref[0])
        ye_scr[pl.ds(r0, size), :] = jnp.where(j == 0, y, ye_scr[pl.ds(r0, size), :] + y)

    n_gran = (cnt + (MOE_GRAN - 1)) // MOE_GRAN
    n_big = n_gran // 4
    rem = n_gran % 4

    def big(i, carry):
        expert_rows(pl.multiple_of(i * 256, 256), 256)
        return carry

    lax.fori_loop(0, n_big - 1, big, 0)
    for r in range(min(4, (tm - 256) // MOE_GRAN + 1)):
        @pl.when((n_big >= 1) & (rem == r))
        def _():
            expert_rows(pl.multiple_of((n_big - 1) * 256, 256), 256 + r * MOE_GRAN)

    @pl.when((n_big == 0) & (rem >= 2))
    def _():
        expert_rows(0, 128)

    @pl.when((n_big == 0) & (rem % 2 == 1))
    def _():
        expert_rows(pl.multiple_of((rem // 2) * 128, 128), 64)

    @pl.when(j == nj - 1)
    def _():
        def scatter(rc, carry):
            r0 = pl.multiple_of(rc * 256, 256)
            gt = gate_ref[pl.ds(r0, 256), :]
            g_e = jnp.sum(jnp.where(lane == e, gt, 0.0), axis=-1, keepdims=True)
            r_e = jnp.sum(jnp.where(lane == e, rank_scr[pl.ds(r0, 256), :], 0.0), axis=-1, keepdims=True)
            lo = cnt_smem[rc * N_EXPERTS + e]
            hi = cnt_smem[(rc + 1) * N_EXPERTS + e]

            def scatter_sub(sb, c2):
                c0 = pl.multiple_of(sb * MOE_SUB, MOE_SUB)
                c_iota = lax.broadcasted_iota(jnp.int32, (256, MOE_SUB), 1) + c0
                sel_t = jnp.where((r_e == c_iota.astype(F32)) & (g_e > 0.0), 1.0, 0.0).astype(BF16)
                ye = ye_scr[pl.ds(c0, MOE_SUB), :].astype(BF16)
                o_ref[pl.ds(r0, 256), :] += g_e * _dot(sel_t, ye)
                return c2

            lax.fori_loop(lo // MOE_SUB, (hi + (MOE_SUB - 1)) // MOE_SUB, scatter_sub, 0)

            @pl.when(e == ne - 1)
            def _():
                o_ref[pl.ds(r0, 256), :] = (x_ref[pl.ds(r0, 256), :]
                                            + mod_ref[0][5:6, :] * o_ref[pl.ds(r0, 256), :])
            return carry

        lax.fori_loop(0, tm // 256, scatter, 0)


def _moe(tok, gates, w1, w3, w2, x, mods_l, dims):
    n, d = tok.shape
    tm = dims["tm_moe"] if n % dims["tm_moe"] == 0 else dims["tm_big"]
    mrow = dims["mod_row"](tm)
    ne, _, ff = w1.shape
    tf = 896
    tri =jnp.asarray(np.tril(np.ones((256, 256)), -1), BF16)
    once = pl.Buffered(1)
    return pl.pallas_call(
        functools.partial(_moe_kernel, tm=tm),
        grid=(n // tm, ne, ff // tf),
        in_specs=[pl.BlockSpec((tm, d), lambda i, e, j: (i, 0), pipeline_mode=once),
                  pl.BlockSpec((tm, LANES), lambda i, e, j: (i, 0), pipeline_mode=once),
                  pl.BlockSpec((256, 256), lambda i, e, j: (0, 0), pipeline_mode=once),
                  pl.BlockSpec((1, d, tf), lambda i, e, j: (e, 0, j)),
                  pl.BlockSpec((1, d, tf), lambda i, e, j: (e, 0, j)),
                  pl.BlockSpec((1, tf, d), lambda i, e, j: (e, j, 0)),
                  pl.BlockSpec((tm, d), lambda i, e, j: (i, 0), pipeline_mode=once),
                  pl.BlockSpec((1, 6, d), lambda i, e, j: (mrow(i), 0, 0))],
        out_specs=pl.BlockSpec((tm, d), lambda i, e, j: (i, 0), pipeline_mode=once),
        out_shape=jax.ShapeDtypeStruct((n, d), F32),
        scratch_shapes=[pltpu.VMEM((tm, LANES), F32), pltpu.VMEM((LANES, tm), F32), pltpu.VMEM((LANES, tm), F32),
                        pltpu.VMEM((tm, d), BF16), pltpu.VMEM((tm, d), F32), pltpu.SMEM(((tm // 256 + 1) * N_EXPERTS,), jnp.int32)],
        compiler_params=_cp(("arbitrary", "arbitrary", "arbitrary"), VMEM_LIMIT_MOE),
        name="moe_experts",
    )(tok, gates, tri, w1, w3, w2, x, mods_l)


def _make_dims(B, S, n_ctx):
    n_lat, n = B * S, B * (S + n_ctx)

    def pick(prefs):
        for t in prefs:
            if S % t == 0 and n_lat % t == 0 and (n - n_lat) % t == 0:
                return t
        raise ValueError("no row tile divides the latent and context token counts")

    def mod_row(tm):
        return lambda i: jnp.minimum((i * tm) // S, B)

    return {"B": B, "S": S, "n_ctx": n_ctx, "n_lat": n_lat, "n": n,
            "tm_big": pick((1024, 512, 256)), "tm_small": pick((512, 256)), "tm_moe": pick((2048, 1024, 512, 256)),
            "mod_row": mod_row}


def kernel(x, c, ctx, c_ctx, w_ada, b_ada, norm_mix, norm_ffn, w_in, hy_short_w, hy_short_b, hy_w1, hy_b1, hy_w2, hy_b2, hy_w3, hy_b3, hy_w4, hy_freq, hy_bias, cf_dw_w, cf_dw_b, cf_ln_g, cf_ln_b, gqa_qn, gqa_kn, diff_qn, diff_kn, diff_lq1, diff_lk1, diff_lq2, diff_lk2, diff_subln, w_branch, w_out, ffn_w1, ffn_w3, ffn_w2, moe_router, moe_w1, moe_w3, moe_w2):
    B, S, D = x.shape
    n_ctx = ctx.shape[1]
    depth = w_ada.shape[0]
    dims = _make_dims(B, S, n_ctx)
    n_lat = dims["n_lat"]
    tm_s = dims["tm_small"]

    c16 = jnp.concatenate([c, c_ctx[None, :], jnp.zeros((16 - B - 1, D), F32)], axis=0)
    mods = _ada_mods(c16, w_ada, b_ada).reshape(depth, 16, 6, D)

    xa = jnp.concatenate([x.reshape(n_lat, D), ctx.reshape(B * n_ctx, D)], axis=0)
    tables = _rope_tables(S, tm_s, HEAD_DIM) + _rope_tables(S, tm_s, DIFF_D)
    fft_lat = _fft_consts(S)
    mats_ctx = _dft_mats(n_ctx)

    for l in range(depth):
        lam_init = 0.8 - 0.6 * math.exp(-0.3 * l)
        u_small, gate = _in_proj(xa, mods[l], norm_mix[l].reshape(1, D), w_in[l].astype(BF16), dims)

        hy_p = (hy_short_w[l], hy_short_b[l], hy_w1[l], hy_b1[l], hy_w2[l], hy_b2[l], hy_w3[l], hy_b3[l],
                hy_w4[l], hy_freq[l], hy_bias[l])
        cf_p = (cf_dw_w[l], cf_dw_b[l], cf_ln_g[l], cf_ln_b[l])
        last = l == depth - 1
        n_out = n_lat if last else dims["n"]
        ctx_base = n_lat // n_ctx
        y_hy = _hyena_fft(u_small, S, B, hy_p, fft_lat, n_out)
        y_cf = _conformer(u_small, S, B, 0, cf_p, n_out)
        if not last:
            y_hy = _hyena(u_small, n_ctx, B, ctx_base, hy_p, mats_ctx, n_out, y_hy)
            y_cf = _conformer(u_small, n_ctx, B, ctx_base, cf_p, n_out, y_cf)

        qg, kg, vag, vbg, qd, kd, vad, vbd = _qkv_prep(u_small, gqa_qn[l], gqa_kn[l], diff_qn[l], diff_kn[l],
                                                      tables, dims)

        aux = jnp.zeros((8, LANES), F32)
        aux = aux.at[0, :DIFF_D].set(diff_lq1[l]).at[1, :DIFF_D].set(diff_lk1[l])
        aux = aux.at[2, :DIFF_D].set(diff_lq2[l]).at[3, :DIFF_D].set(diff_lk2[l])
        aux = aux.at[4, :].set(jnp.tile(diff_subln[l], 2))

        def score_bound(qn, kn, d):
            return 1.02 * d * jnp.max(jnp.abs(qn)) * jnp.max(jnp.abs(kn)) * (LOG2E * d ** -0.5) + 0.1

        bounds = (score_bound(gqa_qn[l], gqa_kn[l], HEAD_DIM), score_bound(diff_qn[l], diff_kn[l], DIFF_D))
        y_att = []
        for (q, k, va, vb, is_diff) in ((qg, kg, vag, vbg, False), (qd, kd, vad, vbd, True)):
            aux = aux.at[5, :].set(bounds[int(is_diff)])
            y = _flash(q, k, va, vb, aux, B, S, 0, ((S, 0), (n_ctx, ctx_base)), is_diff, lam_init, n_out, 0)
            if not last:
                y = _flash(q, k, va, vb, aux, B, n_ctx, n_lat, ((n_ctx, ctx_base),), is_diff, lam_init,
                           n_out, n_lat, y)
            y_att.append(y)

        xa = _merge((y_hy, y_cf, y_att[0], y_att[1]), gate, w_branch[l].astype(BF16), w_out[l].astype(BF16),
                    xa, mods[l], dims)

        i = l // 2
        if l % 2 == 0:
            xa = _ffn(xa, mods[l], norm_ffn[l].reshape(1, D), ffn_w1[i].astype(BF16), ffn_w3[i].astype(BF16),
                      ffn_w2[i].astype(BF16), dims)
        else:
            tok, gates = _route(xa, mods[l], norm_ffn[l].reshape(1, D), moe_router[i], dims)
            xa = _moe(tok, gates, moe_w1[i].astype(BF16), moe_w3[i].astype(BF16), moe_w2[i].astype(BF16),
                      xa, mods[l], dims)
    return xa[:n_lat].reshape(B, S, D)
```

```python
import functools
import math

import numpy as np
import jax
import jax.numpy as jnp
from jax import lax
from jax.experimental import pallas as pl
from jax.experimental.pallas import tpu as pltpu

F32 = jnp.float32
BF16 = jnp.bfloat16

D_MODEL = 1024
GRID_W = 64
BRANCH_W = 256
EPS = 1e-6
HY_EMB = 33
HY_EMB_PAD = 128
HY_FFN = 64
HY_TARGET = 1e-2
HY_FAST = 0.3
HY_SLOW = 1.5
CF_WIDTH = 31
HEAD_DIM = 64
DIFF_D = 32
ROPE_BASE = 10000.0
N_EXPERTS = 8
LANES = 128
LOG2E = 1.4426950408889634
FIXED_SHIFT_LIMIT = 60.0
SMALL_COLS = 2560
GATE_COLS = 4 * D_MODEL
VMEM_LIMIT = 48 * 1024 * 1024
VMEM_LIMIT_MOE = 56 * 1024 * 1024


def _cp(sem, vmem=VMEM_LIMIT):
    return pltpu.CompilerParams(dimension_semantics=sem, vmem_limit_bytes=vmem)


def _sigmoid(x):
    return 1.0 / (1.0 + jnp.exp(-x))


def _dot(a, b):
    return jnp.dot(a, b, preferred_element_type=F32)


def _split(a):
    hi = a.astype(BF16)
    lo = (a - hi.astype(F32)).astype(BF16)
    return hi, lo


def _dot3(a, b):
    ah, al = _split(a)
    bh, bl = _split(b)
    return _dot(ah, bh) + (_dot(al, bh) + _dot(ah, bl))


def _norm_mod(x, gain, shift, scale):
    ms = jnp.mean(x * x, axis=-1, keepdims=True)
    return (x * lax.rsqrt(ms + EPS) * gain) * (1.0 + scale) + shift


def _ada_kernel(c_ref, w_ref, b_ref, o_ref):
    c = c_ref[...]
    s = c * _sigmoid(c)
    o_ref[0] = _dot3(s, w_ref[0]) + b_ref[0]


def _ada_mods(c16, w_ada, b_ada):
    depth, d, n6 = w_ada.shape
    tn = 512
    return pl.pallas_call(
        _ada_kernel,
        grid=(depth, n6 // tn),
        in_specs=[pl.BlockSpec((16, d), lambda l, j: (0, 0)),
                  pl.BlockSpec((1, d, tn), lambda l, j: (l, 0, j)),
                  pl.BlockSpec((1, 1, tn), lambda l, j: (l, 0, j))],
        out_specs=pl.BlockSpec((1, 16, tn), lambda l, j: (l, 0, j)),
        out_shape=jax.ShapeDtypeStruct((depth, 16, n6), F32),
        compiler_params=_cp(("arbitrary", "arbitrary")),
        name="ada_mods",
    )(c16, w_ada, b_ada.reshape(depth, 1, n6))


def _in_proj_kernel(x_ref, mod_ref, g_ref, w_ref, o_ref, h_scr, *, gate):
    @pl.when(pl.program_id(1) == 0)
    def _():
        m = mod_ref[0]
        h_scr[...] = _norm_mod(x_ref[...], g_ref[...], m[0:1, :], m[1:2, :]).astype(BF16)

    r = _dot(h_scr[...], w_ref[...])
    o_ref[...] = _sigmoid(r).astype(o_ref.dtype) if gate else r


def _in_proj(x, mods_l, gain, w_bf, dims):
    n, d = x.shape
    tm = dims["tm_big"]
    mrow = dims["mod_row"](tm)

    def call(w, tn, gate, dtype):
        n_cols = w.shape[1]
        return pl.pallas_call(
            functools.partial(_in_proj_kernel, gate=gate),
            grid=(n // tm, n_cols // tn),
            in_specs=[pl.BlockSpec((tm, d), lambda i, j: (i, 0)),
                      pl.BlockSpec((1, 6, d), lambda i, j: (mrow(i), 0, 0)),
                      pl.BlockSpec((1, d), lambda i, j: (0, 0)),
                      pl.BlockSpec((d, tn), lambda i, j: (0, j))],
            out_specs=pl.BlockSpec((tm, tn), lambda i, j: (i, j)),
            out_shape=jax.ShapeDtypeStruct((n, n_cols), dtype),
            scratch_shapes=[pltpu.VMEM((tm, d), BF16)],
            compiler_params=_cp(("arbitrary", "arbitrary")),
            name="in_proj_gate" if gate else "in_proj_mix",
        )(x, mods_l, gain, w)

    return (call(w_bf[:, :SMALL_COLS], SMALL_COLS // 2, False, F32),
            call(w_bf[:, SMALL_COLS:], GATE_COLS // 4, True, BF16))


def _short_conv_kernel(u_ref, w_ref, b_ref, o_ref, pad_scr, *, L, tc, planes):
    if planes:
        o_ref = o_ref.at[0]
    pad_scr[0:8, :] = jnp.zeros((8, BRANCH_W), F32)
    pad_scr[8 + L:16 + L, :] = jnp.zeros((8, BRANCH_W), F32)
    pad_scr[8:8 + L, :] = u_ref[...]
    w = w_ref[0]
    b = b_ref[0]

    def body(c, carry):
        t0 = pl.multiple_of(c * tc, tc)
        win = pad_scr[pl.ds(t0, tc + 16), :]
        acc = b + w[0:1, :] * win[7:7 + tc]
        acc = acc + w[1:2, :] * win[8:8 + tc]
        acc = acc + w[2:3, :] * win[9:9 + tc]
        o_ref[pl.ds(t0, tc), :] = acc
        return carry

    lax.fori_loop(0, L // tc, body, 0)


def _short_conv(u_small, w, b, L, nseq, row_base, planes=False):
    w3 = jnp.transpose(w.reshape(3, 3, BRANCH_W), (1, 0, 2))
    w3 = jnp.pad(w3, ((0, 0), (0, 5), (0, 0)))
    b3 = b.reshape(3, 1, BRANCH_W)
    tc = min(L, 256)
    return pl.pallas_call(
        functools.partial(_short_conv_kernel, L=L, tc=tc, planes=planes),
        grid=(nseq, 3),
        in_specs=[pl.BlockSpec((L, BRANCH_W), lambda s, j: (row_base + s, j)),
                  pl.BlockSpec((1, 8, BRANCH_W), lambda s, j: (j, 0, 0)),
                  pl.BlockSpec((1, 1, BRANCH_W), lambda s, j: (j, 0, 0))],
        out_specs=(pl.BlockSpec((1, L, BRANCH_W), lambda s, j: (j, s, 0)) if planes
                   else pl.BlockSpec((L, BRANCH_W), lambda s, j: (s, j))),
        out_shape=jax.ShapeDtypeStruct((3, nseq * L, BRANCH_W) if planes else (nseq * L, 3 * BRANCH_W), F32),
        scratch_shapes=[pltpu.VMEM((L + 16, BRANCH_W), F32)],
        compiler_params=_cp(("arbitrary", "arbitrary")),
        name="hy_short_conv",
    )(u_small, w3, b3)


def _filter_consts(L):
    t = np.linspace(0.0, 1.0, L)[:, None]
    bands = (HY_EMB - 1) // 2
    fr = np.linspace(1e-4, bands - 1, bands)[None, :]
    wpos = 2.0 * math.pi * np.arange(L)[:, None] / L
    z = np.concatenate([t, np.cos(fr * wpos), -np.sin(fr * wpos)], axis=-1)
    z = np.pad(z, ((0, 0), (0, HY_EMB_PAD - HY_EMB)))
    deltas = np.abs(np.linspace(math.log(HY_TARGET) / HY_SLOW, math.log(HY_TARGET) / HY_FAST, BRANCH_W))
    win = np.exp(-t * deltas[None, :])
    return jnp.asarray(z, F32), jnp.asarray(win, F32)


def _filter_kernel(z_ref, w1, b1, w2, b2, w3, b3, w4, fq, win_ref, hf_ref, ss_ref, *, tr, planes):
    i = pl.program_id(0)
    f = fq[...]
    a = jnp.sin(f * (_dot3(z_ref[...], w1[...]) + b1[...]))
    a = jnp.sin(f * (_dot3(a, w2[...]) + b2[...]))
    a = jnp.sin(f * (_dot3(a, w3[...]) + b3[...]))
    h = _dot3(a, w4[...])
    win = win_ref[...]
    h = h * jnp.concatenate([win, win, win, win], axis=1)
    row = lax.broadcasted_iota(jnp.int32, h.shape, 0) + i * tr
    col = lax.broadcasted_iota(jnp.int32, h.shape, 1)
    h = jnp.where((row == 0) & (col >= 2 * BRANCH_W), 0.0, h)
    if planes:
        for d in range(4):
            hf_ref[d] = h[:, d * BRANCH_W:(d + 1) * BRANCH_W]
    else:
        hf_ref[...] = h

    @pl.when(i == 0)
    def _():
        ss_ref[...] = jnp.zeros_like(ss_ref)

    ss_ref[...] += jnp.sum(h * h, axis=0, keepdims=True)


def _hyena_filter(L, w1, b1, w2, b2, w3, b3, w4, freq, planes=False):
    z, win = _filter_consts(L)
    tr = min(L, 256)
    w1p = jnp.pad(w1, ((0, HY_EMB_PAD - HY_EMB), (0, 0)))
    full = lambda a: pl.BlockSpec(a.shape, lambda i: (0,) * a.ndim)
    args = [w1p, b1.reshape(1, -1), w2, b2.reshape(1, -1), w3, b3.reshape(1, -1), w4, freq.reshape(1, -1)]
    n_out = w4.shape[1]
    return pl.pallas_call(
        functools.partial(_filter_kernel, tr=tr, planes=planes),
        grid=(L // tr,),
        in_specs=[pl.BlockSpec((tr, HY_EMB_PAD), lambda i: (i, 0))] + [full(a) for a in args]
        + [pl.BlockSpec((tr, BRANCH_W), lambda i: (i, 0))],
        out_specs=[pl.BlockSpec((4, tr, BRANCH_W), lambda i: (0, i, 0)) if planes
                   else pl.BlockSpec((tr, n_out), lambda i: (i, 0)),
                   pl.BlockSpec((1, n_out), lambda i: (0, 0))],
        out_shape=[jax.ShapeDtypeStruct((4, L, BRANCH_W) if planes else (L, n_out), F32),
                   jax.ShapeDtypeStruct((1, n_out), F32)],
        compiler_params=_cp(("arbitrary",)),
        name="hy_filter",
    )(z, *args, win)


def _dft_mats(L):
    N = 2 * L
    blk = 64
    k = jnp.arange(L, dtype=jnp.int32)[:, None]
    nh = jnp.arange(L // blk, dtype=jnp.int32)[None, :]
    nl = jnp.arange(blk, dtype=jnp.int32)[None, :]
    w = 2.0 * math.pi / N
    a = ((k * (blk * nh)) % N).astype(F32) * w
    b = ((k * nl) % N).astype(F32) * w
    ca, sa, cb, sb = jnp.cos(a), jnp.sin(a), jnp.cos(b), jnp.sin(b)
    cos = (ca[:, :, None] * cb[:, None, :] - sa[:, :, None] * sb[:, None, :]).reshape(L, L)
    sin = (sa[:, :, None] * cb[:, None, :] + ca[:, :, None] * sb[:, None, :]).reshape(L, L)
    alt = jnp.where(jnp.arange(L) % 2 == 0, 1.0, -1.0).astype(F32)
    first = (jnp.arange(L) == 0)
    s_f = jnp.where(first[:, None], alt[None, :], -sin)
    fwd = jnp.concatenate([cos, s_f], axis=0).astype(BF16)
    colscale = jnp.where(first, 1.0 / N, 2.0 / N).astype(F32)
    g_c = cos * colscale[None, :]
    g_s = jnp.where(first[None, :], alt[:, None] / N, -sin * (2.0 / N))
    inv = jnp.concatenate([g_c, g_s], axis=1).astype(BF16)
    return fwd, inv


def _dft_fwd_kernel(f_ref, z_ref, o_ref):
    o_ref[0] = _dot(f_ref[...], z_ref[...].astype(BF16))


def _dft_fwd(fwd, z2d, L, nb, zmap):
    tm = min(2 * L, 1024)
    return pl.pallas_call(
        _dft_fwd_kernel,
        grid=(2 * L // tm, nb),
        in_specs=[pl.BlockSpec((tm, L), lambda i, b: (i, 0)),
                  pl.BlockSpec((L, BRANCH_W), lambda i, b: zmap(b))],
        out_specs=pl.BlockSpec((1, tm, BRANCH_W), lambda i, b: (b, i, 0)),
        out_shape=jax.ShapeDtypeStruct((nb, 2 * L, BRANCH_W), F32),
        compiler_params=_cp(("arbitrary", "arbitrary")),
        name="hy_dft_fwd",
    )(fwd, z2d)


def _spec_prod_kernel(z_ref, kf_ref, kb_ref, sf_ref, sb_ref, y_ref, *, tk):
    i = pl.program_id(1)
    s = lax.rsqrt(sf_ref[...] + sb_ref[...] + EPS)
    zr, zi = z_ref[0, 0], z_ref[0, 1]
    fr, fi = kf_ref[0, 0], kf_ref[0, 1]
    br, bi = kb_ref[0, 0], kb_ref[0, 1]
    row = lax.broadcasted_iota(jnp.int32, (tk, BRANCH_W), 0) + i * tk
    first = row == 0
    kr = (fr + br) * s
    ki = jnp.where(first, fi + bi, fi - bi) * s
    yr = jnp.where(first, zr * kr, zr * kr - zi * ki)
    yi = jnp.where(first, zi * ki, zr * ki + zi * kr)
    y_ref[0, 0] = yr.astype(BF16)
    y_ref[0, 1] = yi.astype(BF16)


def _spec_prod(zf, kfs, ss, order, L, nb):
    tk = min(L, 512)
    zf4 = zf.reshape(nb, 2, L, BRANCH_W)
    kf4 = kfs.reshape(4, 2, L, BRANCH_W)
    y = pl.pallas_call(
        functools.partial(_spec_prod_kernel, tk=tk),
        grid=(nb, L // tk),
        in_specs=[pl.BlockSpec((1, 2, tk, BRANCH_W), lambda b, i: (b, 0, i, 0)),
                  pl.BlockSpec((1, 2, tk, BRANCH_W), lambda b, i: (order, 0, i, 0)),
                  pl.BlockSpec((1, 2, tk, BRANCH_W), lambda b, i: (2 + order, 0, i, 0)),
                  pl.BlockSpec((1, BRANCH_W), lambda b, i: (0, order)),
                  pl.BlockSpec((1, BRANCH_W), lambda b, i: (0, 2 + order))],
        out_specs=pl.BlockSpec((1, 2, tk, BRANCH_W), lambda b, i: (b, 0, i, 0)),
        out_shape=jax.ShapeDtypeStruct((nb, 2, L, BRANCH_W), BF16),
        compiler_params=_cp(("arbitrary", "arbitrary")),
        name="hy_spec_prod",
    )(zf4, kf4, kf4, ss, ss)
    return y.reshape(nb, 2 * L, BRANCH_W)


def _dft_inv_kernel(g_ref, y_ref, gate_ref, zp_ref, bias_ref, *rest):
    o_ref = rest[-1]
    conv = _dot(g_ref[...], y_ref[0])
    o_ref[...] = (gate_ref[...] * (conv + bias_ref[...] * zp_ref[...])).astype(o_ref.dtype)


def _dft_inv(inv, y, xs, gate_col, zprev, zprev_col, bias, L, nb, out_dtype, out_rows=None, out_base=0,
             prev=None):
    tm = min(L, 512)
    nt = L // tm
    args = [inv, y, xs, zprev, bias]
    in_specs = [pl.BlockSpec((tm, 2 * L), lambda i, b: (i, 0)),
                pl.BlockSpec((1, 2 * L, BRANCH_W), lambda i, b: (b, 0, 0)),
                pl.BlockSpec((tm, BRANCH_W), lambda i, b: (b * nt + i, gate_col)),
                pl.BlockSpec((tm, BRANCH_W), lambda i, b: (b * nt + i, zprev_col)),
                pl.BlockSpec((1, BRANCH_W), lambda i, b: (0, 0))]
    args, in_specs, alias = _into(prev, args, in_specs)
    return pl.pallas_call(
        _dft_inv_kernel,
        grid=(nt, nb),
        in_specs=in_specs,
        out_specs=pl.BlockSpec((tm, BRANCH_W), lambda i, b: ((out_base + b) * nt + i, 0)),
        out_shape=jax.ShapeDtypeStruct((out_rows or nb * L, BRANCH_W), out_dtype),
        input_output_aliases=alias,
        compiler_params=_cp(("arbitrary", "arbitrary")),
        name="hy_dft_inv",
    )(*args)


def _hyena(u_small, L, nseq, row_base, p, mats, out_rows, prev=None):
    (short_w, short_b, w1, b1, w2, b2, w3, b3, w4, freq, bias) = p
    fwd, inv = mats
    xs = _short_conv(u_small, short_w, short_b, L, nseq, row_base)
    hf, ss = _hyena_filter(L, w1, b1, w2, b2, w3, b3, w4, freq)
    kfs = _dft_fwd(fwd, hf, L, 4, lambda b: (0, b))
    zf = _dft_fwd(fwd, xs, L, nseq, lambda b: (b, 2))
    y = _spec_prod(zf, kfs, ss, 0, L, nseq)
    z1 = _dft_inv(inv, y, xs, 0, xs, 2, bias[0:1], L, nseq, F32)
    zf = _dft_fwd(fwd, z1, L, nseq, lambda b: (b, 0))
    y = _spec_prod(zf, kfs, ss, 1, L, nseq)
    return _dft_inv(inv, y, xs, 1, z1, 0, bias[1:2], L, nseq, BF16, out_rows, row_base, prev)


FFT_N1 = 64
FFT_KB = 16
FFT_COLS = 16384


def _fft_consts(L):
    N = 2 * L
    n2s = N // FFT_N1
    h = FFT_N1 // 2
    k1 = jnp.arange(FFT_N1, dtype=jnp.int32)
    ang_a = ((k1[:, None] * jnp.arange(h, dtype=jnp.int32)[None, :]) % FFT_N1).astype(F32) * (2.0 * math.pi / FFT_N1)
    ca, sa = jnp.cos(ang_a), jnp.sin(ang_a)
    fa = jnp.concatenate([ca, -sa], axis=0).astype(BF16)
    fai = (jnp.concatenate([ca.T, -sa.T], axis=1) / N).astype(BF16)
    k = k1[:, None, None] + FFT_N1 * jnp.arange(n2s, dtype=jnp.int32)[None, :, None]
    n2 = jnp.arange(n2s, dtype=jnp.int32)[None, None, :]
    ang = ((k * n2) % N).astype(F32) * (2.0 * math.pi / N)
    c, s = jnp.cos(ang), jnp.sin(ang)
    g = jnp.concatenate([jnp.concatenate([c, s], axis=2), jnp.concatenate([-s, c], axis=2)], axis=1)
    ct, st = jnp.swapaxes(c, 1, 2), jnp.swapaxes(s, 1, 2)
    gi = jnp.concatenate([jnp.concatenate([ct, -st], axis=2), jnp.concatenate([st, ct], axis=2)], axis=1)
    return fa, fai, g.astype(BF16), gi.astype(BF16)


def _fft_a_kernel(f_ref, z_ref, o_ref):
    o_ref[...] = _dot(f_ref[...], z_ref[0].astype(BF16)).astype(o_ref.dtype)


def _fft_a(fa, z3, zmap, nb, L):
    h = FFT_N1 // 2
    cols = (L // h) * BRANCH_W
    tn = min(cols, FFT_COLS)
    return pl.pallas_call(
        _fft_a_kernel,
        grid=(nb, cols // tn),
        in_specs=[pl.BlockSpec(fa.shape, lambda b, j: (0, 0)),
                  pl.BlockSpec((1, h, tn), lambda b, j: zmap(b) + (j,))],
        out_specs=pl.BlockSpec((2 * FFT_N1, tn), lambda b, j: (b, j)),
        out_shape=jax.ShapeDtypeStruct((nb * 2 * FFT_N1, cols), BF16),
        compiler_params=_cp(("arbitrary", "arbitrary")),
        name="hy_fft_a",
    )(fa, z3)


def _fft_spec_kernel(a_ref, g_ref, o_ref):
    for j in range(FFT_KB):
        s = _dot(g_ref[j], jnp.concatenate([a_ref[0, 0, j], a_ref[0, 1, j]], axis=0))
        half = s.shape[0] // 2
        o_ref[0, j, 0] = s[:half]
        o_ref[0, j, 1] = s[half:]


def _fft_spec(g, a5):
    nb, _, n1s, n2s, w = a5.shape
    return pl.pallas_call(
        _fft_spec_kernel,
        grid=(n1s // FFT_KB, nb),
        in_specs=[pl.BlockSpec((1, 2, FFT_KB, n2s, w), lambda i, b: (b, 0, i, 0, 0)),
                  pl.BlockSpec((FFT_KB, 2 * n2s, 2 * n2s), lambda i, b: (i, 0, 0))],
        out_specs=pl.BlockSpec((1, FFT_KB, 2, n2s, w), lambda i, b: (b, i, 0, 0, 0)),
        out_shape=jax.ShapeDtypeStruct((nb, n1s, 2, n2s, w), F32),
        compiler_params=_cp(("arbitrary", "arbitrary")),
        name="hy_fft_spec",
    )(a5, g)


def _fft_conv_kernel(a_ref, g_ref, gi_ref, sf_ref, sb_ref, ssf_ref, ssb_ref, o_ref):
    scale = lax.rsqrt(ssf_ref[...] + ssb_ref[...] + EPS)
    for j in range(FFT_KB):
        s = _dot(g_ref[j], jnp.concatenate([a_ref[0, 0, j], a_ref[0, 1, j]], axis=0))
        half = s.shape[0] // 2
        sr, si = s[:half], s[half:]
        kr = (sf_ref[0, j, 0] + sb_ref[0, j, 0]) * scale
        ki = (sf_ref[0, j, 1] - sb_ref[0, j, 1]) * scale
        y = jnp.concatenate([sr * kr - si * ki, sr * ki + si * kr], axis=0).astype(BF16)
        t = _dot(gi_ref[j], y)
        o_ref[0, 0, j] = t[:half].astype(BF16)
        o_ref[0, 1, j] = t[half:].astype(BF16)


def _fft_conv(g, gi, a5, kspec, ss, order):
    nb, _, n1s, n2s, w = a5.shape
    kblk = lambda d: pl.BlockSpec((1, FFT_KB, 2, n2s, w), lambda i, b: (d, i, 0, 0, 0))
    gblk = pl.BlockSpec((FFT_KB, 2 * n2s, 2 * n2s), lambda i, b: (i, 0, 0))
    ablk = pl.BlockSpec((1, 2, FFT_KB, n2s, w), lambda i, b: (b, 0, i, 0, 0))
    return pl.pallas_call(
        _fft_conv_kernel,
        grid=(n1s // FFT_KB, nb),
        in_specs=[ablk, gblk, gblk, kblk(order), kblk(2 + order),
                  pl.BlockSpec((1, w), lambda i, b: (0, order)),
                  pl.BlockSpec((1, w), lambda i, b: (0, 2 + order))],
        out_specs=ablk,
        out_shape=jax.ShapeDtypeStruct(a5.shape, BF16),
        compiler_params=_cp(("arbitrary", "arbitrary")),
        name="hy_fft_conv",
    )(a5, g, gi, kspec, kspec, ss, ss)


def _fft_ainv_kernel(f_ref, a_ref, gate_ref, zp_ref, bias_ref, o_ref):
    conv = _dot(f_ref[...], a_ref[...])
    o_ref[...] = (gate_ref[0] * (conv + bias_ref[...] * zp_ref[0])).astype(o_ref.dtype)


def _fft_ainv(fai, a2, gate3, gate_plane, zprev3, zprev_plane, bias, L, nb, out_dtype, out_view_rows):
    h = FFT_N1 // 2
    cols = (L // h) * BRANCH_W
    tn = min(cols, FFT_COLS)
    return pl.pallas_call(
        _fft_ainv_kernel,
        grid=(nb, cols // tn),
        in_specs=[pl.BlockSpec(fai.shape, lambda b, j: (0, 0)),
                  pl.BlockSpec((2 * FFT_N1, tn), lambda b, j: (b, j)),
                  pl.BlockSpec((1, h, tn), lambda b, j: (gate_plane, b, j)),
                  pl.BlockSpec((1, h, tn), lambda b, j: (zprev_plane, b, j)),
                  pl.BlockSpec((1, tn), lambda b, j: (0, 0))],
        out_specs=pl.BlockSpec((h, tn), lambda b, j: (b, j)),
        out_shape=jax.ShapeDtypeStruct((out_view_rows, cols), out_dtype),
        compiler_params=_cp(("arbitrary", "arbitrary")),
        name="hy_fft_ainv",
    )(fai, a2, gate3, zprev3, jnp.tile(bias, (1, tn // BRANCH_W)))


def _hyena_fft(u_small, L, nseq, p, consts, out_rows):
    (short_w, short_b, w1, b1, w2, b2, w3, b3, w4, freq, bias) = p
    fa, fai, g, gi = consts
    h = FFT_N1 // 2
    n2s = L // h
    cols = n2s * BRANCH_W
    xs3 = _short_conv(u_small, short_w, short_b, L, nseq, 0, planes=True).reshape(3, nseq * h, cols)
    hf4, ss = _hyena_filter(L, w1, b1, w2, b2, w3, b3, w4, freq, planes=True)
    ka = _fft_a(fa, hf4.reshape(4, h, cols), lambda b: (b, 0), 4, L)
    kspec = _fft_spec(g, ka.reshape(4, 2, FFT_N1, n2s, BRANCH_W))

    def conv(z3, zmap, order, gate_plane, zprev_plane, out_dtype, out_view_rows):
        a = _fft_a(fa, z3, zmap, nseq, L)
        a = _fft_conv(g, gi, a.reshape(nseq, 2, FFT_N1, n2s, BRANCH_W), kspec, ss, order)
        return _fft_ainv(fai, a.reshape(nseq * 2 * FFT_N1, cols), xs3, gate_plane, z3, zprev_plane,
                         bias[order:order + 1], L, nseq, out_dtype, out_view_rows)

    z1 = conv(xs3, lambda b: (2, b), 0, 0, 2, F32, nseq * h).reshape(1, nseq * h, cols)
    y = conv(z1, lambda b: (0, b), 1, 1, 0, BF16, out_rows // n2s)
    return y.reshape(out_rows, BRANCH_W)


def _conformer_kernel(a_ref, g_ref, w_ref, b_ref, lg_ref, lb_ref, *rest, L, tc):
    o_ref, pad_scr = rest[-2:]
    pad_scr[0:16, :] = jnp.zeros((16, BRANCH_W), F32)
    pad_scr[16 + L:32 + L, :] = jnp.zeros((16, BRANCH_W), F32)

    def glu(c, carry):
        t0 = pl.multiple_of(c * tc, tc)
        pad_scr[pl.ds(16 + t0, tc), :] = a_ref[pl.ds(t0, tc), :] * _sigmoid(g_ref[pl.ds(t0, tc), :])
        return carry

    lax.fori_loop(0, L // tc, glu, 0)
    b = b_ref[...]
    lg = lg_ref[...]
    lb = lb_ref[...]

    def body(c, carry):
        t0 = pl.multiple_of(c * tc, tc)
        win = pad_scr[pl.ds(t0, tc + 32), :]
        acc = jnp.zeros((tc, BRANCH_W), F32) + b
        for r in range(8):
            sh = win[r:r + tc + 24]
            for a in range(4):
                m = 8 * a + r
                if 1 <= m <= CF_WIDTH:
                    acc = acc + w_ref[m - 1:m, :] * sh[8 * a:8 * a + tc]
        mu = jnp.mean(acc, axis=-1, keepdims=True)
        xc = acc - mu
        var = jnp.mean(xc * xc, axis=-1, keepdims=True)
        y = xc * lax.rsqrt(var + EPS) * lg + lb
        o_ref[pl.ds(t0, tc), :] = (y * _sigmoid(y)).astype(o_ref.dtype)
        return carry

    lax.fori_loop(0, L // tc, body, 0)


def _conformer(u_small, L, nseq, row_base, p, out_rows, prev=None):
    dw_w, dw_b, ln_g, ln_b = p
    tc = 128
    wpad = jnp.pad(dw_w, ((0, 32 - CF_WIDTH), (0, 0)))
    row = lambda a: a.reshape(1, BRANCH_W)
    vec = pl.BlockSpec((1, BRANCH_W), lambda s: (0, 0))
    args = [u_small, u_small, wpad, row(dw_b), row(ln_g), row(ln_b)]
    in_specs = [pl.BlockSpec((L, BRANCH_W), lambda s: (row_base + s, 3)),
                pl.BlockSpec((L, BRANCH_W), lambda s: (row_base + s, 4)),
                pl.BlockSpec((32, BRANCH_W), lambda s: (0, 0)), vec, vec, vec]
    args, in_specs, alias = _into(prev, args, in_specs)
    return pl.pallas_call(
        functools.partial(_conformer_kernel, L=L, tc=tc),
        grid=(nseq,),
        in_specs=in_specs,
        out_specs=pl.BlockSpec((L, BRANCH_W), lambda s: (row_base + s, 0)),
        out_shape=jax.ShapeDtypeStruct((out_rows, BRANCH_W), BF16),
        input_output_aliases=alias,
        scratch_shapes=[pltpu.VMEM((L + 32, BRANCH_W), F32)],
        compiler_params=_cp(("arbitrary",)),
        name="conformer",
    )(*args)


def _rope_tables(S, pad_rows, head):
    half = head // 2
    nf = half // 2
    lane = np.arange(LANES)
    inv_lane = (ROPE_BASE ** (-(np.arange(nf)) / nf))[(lane % half) % nf]
    is_row = (lane % head) < half
    pos = jnp.arange(S, dtype=jnp.int32)
    rows = (pos // GRID_W).astype(F32)[:, None]
    cols = (pos % GRID_W).astype(F32)[:, None]
    ang = jnp.where(jnp.asarray(is_row)[None, :], rows, cols) * jnp.asarray(inv_lane, F32)[None, :]
    cos = jnp.concatenate([jnp.cos(ang), jnp.ones((pad_rows, LANES), F32)], axis=0)
    sin = jnp.concatenate([jnp.sin(ang), jnp.zeros((pad_rows, LANES), F32)], axis=0)
    return cos, sin


def _group_ones(width, group):
    idx = np.arange(width)
    return jnp.asarray((idx[:, None] // group) == (idx[None, :] // group), BF16)


def _head_norm_rope(x, ones, group, gain, cos, sin, nf, out_scale):
    w = x.shape[1]
    hi, lo = _split(x * x)
    ms = (_dot(hi, ones) + _dot(lo, ones)) * (1.0 / group)
    xn = x * lax.rsqrt(ms + EPS) * gain
    reps = w // LANES
    c = jnp.concatenate([cos] * reps, axis=1) if reps > 1 else cos
    s = jnp.concatenate([sin] * reps, axis=1) if reps > 1 else sin
    lane = lax.broadcasted_iota(jnp.int32, x.shape, 1)
    first = (lane % (2 * nf)) < nf
    rot = jnp.where(first, -pltpu.roll(xn, w - nf, 1), pltpu.roll(xn, nf, 1))
    return (xn * c + rot * s) * out_scale


def _qkv_kernel(gq_ref, gkv_ref, dq_ref, dk_ref, dv_ref, cg_ref, sg_ref, cd_ref, sd_ref,
                o64_ref, o32_ref, gqn_ref, gkn_ref, dqn_ref, dkn_ref,
                qg_ref, kg_ref, vag_ref, vbg_ref, qd_ref, kd_ref, vad_ref, vbd_ref):
    cg, sg, cd, sd = cg_ref[...], sg_ref[...], cd_ref[...], sd_ref[...]
    o64, o32 = o64_ref[...], o32_ref[...]
    q = _head_norm_rope(gq_ref[...], o64, HEAD_DIM, gqn_ref[...], cg, sg, HEAD_DIM // 4, LOG2E * HEAD_DIM ** -0.5)
    qg_ref[...] = q.astype(BF16)
    kv = gkv_ref[...]
    k = _head_norm_rope(kv[:, :LANES], o64[:LANES, :LANES], HEAD_DIM, gkn_ref[...], cg, sg, HEAD_DIM // 4, 1.0)
    v = kv[:, LANES:]
    kk = jnp.concatenate([k, k], axis=1)
    vv = jnp.concatenate([v, v], axis=1)
    quarter = lax.broadcasted_iota(jnp.int32, kk.shape, 1) // HEAD_DIM
    kr = pltpu.roll(kk, HEAD_DIM, 1)
    vr = pltpu.roll(vv, HEAD_DIM, 1)
    kg_ref[...] = jnp.where((quarter == 0) | (quarter == 3), kk, kr).astype(BF16)
    vag_ref[...] = jnp.where(quarter == 0, vv, jnp.where(quarter == 2, vr, 0.0)).astype(BF16)
    vbg_ref[...] = jnp.where(quarter == 1, vr, jnp.where(quarter == 3, vv, 0.0)).astype(BF16)
    qd = _head_norm_rope(dq_ref[...], o32, DIFF_D, dqn_ref[...], cd, sd, DIFF_D // 4, LOG2E * DIFF_D ** -0.5)
    qd_ref[...] = qd.astype(BF16)
    kd = _head_norm_rope(dk_ref[...], o32, DIFF_D, dkn_ref[...], cd, sd, DIFF_D // 4, 1.0)
    kd_ref[...] = kd.astype(BF16)
    vd = dv_ref[...]
    even = (lax.broadcasted_iota(jnp.int32, vd.shape, 1) // HEAD_DIM) % 2 == 0
    vad_ref[...] = jnp.where(even, vd, 0.0).astype(BF16)
    vbd_ref[...] = jnp.where(even, 0.0, vd).astype(BF16)


def _qkv_prep(u_small, gqn, gkn, dqn, dkn, tables, dims):
    n = u_small.shape[0]
    tm = dims["tm_small"]
    S = dims["S"]
    n_lat_tiles = dims["n_lat"] // tm
    per_seq = S // tm
    tmap = lambda i: (jnp.where(i < n_lat_tiles, i % per_seq, per_seq), 0)
    col = lambda c: pl.BlockSpec((tm, BRANCH_W), lambda i: (i, c))
    tab = pl.BlockSpec((tm, LANES), tmap)
    full = lambda a: pl.BlockSpec(a.shape, lambda i: (0,) * a.ndim)
    o64, o32 = _group_ones(BRANCH_W, HEAD_DIM), _group_ones(BRANCH_W, DIFF_D)
    gains = [jnp.tile(gqn, 4).reshape(1, 256), jnp.tile(gkn, 2).reshape(1, 128),
             jnp.tile(dqn.reshape(-1), 4).reshape(1, 256), jnp.tile(dkn.reshape(-1), 4).reshape(1, 256)]
    out = pl.BlockSpec((tm, BRANCH_W), lambda i: (i, 0))
    return pl.pallas_call(
        _qkv_kernel,
        grid=(n // tm,),
        in_specs=[col(5), col(6), col(7), col(8), col(9), tab, tab, tab, tab, full(o64), full(o32)]
        + [full(g) for g in gains],
        out_specs=[out] * 8,
        out_shape=[jax.ShapeDtypeStruct((n, BRANCH_W), BF16)] * 8,
        compiler_params=_cp(("arbitrary",)),
        name="qkv_prep",
    )(u_small, u_small, u_small, u_small, u_small, *tables, o64, o32, *gains)


def _lane_pick(lane_lo, a, b):
    return jnp.where(lane_lo, a, b)


def _flash_kernel(*refs, segs, tk, tq, diff, lam_init):
    q_ref, o_ref = refs[0], refs[-1]
    seg_refs = [refs[1 + 3 * i:4 + 3 * i] for i in range(len(segs))]
    aux_ref = refs[1 + 3 * len(segs)]
    lane = lax.broadcasted_iota(jnp.int32, (1, LANES), 1)
    lane_lo = lane < HEAD_DIM
    if diff:
        masks = [(lane >= g * DIFF_D) & (lane < (g + 1) * DIFF_D) for g in range(4)]
        acc_of = [0, 1, 0, 1]
        use_a = [True, True, False, False]
        n_acc = 2
    else:
        masks = [lane_lo, ~lane_lo]
        acc_of = [0, 0]
        use_a = [True, False]
        n_acc = 1
    R = len(masks)
    pairs = [slice(p * LANES, (p + 1) * LANES) for p in range(2)]
    qsts = [jnp.concatenate([jnp.where(m, q_ref[:, ps], jnp.zeros((tq, LANES), BF16)) for m in masks], axis=0)
            for ps in pairs]
    lo_hi = []
    for a in range(n_acc):
        rs = [r for r in range(R) if acc_of[r] == a]
        lo_hi.append(([r for r in rs if use_a[r]][0], [r for r in rs if not use_a[r]][0]))

    def rows(x, r):
        return x[r * tq:(r + 1) * tq]

    shift = jnp.max(aux_ref[5:6, :])

    def run(fixed):
        def chunk(kv_refs, t0, size, carry):
            k_ref, va_ref, vb_ref = kv_refs
            new = []
            for p, ps in enumerate(pairs):
                m_run, l_run, accs = carry[p]
                k = k_ref[pl.ds(t0, size), ps]
                s = lax.dot_general(qsts[p], k, (((1,), (1,)), ((), ())), preferred_element_type=F32)
                if fixed:
                    m_new = m_run
                    pr = jnp.exp2(s - shift)
                    l_new = l_run + jnp.sum(pr, axis=-1, keepdims=True)
                else:
                    m_new = jnp.maximum(m_run, jnp.max(s, axis=-1, keepdims=True))
                    alpha = jnp.exp2(m_run - m_new)
                    pr = jnp.exp2(s - m_new)
                    l_new = alpha * l_run + jnp.sum(pr, axis=-1, keepdims=True)
                prb = pr.astype(BF16)
                va = va_ref[pl.ds(t0, size), ps]
                vb = vb_ref[pl.ds(t0, size), ps]
                new_accs = []
                for a, (r_lo, r_hi) in enumerate(lo_hi):
                    upd = _dot(rows(prb, r_lo), va) + _dot(rows(prb, r_hi), vb)
                    if fixed:
                        new_accs.append(accs[a] + upd)
                    else:
                        al = _lane_pick(lane_lo, rows(alpha, r_lo), rows(alpha, r_hi))
                        new_accs.append(accs[a] * al + upd)
                new.append((m_new, l_new, tuple(new_accs)))
            return tuple(new)

        one = (jnp.full((R * tq, 1), -jnp.inf, F32), jnp.zeros((R * tq, 1), F32),
               tuple(jnp.zeros((tq, LANES), F32) for _ in range(n_acc)))
        carry = (one, one)
        for kv_refs, T in zip(seg_refs, segs):
            n_main = T // tk
            if n_main:
                carry = lax.fori_loop(
                    0, n_main, lambda c, cr, kv_refs=kv_refs: chunk(kv_refs, pl.multiple_of(c * tk, tk), tk, cr),
                    carry, unroll=2 if n_main % 2 == 0 else 1)
            if T - n_main * tk:
                carry = chunk(kv_refs, n_main * tk, T - n_main * tk, carry)

        outs = []
        for p in range(2):
            _, l_fin, accs = carry[p]
            inv_l = 1.0 / l_fin
            norm = [accs[a] * _lane_pick(lane_lo, rows(inv_l, r_lo), rows(inv_l, r_hi))
                    for a, (r_lo, r_hi) in enumerate(lo_hi)]
            if diff:
                aux = aux_ref[...]
                lam = (jnp.exp(jnp.sum(aux[0:1] * aux[1:2], axis=-1, keepdims=True))
                       - jnp.exp(jnp.sum(aux[2:3] * aux[3:4], axis=-1, keepdims=True)) + lam_init)
                o = norm[0] - lam * norm[1]
                sq = o * o
                s_lo = jnp.sum(jnp.where(lane_lo, sq, 0.0), axis=-1, keepdims=True)
                s_hi = jnp.sum(jnp.where(lane_lo, 0.0, sq), axis=-1, keepdims=True)
                ms = _lane_pick(lane_lo, s_lo, s_hi) * (1.0 / HEAD_DIM)
                o = o * lax.rsqrt(ms + EPS) * aux[4:5] * (1.0 - lam_init)
            else:
                o = norm[0]
            outs.append(o)
        o_ref[...] = jnp.concatenate(outs, axis=1).astype(o_ref.dtype)

    @pl.when(shift < FIXED_SHIFT_LIMIT)
    def _():
        run(True)

    @pl.when(jnp.logical_not(shift < FIXED_SHIFT_LIMIT))
    def _():
        run(False)


def _into(prev, args, in_specs):
    if prev is None:
        return args, in_specs, {}
    return args + [prev], in_specs + [pl.BlockSpec(memory_space=pl.ANY)], {len(args): 0}


def _flash(q, k, va, vb, aux, nb, Lq, q_base, segs, diff, lam_init, out_rows, out_base, prev=None):
    tq = min(512, Lq)
    tk = 512 if diff else 1024
    nt = Lq // tq
    q_base, out_base = q_base // tq, out_base // tq
    args = [q]
    in_specs = [pl.BlockSpec((tq, BRANCH_W), lambda b, i: (q_base + b * nt + i, 0))]
    for length, base in segs:
        spec = pl.BlockSpec((length, BRANCH_W), lambda b, i, base=base: (base + b, 0))
        args += [k, va, vb]
        in_specs += [spec, spec, spec]
    args.append(aux)
    in_specs.append(pl.BlockSpec((8, LANES), lambda b, i: (0, 0)))
    args, in_specs, alias = _into(prev, args, in_specs)
    return pl.pallas_call(
        functools.partial(_flash_kernel, segs=tuple(s[0] for s in segs), tk=tk, tq=tq, diff=diff,
                          lam_init=lam_init),
        grid=(nb, nt),
        in_specs=in_specs,
        out_specs=pl.BlockSpec((tq, BRANCH_W), lambda b, i: (out_base + b * nt + i, 0)),
        out_shape=jax.ShapeDtypeStruct((out_rows, BRANCH_W), BF16),
        input_output_aliases=alias,
        compiler_params=_cp(("arbitrary", "arbitrary")),
        name="flash_diff" if diff else "flash_gqa",
    )(*args)


def _merge_kernel(y0, y1, y2, y3, g_ref, wb_ref, wo_ref, x_ref, mod_ref, o_ref):
    d = D_MODEL
    acc = g_ref[:, 0:d].astype(F32) * _dot(y0[...], wb_ref[0])
    acc = acc + g_ref[:, d:2 * d].astype(F32) * _dot(y1[...], wb_ref[1])
    acc = acc + g_ref[:, 2 * d:3 * d].astype(F32) * _dot(y2[...], wb_ref[2])
    acc = acc + g_ref[:, 3 * d:4 * d].astype(F32) * _dot(y3[...], wb_ref[3])
    mix = _dot(acc.astype(BF16), wo_ref[...])
    o_ref[...] = x_ref[...] + mod_ref[0][2:3, :] * mix


def _merge(ys, gate, wb, wo, x, mods_l, dims):
    n, d = ys[0].shape[0], x.shape[1]
    tm = dims["tm_small"]
    mrow = dims["mod_row"](tm)
    yspec = pl.BlockSpec((tm, BRANCH_W), lambda i: (i, 0))
    return pl.pallas_call(
        _merge_kernel,
        grid=(n // tm,),
        in_specs=[yspec] * 4 + [pl.BlockSpec((tm, GATE_COLS), lambda i: (i, 0)),
                                pl.BlockSpec(wb.shape, lambda i: (0, 0, 0)),
                                pl.BlockSpec(wo.shape, lambda i: (0, 0)),
                                pl.BlockSpec((tm, d), lambda i: (i, 0)),
                                pl.BlockSpec((1, 6, d), lambda i: (mrow(i), 0, 0))],
        out_specs=pl.BlockSpec((tm, d), lambda i: (i, 0)),
        out_shape=jax.ShapeDtypeStruct((n, d), F32),
        compiler_params=_cp(("arbitrary",)),
        name="merge",
    )(*ys, gate, wb, wo, x, mods_l)


def _ffn_kernel(x_ref, mod_ref, g_ref, w1_ref, w3_ref, w2_ref, o_ref, h_scr, acc_scr):
    j = pl.program_id(1)

    @pl.when(j == 0)
    def _():
        m = mod_ref[0]
        h_scr[...] = _norm_mod(x_ref[...], g_ref[...], m[3:4, :], m[4:5, :]).astype(BF16)
        acc_scr[...] = jnp.zeros_like(acc_scr)

    h = h_scr[...]
    a = _dot(h, w1_ref[...])
    b = _dot(h, w3_ref[...])
    t = (a * _sigmoid(a) * b).astype(BF16)
    acc_scr[...] += _dot(t, w2_ref[...])

    @pl.when(j == pl.num_programs(1) - 1)
    def _():
        o_ref[...] = x_ref[...] + mod_ref[0][5:6, :] * acc_scr[...]


def _ffn(x, mods_l, gain, w1, w3, w2, dims):
    n, d = x.shape
    tm, tf = dims["tm_big"], 256
    mrow = dims["mod_row"](tm)
    return pl.pallas_call(
        _ffn_kernel,
        grid=(n // tm, w1.shape[1] // tf),
        in_specs=[pl.BlockSpec((tm, d), lambda i, j: (i, 0)),
                  pl.BlockSpec((1, 6, d), lambda i, j: (mrow(i), 0, 0)),
                  pl.BlockSpec((1, d), lambda i, j: (0, 0)),
                  pl.BlockSpec((d, tf), lambda i, j: (0, j)),
                  pl.BlockSpec((d, tf), lambda i, j: (0, j)),
                  pl.BlockSpec((tf, d), lambda i, j: (j, 0))],
        out_specs=pl.BlockSpec((tm, d), lambda i, j: (i, 0)),
        out_shape=jax.ShapeDtypeStruct((n, d), F32),
        scratch_shapes=[pltpu.VMEM((tm, d), BF16), pltpu.VMEM((tm, d), F32)],
        compiler_params=_cp(("arbitrary", "arbitrary")),
        name="ffn_dense",
    )(x, mods_l, gain, w1, w3, w2)


def _route_kernel(x_ref, mod_ref, g_ref, wr_ref, tok_ref, gate_ref):
    m = mod_ref[0]
    h = _norm_mod(x_ref[...], g_ref[...], m[3:4, :], m[4:5, :])
    tok_ref[...] = h.astype(BF16)
    logits = _dot3(h, wr_ref[...])
    lane = lax.broadcasted_iota(jnp.int32, logits.shape, 1)
    lg = jnp.where(lane < N_EXPERTS, logits, -jnp.inf)
    m1 = jnp.max(lg, axis=-1, keepdims=True)
    i1 = jnp.min(jnp.where(lg == m1, lane, LANES), axis=-1, keepdims=True)
    lg2 = jnp.where(lane == i1, -jnp.inf, lg)
    m2 = jnp.max(lg2, axis=-1, keepdims=True)
    i2 = jnp.min(jnp.where(lg2 == m2, lane, LANES), axis=-1, keepdims=True)
    e2 = jnp.exp(m2 - m1)
    g1 = 1.0 / (1.0 + e2)
    gate_ref[...] = jnp.where(lane == i1, g1, jnp.where(lane == i2, e2 * g1, 0.0))


def _route(x, mods_l, gain, w_router, dims):
    n, d = x.shape
    tm = dims["tm_small"]
    mrow = dims["mod_row"](tm)
    wr = jnp.pad(w_router, ((0, 0), (0, LANES - N_EXPERTS)))
    return pl.pallas_call(
        _route_kernel,
        grid=(n // tm,),
        in_specs=[pl.BlockSpec((tm, d), lambda i: (i, 0)),
                  pl.BlockSpec((1, 6, d), lambda i: (mrow(i), 0, 0)),
                  pl.BlockSpec((1, d), lambda i: (0, 0)),
                  pl.BlockSpec((d, LANES), lambda i: (0, 0))],
        out_specs=[pl.BlockSpec((tm, d), lambda i: (i, 0)), pl.BlockSpec((tm, LANES), lambda i: (i, 0))],
        out_shape=[jax.ShapeDtypeStruct((n, d), BF16), jax.ShapeDtypeStruct((n, LANES), F32)],
        compiler_params=_cp(("arbitrary",)),
        name="moe_route",
    )(x, mods_l, gain, wr)


MOE_SUB = 128
MOE_GATHER = 256
MOE_GRAN = 64


def _moe_kernel(tok_ref, gate_ref, tri_ref, w1_ref, w3_ref, w2_ref, x_ref, mod_ref, o_ref,
                rank_scr, rank_t_scr, mask_t_scr, xe_scr, ye_scr, cnt_smem, *, tm):
    e = pl.program_id(1)
    j = pl.program_id(2)
    ne = pl.num_programs(1)
    nj = pl.num_programs(2)
    lane = lax.broadcasted_iota(jnp.int32, (1, LANES), 1)

    @pl.when((e == 0) & (j == 0))
    def _():
        tri = tri_ref[...]
        carry = jnp.zeros((1, LANES), F32)
        def put_counts(blk, counts):
            for ee in range(N_EXPERTS):
                cnt_smem[blk * N_EXPERTS + ee] = jnp.sum(jnp.where(lane == ee, counts, 0.0)).astype(jnp.int32)

        for blk in range(tm // 256):
            put_counts(blk, carry)
            rs = slice(blk * 256, (blk + 1) * 256)
            msk = (gate_ref[rs, :] > 0.0).astype(BF16)
            rank_scr[rs, :] = _dot(tri, msk) + carry
            carry = carry + jnp.sum(msk.astype(F32), axis=0, keepdims=True)
        rank_t_scr[...] = jnp.transpose(rank_scr[...])
        mask_t_scr[...] = jnp.transpose((gate_ref[...] > 0.0).astype(F32))
        put_counts(tm // 256, carry)
        o_ref[...] = jnp.zeros_like(o_ref)
        ye_scr[...] = jnp.zeros_like(ye_scr)

    cnt = cnt_smem[(tm // 256) * N_EXPERTS + e]

    n_sub = (cnt + (MOE_GATHER - 1)) // MOE_GATHER

    @pl.when(j == 0)
    def _():
        rk = rank_t_scr[pl.ds(e, 1), :]
        mk = mask_t_scr[pl.ds(e, 1), :]

        def gather(sb, carry):
            r0 = pl.multiple_of(sb * MOE_GATHER, MOE_GATHER)
            r_iota = lax.broadcasted_iota(jnp.int32, (MOE_GATHER, tm), 0) + r0
            sel = jnp.where((rk == r_iota.astype(F32)) & (mk > 0.0), 1.0, 0.0).astype(BF16)
            xe_scr[pl.ds(r0, MOE_GATHER), :] = _dot(sel, tok_ref[...]).astype(BF16)
            return carry

        lax.fori_loop(0, n_sub, gather, 0)

    def expert_rows(r0, size):
        xe = xe_scr[pl.ds(r0, size), :]
        a = _dot(xe, w1_ref[0])
        b = _dot(xe, w3_ref[0])
        t = (a * _sigmoid(a) * b).astype(BF16)
        y = _dot(t, w2_ref[0])
        ye_scr[pl.ds(r0, size), :] = jnp.where(j == 0, y, ye_scr[pl.ds(r0, size), :] + y)

    n_gran = (cnt + (MOE_GRAN - 1)) // MOE_GRAN
    n_big = n_gran // 4
    rem = n_gran % 4

    def big(i, carry):
        expert_rows(pl.multiple_of(i * 256, 256), 256)
        return carry

    lax.fori_loop(0, n_big - 1, big, 0)
    for r in range(min(4, (tm - 256) // MOE_GRAN + 1)):
        @pl.when((n_big >= 1) & (rem == r))
        def _():
            expert_rows(pl.multiple_of((n_big - 1) * 256, 256), 256 + r * MOE_GRAN)

    @pl.when((n_big == 0) & (rem >= 2))
    def _():
        expert_rows(0, 128)

    @pl.when((n_big == 0) & (rem % 2 == 1))
    def _():
        expert_rows(pl.multiple_of((rem // 2) * 128, 128), 64)

    @pl.when(j == nj - 1)
    def _():
        def scatter(rc, carry):
            r0 = pl.multiple_of(rc * 256, 256)
            gt = gate_ref[pl.ds(r0, 256), :]
            g_e = jnp.sum(jnp.where(lane == e, gt, 0.0), axis=-1, keepdims=True)
            r_e = jnp.sum(jnp.where(lane == e, rank_scr[pl.ds(r0, 256), :], 0.0), axis=-1, keepdims=True)
            lo = cnt_smem[rc * N_EXPERTS + e]
            hi = cnt_smem[(rc + 1) * N_EXPERTS + e]

            def scatter_sub(sb, c2):
                c0 = pl.multiple_of(sb * MOE_SUB, MOE_SUB)
                c_iota = lax.broadcasted_iota(jnp.int32, (256, MOE_SUB), 1) + c0
                sel_t = jnp.where((r_e == c_iota.astype(F32)) & (g_e > 0.0), 1.0, 0.0).astype(BF16)
                ye = ye_scr[pl.ds(c0, MOE_SUB), :].astype(BF16)
                o_ref[pl.ds(r0, 256), :] += g_e * _dot(sel_t, ye)
                return c2

            lax.fori_loop(lo // MOE_SUB, (hi + (MOE_SUB - 1)) // MOE_SUB, scatter_sub, 0)

            @pl.when(e == ne - 1)
            def _():
                o_ref[pl.ds(r0, 256), :] = (x_ref[pl.ds(r0, 256), :]
                                            + mod_ref[0][5:6, :] * o_ref[pl.ds(r0, 256), :])
            return carry

        lax.fori_loop(0, tm // 256, scatter, 0)


def _moe(tok, gates, w1, w3, w2, x, mods_l, dims):
    n, d = tok.shape
    tm = dims["tm_moe"] if n % dims["tm_moe"] == 0 else dims["tm_big"]
    mrow = dims["mod_row"](tm)
    ne, _, ff = w1.shape
    tf = 896
    tri =jnp.asarray(np.tril(np.ones((256, 256)), -1), BF16)
    once = pl.Buffered(1)
    return pl.pallas_call(
        functools.partial(_moe_kernel, tm=tm),
        grid=(n // tm, ne, ff // tf),
        in_specs=[pl.BlockSpec((tm, d), lambda i, e, j: (i, 0), pipeline_mode=once),
                  pl.BlockSpec((tm, LANES), lambda i, e, j: (i, 0), pipeline_mode=once),
                  pl.BlockSpec((256, 256), lambda i, e, j: (0, 0), pipeline_mode=once),
                  pl.BlockSpec((1, d, tf), lambda i, e, j: (e, 0, j)),
                  pl.BlockSpec((1, d, tf), lambda i, e, j: (e, 0, j)),
                  pl.BlockSpec((1, tf, d), lambda i, e, j: (e, j, 0)),
                  pl.BlockSpec((tm, d), lambda i, e, j: (i, 0), pipeline_mode=once),
                  pl.BlockSpec((1, 6, d), lambda i, e, j: (mrow(i), 0, 0))],
        out_specs=pl.BlockSpec((tm, d), lambda i, e, j: (i, 0), pipeline_mode=once),
        out_shape=jax.ShapeDtypeStruct((n, d), F32),
        scratch_shapes=[pltpu.VMEM((tm, LANES), F32), pltpu.VMEM((LANES, tm), F32), pltpu.VMEM((LANES, tm), F32),
                        pltpu.VMEM((tm, d), BF16), pltpu.VMEM((tm, d), F32), pltpu.SMEM(((tm // 256 + 1) * N_EXPERTS,), jnp.int32)],
        compiler_params=_cp(("arbitrary", "arbitrary", "arbitrary"), VMEM_LIMIT_MOE),
        name="moe_experts",
    )(tok, gates, tri, w1, w3, w2, x, mods_l)


def _make_dims(B, S, n_ctx):
    n_lat, n = B * S, B * (S + n_ctx)

    def pick(prefs):
        for t in prefs:
            if S % t == 0 and n_lat % t == 0 and (n - n_lat) % t == 0:
                return t
        raise ValueError("no row tile divides the latent and context token counts")

    def mod_row(tm):
        return lambda i: jnp.minimum((i * tm) // S, B)

    return {"B": B, "S": S, "n_ctx": n_ctx, "n_lat": n_lat, "n": n,
            "tm_big": pick((1024, 512, 256)), "tm_small": pick((512, 256)), "tm_moe": pick((2048, 1024, 512, 256)),
            "mod_row": mod_row}


def kernel(x, c, ctx, c_ctx, w_ada, b_ada, norm_mix, norm_ffn, w_in, hy_short_w, hy_short_b, hy_w1, hy_b1, hy_w2, hy_b2, hy_w3, hy_b3, hy_w4, hy_freq, hy_bias, cf_dw_w, cf_dw_b, cf_ln_g, cf_ln_b, gqa_qn, gqa_kn, diff_qn, diff_kn, diff_lq1, diff_lk1, diff_lq2, diff_lk2, diff_subln, w_branch, w_out, ffn_w1, ffn_w3, ffn_w2, moe_router, moe_w1, moe_w3, moe_w2):
    B, S, D = x.shape
    n_ctx = ctx.shape[1]
    depth = w_ada.shape[0]
    dims = _make_dims(B, S, n_ctx)
    n_lat = dims["n_lat"]
    tm_s = dims["tm_small"]

    c16 = jnp.concatenate([c, c_ctx[None, :], jnp.zeros((16 - B - 1, D), F32)], axis=0)
    mods = _ada_mods(c16, w_ada, b_ada).reshape(depth, 16, 6, D)

    xa = jnp.concatenate([x.reshape(n_lat, D), ctx.reshape(B * n_ctx, D)], axis=0)
    tables = _rope_tables(S, tm_s, HEAD_DIM) + _rope_tables(S, tm_s, DIFF_D)
    fft_lat = _fft_consts(S)
    mats_ctx = _dft_mats(n_ctx)

    for l in range(depth):
        lam_init = 0.8 - 0.6 * math.exp(-0.3 * l)
        u_small, gate = _in_proj(xa, mods[l], norm_mix[l].reshape(1, D), w_in[l].astype(BF16), dims)

        hy_p = (hy_short_w[l], hy_short_b[l], hy_w1[l], hy_b1[l], hy_w2[l], hy_b2[l], hy_w3[l], hy_b3[l],
                hy_w4[l], hy_freq[l], hy_bias[l])
        cf_p = (cf_dw_w[l], cf_dw_b[l], cf_ln_g[l], cf_ln_b[l])
        last = l == depth - 1
        n_out = n_lat if last else dims["n"]
        ctx_base = n_lat // n_ctx
        y_hy = _hyena_fft(u_small, S, B, hy_p, fft_lat, n_out)
        y_cf = _conformer(u_small, S, B, 0, cf_p, n_out)
        if not last:
            y_hy = _hyena(u_small, n_ctx, B, ctx_base, hy_p, mats_ctx, n_out, y_hy)
            y_cf = _conformer(u_small, n_ctx, B, ctx_base, cf_p, n_out, y_cf)

        qg, kg, vag, vbg, qd, kd, vad, vbd = _qkv_prep(u_small, gqa_qn[l], gqa_kn[l], diff_qn[l], diff_kn[l],
                                                      tables, dims)

        aux = jnp.zeros((8, LANES), F32)
        aux = aux.at[0, :DIFF_D].set(diff_lq1[l]).at[1, :DIFF_D].set(diff_lk1[l])
        aux = aux.at[2, :DIFF_D].set(diff_lq2[l]).at[3, :DIFF_D].set(diff_lk2[l])
        aux = aux.at[4, :].set(jnp.tile(diff_subln[l], 2))

        def score_bound(qn, kn, d):
            return 1.02 * d * jnp.max(jnp.abs(qn)) * jnp.max(jnp.abs(kn)) * (LOG2E * d ** -0.5) + 0.1

        bounds = (score_bound(gqa_qn[l], gqa_kn[l], HEAD_DIM), score_bound(diff_qn[l], diff_kn[l], DIFF_D))
        y_att = []
        for (q, k, va, vb, is_diff) in ((qg, kg, vag, vbg, False), (qd, kd, vad, vbd, True)):
            aux = aux.at[5, :].set(bounds[int(is_diff)])
            y = _flash(q, k, va, vb, aux, B, S, 0, ((S, 0), (n_ctx, ctx_base)), is_diff, lam_init, n_out, 0)
            if not last:
                y = _flash(q, k, va, vb, aux, B, n_ctx, n_lat, ((n_ctx, ctx_base),), is_diff, lam_init,
                           n_out, n_lat, y)
            y_att.append(y)

        xa = _merge((y_hy, y_cf, y_att[0], y_att[1]), gate, w_branch[l].astype(BF16), w_out[l].astype(BF16),
                    xa, mods[l], dims)

        i = l // 2
        if l % 2 == 0:
            xa = _ffn(xa, mods[l], norm_ffn[l].reshape(1, D), ffn_w1[i].astype(BF16), ffn_w3[i].astype(BF16),
                      ffn_w2[i].astype(BF16), dims)
        else:
            tok, gates = _route(xa, mods[l], norm_ffn[l].reshape(1, D), moe_router[i], dims)
            xa = _moe(tok, gates, moe_w1[i].astype(BF16), moe_w3[i].astype(BF16), moe_w2[i].astype(BF16),
                      xa, mods[l], dims)
    return xa[:n_lat].reshape(B, S, D)
```

```python
import functools
import math

import numpy as np
import jax
import jax.numpy as jnp
from jax import lax
from jax.experimental import pallas as pl
from jax.experimental.pallas import tpu as pltpu

F32 = jnp.float32
BF16 = jnp.bfloat16

D_MODEL = 1024
GRID_W = 64
BRANCH_W = 256
EPS = 1e-6
HY_EMB = 33
HY_EMB_PAD = 128
HY_FFN = 64
HY_TARGET = 1e-2
HY_FAST = 0.3
HY_SLOW = 1.5
CF_WIDTH = 31
HEAD_DIM = 64
DIFF_D = 32
ROPE_BASE = 10000.0
N_EXPERTS = 8
LANES = 128
LOG2E = 1.4426950408889634
FIXED_SHIFT_LIMIT = 60.0
SMALL_COLS = 2560
GATE_COLS = 4 * D_MODEL
VMEM_LIMIT = 48 * 1024 * 1024
VMEM_LIMIT_BIG = 56 * 1024 * 1024


def _cp(sem, vmem=VMEM_LIMIT):
    return pltpu.CompilerParams(dimension_semantics=sem, vmem_limit_bytes=vmem)


def _sigmoid(x):
    return 1.0 / (1.0 + jnp.exp(-x))


def _dot(a, b):
    return jnp.dot(a, b, preferred_element_type=F32)


def _split(a):
    hi = a.astype(BF16)
    lo = (a - hi.astype(F32)).astype(BF16)
    return hi, lo


def _dot3(a, b):
    ah, al = _split(a)
    bh, bl = _split(b)
    return _dot(ah, bh) + (_dot(al, bh) + _dot(ah, bl))


def _norm_mod(x, gain, shift, scale):
    ms = jnp.mean(x * x, axis=-1, keepdims=True)
    return (x * lax.rsqrt(ms + EPS) * gain) * (1.0 + scale) + shift


def _ada_kernel(c_ref, w_ref, b_ref, o_ref):
    c = c_ref[...]
    s = c * _sigmoid(c)
    o_ref[0] = _dot3(s, w_ref[0]) + b_ref[0]


def _ada_mods(c16, w_ada, b_ada):
    depth, d, n6 = w_ada.shape
    tn = 512
    return pl.pallas_call(
        _ada_kernel,
        grid=(depth, n6 // tn),
        in_specs=[pl.BlockSpec((16, d), lambda l, j: (0, 0)),
                  pl.BlockSpec((1, d, tn), lambda l, j: (l, 0, j)),
                  pl.BlockSpec((1, 1, tn), lambda l, j: (l, 0, j))],
        out_specs=pl.BlockSpec((1, 16, tn), lambda l, j: (l, 0, j)),
        out_shape=jax.ShapeDtypeStruct((depth, 16, n6), F32),
        compiler_params=_cp(("arbitrary", "arbitrary")),
        name="ada_mods",
    )(c16, w_ada, b_ada.reshape(depth, 1, n6))


def _in_proj_kernel(x_ref, mod_ref, g_ref, w_ref, o_ref, h_scr, *, gate):
    @pl.when(pl.program_id(1) == 0)
    def _():
        m = mod_ref[0]
        h_scr[...] = _norm_mod(x_ref[...], g_ref[...], m[0:1, :], m[1:2, :]).astype(BF16)

    r = _dot(h_scr[...], w_ref[...])
    o_ref[...] = _sigmoid(r).astype(o_ref.dtype) if gate else r


def _in_proj(x, mods_l, gain, w_bf, dims):
    n, d = x.shape
    tm = dims["tm_big"]
    mrow = dims["mod_row"](tm)

    def call(w, tn, gate, dtype):
        n_cols = w.shape[1]
        return pl.pallas_call(
            functools.partial(_in_proj_kernel, gate=gate),
            grid=(n // tm, n_cols // tn),
            in_specs=[pl.BlockSpec((tm, d), lambda i, j: (i, 0)),
                      pl.BlockSpec((1, 6, d), lambda i, j: (mrow(i), 0, 0)),
                      pl.BlockSpec((1, d), lambda i, j: (0, 0)),
                      pl.BlockSpec((d, tn), lambda i, j: (0, j))],
            out_specs=pl.BlockSpec((tm, tn), lambda i, j: (i, j)),
            out_shape=jax.ShapeDtypeStruct((n, n_cols), dtype),
            scratch_shapes=[pltpu.VMEM((tm, d), BF16)],
            compiler_params=_cp(("arbitrary", "arbitrary")),
            name="in_proj_gate" if gate else "in_proj_mix",
        )(x, mods_l, gain, w)

    return (call(w_bf[:, :SMALL_COLS], SMALL_COLS // 2, False, F32),
            call(w_bf[:, SMALL_COLS:], GATE_COLS // 4, True, BF16))


def _short_conv_kernel(u_ref, w_ref, b_ref, o_ref, pad_scr, *, L, tc, planes):
    if planes:
        o_ref = o_ref.at[0]
    pad_scr[0:8, :] = jnp.zeros((8, BRANCH_W), F32)
    pad_scr[8 + L:16 + L, :] = jnp.zeros((8, BRANCH_W), F32)
    pad_scr[8:8 + L, :] = u_ref[...]
    w = w_ref[0]
    b = b_ref[0]

    def body(c, carry):
        t0 = pl.multiple_of(c * tc, tc)
        win = pad_scr[pl.ds(t0, tc + 16), :]
        acc = b + w[0:1, :] * win[7:7 + tc]
        acc = acc + w[1:2, :] * win[8:8 + tc]
        acc = acc + w[2:3, :] * win[9:9 + tc]
        o_ref[pl.ds(t0, tc), :] = acc
        return carry

    lax.fori_loop(0, L // tc, body, 0)


def _short_conv(u_small, w, b, L, nseq, row_base, planes=False):
    w3 = jnp.transpose(w.reshape(3, 3, BRANCH_W), (1, 0, 2))
    w3 = jnp.pad(w3, ((0, 0), (0, 5), (0, 0)))
    b3 = b.reshape(3, 1, BRANCH_W)
    tc = min(L, 256)
    return pl.pallas_call(
        functools.partial(_short_conv_kernel, L=L, tc=tc, planes=planes),
        grid=(nseq, 3),
        in_specs=[pl.BlockSpec((L, BRANCH_W), lambda s, j: (row_base + s, j)),
                  pl.BlockSpec((1, 8, BRANCH_W), lambda s, j: (j, 0, 0)),
                  pl.BlockSpec((1, 1, BRANCH_W), lambda s, j: (j, 0, 0))],
        out_specs=(pl.BlockSpec((1, L, BRANCH_W), lambda s, j: (j, s, 0)) if planes
                   else pl.BlockSpec((L, BRANCH_W), lambda s, j: (s, j))),
        out_shape=jax.ShapeDtypeStruct((3, nseq * L, BRANCH_W) if planes else (nseq * L, 3 * BRANCH_W), F32),
        scratch_shapes=[pltpu.VMEM((L + 16, BRANCH_W), F32)],
        compiler_params=_cp(("arbitrary", "arbitrary")),
        name="hy_short_conv",
    )(u_small, w3, b3)


def _filter_consts(L):
    t = np.linspace(0.0, 1.0, L)[:, None]
    bands = (HY_EMB - 1) // 2
    fr = np.linspace(1e-4, bands - 1, bands)[None, :]
    wpos = 2.0 * math.pi * np.arange(L)[:, None] / L
    z = np.concatenate([t, np.cos(fr * wpos), -np.sin(fr * wpos)], axis=-1)
    z = np.pad(z, ((0, 0), (0, HY_EMB_PAD - HY_EMB)))
    deltas = np.abs(np.linspace(math.log(HY_TARGET) / HY_SLOW, math.log(HY_TARGET) / HY_FAST, BRANCH_W))
    win = np.exp(-t * deltas[None, :])
    return jnp.asarray(z, F32), jnp.asarray(win, F32)


def _filter_kernel(z_ref, w1, b1, w2, b2, w3, b3, w4, fq, win_ref, hf_ref, ss_ref, *, tr, planes):
    i = pl.program_id(0)
    f = fq[...]
    a = jnp.sin(f * (_dot3(z_ref[...], w1[...]) + b1[...]))
    a = jnp.sin(f * (_dot3(a, w2[...]) + b2[...]))
    a = jnp.sin(f * (_dot3(a, w3[...]) + b3[...]))
    h = _dot3(a, w4[...])
    win = win_ref[...]
    h = h * jnp.concatenate([win, win, win, win], axis=1)
    row = lax.broadcasted_iota(jnp.int32, h.shape, 0) + i * tr
    col = lax.broadcasted_iota(jnp.int32, h.shape, 1)
    h = jnp.where((row == 0) & (col >= 2 * BRANCH_W), 0.0, h)
    if planes:
        for d in range(4):
            hf_ref[d] = h[:, d * BRANCH_W:(d + 1) * BRANCH_W]
    else:
        hf_ref[...] = h

    @pl.when(i == 0)
    def _():
        ss_ref[...] = jnp.zeros_like(ss_ref)

    ss_ref[...] += jnp.sum(h * h, axis=0, keepdims=True)


def _hyena_filter(L, w1, b1, w2, b2, w3, b3, w4, freq, planes=False):
    z, win = _filter_consts(L)
    tr = min(L, 256)
    w1p = jnp.pad(w1, ((0, HY_EMB_PAD - HY_EMB), (0, 0)))
    full = lambda a: pl.BlockSpec(a.shape, lambda i: (0,) * a.ndim)
    args = [w1p, b1.reshape(1, -1), w2, b2.reshape(1, -1), w3, b3.reshape(1, -1), w4, freq.reshape(1, -1)]
    n_out = w4.shape[1]
    return pl.pallas_call(
        functools.partial(_filter_kernel, tr=tr, planes=planes),
        grid=(L // tr,),
        in_specs=[pl.BlockSpec((tr, HY_EMB_PAD), lambda i: (i, 0))] + [full(a) for a in args]
        + [pl.BlockSpec((tr, BRANCH_W), lambda i: (i, 0))],
        out_specs=[pl.BlockSpec((4, tr, BRANCH_W), lambda i: (0, i, 0)) if planes
                   else pl.BlockSpec((tr, n_out), lambda i: (i, 0)),
                   pl.BlockSpec((1, n_out), lambda i: (0, 0))],
        out_shape=[jax.ShapeDtypeStruct((4, L, BRANCH_W) if planes else (L, n_out), F32),
                   jax.ShapeDtypeStruct((1, n_out), F32)],
        compiler_params=_cp(("arbitrary",)),
        name="hy_filter",
    )(z, *args, win)


def _dft_mats(L):
    N = 2 * L
    blk = 64
    k = jnp.arange(L, dtype=jnp.int32)[:, None]
    nh = jnp.arange(L // blk, dtype=jnp.int32)[None, :]
    nl = jnp.arange(blk, dtype=jnp.int32)[None, :]
    w = 2.0 * math.pi / N
    a = ((k * (blk * nh)) % N).astype(F32) * w
    b = ((k * nl) % N).astype(F32) * w
    ca, sa, cb, sb = jnp.cos(a), jnp.sin(a), jnp.cos(b), jnp.sin(b)
    cos = (ca[:, :, None] * cb[:, None, :] - sa[:, :, None] * sb[:, None, :]).reshape(L, L)
    sin = (sa[:, :, None] * cb[:, None, :] + ca[:, :, None] * sb[:, None, :]).reshape(L, L)
    alt = jnp.where(jnp.arange(L) % 2 == 0, 1.0, -1.0).astype(F32)
    first = (jnp.arange(L) == 0)
    s_f = jnp.where(first[:, None], alt[None, :], -sin)
    fwd = jnp.concatenate([cos, s_f], axis=0).astype(BF16)
    colscale = jnp.where(first, 1.0 / N, 2.0 / N).astype(F32)
    g_c = cos * colscale[None, :]
    g_s = jnp.where(first[None, :], alt[:, None] / N, -sin * (2.0 / N))
    inv = jnp.concatenate([g_c, g_s], axis=1).astype(BF16)
    return fwd, inv


def _dft_fwd_kernel(f_ref, z_ref, o_ref):
    o_ref[0] = _dot(f_ref[...], z_ref[...].astype(BF16))


def _dft_fwd(fwd, z2d, L, nb, zmap):
    tm = min(2 * L, 1024)
    return pl.pallas_call(
        _dft_fwd_kernel,
        grid=(2 * L // tm, nb),
        in_specs=[pl.BlockSpec((tm, L), lambda i, b: (i, 0)),
                  pl.BlockSpec((L, BRANCH_W), lambda i, b: zmap(b))],
        out_specs=pl.BlockSpec((1, tm, BRANCH_W), lambda i, b: (b, i, 0)),
        out_shape=jax.ShapeDtypeStruct((nb, 2 * L, BRANCH_W), F32),
        compiler_params=_cp(("arbitrary", "arbitrary")),
        name="hy_dft_fwd",
    )(fwd, z2d)


def _spec_prod_kernel(z_ref, kf_ref, kb_ref, sf_ref, sb_ref, y_ref, *, tk):
    i = pl.program_id(1)
    s = lax.rsqrt(sf_ref[...] + sb_ref[...] + EPS)
    zr, zi = z_ref[0, 0], z_ref[0, 1]
    fr, fi = kf_ref[0, 0], kf_ref[0, 1]
    br, bi = kb_ref[0, 0], kb_ref[0, 1]
    row = lax.broadcasted_iota(jnp.int32, (tk, BRANCH_W), 0) + i * tk
    first = row == 0
    kr = (fr + br) * s
    ki = jnp.where(first, fi + bi, fi - bi) * s
    yr = jnp.where(first, zr * kr, zr * kr - zi * ki)
    yi = jnp.where(first, zi * ki, zr * ki + zi * kr)
    y_ref[0, 0] = yr.astype(BF16)
    y_ref[0, 1] = yi.astype(BF16)


def _spec_prod(zf, kfs, ss, order, L, nb):
    tk = min(L, 512)
    zf4 = zf.reshape(nb, 2, L, BRANCH_W)
    kf4 = kfs.reshape(4, 2, L, BRANCH_W)
    y = pl.pallas_call(
        functools.partial(_spec_prod_kernel, tk=tk),
        grid=(nb, L // tk),
        in_specs=[pl.BlockSpec((1, 2, tk, BRANCH_W), lambda b, i: (b, 0, i, 0)),
                  pl.BlockSpec((1, 2, tk, BRANCH_W), lambda b, i: (order, 0, i, 0)),
                  pl.BlockSpec((1, 2, tk, BRANCH_W), lambda b, i: (2 + order, 0, i, 0)),
                  pl.BlockSpec((1, BRANCH_W), lambda b, i: (0, order)),
                  pl.BlockSpec((1, BRANCH_W), lambda b, i: (0, 2 + order))],
        out_specs=pl.BlockSpec((1, 2, tk, BRANCH_W), lambda b, i: (b, 0, i, 0)),
        out_shape=jax.ShapeDtypeStruct((nb, 2, L, BRANCH_W), BF16),
        compiler_params=_cp(("arbitrary", "arbitrary")),
        name="hy_spec_prod",
    )(zf4, kf4, kf4, ss, ss)
    return y.reshape(nb, 2 * L, BRANCH_W)


def _dft_inv_kernel(g_ref, y_ref, gate_ref, zp_ref, bias_ref, *rest):
    o_ref = rest[-1]
    conv = _dot(g_ref[...], y_ref[0])
    o_ref[...] = (gate_ref[...] * (conv + bias_ref[...] * zp_ref[...])).astype(o_ref.dtype)


def _dft_inv(inv, y, xs, gate_col, zprev, zprev_col, bias, L, nb, out_dtype, out_rows=None, out_base=0,
             prev=None):
    tm = min(L, 512)
    nt = L // tm
    args = [inv, y, xs, zprev, bias]
    in_specs = [pl.BlockSpec((tm, 2 * L), lambda i, b: (i, 0)),
                pl.BlockSpec((1, 2 * L, BRANCH_W), lambda i, b: (b, 0, 0)),
                pl.BlockSpec((tm, BRANCH_W), lambda i, b: (b * nt + i, gate_col)),
                pl.BlockSpec((tm, BRANCH_W), lambda i, b: (b * nt + i, zprev_col)),
                pl.BlockSpec((1, BRANCH_W), lambda i, b: (0, 0))]
    args, in_specs, alias = _into(prev, args, in_specs)
    return pl.pallas_call(
        _dft_inv_kernel,
        grid=(nt, nb),
        in_specs=in_specs,
        out_specs=pl.BlockSpec((tm, BRANCH_W), lambda i, b: ((out_base + b) * nt + i, 0)),
        out_shape=jax.ShapeDtypeStruct((out_rows or nb * L, BRANCH_W), out_dtype),
        input_output_aliases=alias,
        compiler_params=_cp(("arbitrary", "arbitrary")),
        name="hy_dft_inv",
    )(*args)


def _hyena(u_small, L, nseq, row_base, p, mats, out_rows, prev=None):
    (short_w, short_b, w1, b1, w2, b2, w3, b3, w4, freq, bias) = p
    fwd, inv = mats
    xs = _short_conv(u_small, short_w, short_b, L, nseq, row_base)
    hf, ss = _hyena_filter(L, w1, b1, w2, b2, w3, b3, w4, freq)
    kfs = _dft_fwd(fwd, hf, L, 4, lambda b: (0, b))
    zf = _dft_fwd(fwd, xs, L, nseq, lambda b: (b, 2))
    y = _spec_prod(zf, kfs, ss, 0, L, nseq)
    z1 = _dft_inv(inv, y, xs, 0, xs, 2, bias[0:1], L, nseq, F32)
    zf = _dft_fwd(fwd, z1, L, nseq, lambda b: (b, 0))
    y = _spec_prod(zf, kfs, ss, 1, L, nseq)
    return _dft_inv(inv, y, xs, 1, z1, 0, bias[1:2], L, nseq, BF16, out_rows, row_base, prev)


FFT_N1 = 64
FFT_KB = 16
FFT_COLS = 16384


def _fft_consts(L):
    N = 2 * L
    n2s = N // FFT_N1
    h = FFT_N1 // 2
    k1 = jnp.arange(FFT_N1, dtype=jnp.int32)
    ang_a = ((k1[:, None] * jnp.arange(h, dtype=jnp.int32)[None, :]) % FFT_N1).astype(F32) * (2.0 * math.pi / FFT_N1)
    ca, sa = jnp.cos(ang_a), jnp.sin(ang_a)
    fa = jnp.concatenate([ca, -sa], axis=0).astype(BF16)
    fai = (jnp.concatenate([ca.T, -sa.T], axis=1) / N).astype(BF16)
    k = k1[:, None, None] + FFT_N1 * jnp.arange(n2s, dtype=jnp.int32)[None, :, None]
    n2 = jnp.arange(n2s, dtype=jnp.int32)[None, None, :]
    ang = ((k * n2) % N).astype(F32) * (2.0 * math.pi / N)
    c, s = jnp.cos(ang), jnp.sin(ang)
    g = jnp.concatenate([jnp.concatenate([c, s], axis=2), jnp.concatenate([-s, c], axis=2)], axis=1)
    ct, st = jnp.swapaxes(c, 1, 2), jnp.swapaxes(s, 1, 2)
    gi = jnp.concatenate([jnp.concatenate([ct, -st], axis=2), jnp.concatenate([st, ct], axis=2)], axis=1)
    return fa, fai, g.astype(BF16), gi.astype(BF16)


def _fft_a_kernel(f_ref, z_ref, o_ref):
    o_ref[...] = _dot(f_ref[...], z_ref[0].astype(BF16)).astype(o_ref.dtype)


def _fft_a(fa, z3, zmap, nb, L):
    h = FFT_N1 // 2
    cols = (L // h) * BRANCH_W
    tn = min(cols, FFT_COLS)
    return pl.pallas_call(
        _fft_a_kernel,
        grid=(nb, cols // tn),
        in_specs=[pl.BlockSpec(fa.shape, lambda b, j: (0, 0)),
                  pl.BlockSpec((1, h, tn), lambda b, j: zmap(b) + (j,))],
        out_specs=pl.BlockSpec((2 * FFT_N1, tn), lambda b, j: (b, j)),
        out_shape=jax.ShapeDtypeStruct((nb * 2 * FFT_N1, cols), BF16),
        compiler_params=_cp(("arbitrary", "arbitrary")),
        name="hy_fft_a",
    )(fa, z3)


def _fft_spec_kernel(a_ref, g_ref, o_ref):
    for j in range(FFT_KB):
        s = _dot(g_ref[j], jnp.concatenate([a_ref[0, 0, j], a_ref[0, 1, j]], axis=0))
        half = s.shape[0] // 2
        o_ref[0, j, 0] = s[:half]
        o_ref[0, j, 1] = s[half:]


def _fft_spec(g, a5):
    nb, _, n1s, n2s, w = a5.shape
    return pl.pallas_call(
        _fft_spec_kernel,
        grid=(n1s // FFT_KB, nb),
        in_specs=[pl.BlockSpec((1, 2, FFT_KB, n2s, w), lambda i, b: (b, 0, i, 0, 0)),
                  pl.BlockSpec((FFT_KB, 2 * n2s, 2 * n2s), lambda i, b: (i, 0, 0))],
        out_specs=pl.BlockSpec((1, FFT_KB, 2, n2s, w), lambda i, b: (b, i, 0, 0, 0)),
        out_shape=jax.ShapeDtypeStruct((nb, n1s, 2, n2s, w), F32),
        compiler_params=_cp(("arbitrary", "arbitrary")),
        name="hy_fft_spec",
    )(a5, g)


def _fft_conv_kernel(a_ref, g_ref, gi_ref, sf_ref, sb_ref, ssf_ref, ssb_ref, o_ref):
    scale = lax.rsqrt(ssf_ref[...] + ssb_ref[...] + EPS)
    for j in range(FFT_KB):
        s = _dot(g_ref[j], jnp.concatenate([a_ref[0, 0, j], a_ref[0, 1, j]], axis=0))
        half = s.shape[0] // 2
        sr, si = s[:half], s[half:]
        kr = (sf_ref[0, j, 0] + sb_ref[0, j, 0]) * scale
        ki = (sf_ref[0, j, 1] - sb_ref[0, j, 1]) * scale
        y = jnp.concatenate([sr * kr - si * ki, sr * ki + si * kr], axis=0).astype(BF16)
        t = _dot(gi_ref[j], y)
        o_ref[0, 0, j] = t[:half].astype(BF16)
        o_ref[0, 1, j] = t[half:].astype(BF16)


def _fft_conv(g, gi, a5, kspec, ss, order):
    nb, _, n1s, n2s, w = a5.shape
    kblk = lambda d: pl.BlockSpec((1, FFT_KB, 2, n2s, w), lambda i, b: (d, i, 0, 0, 0))
    gblk = pl.BlockSpec((FFT_KB, 2 * n2s, 2 * n2s), lambda i, b: (i, 0, 0))
    ablk = pl.BlockSpec((1, 2, FFT_KB, n2s, w), lambda i, b: (b, 0, i, 0, 0))
    return pl.pallas_call(
        _fft_conv_kernel,
        grid=(n1s // FFT_KB, nb),
        in_specs=[ablk, gblk, gblk, kblk(order), kblk(2 + order),
                  pl.BlockSpec((1, w), lambda i, b: (0, order)),
                  pl.BlockSpec((1, w), lambda i, b: (0, 2 + order))],
        out_specs=ablk,
        out_shape=jax.ShapeDtypeStruct(a5.shape, BF16),
        compiler_params=_cp(("arbitrary", "arbitrary")),
        name="hy_fft_conv",
    )(a5, g, gi, kspec, kspec, ss, ss)


def _fft_ainv_kernel(f_ref, a_ref, gate_ref, zp_ref, bias_ref, o_ref):
    conv = _dot(f_ref[...], a_ref[...])
    o_ref[...] = (gate_ref[0] * (conv + bias_ref[...] * zp_ref[0])).astype(o_ref.dtype)


def _fft_ainv(fai, a2, gate3, gate_plane, zprev3, zprev_plane, bias, L, nb, out_dtype, out_view_rows):
    h = FFT_N1 // 2
    cols = (L // h) * BRANCH_W
    tn = min(cols, FFT_COLS)
    return pl.pallas_call(
        _fft_ainv_kernel,
        grid=(nb, cols // tn),
        in_specs=[pl.BlockSpec(fai.shape, lambda b, j: (0, 0)),
                  pl.BlockSpec((2 * FFT_N1, tn), lambda b, j: (b, j)),
                  pl.BlockSpec((1, h, tn), lambda b, j: (gate_plane, b, j)),
                  pl.BlockSpec((1, h, tn), lambda b, j: (zprev_plane, b, j)),
                  pl.BlockSpec((1, tn), lambda b, j: (0, 0))],
        out_specs=pl.BlockSpec((h, tn), lambda b, j: (b, j)),
        out_shape=jax.ShapeDtypeStruct((out_view_rows, cols), out_dtype),
        compiler_params=_cp(("arbitrary", "arbitrary")),
        name="hy_fft_ainv",
    )(fai, a2, gate3, zprev3, jnp.tile(bias, (1, tn // BRANCH_W)))


def _hyena_fft(u_small, L, nseq, p, consts, out_rows):
    (short_w, short_b, w1, b1, w2, b2, w3, b3, w4, freq, bias) = p
    fa, fai, g, gi = consts
    h = FFT_N1 // 2
    n2s = L // h
    cols = n2s * BRANCH_W
    xs3 = _short_conv(u_small, short_w, short_b, L, nseq, 0, planes=True).reshape(3, nseq * h, cols)
    hf4, ss = _hyena_filter(L, w1, b1, w2, b2, w3, b3, w4, freq, planes=True)
    ka = _fft_a(fa, hf4.reshape(4, h, cols), lambda b: (b, 0), 4, L)
    kspec = _fft_spec(g, ka.reshape(4, 2, FFT_N1, n2s, BRANCH_W))

    def conv(z3, zmap, order, gate_plane, zprev_plane, out_dtype, out_view_rows):
        a = _fft_a(fa, z3, zmap, nseq, L)
        a = _fft_conv(g, gi, a.reshape(nseq, 2, FFT_N1, n2s, BRANCH_W), kspec, ss, order)
        return _fft_ainv(fai, a.reshape(nseq * 2 * FFT_N1, cols), xs3, gate_plane, z3, zprev_plane,
                         bias[order:order + 1], L, nseq, out_dtype, out_view_rows)

    z1 = conv(xs3, lambda b: (2, b), 0, 0, 2, F32, nseq * h).reshape(1, nseq * h, cols)
    y = conv(z1, lambda b: (0, b), 1, 1, 0, BF16, out_rows // n2s)
    return y.reshape(out_rows, BRANCH_W)


def _conformer_kernel(a_ref, g_ref, w_ref, b_ref, lg_ref, lb_ref, *rest, L, tc):
    o_ref, pad_scr = rest[-2:]
    pad_scr[0:16, :] = jnp.zeros((16, BRANCH_W), F32)
    pad_scr[16 + L:32 + L, :] = jnp.zeros((16, BRANCH_W), F32)

    def glu(c, carry):
        t0 = pl.multiple_of(c * tc, tc)
        pad_scr[pl.ds(16 + t0, tc), :] = a_ref[pl.ds(t0, tc), :] * _sigmoid(g_ref[pl.ds(t0, tc), :])
        return carry

    lax.fori_loop(0, L // tc, glu, 0)
    b = b_ref[...]
    lg = lg_ref[...]
    lb = lb_ref[...]

    def body(c, carry):
        t0 = pl.multiple_of(c * tc, tc)
        win = pad_scr[pl.ds(t0, tc + 32), :]
        acc = jnp.zeros((tc, BRANCH_W), F32) + b
        for r in range(8):
            sh = win[r:r + tc + 24]
            for a in range(4):
                m = 8 * a + r
                if 1 <= m <= CF_WIDTH:
                    acc = acc + w_ref[m - 1:m, :] * sh[8 * a:8 * a + tc]
        mu = jnp.mean(acc, axis=-1, keepdims=True)
        xc = acc - mu
        var = jnp.mean(xc * xc, axis=-1, keepdims=True)
        y = xc * lax.rsqrt(var + EPS) * lg + lb
        o_ref[pl.ds(t0, tc), :] = (y * _sigmoid(y)).astype(o_ref.dtype)
        return carry

    lax.fori_loop(0, L // tc, body, 0)


def _conformer(u_small, L, nseq, row_base, p, out_rows, prev=None):
    dw_w, dw_b, ln_g, ln_b = p
    tc = 128
    wpad = jnp.pad(dw_w, ((0, 32 - CF_WIDTH), (0, 0)))
    row = lambda a: a.reshape(1, BRANCH_W)
    vec = pl.BlockSpec((1, BRANCH_W), lambda s: (0, 0))
    args = [u_small, u_small, wpad, row(dw_b), row(ln_g), row(ln_b)]
    in_specs = [pl.BlockSpec((L, BRANCH_W), lambda s: (row_base + s, 3)),
                pl.BlockSpec((L, BRANCH_W), lambda s: (row_base + s, 4)),
                pl.BlockSpec((32, BRANCH_W), lambda s: (0, 0)), vec, vec, vec]
    args, in_specs, alias = _into(prev, args, in_specs)
    return pl.pallas_call(
        functools.partial(_conformer_kernel, L=L, tc=tc),
        grid=(nseq,),
        in_specs=in_specs,
        out_specs=pl.BlockSpec((L, BRANCH_W), lambda s: (row_base + s, 0)),
        out_shape=jax.ShapeDtypeStruct((out_rows, BRANCH_W), BF16),
        input_output_aliases=alias,
        scratch_shapes=[pltpu.VMEM((L + 32, BRANCH_W), F32)],
        compiler_params=_cp(("arbitrary",)),
        name="conformer",
    )(*args)


def _rope_tables(S, pad_rows, head):
    half = head // 2
    nf = half // 2
    lane = np.arange(LANES)
    inv_lane = (ROPE_BASE ** (-(np.arange(nf)) / nf))[(lane % half) % nf]
    is_row = (lane % head) < half
    pos = jnp.arange(S, dtype=jnp.int32)
    rows = (pos // GRID_W).astype(F32)[:, None]
    cols = (pos % GRID_W).astype(F32)[:, None]
    ang = jnp.where(jnp.asarray(is_row)[None, :], rows, cols) * jnp.asarray(inv_lane, F32)[None, :]
    cos = jnp.concatenate([jnp.cos(ang), jnp.ones((pad_rows, LANES), F32)], axis=0)
    sin = jnp.concatenate([jnp.sin(ang), jnp.zeros((pad_rows, LANES), F32)], axis=0)
    return cos, sin


def _group_ones(width, group):
    idx = np.arange(width)
    return jnp.asarray((idx[:, None] // group) == (idx[None, :] // group), BF16)


def _head_norm_rope(x, ones, group, gain, cos, sin, nf, out_scale):
    w = x.shape[1]
    hi, lo = _split(x * x)
    ms = (_dot(hi, ones) + _dot(lo, ones)) * (1.0 / group)
    xn = x * lax.rsqrt(ms + EPS) * gain
    reps = w // LANES
    c = jnp.concatenate([cos] * reps, axis=1) if reps > 1 else cos
    s = jnp.concatenate([sin] * reps, axis=1) if reps > 1 else sin
    lane = lax.broadcasted_iota(jnp.int32, x.shape, 1)
    first = (lane % (2 * nf)) < nf
    rot = jnp.where(first, -pltpu.roll(xn, w - nf, 1), pltpu.roll(xn, nf, 1))
    return (xn * c + rot * s) * out_scale


def _qkv_kernel(gq_ref, gkv_ref, dq_ref, dk_ref, dv_ref, cg_ref, sg_ref, cd_ref, sd_ref,
                o64_ref, o32_ref, gqn_ref, gkn_ref, dqn_ref, dkn_ref,
                qg_ref, kg_ref, vag_ref, vbg_ref, qd_ref, kd_ref, vad_ref, vbd_ref):
    cg, sg, cd, sd = cg_ref[...], sg_ref[...], cd_ref[...], sd_ref[...]
    o64, o32 = o64_ref[...], o32_ref[...]
    q = _head_norm_rope(gq_ref[...], o64, HEAD_DIM, gqn_ref[...], cg, sg, HEAD_DIM // 4, LOG2E * HEAD_DIM ** -0.5)
    qg_ref[...] = q.astype(BF16)
    kv = gkv_ref[...]
    k = _head_norm_rope(kv[:, :LANES], o64[:LANES, :LANES], HEAD_DIM, gkn_ref[...], cg, sg, HEAD_DIM // 4, 1.0)
    v = kv[:, LANES:]
    kk = jnp.concatenate([k, k], axis=1)
    vv = jnp.concatenate([v, v], axis=1)
    quarter = lax.broadcasted_iota(jnp.int32, kk.shape, 1) // HEAD_DIM
    kr = pltpu.roll(kk, HEAD_DIM, 1)
    vr = pltpu.roll(vv, HEAD_DIM, 1)
    kg_ref[...] = jnp.where((quarter == 0) | (quarter == 3), kk, kr).astype(BF16)
    vag_ref[...] = jnp.where(quarter == 0, vv, jnp.where(quarter == 2, vr, 0.0)).astype(BF16)
    vbg_ref[...] = jnp.where(quarter == 1, vr, jnp.where(quarter == 3, vv, 0.0)).astype(BF16)
    qd = _head_norm_rope(dq_ref[...], o32, DIFF_D, dqn_ref[...], cd, sd, DIFF_D // 4, LOG2E * DIFF_D ** -0.5)
    qd_ref[...] = qd.astype(BF16)
    kd = _head_norm_rope(dk_ref[...], o32, DIFF_D, dkn_ref[...], cd, sd, DIFF_D // 4, 1.0)
    kd_ref[...] = kd.astype(BF16)
    vd = dv_ref[...]
    even = (lax.broadcasted_iota(jnp.int32, vd.shape, 1) // HEAD_DIM) % 2 == 0
    vad_ref[...] = jnp.where(even, vd, 0.0).astype(BF16)
    vbd_ref[...] = jnp.where(even, 0.0, vd).astype(BF16)


def _qkv_prep(u_small, gqn, gkn, dqn, dkn, tables, dims):
    n = u_small.shape[0]
    tm = dims["tm_small"]
    S = dims["S"]
    n_lat_tiles = dims["n_lat"] // tm
    per_seq = S // tm
    tmap = lambda i: (jnp.where(i < n_lat_tiles, i % per_seq, per_seq), 0)
    col = lambda c: pl.BlockSpec((tm, BRANCH_W), lambda i: (i, c))
    tab = pl.BlockSpec((tm, LANES), tmap)
    full = lambda a: pl.BlockSpec(a.shape, lambda i: (0,) * a.ndim)
    o64, o32 = _group_ones(BRANCH_W, HEAD_DIM), _group_ones(BRANCH_W, DIFF_D)
    gains = [jnp.tile(gqn, 4).reshape(1, 256), jnp.tile(gkn, 2).reshape(1, 128),
             jnp.tile(dqn.reshape(-1), 4).reshape(1, 256), jnp.tile(dkn.reshape(-1), 4).reshape(1, 256)]
    out = pl.BlockSpec((tm, BRANCH_W), lambda i: (i, 0))
    return pl.pallas_call(
        _qkv_kernel,
        grid=(n // tm,),
        in_specs=[col(5), col(6), col(7), col(8), col(9), tab, tab, tab, tab, full(o64), full(o32)]
        + [full(g) for g in gains],
        out_specs=[out] * 8,
        out_shape=[jax.ShapeDtypeStruct((n, BRANCH_W), BF16)] * 8,
        compiler_params=_cp(("arbitrary",)),
        name="qkv_prep",
    )(u_small, u_small, u_small, u_small, u_small, *tables, o64, o32, *gains)


def _lane_pick(lane_lo, a, b):
    return jnp.where(lane_lo, a, b)


def _flash_kernel(*refs, segs, tk, tq, diff, lam_init):
    q_ref, o_ref = refs[0], refs[-1]
    seg_refs = [refs[1 + 3 * i:4 + 3 * i] for i in range(len(segs))]
    aux_ref = refs[1 + 3 * len(segs)]
    lane = lax.broadcasted_iota(jnp.int32, (1, LANES), 1)
    lane_lo = lane < HEAD_DIM
    if diff:
        masks = [(lane >= g * DIFF_D) & (lane < (g + 1) * DIFF_D) for g in range(4)]
        acc_of = [0, 1, 0, 1]
        use_a = [True, True, False, False]
        n_acc = 2
    else:
        masks = [lane_lo, ~lane_lo]
        acc_of = [0, 0]
        use_a = [True, False]
        n_acc = 1
    R = len(masks)
    pairs = [slice(p * LANES, (p + 1) * LANES) for p in range(2)]
    qsts = [jnp.concatenate([jnp.where(m, q_ref[:, ps], jnp.zeros((tq, LANES), BF16)) for m in masks], axis=0)
            for ps in pairs]
    lo_hi = []
    for a in range(n_acc):
        rs = [r for r in range(R) if acc_of[r] == a]
        lo_hi.append(([r for r in rs if use_a[r]][0], [r for r in rs if not use_a[r]][0]))

    def rows(x, r):
        return x[r * tq:(r + 1) * tq]

    shift = jnp.max(aux_ref[5:6, :])

    def run(fixed):
        def chunk(kv_refs, t0, size, carry):
            k_ref, va_ref, vb_ref = kv_refs
            new = []
            for p, ps in enumerate(pairs):
                m_run, l_run, accs = carry[p]
                k = k_ref[pl.ds(t0, size), ps]
                s = lax.dot_general(qsts[p], k, (((1,), (1,)), ((), ())), preferred_element_type=F32)
                if fixed:
                    m_new = m_run
                    pr = jnp.exp2(s - shift)
                    l_new = l_run + jnp.sum(pr, axis=-1, keepdims=True)
                else:
                    m_new = jnp.maximum(m_run, jnp.max(s, axis=-1, keepdims=True))
                    alpha = jnp.exp2(m_run - m_new)
                    pr = jnp.exp2(s - m_new)
                    l_new = alpha * l_run + jnp.sum(pr, axis=-1, keepdims=True)
                prb = pr.astype(BF16)
                va = va_ref[pl.ds(t0, size), ps]
                vb = vb_ref[pl.ds(t0, size), ps]
                new_accs = []
                for a, (r_lo, r_hi) in enumerate(lo_hi):
                    upd = _dot(rows(prb, r_lo), va) + _dot(rows(prb, r_hi), vb)
                    if fixed:
                        new_accs.append(accs[a] + upd)
                    else:
                        al = _lane_pick(lane_lo, rows(alpha, r_lo), rows(alpha, r_hi))
                        new_accs.append(accs[a] * al + upd)
                new.append((m_new, l_new, tuple(new_accs)))
            return tuple(new)

        one = (jnp.full((R * tq, 1), -jnp.inf, F32), jnp.zeros((R * tq, 1), F32),
               tuple(jnp.zeros((tq, LANES), F32) for _ in range(n_acc)))
        carry = (one, one)
        for kv_refs, T in zip(seg_refs, segs):
            n_main = T // tk
            if n_main:
                carry = lax.fori_loop(
                    0, n_main, lambda c, cr, kv_refs=kv_refs: chunk(kv_refs, pl.multiple_of(c * tk, tk), tk, cr),
                    carry, unroll=2 if n_main % 2 == 0 else 1)
            if T - n_main * tk:
                carry = chunk(kv_refs, n_main * tk, T - n_main * tk, carry)

        outs = []
        for p in range(2):
            _, l_fin, accs = carry[p]
            inv_l = 1.0 / l_fin
            norm = [accs[a] * _lane_pick(lane_lo, rows(inv_l, r_lo), rows(inv_l, r_hi))
                    for a, (r_lo, r_hi) in enumerate(lo_hi)]
            if diff:
                aux = aux_ref[...]
                lam = (jnp.exp(jnp.sum(aux[0:1] * aux[1:2], axis=-1, keepdims=True))
                       - jnp.exp(jnp.sum(aux[2:3] * aux[3:4], axis=-1, keepdims=True)) + lam_init)
                o = norm[0] - lam * norm[1]
                sq = o * o
                s_lo = jnp.sum(jnp.where(lane_lo, sq, 0.0), axis=-1, keepdims=True)
                s_hi = jnp.sum(jnp.where(lane_lo, 0.0, sq), axis=-1, keepdims=True)
                ms = _lane_pick(lane_lo, s_lo, s_hi) * (1.0 / HEAD_DIM)
                o = o * lax.rsqrt(ms + EPS) * aux[4:5] * (1.0 - lam_init)
            else:
                o = norm[0]
            outs.append(o)
        o_ref[...] = jnp.concatenate(outs, axis=1).astype(o_ref.dtype)

    @pl.when(shift < FIXED_SHIFT_LIMIT)
    def _():
        run(True)

    @pl.when(jnp.logical_not(shift < FIXED_SHIFT_LIMIT))
    def _():
        run(False)


def _into(prev, args, in_specs):
    if prev is None:
        return args, in_specs, {}
    return args + [prev], in_specs + [pl.BlockSpec(memory_space=pl.ANY)], {len(args): 0}


def _flash(q, k, va, vb, aux, nb, Lq, q_base, segs, diff, lam_init, out_rows, out_base, prev=None):
    tq = min(512, Lq)
    tk = 1024
    nt = Lq // tq
    q_base, out_base = q_base // tq, out_base // tq
    args = [q]
    in_specs = [pl.BlockSpec((tq, BRANCH_W), lambda b, i: (q_base + b * nt + i, 0))]
    for length, base in segs:
        spec = pl.BlockSpec((length, BRANCH_W), lambda b, i, base=base: (base + b, 0))
        args += [k, va, vb]
        in_specs += [spec, spec, spec]
    args.append(aux)
    in_specs.append(pl.BlockSpec((8, LANES), lambda b, i: (0, 0)))
    args, in_specs, alias = _into(prev, args, in_specs)
    return pl.pallas_call(
        functools.partial(_flash_kernel, segs=tuple(s[0] for s in segs), tk=tk, tq=tq, diff=diff,
                          lam_init=lam_init),
        grid=(nb, nt),
        in_specs=in_specs,
        out_specs=pl.BlockSpec((tq, BRANCH_W), lambda b, i: (out_base + b * nt + i, 0)),
        out_shape=jax.ShapeDtypeStruct((out_rows, BRANCH_W), BF16),
        input_output_aliases=alias,
        compiler_params=_cp(("arbitrary", "arbitrary"), VMEM_LIMIT_BIG),
        name="flash_diff" if diff else "flash_gqa",
    )(*args)


def _merge_kernel(y0, y1, y2, y3, g_ref, wb_ref, wo_ref, x_ref, mod_ref, o_ref):
    d = D_MODEL
    acc = g_ref[:, 0:d].astype(F32) * _dot(y0[...], wb_ref[0])
    acc = acc + g_ref[:, d:2 * d].astype(F32) * _dot(y1[...], wb_ref[1])
    acc = acc + g_ref[:, 2 * d:3 * d].astype(F32) * _dot(y2[...], wb_ref[2])
    acc = acc + g_ref[:, 3 * d:4 * d].astype(F32) * _dot(y3[...], wb_ref[3])
    mix = _dot(acc.astype(BF16), wo_ref[...])
    o_ref[...] = x_ref[...] + mod_ref[0][2:3, :] * mix


def _merge(ys, gate, wb, wo, x, mods_l, dims):
    n, d = ys[0].shape[0], x.shape[1]
    tm = dims["tm_small"]
    mrow = dims["mod_row"](tm)
    yspec = pl.BlockSpec((tm, BRANCH_W), lambda i: (i, 0))
    return pl.pallas_call(
        _merge_kernel,
        grid=(n // tm,),
        in_specs=[yspec] * 4 + [pl.BlockSpec((tm, GATE_COLS), lambda i: (i, 0)),
                                pl.BlockSpec(wb.shape, lambda i: (0, 0, 0)),
                                pl.BlockSpec(wo.shape, lambda i: (0, 0)),
                                pl.BlockSpec((tm, d), lambda i: (i, 0)),
                                pl.BlockSpec((1, 6, d), lambda i: (mrow(i), 0, 0))],
        out_specs=pl.BlockSpec((tm, d), lambda i: (i, 0)),
        out_shape=jax.ShapeDtypeStruct((n, d), F32),
        compiler_params=_cp(("arbitrary",)),
        name="merge",
    )(*ys, gate, wb, wo, x, mods_l)


def _ffn_kernel(x_ref, mod_ref, g_ref, w1_ref, w3_ref, w2_ref, o_ref, h_scr, acc_scr):
    j = pl.program_id(1)

    @pl.when(j == 0)
    def _():
        m = mod_ref[0]
        h_scr[...] = _norm_mod(x_ref[...], g_ref[...], m[3:4, :], m[4:5, :]).astype(BF16)
        acc_scr[...] = jnp.zeros_like(acc_scr)

    h = h_scr[...]
    a = _dot(h, w1_ref[...])
    b = _dot(h, w3_ref[...])
    t = (a * _sigmoid(a) * b).astype(BF16)
    acc_scr[...] += _dot(t, w2_ref[...])

    @pl.when(j == pl.num_programs(1) - 1)
    def _():
        o_ref[...] = x_ref[...] + mod_ref[0][5:6, :] * acc_scr[...]


def _ffn(x, mods_l, gain, w1, w3, w2, dims):
    n, d = x.shape
    tm, tf = dims["tm_big"], 256
    mrow = dims["mod_row"](tm)
    return pl.pallas_call(
        _ffn_kernel,
        grid=(n // tm, w1.shape[1] // tf),
        in_specs=[pl.BlockSpec((tm, d), lambda i, j: (i, 0)),
                  pl.BlockSpec((1, 6, d), lambda i, j: (mrow(i), 0, 0)),
                  pl.BlockSpec((1, d), lambda i, j: (0, 0)),
                  pl.BlockSpec((d, tf), lambda i, j: (0, j)),
                  pl.BlockSpec((d, tf), lambda i, j: (0, j)),
                  pl.BlockSpec((tf, d), lambda i, j: (j, 0))],
        out_specs=pl.BlockSpec((tm, d), lambda i, j: (i, 0)),
        out_shape=jax.ShapeDtypeStruct((n, d), F32),
        scratch_shapes=[pltpu.VMEM((tm, d), BF16), pltpu.VMEM((tm, d), F32)],
        compiler_params=_cp(("arbitrary", "arbitrary")),
        name="ffn_dense",
    )(x, mods_l, gain, w1, w3, w2)


def _route_kernel(x_ref, mod_ref, g_ref, wr_ref, tok_ref, gate_ref):
    m = mod_ref[0]
    h = _norm_mod(x_ref[...], g_ref[...], m[3:4, :], m[4:5, :])
    tok_ref[...] = h.astype(BF16)
    logits = _dot3(h, wr_ref[...])
    lane = lax.broadcasted_iota(jnp.int32, logits.shape, 1)
    lg = jnp.where(lane < N_EXPERTS, logits, -jnp.inf)
    m1 = jnp.max(lg, axis=-1, keepdims=True)
    i1 = jnp.min(jnp.where(lg == m1, lane, LANES), axis=-1, keepdims=True)
    lg2 = jnp.where(lane == i1, -jnp.inf, lg)
    m2 = jnp.max(lg2, axis=-1, keepdims=True)
    i2 = jnp.min(jnp.where(lg2 == m2, lane, LANES), axis=-1, keepdims=True)
    e2 = jnp.exp(m2 - m1)
    g1 = 1.0 / (1.0 + e2)
    gate_ref[...] = jnp.where(lane == i1, g1, jnp.where(lane == i2, e2 * g1, 0.0))


def _route(x, mods_l, gain, w_router, dims):
    n, d = x.shape
    tm = dims["tm_small"]
    mrow = dims["mod_row"](tm)
    wr = jnp.pad(w_router, ((0, 0), (0, LANES - N_EXPERTS)))
    return pl.pallas_call(
        _route_kernel,
        grid=(n // tm,),
        in_specs=[pl.BlockSpec((tm, d), lambda i: (i, 0)),
                  pl.BlockSpec((1, 6, d), lambda i: (mrow(i), 0, 0)),
                  pl.BlockSpec((1, d), lambda i: (0, 0)),
                  pl.BlockSpec((d, LANES), lambda i: (0, 0))],
        out_specs=[pl.BlockSpec((tm, d), lambda i: (i, 0)), pl.BlockSpec((tm, LANES), lambda i: (i, 0))],
        out_shape=[jax.ShapeDtypeStruct((n, d), BF16), jax.ShapeDtypeStruct((n, LANES), F32)],
        compiler_params=_cp(("arbitrary",)),
        name="moe_route",
    )(x, mods_l, gain, wr)


MOE_SUB = 128
MOE_GATHER = 256
MOE_GRAN = 64


def _moe_kernel(tok_ref, gate_ref, tri_ref, w1_ref, w3_ref, w2_ref, x_ref, mod_ref, o_ref,
                rank_scr, rank_t_scr, mask_t_scr, xe_scr, ye_scr, cnt_smem, *, tm):
    e = pl.program_id(1)
    j = pl.program_id(2)
    ne = pl.num_programs(1)
    nj = pl.num_programs(2)
    lane = lax.broadcasted_iota(jnp.int32, (1, LANES), 1)

    @pl.when((e == 0) & (j == 0))
    def _():
        tri = tri_ref[...]
        carry = jnp.zeros((1, LANES), F32)
        def put_counts(blk, counts):
            for ee in range(N_EXPERTS):
                cnt_smem[blk * N_EXPERTS + ee] = jnp.sum(jnp.where(lane == ee, counts, 0.0)).astype(jnp.int32)

        for blk in range(tm // 256):
            put_counts(blk, carry)
            rs = slice(blk * 256, (blk + 1) * 256)
            msk = (gate_ref[rs, :] > 0.0).astype(BF16)
            rank_scr[rs, :] = _dot(tri, msk) + carry
            carry = carry + jnp.sum(msk.astype(F32), axis=0, keepdims=True)
        rank_t_scr[...] = jnp.transpose(rank_scr[...])
        mask_t_scr[...] = jnp.transpose((gate_ref[...] > 0.0).astype(F32))
        put_counts(tm // 256, carry)
        o_ref[...] = jnp.zeros_like(o_ref)
        ye_scr[...] = jnp.zeros_like(ye_scr)

    cnt = cnt_smem[(tm // 256) * N_EXPERTS + e]

    n_sub = (cnt + (MOE_GATHER - 1)) // MOE_GATHER

    @pl.when(j == 0)
    def _():
        rk = rank_t_scr[pl.ds(e, 1), :]
        mk = mask_t_scr[pl.ds(e, 1), :]

        def gather(sb, carry):
            r0 = pl.multiple_of(sb * MOE_GATHER, MOE_GATHER)
            r_iota = lax.broadcasted_iota(jnp.int32, (MOE_GATHER, tm), 0) + r0
            sel = jnp.where((rk == r_iota.astype(F32)) & (mk > 0.0), 1.0, 0.0).astype(BF16)
            xe_scr[pl.ds(r0, MOE_GATHER), :] = _dot(sel, tok_ref[...]).astype(BF16)
            return carry

        lax.fori_loop(0, n_sub, gather, 0)

    def expert_rows(r0, size):
        xe = xe_scr[pl.ds(r0, size), :]
        a = _dot(xe, w1_ref[0])
        b = _dot(xe, w3_ref[0])
        t = (a * _sigmoid(a) * b).astype(BF16)
        y = _dot(t, w2_ref[0])
        ye_scr[pl.ds(r0, size), :] = jnp.where(j == 0, y, ye_scr[pl.ds(r0, size), :] + y)

    n_gran = (cnt + (MOE_GRAN - 1)) // MOE_GRAN
    n_big = n_gran // 4
    rem = n_gran % 4

    def big(i, carry):
        expert_rows(pl.multiple_of(i * 256, 256), 256)
        return carry

    lax.fori_loop(0, n_big - 1, big, 0)
    for r in range(min(4, (tm - 256) // MOE_GRAN + 1)):
        @pl.when((n_big >= 1) & (rem == r))
        def _():
            expert_rows(pl.multiple_of((n_big - 1) * 256, 256), 256 + r * MOE_GRAN)

    @pl.when((n_big == 0) & (rem >= 2))
    def _():
        expert_rows(0, 128)

    @pl.when((n_big == 0) & (rem % 2 == 1))
    def _():
        expert_rows(pl.multiple_of((rem // 2) * 128, 128), 64)

    @pl.when(j == nj - 1)
    def _():
        def scatter(rc, carry):
            r0 = pl.multiple_of(rc * 256, 256)
            gt = gate_ref[pl.ds(r0, 256), :]
            g_e = jnp.sum(jnp.where(lane == e, gt, 0.0), axis=-1, keepdims=True)
            r_e = jnp.sum(jnp.where(lane == e, rank_scr[pl.ds(r0, 256), :], 0.0), axis=-1, keepdims=True)
            lo = cnt_smem[rc * N_EXPERTS + e]
            hi = cnt_smem[(rc + 1) * N_EXPERTS + e]

            def scatter_sub(sb, c2):
                c0 = pl.multiple_of(sb * MOE_SUB, MOE_SUB)
                c_iota = lax.broadcasted_iota(jnp.int32, (256, MOE_SUB), 1) + c0
                sel_t = jnp.where((r_e == c_iota.astype(F32)) & (g_e > 0.0), 1.0, 0.0).astype(BF16)
                ye = ye_scr[pl.ds(c0, MOE_SUB), :].astype(BF16)
                o_ref[pl.ds(r0, 256), :] += g_e * _dot(sel_t, ye)
                return c2

            lax.fori_loop(lo // MOE_SUB, (hi + (MOE_SUB - 1)) // MOE_SUB, scatter_sub, 0)

            @pl.when(e == ne - 1)
            def _():
                o_ref[pl.ds(r0, 256), :] = (x_ref[pl.ds(r0, 256), :]
                                            + mod_ref[0][5:6, :] * o_ref[pl.ds(r0, 256), :])
            return carry

        lax.fori_loop(0, tm // 256, scatter, 0)


def _moe(tok, gates, w1, w3, w2, x, mods_l, dims):
    n, d = tok.shape
    tm = dims["tm_moe"] if n % dims["tm_moe"] == 0 else dims["tm_big"]
    mrow = dims["mod_row"](tm)
    ne, _, ff = w1.shape
    tf = 896
    tri =jnp.asarray(np.tril(np.ones((256, 256)), -1), BF16)
    once = pl.Buffered(1)
    return pl.pallas_call(
        functools.partial(_moe_kernel, tm=tm),
        grid=(n // tm, ne, ff // tf),
        in_specs=[pl.BlockSpec((tm, d), lambda i, e, j: (i, 0), pipeline_mode=once),
                  pl.BlockSpec((tm, LANES), lambda i, e, j: (i, 0), pipeline_mode=once),
                  pl.BlockSpec((256, 256), lambda i, e, j: (0, 0), pipeline_mode=once),
                  pl.BlockSpec((1, d, tf), lambda i, e, j: (e, 0, j)),
                  pl.BlockSpec((1, d, tf), lambda i, e, j: (e, 0, j)),
                  pl.BlockSpec((1, tf, d), lambda i, e, j: (e, j, 0)),
                  pl.BlockSpec((tm, d), lambda i, e, j: (i, 0), pipeline_mode=once),
                  pl.BlockSpec((1, 6, d), lambda i, e, j: (mrow(i), 0, 0))],
        out_specs=pl.BlockSpec((tm, d), lambda i, e, j: (i, 0), pipeline_mode=once),
        out_shape=jax.ShapeDtypeStruct((n, d), F32),
        scratch_shapes=[pltpu.VMEM((tm, LANES), F32), pltpu.VMEM((LANES, tm), F32), pltpu.VMEM((LANES, tm), F32),
                        pltpu.VMEM((tm, d), BF16), pltpu.VMEM((tm, d), F32), pltpu.SMEM(((tm // 256 + 1) * N_EXPERTS,), jnp.int32)],
        compiler_params=_cp(("arbitrary", "arbitrary", "arbitrary"), VMEM_LIMIT_BIG),
        name="moe_experts",
    )(tok, gates, tri, w1, w3, w2, x, mods_l)


def _make_dims(B, S, n_ctx):
    n_lat, n = B * S, B * (S + n_ctx)

    def pick(prefs):
        for t in prefs:
            if S % t == 0 and n_lat % t == 0 and (n - n_lat) % t == 0:
                return t
        raise ValueError("no row tile divides the latent and context token counts")

    def mod_row(tm):
        return lambda i: jnp.minimum((i * tm) // S, B)

    return {"B": B, "S": S, "n_ctx": n_ctx, "n_lat": n_lat, "n": n,
            "tm_big": pick((1024, 512, 256)), "tm_small": pick((512, 256)), "tm_moe": pick((2048, 1024, 512, 256)),
            "mod_row": mod_row}


def kernel(x, c, ctx, c_ctx, w_ada, b_ada, norm_mix, norm_ffn, w_in, hy_short_w, hy_short_b, hy_w1, hy_b1, hy_w2, hy_b2, hy_w3, hy_b3, hy_w4, hy_freq, hy_bias, cf_dw_w, cf_dw_b, cf_ln_g, cf_ln_b, gqa_qn, gqa_kn, diff_qn, diff_kn, diff_lq1, diff_lk1, diff_lq2, diff_lk2, diff_subln, w_branch, w_out, ffn_w1, ffn_w3, ffn_w2, moe_router, moe_w1, moe_w3, moe_w2):
    B, S, D = x.shape
    n_ctx = ctx.shape[1]
    depth = w_ada.shape[0]
    dims = _make_dims(B, S, n_ctx)
    n_lat = dims["n_lat"]
    tm_s = dims["tm_small"]

    c16 = jnp.concatenate([c, c_ctx[None, :], jnp.zeros((16 - B - 1, D), F32)], axis=0)
    mods = _ada_mods(c16, w_ada, b_ada).reshape(depth, 16, 6, D)

    xa = jnp.concatenate([x.reshape(n_lat, D), ctx.reshape(B * n_ctx, D)], axis=0)
    tables = _rope_tables(S, tm_s, HEAD_DIM) + _rope_tables(S, tm_s, DIFF_D)
    fft_lat = _fft_consts(S)
    mats_ctx = _dft_mats(n_ctx)

    for l in range(depth):
        lam_init = 0.8 - 0.6 * math.exp(-0.3 * l)
        u_small, gate = _in_proj(xa, mods[l], norm_mix[l].reshape(1, D), w_in[l].astype(BF16), dims)

        hy_p = (hy_short_w[l], hy_short_b[l], hy_w1[l], hy_b1[l], hy_w2[l], hy_b2[l], hy_w3[l], hy_b3[l],
                hy_w4[l], hy_freq[l], hy_bias[l])
        cf_p = (cf_dw_w[l], cf_dw_b[l], cf_ln_g[l], cf_ln_b[l])
        last = l == depth - 1
        n_out = n_lat if last else dims["n"]
        ctx_base = n_lat // n_ctx
        y_hy = _hyena_fft(u_small, S, B, hy_p, fft_lat, n_out)
        y_cf = _conformer(u_small, S, B, 0, cf_p, n_out)
        if not last:
            y_hy = _hyena(u_small, n_ctx, B, ctx_base, hy_p, mats_ctx, n_out, y_hy)
            y_cf = _conformer(u_small, n_ctx, B, ctx_base, cf_p, n_out, y_cf)

        qg, kg, vag, vbg, qd, kd, vad, vbd = _qkv_prep(u_small, gqa_qn[l], gqa_kn[l], diff_qn[l], diff_kn[l],
                                                      tables, dims)

        aux = jnp.zeros((8, LANES), F32)
        aux = aux.at[0, :DIFF_D].set(diff_lq1[l]).at[1, :DIFF_D].set(diff_lk1[l])
        aux = aux.at[2, :DIFF_D].set(diff_lq2[l]).at[3, :DIFF_D].set(diff_lk2[l])
        aux = aux.at[4, :].set(jnp.tile(diff_subln[l], 2))

        def score_bound(qn, kn, d):
            return 1.02 * d * jnp.max(jnp.abs(qn)) * jnp.max(jnp.abs(kn)) * (LOG2E * d ** -0.5) + 0.1

        bounds = (score_bound(gqa_qn[l], gqa_kn[l], HEAD_DIM), score_bound(diff_qn[l], diff_kn[l], DIFF_D))
        y_att = []
        for (q, k, va, vb, is_diff) in ((qg, kg, vag, vbg, False), (qd, kd, vad, vbd, True)):
            aux = aux.at[5, :].set(bounds[int(is_diff)])
            y = _flash(q, k, va, vb, aux, B, S, 0, ((S, 0), (n_ctx, ctx_base)), is_diff, lam_init, n_out, 0)
            if not last:
                y = _flash(q, k, va, vb, aux, B, n_ctx, n_lat, ((n_ctx, ctx_base),), is_diff, lam_init,
                           n_out, n_lat, y)
            y_att.append(y)

        xa = _merge((y_hy, y_cf, y_att[0], y_att[1]), gate, w_branch[l].astype(BF16), w_out[l].astype(BF16),
                    xa, mods[l], dims)

        i = l // 2
        if l % 2 == 0:
            xa = _ffn(xa, mods[l], norm_ffn[l].reshape(1, D), ffn_w1[i].astype(BF16), ffn_w3[i].astype(BF16),
                      ffn_w2[i].astype(BF16), dims)
        else:
            tok, gates = _route(xa, mods[l], norm_ffn[l].reshape(1, D), moe_router[i], dims)
            xa = _moe(tok, gates, moe_w1[i].astype(BF16), moe_w3[i].astype(BF16), moe_w2[i].astype(BF16),
                      xa, mods[l], dims)
    return xa[:n_lat].reshape(B, S, D)
```

```python
import functools
import math

import numpy as np
import jax
import jax.numpy as jnp
from jax import lax
from jax.experimental import pallas as pl
from jax.experimental.pallas import tpu as pltpu

F32 = jnp.float32
BF16 = jnp.bfloat16

D_MODEL = 1024
GRID_W = 64
BRANCH_W = 256
EPS = 1e-6
HY_EMB = 33
HY_EMB_PAD = 128
HY_FFN = 64
HY_TARGET = 1e-2
HY_FAST = 0.3
HY_SLOW = 1.5
CF_WIDTH = 31
HEAD_DIM = 64
DIFF_D = 32
ROPE_BASE = 10000.0
N_EXPERTS = 8
LANES = 128
LOG2E = 1.4426950408889634
FIXED_SHIFT_LIMIT = 60.0
SMALL_COLS = 2560
GATE_COLS = 4 * D_MODEL
VMEM_LIMIT = 48 * 1024 * 1024
VMEM_LIMIT_BIG = 56 * 1024 * 1024


def _cp(sem, vmem=VMEM_LIMIT):
    return pltpu.CompilerParams(dimension_semantics=sem, vmem_limit_bytes=vmem)


def _sigmoid(x):
    return 1.0 / (1.0 + jnp.exp(-x))


def _dot(a, b):
    return jnp.dot(a, b, preferred_element_type=F32)


def _split(a):
    hi = a.astype(BF16)
    lo = (a - hi.astype(F32)).astype(BF16)
    return hi, lo


def _dot3(a, b):
    ah, al = _split(a)
    bh, bl = _split(b)
    return _dot(ah, bh) + (_dot(al, bh) + _dot(ah, bl))


def _norm_mod(x, gain, shift, scale):
    ms = jnp.mean(x * x, axis=-1, keepdims=True)
    return (x * lax.rsqrt(ms + EPS) * gain) * (1.0 + scale) + shift


def _ada_kernel(c_ref, w_ref, b_ref, o_ref):
    c = c_ref[...]
    s = c * _sigmoid(c)
    o_ref[0] = _dot3(s, w_ref[0]) + b_ref[0]


def _ada_mods(c16, w_ada, b_ada):
    depth, d, n6 = w_ada.shape
    tn = 512
    return pl.pallas_call(
        _ada_kernel,
        grid=(depth, n6 // tn),
        in_specs=[pl.BlockSpec((16, d), lambda l, j: (0, 0)),
                  pl.BlockSpec((1, d, tn), lambda l, j: (l, 0, j)),
                  pl.BlockSpec((1, 1, tn), lambda l, j: (l, 0, j))],
        out_specs=pl.BlockSpec((1, 16, tn), lambda l, j: (l, 0, j)),
        out_shape=jax.ShapeDtypeStruct((depth, 16, n6), F32),
        compiler_params=_cp(("arbitrary", "arbitrary")),
        name="ada_mods",
    )(c16, w_ada, b_ada.reshape(depth, 1, n6))


def _in_proj_kernel(x_ref, mod_ref, g_ref, w_ref, o_ref, h_scr, *, gate):
    @pl.when(pl.program_id(1) == 0)
    def _():
        m = mod_ref[0]
        h_scr[...] = _norm_mod(x_ref[...], g_ref[...], m[0:1, :], m[1:2, :]).astype(BF16)

    r = _dot(h_scr[...], w_ref[...])
    o_ref[...] = _sigmoid(r).astype(o_ref.dtype) if gate else r


def _in_proj(x, mods_l, gain, w_bf, dims):
    n, d = x.shape
    tm = dims["tm_big"]
    mrow = dims["mod_row"](tm)

    def call(w, tn, gate, dtype):
        n_cols = w.shape[1]
        return pl.pallas_call(
            functools.partial(_in_proj_kernel, gate=gate),
            grid=(n // tm, n_cols // tn),
            in_specs=[pl.BlockSpec((tm, d), lambda i, j: (i, 0)),
                      pl.BlockSpec((1, 6, d), lambda i, j: (mrow(i), 0, 0)),
                      pl.BlockSpec((1, d), lambda i, j: (0, 0)),
                      pl.BlockSpec((d, tn), lambda i, j: (0, j))],
            out_specs=pl.BlockSpec((tm, tn), lambda i, j: (i, j)),
            out_shape=jax.ShapeDtypeStruct((n, n_cols), dtype),
            scratch_shapes=[pltpu.VMEM((tm, d), BF16)],
            compiler_params=_cp(("arbitrary", "arbitrary")),
            name="in_proj_gate" if gate else "in_proj_mix",
        )(x, mods_l, gain, w)

    return (call(w_bf[:, :SMALL_COLS], SMALL_COLS // 2, False, F32),
            call(w_bf[:, SMALL_COLS:], GATE_COLS // 4, True, BF16))


def _short_conv_kernel(u_ref, w_ref, b_ref, o_ref, pad_scr, *, L, tc, planes):
    if planes:
        o_ref = o_ref.at[0]
    pad_scr[0:8, :] = jnp.zeros((8, BRANCH_W), F32)
    pad_scr[8 + L:16 + L, :] = jnp.zeros((8, BRANCH_W), F32)
    pad_scr[8:8 + L, :] = u_ref[...]
    w = w_ref[0]
    b = b_ref[0]

    def body(c, carry):
        t0 = pl.multiple_of(c * tc, tc)
        win = pad_scr[pl.ds(t0, tc + 16), :]
        acc = b + w[0:1, :] * win[7:7 + tc]
        acc = acc + w[1:2, :] * win[8:8 + tc]
        acc = acc + w[2:3, :] * win[9:9 + tc]
        o_ref[pl.ds(t0, tc), :] = acc
        return carry

    lax.fori_loop(0, L // tc, body, 0)


def _short_conv(u_small, w, b, L, nseq, row_base, planes=False):
    w3 = jnp.transpose(w.reshape(3, 3, BRANCH_W), (1, 0, 2))
    w3 = jnp.pad(w3, ((0, 0), (0, 5), (0, 0)))
    b3 = b.reshape(3, 1, BRANCH_W)
    tc = min(L, 256)
    return pl.pallas_call(
        functools.partial(_short_conv_kernel, L=L, tc=tc, planes=planes),
        grid=(nseq, 3),
        in_specs=[pl.BlockSpec((L, BRANCH_W), lambda s, j: (row_base + s, j)),
                  pl.BlockSpec((1, 8, BRANCH_W), lambda s, j: (j, 0, 0)),
                  pl.BlockSpec((1, 1, BRANCH_W), lambda s, j: (j, 0, 0))],
        out_specs=(pl.BlockSpec((1, L, BRANCH_W), lambda s, j: (j, s, 0)) if planes
                   else pl.BlockSpec((L, BRANCH_W), lambda s, j: (s, j))),
        out_shape=jax.ShapeDtypeStruct((3, nseq * L, BRANCH_W) if planes else (nseq * L, 3 * BRANCH_W), F32),
        scratch_shapes=[pltpu.VMEM((L + 16, BRANCH_W), F32)],
        compiler_params=_cp(("arbitrary", "arbitrary")),
        name="hy_short_conv",
    )(u_small, w3, b3)


def _filter_consts(L):
    t = np.linspace(0.0, 1.0, L)[:, None]
    bands = (HY_EMB - 1) // 2
    fr = np.linspace(1e-4, bands - 1, bands)[None, :]
    wpos = 2.0 * math.pi * np.arange(L)[:, None] / L
    z = np.concatenate([t, np.cos(fr * wpos), -np.sin(fr * wpos)], axis=-1)
    z = np.pad(z, ((0, 0), (0, HY_EMB_PAD - HY_EMB)))
    deltas = np.abs(np.linspace(math.log(HY_TARGET) / HY_SLOW, math.log(HY_TARGET) / HY_FAST, BRANCH_W))
    win = np.exp(-t * deltas[None, :])
    return jnp.asarray(z, F32), jnp.asarray(win, F32)


def _filter_kernel(z_ref, w1, b1, w2, b2, w3, b3, w4, fq, win_ref, hf_ref, ss_ref, *, tr, planes):
    i = pl.program_id(0)
    f = fq[...]
    a = jnp.sin(f * (_dot3(z_ref[...], w1[...]) + b1[...]))
    a = jnp.sin(f * (_dot3(a, w2[...]) + b2[...]))
    a = jnp.sin(f * (_dot3(a, w3[...]) + b3[...]))
    h = _dot3(a, w4[...])
    win = win_ref[...]
    h = h * jnp.concatenate([win, win, win, win], axis=1)
    row = lax.broadcasted_iota(jnp.int32, h.shape, 0) + i * tr
    col = lax.broadcasted_iota(jnp.int32, h.shape, 1)
    h = jnp.where((row == 0) & (col >= 2 * BRANCH_W), 0.0, h)
    if planes:
        for d in range(4):
            hf_ref[d] = h[:, d * BRANCH_W:(d + 1) * BRANCH_W]
    else:
        hf_ref[...] = h

    @pl.when(i == 0)
    def _():
        ss_ref[...] = jnp.zeros_like(ss_ref)

    ss_ref[...] += jnp.sum(h * h, axis=0, keepdims=True)


def _hyena_filter(L, w1, b1, w2, b2, w3, b3, w4, freq, planes=False):
    z, win = _filter_consts(L)
    tr = min(L, 256)
    w1p = jnp.pad(w1, ((0, HY_EMB_PAD - HY_EMB), (0, 0)))
    full = lambda a: pl.BlockSpec(a.shape, lambda i: (0,) * a.ndim)
    args = [w1p, b1.reshape(1, -1), w2, b2.reshape(1, -1), w3, b3.reshape(1, -1), w4, freq.reshape(1, -1)]
    n_out = w4.shape[1]
    return pl.pallas_call(
        functools.partial(_filter_kernel, tr=tr, planes=planes),
        grid=(L // tr,),
        in_specs=[pl.BlockSpec((tr, HY_EMB_PAD), lambda i: (i, 0))] + [full(a) for a in args]
        + [pl.BlockSpec((tr, BRANCH_W), lambda i: (i, 0))],
        out_specs=[pl.BlockSpec((4, tr, BRANCH_W), lambda i: (0, i, 0)) if planes
                   else pl.BlockSpec((tr, n_out), lambda i: (i, 0)),
                   pl.BlockSpec((1, n_out), lambda i: (0, 0))],
        out_shape=[jax.ShapeDtypeStruct((4, L, BRANCH_W) if planes else (L, n_out), F32),
                   jax.ShapeDtypeStruct((1, n_out), F32)],
        compiler_params=_cp(("arbitrary",)),
        name="hy_filter",
    )(z, *args, win)


def _dft_mats(L):
    N = 2 * L
    blk = 64
    k = jnp.arange(L, dtype=jnp.int32)[:, None]
    nh = jnp.arange(L // blk, dtype=jnp.int32)[None, :]
    nl = jnp.arange(blk, dtype=jnp.int32)[None, :]
    w = 2.0 * math.pi / N
    a = ((k * (blk * nh)) % N).astype(F32) * w
    b = ((k * nl) % N).astype(F32) * w
    ca, sa, cb, sb = jnp.cos(a), jnp.sin(a), jnp.cos(b), jnp.sin(b)
    cos = (ca[:, :, None] * cb[:, None, :] - sa[:, :, None] * sb[:, None, :]).reshape(L, L)
    sin = (sa[:, :, None] * cb[:, None, :] + ca[:, :, None] * sb[:, None, :]).reshape(L, L)
    alt = jnp.where(jnp.arange(L) % 2 == 0, 1.0, -1.0).astype(F32)
    first = (jnp.arange(L) == 0)
    s_f = jnp.where(first[:, None], alt[None, :], -sin)
    fwd = jnp.concatenate([cos, s_f], axis=0).astype(BF16)
    colscale = jnp.where(first, 1.0 / N, 2.0 / N).astype(F32)
    g_c = cos * colscale[None, :]
    g_s = jnp.where(first[None, :], alt[:, None] / N, -sin * (2.0 / N))
    inv = jnp.concatenate([g_c, g_s], axis=1).astype(BF16)
    return fwd, inv


def _dft_fwd_kernel(f_ref, z_ref, o_ref):
    o_ref[0] = _dot(f_ref[...], z_ref[...].astype(BF16))


def _dft_fwd(fwd, z2d, L, nb, zmap):
    tm = min(2 * L, 1024)
    return pl.pallas_call(
        _dft_fwd_kernel,
        grid=(2 * L // tm, nb),
        in_specs=[pl.BlockSpec((tm, L), lambda i, b: (i, 0)),
                  pl.BlockSpec((L, BRANCH_W), lambda i, b: zmap(b))],
        out_specs=pl.BlockSpec((1, tm, BRANCH_W), lambda i, b: (b, i, 0)),
        out_shape=jax.ShapeDtypeStruct((nb, 2 * L, BRANCH_W), F32),
        compiler_params=_cp(("arbitrary", "arbitrary")),
        name="hy_dft_fwd",
    )(fwd, z2d)


def _spec_prod_kernel(z_ref, kf_ref, kb_ref, sf_ref, sb_ref, y_ref, *, tk):
    i = pl.program_id(1)
    s = lax.rsqrt(sf_ref[...] + sb_ref[...] + EPS)
    zr, zi = z_ref[0, 0], z_ref[0, 1]
    fr, fi = kf_ref[0, 0], kf_ref[0, 1]
    br, bi = kb_ref[0, 0], kb_ref[0, 1]
    row = lax.broadcasted_iota(jnp.int32, (tk, BRANCH_W), 0) + i * tk
    first = row == 0
    kr = (fr + br) * s
    ki = jnp.where(first, fi + bi, fi - bi) * s
    yr = jnp.where(first, zr * kr, zr * kr - zi * ki)
    yi = jnp.where(first, zi * ki, zr * ki + zi * kr)
    y_ref[0, 0] = yr.astype(BF16)
    y_ref[0, 1] = yi.astype(BF16)


def _spec_prod(zf, kfs, ss, order, L, nb):
    tk = min(L, 512)
    zf4 = zf.reshape(nb, 2, L, BRANCH_W)
    kf4 = kfs.reshape(4, 2, L, BRANCH_W)
    y = pl.pallas_call(
        functools.partial(_spec_prod_kernel, tk=tk),
        grid=(nb, L // tk),
        in_specs=[pl.BlockSpec((1, 2, tk, BRANCH_W), lambda b, i: (b, 0, i, 0)),
                  pl.BlockSpec((1, 2, tk, BRANCH_W), lambda b, i: (order, 0, i, 0)),
                  pl.BlockSpec((1, 2, tk, BRANCH_W), lambda b, i: (2 + order, 0, i, 0)),
                  pl.BlockSpec((1, BRANCH_W), lambda b, i: (0, order)),
                  pl.BlockSpec((1, BRANCH_W), lambda b, i: (0, 2 + order))],
        out_specs=pl.BlockSpec((1, 2, tk, BRANCH_W), lambda b, i: (b, 0, i, 0)),
        out_shape=jax.ShapeDtypeStruct((nb, 2, L, BRANCH_W), BF16),
        compiler_params=_cp(("arbitrary", "arbitrary")),
        name="hy_spec_prod",
    )(zf4, kf4, kf4, ss, ss)
    return y.reshape(nb, 2 * L, BRANCH_W)


def _dft_inv_kernel(g_ref, y_ref, gate_ref, zp_ref, bias_ref, *rest):
    o_ref = rest[-1]
    conv = _dot(g_ref[...], y_ref[0])
    o_ref[...] = (gate_ref[...] * (conv + bias_ref[...] * zp_ref[...])).astype(o_ref.dtype)


def _dft_inv(inv, y, xs, gate_col, zprev, zprev_col, bias, L, nb, out_dtype, out_rows=None, out_base=0,
             prev=None):
    tm = min(L, 512)
    nt = L // tm
    args = [inv, y, xs, zprev, bias]
    in_specs = [pl.BlockSpec((tm, 2 * L), lambda i, b: (i, 0)),
                pl.BlockSpec((1, 2 * L, BRANCH_W), lambda i, b: (b, 0, 0)),
                pl.BlockSpec((tm, BRANCH_W), lambda i, b: (b * nt + i, gate_col)),
                pl.BlockSpec((tm, BRANCH_W), lambda i, b: (b * nt + i, zprev_col)),
                pl.BlockSpec((1, BRANCH_W), lambda i, b: (0, 0))]
    args, in_specs, alias = _into(prev, args, in_specs)
    return pl.pallas_call(
        _dft_inv_kernel,
        grid=(nt, nb),
        in_specs=in_specs,
        out_specs=pl.BlockSpec((tm, BRANCH_W), lambda i, b: ((out_base + b) * nt + i, 0)),
        out_shape=jax.ShapeDtypeStruct((out_rows or nb * L, BRANCH_W), out_dtype),
        input_output_aliases=alias,
        compiler_params=_cp(("arbitrary", "arbitrary")),
        name="hy_dft_inv",
    )(*args)


def _hyena(u_small, L, nseq, row_base, p, mats, out_rows, prev=None):
    (short_w, short_b, w1, b1, w2, b2, w3, b3, w4, freq, bias) = p
    fwd, inv = mats
    xs = _short_conv(u_small, short_w, short_b, L, nseq, row_base)
    hf, ss = _hyena_filter(L, w1, b1, w2, b2, w3, b3, w4, freq)
    kfs = _dft_fwd(fwd, hf, L, 4, lambda b: (0, b))
    zf = _dft_fwd(fwd, xs, L, nseq, lambda b: (b, 2))
    y = _spec_prod(zf, kfs, ss, 0, L, nseq)
    z1 = _dft_inv(inv, y, xs, 0, xs, 2, bias[0:1], L, nseq, F32)
    zf = _dft_fwd(fwd, z1, L, nseq, lambda b: (b, 0))
    y = _spec_prod(zf, kfs, ss, 1, L, nseq)
    return _dft_inv(inv, y, xs, 1, z1, 0, bias[1:2], L, nseq, BF16, out_rows, row_base, prev)


FFT_N1 = 64
FFT_KB = 16
FFT_COLS = 16384


def _fft_consts(L):
    N = 2 * L
    n2s = N // FFT_N1
    h = FFT_N1 // 2
    k1 = jnp.arange(FFT_N1, dtype=jnp.int32)
    ang_a = ((k1[:, None] * jnp.arange(h, dtype=jnp.int32)[None, :]) % FFT_N1).astype(F32) * (2.0 * math.pi / FFT_N1)
    ca, sa = jnp.cos(ang_a), jnp.sin(ang_a)
    fa = jnp.concatenate([ca, -sa], axis=0).astype(BF16)
    fai = (jnp.concatenate([ca.T, -sa.T], axis=1) / N).astype(BF16)
    k = k1[:, None, None] + FFT_N1 * jnp.arange(n2s, dtype=jnp.int32)[None, :, None]
    n2 = jnp.arange(n2s, dtype=jnp.int32)[None, None, :]
    ang = ((k * n2) % N).astype(F32) * (2.0 * math.pi / N)
    c, s = jnp.cos(ang), jnp.sin(ang)
    g = jnp.concatenate([jnp.concatenate([c, s], axis=2), jnp.concatenate([-s, c], axis=2)], axis=1)
    ct, st = jnp.swapaxes(c, 1, 2), jnp.swapaxes(s, 1, 2)
    gi = jnp.concatenate([jnp.concatenate([ct, -st], axis=2), jnp.concatenate([st, ct], axis=2)], axis=1)
    return fa, fai, g.astype(BF16), gi.astype(BF16)


def _fft_a_kernel(f_ref, z_ref, o_ref):
    o_ref[...] = _dot(f_ref[...], z_ref[0].astype(BF16)).astype(o_ref.dtype)


def _fft_a(fa, z3, zmap, nb, L):
    h = FFT_N1 // 2
    cols = (L // h) * BRANCH_W
    tn = min(cols, FFT_COLS)
    return pl.pallas_call(
        _fft_a_kernel,
        grid=(nb, cols // tn),
        in_specs=[pl.BlockSpec(fa.shape, lambda b, j: (0, 0)),
                  pl.BlockSpec((1, h, tn), lambda b, j: zmap(b) + (j,))],
        out_specs=pl.BlockSpec((2 * FFT_N1, tn), lambda b, j: (b, j)),
        out_shape=jax.ShapeDtypeStruct((nb * 2 * FFT_N1, cols), BF16),
        compiler_params=_cp(("arbitrary", "arbitrary")),
        name="hy_fft_a",
    )(fa, z3)


def _fft_spec_kernel(a_ref, g_ref, o_ref):
    for j in range(FFT_KB):
        s = _dot(g_ref[j], jnp.concatenate([a_ref[0, 0, j], a_ref[0, 1, j]], axis=0))
        half = s.shape[0] // 2
        o_ref[0, j, 0] = s[:half]
        o_ref[0, j, 1] = s[half:]


def _fft_spec(g, a5):
    nb, _, n1s, n2s, w = a5.shape
    return pl.pallas_call(
        _fft_spec_kernel,
        grid=(n1s // FFT_KB, nb),
        in_specs=[pl.BlockSpec((1, 2, FFT_KB, n2s, w), lambda i, b: (b, 0, i, 0, 0)),
                  pl.BlockSpec((FFT_KB, 2 * n2s, 2 * n2s), lambda i, b: (i, 0, 0))],
        out_specs=pl.BlockSpec((1, FFT_KB, 2, n2s, w), lambda i, b: (b, i, 0, 0, 0)),
        out_shape=jax.ShapeDtypeStruct((nb, n1s, 2, n2s, w), F32),
        compiler_params=_cp(("arbitrary", "arbitrary")),
        name="hy_fft_spec",
    )(a5, g)


def _fft_conv_kernel(a_ref, g_ref, gi_ref, sf_ref, sb_ref, ssf_ref, ssb_ref, o_ref):
    scale = lax.rsqrt(ssf_ref[...] + ssb_ref[...] + EPS)
    for j in range(FFT_KB):
        s = _dot(g_ref[j], jnp.concatenate([a_ref[0, 0, j], a_ref[0, 1, j]], axis=0))
        half = s.shape[0] // 2
        sr, si = s[:half], s[half:]
        kr = (sf_ref[0, j, 0] + sb_ref[0, j, 0]) * scale
        ki = (sf_ref[0, j, 1] - sb_ref[0, j, 1]) * scale
        y = jnp.concatenate([sr * kr - si * ki, sr * ki + si * kr], axis=0).astype(BF16)
        t = _dot(gi_ref[j], y)
        o_ref[0, 0, j] = t[:half].astype(BF16)
        o_ref[0, 1, j] = t[half:].astype(BF16)


def _fft_conv(g, gi, a5, kspec, ss, order):
    nb, _, n1s, n2s, w = a5.shape
    kblk = lambda d: pl.BlockSpec((1, FFT_KB, 2, n2s, w), lambda i, b: (d, i, 0, 0, 0))
    gblk = pl.BlockSpec((FFT_KB, 2 * n2s, 2 * n2s), lambda i, b: (i, 0, 0))
    ablk = pl.BlockSpec((1, 2, FFT_KB, n2s, w), lambda i, b: (b, 0, i, 0, 0))
    return pl.pallas_call(
        _fft_conv_kernel,
        grid=(n1s // FFT_KB, nb),
        in_specs=[ablk, gblk, gblk, kblk(order), kblk(2 + order),
                  pl.BlockSpec((1, w), lambda i, b: (0, order)),
                  pl.BlockSpec((1, w), lambda i, b: (0, 2 + order))],
        out_specs=ablk,
        out_shape=jax.ShapeDtypeStruct(a5.shape, BF16),
        compiler_params=_cp(("arbitrary", "arbitrary")),
        name="hy_fft_conv",
    )(a5, g, gi, kspec, kspec, ss, ss)


def _fft_ainv_kernel(f_ref, a_ref, gate_ref, zp_ref, bias_ref, o_ref):
    conv = _dot(f_ref[...], a_ref[...])
    o_ref[...] = (gate_ref[0] * (conv + bias_ref[...] * zp_ref[0])).astype(o_ref.dtype)


def _fft_ainv(fai, a2, gate3, gate_plane, zprev3, zprev_plane, bias, L, nb, out_dtype, out_view_rows):
    h = FFT_N1 // 2
    cols = (L // h) * BRANCH_W
    tn = min(cols, FFT_COLS)
    return pl.pallas_call(
        _fft_ainv_kernel,
        grid=(nb, cols // tn),
        in_specs=[pl.BlockSpec(fai.shape, lambda b, j: (0, 0)),
                  pl.BlockSpec((2 * FFT_N1, tn), lambda b, j: (b, j)),
                  pl.BlockSpec((1, h, tn), lambda b, j: (gate_plane, b, j)),
                  pl.BlockSpec((1, h, tn), lambda b, j: (zprev_plane, b, j)),
                  pl.BlockSpec((1, tn), lambda b, j: (0, 0))],
        out_specs=pl.BlockSpec((h, tn), lambda b, j: (b, j)),
        out_shape=jax.ShapeDtypeStruct((out_view_rows, cols), out_dtype),
        compiler_params=_cp(("arbitrary", "arbitrary")),
        name="hy_fft_ainv",
    )(fai, a2, gate3, zprev3, jnp.tile(bias, (1, tn // BRANCH_W)))


def _hyena_fft(u_small, L, nseq, p, consts, out_rows):
    (short_w, short_b, w1, b1, w2, b2, w3, b3, w4, freq, bias) = p
    fa, fai, g, gi = consts
    h = FFT_N1 // 2
    n2s = L // h
    cols = n2s * BRANCH_W
    xs3 = _short_conv(u_small, short_w, short_b, L, nseq, 0, planes=True).reshape(3, nseq * h, cols)
    hf4, ss = _hyena_filter(L, w1, b1, w2, b2, w3, b3, w4, freq, planes=True)
    ka = _fft_a(fa, hf4.reshape(4, h, cols), lambda b: (b, 0), 4, L)
    kspec = _fft_spec(g, ka.reshape(4, 2, FFT_N1, n2s, BRANCH_W))

    def conv(z3, zmap, order, gate_plane, zprev_plane, out_dtype, out_view_rows):
        a = _fft_a(fa, z3, zmap, nseq, L)
        a = _fft_conv(g, gi, a.reshape(nseq, 2, FFT_N1, n2s, BRANCH_W), kspec, ss, order)
        return _fft_ainv(fai, a.reshape(nseq * 2 * FFT_N1, cols), xs3, gate_plane, z3, zprev_plane,
                         bias[order:order + 1], L, nseq, out_dtype, out_view_rows)

    z1 = conv(xs3, lambda b: (2, b), 0, 0, 2, F32, nseq * h).reshape(1, nseq * h, cols)
    y = conv(z1, lambda b: (0, b), 1, 1, 0, BF16, out_rows // n2s)
    return y.reshape(out_rows, BRANCH_W)


def _conformer_kernel(a_ref, g_ref, w_ref, b_ref, lg_ref, lb_ref, *rest, L, tc):
    o_ref, pad_scr = rest[-2:]
    pad_scr[0:16, :] = jnp.zeros((16, BRANCH_W), F32)
    pad_scr[16 + L:32 + L, :] = jnp.zeros((16, BRANCH_W), F32)

    def glu(c, carry):
        t0 = pl.multiple_of(c * tc, tc)
        pad_scr[pl.ds(16 + t0, tc), :] = a_ref[pl.ds(t0, tc), :] * _sigmoid(g_ref[pl.ds(t0, tc), :])
        return carry

    lax.fori_loop(0, L // tc, glu, 0)
    b = b_ref[...]
    lg = lg_ref[...]
    lb = lb_ref[...]

    def body(c, carry):
        t0 = pl.multiple_of(c * tc, tc)
        win = pad_scr[pl.ds(t0, tc + 32), :]
        acc = jnp.zeros((tc, BRANCH_W), F32) + b
        for r in range(8):
            sh = win[r:r + tc + 24]
            for a in range(4):
                m = 8 * a + r
                if 1 <= m <= CF_WIDTH:
                    acc = acc + w_ref[m - 1:m, :] * sh[8 * a:8 * a + tc]
        mu = jnp.mean(acc, axis=-1, keepdims=True)
        xc = acc - mu
        var = jnp.mean(xc * xc, axis=-1, keepdims=True)
        y = xc * lax.rsqrt(var + EPS) * lg + lb
        o_ref[pl.ds(t0, tc), :] = (y * _sigmoid(y)).astype(o_ref.dtype)
        return carry

    lax.fori_loop(0, L // tc, body, 0)


def _conformer(u_small, L, nseq, row_base, p, out_rows, prev=None):
    dw_w, dw_b, ln_g, ln_b = p
    tc = 128
    wpad = jnp.pad(dw_w, ((0, 32 - CF_WIDTH), (0, 0)))
    row = lambda a: a.reshape(1, BRANCH_W)
    vec = pl.BlockSpec((1, BRANCH_W), lambda s: (0, 0))
    args = [u_small, u_small, wpad, row(dw_b), row(ln_g), row(ln_b)]
    in_specs = [pl.BlockSpec((L, BRANCH_W), lambda s: (row_base + s, 3)),
                pl.BlockSpec((L, BRANCH_W), lambda s: (row_base + s, 4)),
                pl.BlockSpec((32, BRANCH_W), lambda s: (0, 0)), vec, vec, vec]
    args, in_specs, alias = _into(prev, args, in_specs)
    return pl.pallas_call(
        functools.partial(_conformer_kernel, L=L, tc=tc),
        grid=(nseq,),
        in_specs=in_specs,
        out_specs=pl.BlockSpec((L, BRANCH_W), lambda s: (row_base + s, 0)),
        out_shape=jax.ShapeDtypeStruct((out_rows, BRANCH_W), BF16),
        input_output_aliases=alias,
        scratch_shapes=[pltpu.VMEM((L + 32, BRANCH_W), F32)],
        compiler_params=_cp(("arbitrary",)),
        name="conformer",
    )(*args)


def _rope_tables(S, pad_rows, head):
    half = head // 2
    nf = half // 2
    lane = np.arange(LANES)
    inv_lane = (ROPE_BASE ** (-(np.arange(nf)) / nf))[(lane % half) % nf]
    is_row = (lane % head) < half
    pos = jnp.arange(S, dtype=jnp.int32)
    rows = (pos // GRID_W).astype(F32)[:, None]
    cols = (pos % GRID_W).astype(F32)[:, None]
    ang = jnp.where(jnp.asarray(is_row)[None, :], rows, cols) * jnp.asarray(inv_lane, F32)[None, :]
    cos = jnp.concatenate([jnp.cos(ang), jnp.ones((pad_rows, LANES), F32)], axis=0)
    sin = jnp.concatenate([jnp.sin(ang), jnp.zeros((pad_rows, LANES), F32)], axis=0)
    return cos, sin


def _group_ones(width, group):
    idx = np.arange(width)
    return jnp.asarray((idx[:, None] // group) == (idx[None, :] // group), BF16)


def _head_norm_rope(x, ones, group, gain, cos, sin, nf, out_scale):
    w = x.shape[1]
    hi, lo = _split(x * x)
    ms = (_dot(hi, ones) + _dot(lo, ones)) * (1.0 / group)
    xn = x * lax.rsqrt(ms + EPS) * gain
    reps = w // LANES
    c = jnp.concatenate([cos] * reps, axis=1) if reps > 1 else cos
    s = jnp.concatenate([sin] * reps, axis=1) if reps > 1 else sin
    lane = lax.broadcasted_iota(jnp.int32, x.shape, 1)
    first = (lane % (2 * nf)) < nf
    rot = jnp.where(first, -pltpu.roll(xn, w - nf, 1), pltpu.roll(xn, nf, 1))
    return (xn * c + rot * s) * out_scale


def _qkv_kernel(gq_ref, gkv_ref, dq_ref, dk_ref, dv_ref, cg_ref, sg_ref, cd_ref, sd_ref,
                o64_ref, o32_ref, gqn_ref, gkn_ref, dqn_ref, dkn_ref,
                qg_ref, kg_ref, vag_ref, vbg_ref, qd_ref, kd_ref, vad_ref, vbd_ref):
    cg, sg, cd, sd = cg_ref[...], sg_ref[...], cd_ref[...], sd_ref[...]
    o64, o32 = o64_ref[...], o32_ref[...]
    q = _head_norm_rope(gq_ref[...], o64, HEAD_DIM, gqn_ref[...], cg, sg, HEAD_DIM // 4, LOG2E * HEAD_DIM ** -0.5)
    qg_ref[...] = q.astype(BF16)
    kv = gkv_ref[...]
    k = _head_norm_rope(kv[:, :LANES], o64[:LANES, :LANES], HEAD_DIM, gkn_ref[...], cg, sg, HEAD_DIM // 4, 1.0)
    v = kv[:, LANES:]
    kk = jnp.concatenate([k, k], axis=1)
    vv = jnp.concatenate([v, v], axis=1)
    quarter = lax.broadcasted_iota(jnp.int32, kk.shape, 1) // HEAD_DIM
    kr = pltpu.roll(kk, HEAD_DIM, 1)
    vr = pltpu.roll(vv, HEAD_DIM, 1)
    kg_ref[...] = jnp.where((quarter == 0) | (quarter == 3), kk, kr).astype(BF16)
    vag_ref[...] = jnp.where(quarter == 0, vv, jnp.where(quarter == 2, vr, 0.0)).astype(BF16)
    vbg_ref[...] = jnp.where(quarter == 1, vr, jnp.where(quarter == 3, vv, 0.0)).astype(BF16)
    qd = _head_norm_rope(dq_ref[...], o32, DIFF_D, dqn_ref[...], cd, sd, DIFF_D // 4, LOG2E * DIFF_D ** -0.5)
    qd_ref[...] = qd.astype(BF16)
    kd = _head_norm_rope(dk_ref[...], o32, DIFF_D, dkn_ref[...], cd, sd, DIFF_D // 4, 1.0)
    kd_ref[...] = kd.astype(BF16)
    vd = dv_ref[...]
    even = (lax.broadcasted_iota(jnp.int32, vd.shape, 1) // HEAD_DIM) % 2 == 0
    vad_ref[...] = jnp.where(even, vd, 0.0).astype(BF16)
    vbd_ref[...] = jnp.where(even, 0.0, vd).astype(BF16)


def _qkv_prep(u_small, gqn, gkn, dqn, dkn, tables, dims):
    n = u_small.shape[0]
    tm = dims["tm_small"]
    S = dims["S"]
    n_lat_tiles = dims["n_lat"] // tm
    per_seq = S // tm
    tmap = lambda i: (jnp.where(i < n_lat_tiles, i % per_seq, per_seq), 0)
    col = lambda c: pl.BlockSpec((tm, BRANCH_W), lambda i: (i, c))
    tab = pl.BlockSpec((tm, LANES), tmap)
    full = lambda a: pl.BlockSpec(a.shape, lambda i: (0,) * a.ndim)
    o64, o32 = _group_ones(BRANCH_W, HEAD_DIM), _group_ones(BRANCH_W, DIFF_D)
    gains = [jnp.tile(gqn, 4).reshape(1, 256), jnp.tile(gkn, 2).reshape(1, 128),
             jnp.tile(dqn.reshape(-1), 4).reshape(1, 256), jnp.tile(dkn.reshape(-1), 4).reshape(1, 256)]
    out = pl.BlockSpec((tm, BRANCH_W), lambda i: (i, 0))
    return pl.pallas_call(
        _qkv_kernel,
        grid=(n // tm,),
        in_specs=[col(5), col(6), col(7), col(8), col(9), tab, tab, tab, tab, full(o64), full(o32)]
        + [full(g) for g in gains],
        out_specs=[out] * 8,
        out_shape=[jax.ShapeDtypeStruct((n, BRANCH_W), BF16)] * 8,
        compiler_params=_cp(("arbitrary",)),
        name="qkv_prep",
    )(u_small, u_small, u_small, u_small, u_small, *tables, o64, o32, *gains)


def _lane_pick(lane_lo, a, b):
    return jnp.where(lane_lo, a, b)


def _flash_kernel(*refs, segs, tk, tq, diff, lam_init):
    q_ref, o_ref = refs[0], refs[-1]
    seg_refs = [refs[1 + 3 * i:4 + 3 * i] for i in range(len(segs))]
    aux_ref = refs[1 + 3 * len(segs)]
    lane = lax.broadcasted_iota(jnp.int32, (1, LANES), 1)
    lane_lo = lane < HEAD_DIM
    if diff:
        masks = [(lane >= g * DIFF_D) & (lane < (g + 1) * DIFF_D) for g in range(4)]
        acc_of = [0, 1, 0, 1]
        use_a = [True, True, False, False]
        n_acc = 2
    else:
        masks = [lane_lo, ~lane_lo]
        acc_of = [0, 0]
        use_a = [True, False]
        n_acc = 1
    R = len(masks)
    pairs = [slice(p * LANES, (p + 1) * LANES) for p in range(2)]
    qsts = [jnp.concatenate([jnp.where(m, q_ref[:, ps], jnp.zeros((tq, LANES), BF16)) for m in masks], axis=0)
            for ps in pairs]
    lo_hi = []
    for a in range(n_acc):
        rs = [r for r in range(R) if acc_of[r] == a]
        lo_hi.append(([r for r in rs if use_a[r]][0], [r for r in rs if not use_a[r]][0]))

    def rows(x, r):
        return x[r * tq:(r + 1) * tq]

    shift = jnp.max(aux_ref[5:6, :])

    def run(fixed):
        def chunk(kv_refs, t0, size, carry):
            k_ref, va_ref, vb_ref = kv_refs
            new = []
            for p, ps in enumerate(pairs):
                m_run, l_run, accs = carry[p]
                k = k_ref[pl.ds(t0, size), ps]
                s = lax.dot_general(qsts[p], k, (((1,), (1,)), ((), ())), preferred_element_type=F32)
                if fixed:
                    m_new = m_run
                    pr = jnp.exp2(s - shift)
                    l_new = l_run + jnp.sum(pr, axis=-1, keepdims=True)
                else:
                    m_new = jnp.maximum(m_run, jnp.max(s, axis=-1, keepdims=True))
                    alpha = jnp.exp2(m_run - m_new)
                    pr = jnp.exp2(s - m_new)
                    l_new = alpha * l_run + jnp.sum(pr, axis=-1, keepdims=True)
                prb = pr.astype(BF16)
                va = va_ref[pl.ds(t0, size), ps]
                vb = vb_ref[pl.ds(t0, size), ps]
                new_accs = []
                for a, (r_lo, r_hi) in enumerate(lo_hi):
                    upd = _dot(rows(prb, r_lo), va) + _dot(rows(prb, r_hi), vb)
                    if fixed:
                        new_accs.append(accs[a] + upd)
                    else:
                        al = _lane_pick(lane_lo, rows(alpha, r_lo), rows(alpha, r_hi))
                        new_accs.append(accs[a] * al + upd)
                new.append((m_new, l_new, tuple(new_accs)))
            return tuple(new)

        one = (jnp.full((R * tq, 1), -jnp.inf, F32), jnp.zeros((R * tq, 1), F32),
               tuple(jnp.zeros((tq, LANES), F32) for _ in range(n_acc)))
        carry = (one, one)
        for kv_refs, T in zip(seg_refs, segs):
            n_main = T // tk
            if n_main:
                carry = lax.fori_loop(
                    0, n_main, lambda c, cr, kv_refs=kv_refs: chunk(kv_refs, pl.multiple_of(c * tk, tk), tk, cr),
                    carry, unroll=4 if (fixed and not diff and n_main % 4 == 0) else 2 if n_main % 2 == 0 else 1)
            if T - n_main * tk:
                carry = chunk(kv_refs, n_main * tk, T - n_main * tk, carry)

        outs = []
        for p in range(2):
            _, l_fin, accs = carry[p]
            inv_l = 1.0 / l_fin
            norm = [accs[a] * _lane_pick(lane_lo, rows(inv_l, r_lo), rows(inv_l, r_hi))
                    for a, (r_lo, r_hi) in enumerate(lo_hi)]
            if diff:
                aux = aux_ref[...]
                lam = (jnp.exp(jnp.sum(aux[0:1] * aux[1:2], axis=-1, keepdims=True))
                       - jnp.exp(jnp.sum(aux[2:3] * aux[3:4], axis=-1, keepdims=True)) + lam_init)
                o = norm[0] - lam * norm[1]
                sq = o * o
                s_lo = jnp.sum(jnp.where(lane_lo, sq, 0.0), axis=-1, keepdims=True)
                s_hi = jnp.sum(jnp.where(lane_lo, 0.0, sq), axis=-1, keepdims=True)
                ms = _lane_pick(lane_lo, s_lo, s_hi) * (1.0 / HEAD_DIM)
                o = o * lax.rsqrt(ms + EPS) * aux[4:5] * (1.0 - lam_init)
            else:
                o = norm[0]
            outs.append(o)
        o_ref[...] = jnp.concatenate(outs, axis=1).astype(o_ref.dtype)

    @pl.when(shift < FIXED_SHIFT_LIMIT)
    def _():
        run(True)

    @pl.when(jnp.logical_not(shift < FIXED_SHIFT_LIMIT))
    def _():
        run(False)


def _into(prev, args, in_specs):
    if prev is None:
        return args, in_specs, {}
    return args + [prev], in_specs + [pl.BlockSpec(memory_space=pl.ANY)], {len(args): 0}


def _flash(q, k, va, vb, aux, nb, Lq, q_base, segs, diff, lam_init, out_rows, out_base, prev=None):
    tq = min(512, Lq)
    tk = 1024
    nt = Lq // tq
    q_base, out_base = q_base // tq, out_base // tq
    args = [q]
    in_specs = [pl.BlockSpec((tq, BRANCH_W), lambda b, i: (q_base + b * nt + i, 0))]
    for length, base in segs:
        spec = pl.BlockSpec((length, BRANCH_W), lambda b, i, base=base: (base + b, 0))
        args += [k, va, vb]
        in_specs += [spec, spec, spec]
    args.append(aux)
    in_specs.append(pl.BlockSpec((8, LANES), lambda b, i: (0, 0)))
    args, in_specs, alias = _into(prev, args, in_specs)
    return pl.pallas_call(
        functools.partial(_flash_kernel, segs=tuple(s[0] for s in segs), tk=tk, tq=tq, diff=diff,
                          lam_init=lam_init),
        grid=(nb, nt),
        in_specs=in_specs,
        out_specs=pl.BlockSpec((tq, BRANCH_W), lambda b, i: (out_base + b * nt + i, 0)),
        out_shape=jax.ShapeDtypeStruct((out_rows, BRANCH_W), BF16),
        input_output_aliases=alias,
        compiler_params=_cp(("arbitrary", "arbitrary"), VMEM_LIMIT_BIG),
        name="flash_diff" if diff else "flash_gqa",
    )(*args)


def _merge_kernel(y0, y1, y2, y3, g_ref, wb_ref, wo_ref, x_ref, mod_ref, o_ref):
    d = D_MODEL
    acc = g_ref[:, 0:d].astype(F32) * _dot(y0[...], wb_ref[0])
    acc = acc + g_ref[:, d:2 * d].astype(F32) * _dot(y1[...], wb_ref[1])
    acc = acc + g_ref[:, 2 * d:3 * d].astype(F32) * _dot(y2[...], wb_ref[2])
    acc = acc + g_ref[:, 3 * d:4 * d].astype(F32) * _dot(y3[...], wb_ref[3])
    mix = _dot(acc.astype(BF16), wo_ref[...])
    o_ref[...] = x_ref[...] + mod_ref[0][2:3, :] * mix


def _merge(ys, gate, wb, wo, x, mods_l, dims):
    n, d = ys[0].shape[0], x.shape[1]
    tm = dims["tm_small"]
    mrow = dims["mod_row"](tm)
    yspec = pl.BlockSpec((tm, BRANCH_W), lambda i: (i, 0))
    return pl.pallas_call(
        _merge_kernel,
        grid=(n // tm,),
        in_specs=[yspec] * 4 + [pl.BlockSpec((tm, GATE_COLS), lambda i: (i, 0)),
                                pl.BlockSpec(wb.shape, lambda i: (0, 0, 0)),
                                pl.BlockSpec(wo.shape, lambda i: (0, 0)),
                                pl.BlockSpec((tm, d), lambda i: (i, 0)),
                                pl.BlockSpec((1, 6, d), lambda i: (mrow(i), 0, 0))],
        out_specs=pl.BlockSpec((tm, d), lambda i: (i, 0)),
        out_shape=jax.ShapeDtypeStruct((n, d), F32),
        compiler_params=_cp(("arbitrary",)),
        name="merge",
    )(*ys, gate, wb, wo, x, mods_l)


def _ffn_kernel(x_ref, mod_ref, g_ref, w1_ref, w3_ref, w2_ref, o_ref, h_scr, acc_scr):
    j = pl.program_id(1)

    @pl.when(j == 0)
    def _():
        m = mod_ref[0]
        h_scr[...] = _norm_mod(x_ref[...], g_ref[...], m[3:4, :], m[4:5, :]).astype(BF16)
        acc_scr[...] = jnp.zeros_like(acc_scr)

    h = h_scr[...]
    a = _dot(h, w1_ref[...])
    b = _dot(h, w3_ref[...])
    t = (a * _sigmoid(a) * b).astype(BF16)
    acc_scr[...] += _dot(t, w2_ref[...])

    @pl.when(j == pl.num_programs(1) - 1)
    def _():
        o_ref[...] = x_ref[...] + mod_ref[0][5:6, :] * acc_scr[...]


def _ffn(x, mods_l, gain, w1, w3, w2, dims):
    n, d = x.shape
    tm, tf = dims["tm_big"], 256
    mrow = dims["mod_row"](tm)
    return pl.pallas_call(
        _ffn_kernel,
        grid=(n // tm, w1.shape[1] // tf),
        in_specs=[pl.BlockSpec((tm, d), lambda i, j: (i, 0)),
                  pl.BlockSpec((1, 6, d), lambda i, j: (mrow(i), 0, 0)),
                  pl.BlockSpec((1, d), lambda i, j: (0, 0)),
                  pl.BlockSpec((d, tf), lambda i, j: (0, j)),
                  pl.BlockSpec((d, tf), lambda i, j: (0, j)),
                  pl.BlockSpec((tf, d), lambda i, j: (j, 0))],
        out_specs=pl.BlockSpec((tm, d), lambda i, j: (i, 0)),
        out_shape=jax.ShapeDtypeStruct((n, d), F32),
        scratch_shapes=[pltpu.VMEM((tm, d), BF16), pltpu.VMEM((tm, d), F32)],
        compiler_params=_cp(("arbitrary", "arbitrary")),
        name="ffn_dense",
    )(x, mods_l, gain, w1, w3, w2)


def _route_kernel(x_ref, mod_ref, g_ref, wr_ref, tok_ref, gate_ref):
    m = mod_ref[0]
    h = _norm_mod(x_ref[...], g_ref[...], m[3:4, :], m[4:5, :])
    tok_ref[...] = h.astype(BF16)
    logits = _dot3(h, wr_ref[...])
    lane = lax.broadcasted_iota(jnp.int32, logits.shape, 1)
    lg = jnp.where(lane < N_EXPERTS, logits, -jnp.inf)
    m1 = jnp.max(lg, axis=-1, keepdims=True)
    i1 = jnp.min(jnp.where(lg == m1, lane, LANES), axis=-1, keepdims=True)
    lg2 = jnp.where(lane == i1, -jnp.inf, lg)
    m2 = jnp.max(lg2, axis=-1, keepdims=True)
    i2 = jnp.min(jnp.where(lg2 == m2, lane, LANES), axis=-1, keepdims=True)
    e2 = jnp.exp(m2 - m1)
    g1 = 1.0 / (1.0 + e2)
    gate_ref[...] = jnp.where(lane == i1, g1, jnp.where(lane == i2, e2 * g1, 0.0))


def _route(x, mods_l, gain, w_router, dims):
    n, d = x.shape
    tm = dims["tm_small"]
    mrow = dims["mod_row"](tm)
    wr = jnp.pad(w_router, ((0, 0), (0, LANES - N_EXPERTS)))
    return pl.pallas_call(
        _route_kernel,
        grid=(n // tm,),
        in_specs=[pl.BlockSpec((tm, d), lambda i: (i, 0)),
                  pl.BlockSpec((1, 6, d), lambda i: (mrow(i), 0, 0)),
                  pl.BlockSpec((1, d), lambda i: (0, 0)),
                  pl.BlockSpec((d, LANES), lambda i: (0, 0))],
        out_specs=[pl.BlockSpec((tm, d), lambda i: (i, 0)), pl.BlockSpec((tm, LANES), lambda i: (i, 0))],
        out_shape=[jax.ShapeDtypeStruct((n, d), BF16), jax.ShapeDtypeStruct((n, LANES), F32)],
        compiler_params=_cp(("arbitrary",)),
        name="moe_route",
    )(x, mods_l, gain, wr)


MOE_SUB = 128
MOE_GATHER = 256
MOE_GRAN = 64


def _moe_kernel(tok_ref, gate_ref, tri_ref, w1_ref, w3_ref, w2_ref, x_ref, mod_ref, o_ref,
                rank_scr, rank_t_scr, mask_t_scr, xe_scr, ye_scr, cnt_smem, *, tm):
    e = pl.program_id(1)
    j = pl.program_id(2)
    ne = pl.num_programs(1)
    nj = pl.num_programs(2)
    lane = lax.broadcasted_iota(jnp.int32, (1, LANES), 1)

    @pl.when((e == 0) & (j == 0))
    def _():
        tri = tri_ref[...]
        carry = jnp.zeros((1, LANES), F32)
        def put_counts(blk, counts):
            for ee in range(N_EXPERTS):
                cnt_smem[blk * N_EXPERTS + ee] = jnp.sum(jnp.where(lane == ee, counts, 0.0)).astype(jnp.int32)

        for blk in range(tm // 256):
            put_counts(blk, carry)
            rs = slice(blk * 256, (blk + 1) * 256)
            msk = (gate_ref[rs, :] > 0.0).astype(BF16)
            rank_scr[rs, :] = _dot(tri, msk) + carry
            carry = carry + jnp.sum(msk.astype(F32), axis=0, keepdims=True)
        rank_t_scr[...] = jnp.transpose(rank_scr[...])
        mask_t_scr[...] = jnp.transpose((gate_ref[...] > 0.0).astype(F32))
        put_counts(tm // 256, carry)
        o_ref[...] = jnp.zeros_like(o_ref)
        ye_scr[...] = jnp.zeros_like(ye_scr)

    cnt = cnt_smem[(tm // 256) * N_EXPERTS + e]

    n_sub = (cnt + (MOE_GATHER - 1)) // MOE_GATHER

    @pl.when(j == 0)
    def _():
        rk = rank_t_scr[pl.ds(e, 1), :]
        mk = mask_t_scr[pl.ds(e, 1), :]

        def gather(sb, carry):
            r0 = pl.multiple_of(sb * MOE_GATHER, MOE_GATHER)
            r_iota = lax.broadcasted_iota(jnp.int32, (MOE_GATHER, tm), 0) + r0
            sel = jnp.where((rk == r_iota.astype(F32)) & (mk > 0.0), 1.0, 0.0).astype(BF16)
            xe_scr[pl.ds(r0, MOE_GATHER), :] = _dot(sel, tok_ref[...]).astype(BF16)
            return carry

        lax.fori_loop(0, n_sub, gather, 0)

    def expert_rows(r0, size):
        xe = xe_scr[pl.ds(r0, size), :]
        a = _dot(xe, w1_ref[0])
        b = _dot(xe, w3_ref[0])
        t = (a * _sigmoid(a) * b).astype(BF16)
        y = _dot(t, w2_ref[0])
        ye_scr[pl.ds(r0, size), :] = jnp.where(j == 0, y, ye_scr[pl.ds(r0, size), :] + y)

    n_gran = (cnt + (MOE_GRAN - 1)) // MOE_GRAN
    n_big = n_gran // 4
    rem = n_gran % 4

    def big(i, carry):
        expert_rows(pl.multiple_of(i * 256, 256), 256)
        return carry

    lax.fori_loop(0, n_big - 1, big, 0)
    for r in range(min(4, (tm - 256) // MOE_GRAN + 1)):
        @pl.when((n_big >= 1) & (rem == r))
        def _():
            expert_rows(pl.multiple_of((n_big - 1) * 256, 256), 256 + r * MOE_GRAN)

    @pl.when((n_big == 0) & (rem >= 2))
    def _():
        expert_rows(0, 128)

    @pl.when((n_big == 0) & (rem % 2 == 1))
    def _():
        expert_rows(pl.multiple_of((rem // 2) * 128, 128), 64)

    @pl.when(j == nj - 1)
    def _():
        def scatter(rc, carry):
            r0 = pl.multiple_of(rc * 256, 256)
            gt = gate_ref[pl.ds(r0, 256), :]
            g_e = jnp.sum(jnp.where(lane == e, gt, 0.0), axis=-1, keepdims=True)
            r_e = jnp.sum(jnp.where(lane == e, rank_scr[pl.ds(r0, 256), :], 0.0), axis=-1, keepdims=True)
            lo = cnt_smem[rc * N_EXPERTS + e]
            hi = cnt_smem[(rc + 1) * N_EXPERTS + e]

            def scatter_sub(sb, c2):
                c0 = pl.multiple_of(sb * MOE_SUB, MOE_SUB)
                c_iota = lax.broadcasted_iota(jnp.int32, (256, MOE_SUB), 1) + c0
                sel_t = jnp.where((r_e == c_iota.astype(F32)) & (g_e > 0.0), 1.0, 0.0).astype(BF16)
                ye = ye_scr[pl.ds(c0, MOE_SUB), :].astype(BF16)
                o_ref[pl.ds(r0, 256), :] += g_e * _dot(sel_t, ye)
                return c2

            lax.fori_loop(lo // MOE_SUB, (hi + (MOE_SUB - 1)) // MOE_SUB, scatter_sub, 0)

            @pl.when(e == ne - 1)
            def _():
                o_ref[pl.ds(r0, 256), :] = (x_ref[pl.ds(r0, 256), :]
                                            + mod_ref[0][5:6, :] * o_ref[pl.ds(r0, 256), :])
            return carry

        lax.fori_loop(0, tm // 256, scatter, 0)


def _moe(tok, gates, w1, w3, w2, x, mods_l, dims):
    n, d = tok.shape
    tm = dims["tm_moe"] if n % dims["tm_moe"] == 0 else dims["tm_big"]
    mrow = dims["mod_row"](tm)
    ne, _, ff = w1.shape
    tf = 896
    tri =jnp.asarray(np.tril(np.ones((256, 256)), -1), BF16)
    once = pl.Buffered(1)
    return pl.pallas_call(
        functools.partial(_moe_kernel, tm=tm),
        grid=(n // tm, ne, ff // tf),
        in_specs=[pl.BlockSpec((tm, d), lambda i, e, j: (i, 0), pipeline_mode=once),
                  pl.BlockSpec((tm, LANES), lambda i, e, j: (i, 0), pipeline_mode=once),
                  pl.BlockSpec((256, 256), lambda i, e, j: (0, 0), pipeline_mode=once),
                  pl.BlockSpec((1, d, tf), lambda i, e, j: (e, 0, j)),
                  pl.BlockSpec((1, d, tf), lambda i, e, j: (e, 0, j)),
                  pl.BlockSpec((1, tf, d), lambda i, e, j: (e, j, 0)),
                  pl.BlockSpec((tm, d), lambda i, e, j: (i, 0), pipeline_mode=once),
                  pl.BlockSpec((1, 6, d), lambda i, e, j: (mrow(i), 0, 0))],
        out_specs=pl.BlockSpec((tm, d), lambda i, e, j: (i, 0), pipeline_mode=once),
        out_shape=jax.ShapeDtypeStruct((n, d), F32),
        scratch_shapes=[pltpu.VMEM((tm, LANES), F32), pltpu.VMEM((LANES, tm), F32), pltpu.VMEM((LANES, tm), F32),
                        pltpu.VMEM((tm, d), BF16), pltpu.VMEM((tm, d), F32), pltpu.SMEM(((tm // 256 + 1) * N_EXPERTS,), jnp.int32)],
        compiler_params=_cp(("arbitrary", "arbitrary", "arbitrary"), VMEM_LIMIT_BIG),
        name="moe_experts",
    )(tok, gates, tri, w1, w3, w2, x, mods_l)


def _make_dims(B, S, n_ctx):
    n_lat, n = B * S, B * (S + n_ctx)

    def pick(prefs):
        for t in prefs:
            if S % t == 0 and n_lat % t == 0 and (n - n_lat) % t == 0:
                return t
        raise ValueError("no row tile divides the latent and context token counts")

    def mod_row(tm):
        return lambda i: jnp.minimum((i * tm) // S, B)

    return {"B": B, "S": S, "n_ctx": n_ctx, "n_lat": n_lat, "n": n,
            "tm_big": pick((1024, 512, 256)), "tm_small": pick((512, 256)), "tm_moe": pick((2048, 1024, 512, 256)),
            "mod_row": mod_row}


def kernel(x, c, ctx, c_ctx, w_ada, b_ada, norm_mix, norm_ffn, w_in, hy_short_w, hy_short_b, hy_w1, hy_b1, hy_w2, hy_b2, hy_w3, hy_b3, hy_w4, hy_freq, hy_bias, cf_dw_w, cf_dw_b, cf_ln_g, cf_ln_b, gqa_qn, gqa_kn, diff_qn, diff_kn, diff_lq1, diff_lk1, diff_lq2, diff_lk2, diff_subln, w_branch, w_out, ffn_w1, ffn_w3, ffn_w2, moe_router, moe_w1, moe_w3, moe_w2):
    B, S, D = x.shape
    n_ctx = ctx.shape[1]
    depth = w_ada.shape[0]
    dims = _make_dims(B, S, n_ctx)
    n_lat = dims["n_lat"]
    tm_s = dims["tm_small"]

    c16 = jnp.concatenate([c, c_ctx[None, :], jnp.zeros((16 - B - 1, D), F32)], axis=0)
    mods = _ada_mods(c16, w_ada, b_ada).reshape(depth, 16, 6, D)

    xa = jnp.concatenate([x.reshape(n_lat, D), ctx.reshape(B * n_ctx, D)], axis=0)
    tables = _rope_tables(S, tm_s, HEAD_DIM) + _rope_tables(S, tm_s, DIFF_D)
    fft_lat = _fft_consts(S)
    mats_ctx = _dft_mats(n_ctx)

    for l in range(depth):
        lam_init = 0.8 - 0.6 * math.exp(-0.3 * l)
        u_small, gate = _in_proj(xa, mods[l], norm_mix[l].reshape(1, D), w_in[l].astype(BF16), dims)

        hy_p = (hy_short_w[l], hy_short_b[l], hy_w1[l], hy_b1[l], hy_w2[l], hy_b2[l], hy_w3[l], hy_b3[l],
                hy_w4[l], hy_freq[l], hy_bias[l])
        cf_p = (cf_dw_w[l], cf_dw_b[l], cf_ln_g[l], cf_ln_b[l])
        last = l == depth - 1
        n_out = n_lat if last else dims["n"]
        ctx_base = n_lat // n_ctx
        y_hy = _hyena_fft(u_small, S, B, hy_p, fft_lat, n_out)
        y_cf = _conformer(u_small, S, B, 0, cf_p, n_out)
        if not last:
            y_hy = _hyena(u_small, n_ctx, B, ctx_base, hy_p, mats_ctx, n_out, y_hy)
            y_cf = _conformer(u_small, n_ctx, B, ctx_base, cf_p, n_out, y_cf)

        qg, kg, vag, vbg, qd, kd, vad, vbd = _qkv_prep(u_small, gqa_qn[l], gqa_kn[l], diff_qn[l], diff_kn[l],
                                                      tables, dims)

        aux = jnp.zeros((8, LANES), F32)
        aux = aux.at[0, :DIFF_D].set(diff_lq1[l]).at[1, :DIFF_D].set(diff_lk1[l])
        aux = aux.at[2, :DIFF_D].set(diff_lq2[l]).at[3, :DIFF_D].set(diff_lk2[l])
        aux = aux.at[4, :].set(jnp.tile(diff_subln[l], 2))

        def score_bound(qn, kn, d):
            return 1.02 * d * jnp.max(jnp.abs(qn)) * jnp.max(jnp.abs(kn)) * (LOG2E * d ** -0.5) + 0.1

        bounds = (score_bound(gqa_qn[l], gqa_kn[l], HEAD_DIM), score_bound(diff_qn[l], diff_kn[l], DIFF_D))
        y_att = []
        for (q, k, va, vb, is_diff) in ((qg, kg, vag, vbg, False), (qd, kd, vad, vbd, True)):
            aux = aux.at[5, :].set(bounds[int(is_diff)])
            y = _flash(q, k, va, vb, aux, B, S, 0, ((S, 0), (n_ctx, ctx_base)), is_diff, lam_init, n_out, 0)
            if not last:
                y = _flash(q, k, va, vb, aux, B, n_ctx, n_lat, ((n_ctx, ctx_base),), is_diff, lam_init,
                           n_out, n_lat, y)
            y_att.append(y)

        xa = _merge((y_hy, y_cf, y_att[0], y_att[1]), gate, w_branch[l].astype(BF16), w_out[l].astype(BF16),
                    xa, mods[l], dims)

        i = l // 2
        if l % 2 == 0:
            xa = _ffn(xa, mods[l], norm_ffn[l].reshape(1, D), ffn_w1[i].astype(BF16), ffn_w3[i].astype(BF16),
                      ffn_w2[i].astype(BF16), dims)
        else:
            tok, gates = _route(xa, mods[l], norm_ffn[l].reshape(1, D), moe_router[i], dims)
            xa = _moe(tok, gates, moe_w1[i].astype(BF16), moe_w3[i].astype(BF16), moe_w2[i].astype(BF16),
                      xa, mods[l], dims)
    return xa[:n_lat].reshape(B, S, D)
```

```python
import functools
import math

import numpy as np
import jax
import jax.numpy as jnp
from jax import lax
from jax.experimental import pallas as pl
from jax.experimental.pallas import tpu as pltpu

F32 = jnp.float32
BF16 = jnp.bfloat16

D_MODEL = 1024
GRID_W = 64
BRANCH_W = 256
EPS = 1e-6
HY_EMB = 33
HY_EMB_PAD = 128
HY_FFN = 64
HY_TARGET = 1e-2
HY_FAST = 0.3
HY_SLOW = 1.5
CF_WIDTH = 31
HEAD_DIM = 64
DIFF_D = 32
ROPE_BASE = 10000.0
N_EXPERTS = 8
LANES = 128
LOG2E = 1.4426950408889634
FIXED_SHIFT_LIMIT = 60.0
SMALL_COLS = 2560
GATE_COLS = 4 * D_MODEL
VMEM_LIMIT = 48 * 1024 * 1024
VMEM_LIMIT_BIG = 56 * 1024 * 1024


def _cp(sem, vmem=VMEM_LIMIT):
    return pltpu.CompilerParams(dimension_semantics=sem, vmem_limit_bytes=vmem)


def _sigmoid(x):
    return 1.0 / (1.0 + jnp.exp(-x))


def _dot(a, b):
    return jnp.dot(a, b, preferred_element_type=F32)


def _split(a):
    hi = a.astype(BF16)
    lo = (a - hi.astype(F32)).astype(BF16)
    return hi, lo


def _dot3(a, b):
    ah, al = _split(a)
    bh, bl = _split(b)
    return _dot(ah, bh) + (_dot(al, bh) + _dot(ah, bl))


def _norm_mod(x, gain, shift, scale):
    ms = jnp.mean(x * x, axis=-1, keepdims=True)
    return (x * lax.rsqrt(ms + EPS) * gain) * (1.0 + scale) + shift


def _ada_kernel(c_ref, w_ref, b_ref, o_ref):
    c = c_ref[...]
    s = c * _sigmoid(c)
    o_ref[0] = _dot3(s, w_ref[0]) + b_ref[0]


def _ada_mods(c16, w_ada, b_ada):
    depth, d, n6 = w_ada.shape
    tn = 512
    return pl.pallas_call(
        _ada_kernel,
        grid=(depth, n6 // tn),
        in_specs=[pl.BlockSpec((16, d), lambda l, j: (0, 0)),
                  pl.BlockSpec((1, d, tn), lambda l, j: (l, 0, j)),
                  pl.BlockSpec((1, 1, tn), lambda l, j: (l, 0, j))],
        out_specs=pl.BlockSpec((1, 16, tn), lambda l, j: (l, 0, j)),
        out_shape=jax.ShapeDtypeStruct((depth, 16, n6), F32),
        compiler_params=_cp(("arbitrary", "arbitrary")),
        name="ada_mods",
    )(c16, w_ada, b_ada.reshape(depth, 1, n6))


def _in_proj_kernel(x_ref, mod_ref, g_ref, w_ref, o_ref, h_scr, *, gate):
    @pl.when(pl.program_id(1) == 0)
    def _():
        m = mod_ref[0]
        h_scr[...] = _norm_mod(x_ref[...], g_ref[...], m[0:1, :], m[1:2, :]).astype(BF16)

    r = _dot(h_scr[...], w_ref[...])
    o_ref[...] = _sigmoid(r).astype(o_ref.dtype) if gate else r


def _in_proj(x, mods_l, gain, w_bf, dims):
    n, d = x.shape
    tm = dims["tm_big"]
    mrow = dims["mod_row"](tm)

    def call(w, tn, gate, dtype):
        n_cols = w.shape[1]
        return pl.pallas_call(
            functools.partial(_in_proj_kernel, gate=gate),
            grid=(n // tm, n_cols // tn),
            in_specs=[pl.BlockSpec((tm, d), lambda i, j: (i, 0)),
                      pl.BlockSpec((1, 6, d), lambda i, j: (mrow(i), 0, 0)),
                      pl.BlockSpec((1, d), lambda i, j: (0, 0)),
                      pl.BlockSpec((d, tn), lambda i, j: (0, j))],
            out_specs=pl.BlockSpec((tm, tn), lambda i, j: (i, j)),
            out_shape=jax.ShapeDtypeStruct((n, n_cols), dtype),
            scratch_shapes=[pltpu.VMEM((tm, d), BF16)],
            compiler_params=_cp(("arbitrary", "arbitrary")),
            name="in_proj_gate" if gate else "in_proj_mix",
        )(x, mods_l, gain, w)

    return (call(w_bf[:, :SMALL_COLS], SMALL_COLS // 2, False, F32),
            call(w_bf[:, SMALL_COLS:], GATE_COLS // 2, True, BF16))


def _short_conv_kernel(u_ref, w_ref, b_ref, o_ref, pad_scr, *, L, tc, planes):
    if planes:
        o_ref = o_ref.at[0]
    pad_scr[0:8, :] = jnp.zeros((8, BRANCH_W), F32)
    pad_scr[8 + L:16 + L, :] = jnp.zeros((8, BRANCH_W), F32)
    pad_scr[8:8 + L, :] = u_ref[...]
    w = w_ref[0]
    b = b_ref[0]

    def body(c, carry):
        t0 = pl.multiple_of(c * tc, tc)
        win = pad_scr[pl.ds(t0, tc + 16), :]
        acc = b + w[0:1, :] * win[7:7 + tc]
        acc = acc + w[1:2, :] * win[8:8 + tc]
        acc = acc + w[2:3, :] * win[9:9 + tc]
        o_ref[pl.ds(t0, tc), :] = acc
        return carry

    lax.fori_loop(0, L // tc, body, 0)


def _short_conv(u_small, w, b, L, nseq, row_base, planes=False):
    w3 = jnp.transpose(w.reshape(3, 3, BRANCH_W), (1, 0, 2))
    w3 = jnp.pad(w3, ((0, 0), (0, 5), (0, 0)))
    b3 = b.reshape(3, 1, BRANCH_W)
    tc = min(L, 256)
    return pl.pallas_call(
        functools.partial(_short_conv_kernel, L=L, tc=tc, planes=planes),
        grid=(nseq, 3),
        in_specs=[pl.BlockSpec((L, BRANCH_W), lambda s, j: (row_base + s, j)),
                  pl.BlockSpec((1, 8, BRANCH_W), lambda s, j: (j, 0, 0)),
                  pl.BlockSpec((1, 1, BRANCH_W), lambda s, j: (j, 0, 0))],
        out_specs=(pl.BlockSpec((1, L, BRANCH_W), lambda s, j: (j, s, 0)) if planes
                   else pl.BlockSpec((L, BRANCH_W), lambda s, j: (s, j))),
        out_shape=jax.ShapeDtypeStruct((3, nseq * L, BRANCH_W) if planes else (nseq * L, 3 * BRANCH_W), F32),
        scratch_shapes=[pltpu.VMEM((L + 16, BRANCH_W), F32)],
        compiler_params=_cp(("arbitrary", "arbitrary")),
        name="hy_short_conv",
    )(u_small, w3, b3)


def _filter_consts(L):
    t = np.linspace(0.0, 1.0, L)[:, None]
    bands = (HY_EMB - 1) // 2
    fr = np.linspace(1e-4, bands - 1, bands)[None, :]
    wpos = 2.0 * math.pi * np.arange(L)[:, None] / L
    z = np.concatenate([t, np.cos(fr * wpos), -np.sin(fr * wpos)], axis=-1)
    z = np.pad(z, ((0, 0), (0, HY_EMB_PAD - HY_EMB)))
    deltas = np.abs(np.linspace(math.log(HY_TARGET) / HY_SLOW, math.log(HY_TARGET) / HY_FAST, BRANCH_W))
    win = np.exp(-t * deltas[None, :])
    return jnp.asarray(z, F32), jnp.asarray(win, F32)


def _filter_kernel(z_ref, w1, b1, w2, b2, w3, b3, w4, fq, win_ref, hf_ref, ss_ref, *, tr, planes):
    i = pl.program_id(0)
    f = fq[...]
    a = jnp.sin(f * (_dot3(z_ref[...], w1[...]) + b1[...]))
    a = jnp.sin(f * (_dot3(a, w2[...]) + b2[...]))
    a = jnp.sin(f * (_dot3(a, w3[...]) + b3[...]))
    h = _dot3(a, w4[...])
    win = win_ref[...]
    h = h * jnp.concatenate([win, win, win, win], axis=1)
    row = lax.broadcasted_iota(jnp.int32, h.shape, 0) + i * tr
    col = lax.broadcasted_iota(jnp.int32, h.shape, 1)
    h = jnp.where((row == 0) & (col >= 2 * BRANCH_W), 0.0, h)
    if planes:
        for d in range(4):
            hf_ref[d] = h[:, d * BRANCH_W:(d + 1) * BRANCH_W]
    else:
        hf_ref[...] = h

    @pl.when(i == 0)
    def _():
        ss_ref[...] = jnp.zeros_like(ss_ref)

    ss_ref[...] += jnp.sum(h * h, axis=0, keepdims=True)


def _hyena_filter(L, w1, b1, w2, b2, w3, b3, w4, freq, planes=False):
    z, win = _filter_consts(L)
    tr = min(L, 256)
    w1p = jnp.pad(w1, ((0, HY_EMB_PAD - HY_EMB), (0, 0)))
    full = lambda a: pl.BlockSpec(a.shape, lambda i: (0,) * a.ndim)
    args = [w1p, b1.reshape(1, -1), w2, b2.reshape(1, -1), w3, b3.reshape(1, -1), w4, freq.reshape(1, -1)]
    n_out = w4.shape[1]
    return pl.pallas_call(
        functools.partial(_filter_kernel, tr=tr, planes=planes),
        grid=(L // tr,),
        in_specs=[pl.BlockSpec((tr, HY_EMB_PAD), lambda i: (i, 0))] + [full(a) for a in args]
        + [pl.BlockSpec((tr, BRANCH_W), lambda i: (i, 0))],
        out_specs=[pl.BlockSpec((4, tr, BRANCH_W), lambda i: (0, i, 0)) if planes
                   else pl.BlockSpec((tr, n_out), lambda i: (i, 0)),
                   pl.BlockSpec((1, n_out), lambda i: (0, 0))],
        out_shape=[jax.ShapeDtypeStruct((4, L, BRANCH_W) if planes else (L, n_out), F32),
                   jax.ShapeDtypeStruct((1, n_out), F32)],
        compiler_params=_cp(("arbitrary",)),
        name="hy_filter",
    )(z, *args, win)


def _dft_mats(L):
    N = 2 * L
    blk = 64
    k = jnp.arange(L, dtype=jnp.int32)[:, None]
    nh = jnp.arange(L // blk, dtype=jnp.int32)[None, :]
    nl = jnp.arange(blk, dtype=jnp.int32)[None, :]
    w = 2.0 * math.pi / N
    a = ((k * (blk * nh)) % N).astype(F32) * w
    b = ((k * nl) % N).astype(F32) * w
    ca, sa, cb, sb = jnp.cos(a), jnp.sin(a), jnp.cos(b), jnp.sin(b)
    cos = (ca[:, :, None] * cb[:, None, :] - sa[:, :, None] * sb[:, None, :]).reshape(L, L)
    sin = (sa[:, :, None] * cb[:, None, :] + ca[:, :, None] * sb[:, None, :]).reshape(L, L)
    alt = jnp.where(jnp.arange(L) % 2 == 0, 1.0, -1.0).astype(F32)
    first = (jnp.arange(L) == 0)
    s_f = jnp.where(first[:, None], alt[None, :], -sin)
    fwd = jnp.concatenate([cos, s_f], axis=0).astype(BF16)
    colscale = jnp.where(first, 1.0 / N, 2.0 / N).astype(F32)
    g_c = cos * colscale[None, :]
    g_s = jnp.where(first[None, :], alt[:, None] / N, -sin * (2.0 / N))
    inv = jnp.concatenate([g_c, g_s], axis=1).astype(BF16)
    return fwd, inv


def _dft_fwd_kernel(f_ref, z_ref, o_ref):
    o_ref[0] = _dot(f_ref[...], z_ref[...].astype(BF16))


def _dft_fwd(fwd, z2d, L, nb, zmap):
    tm = min(2 * L, 1024)
    return pl.pallas_call(
        _dft_fwd_kernel,
        grid=(2 * L // tm, nb),
        in_specs=[pl.BlockSpec((tm, L), lambda i, b: (i, 0)),
                  pl.BlockSpec((L, BRANCH_W), lambda i, b: zmap(b))],
        out_specs=pl.BlockSpec((1, tm, BRANCH_W), lambda i, b: (b, i, 0)),
        out_shape=jax.ShapeDtypeStruct((nb, 2 * L, BRANCH_W), F32),
        compiler_params=_cp(("arbitrary", "arbitrary")),
        name="hy_dft_fwd",
    )(fwd, z2d)


def _spec_prod_kernel(z_ref, kf_ref, kb_ref, sf_ref, sb_ref, y_ref, *, tk):
    i = pl.program_id(1)
    s = lax.rsqrt(sf_ref[...] + sb_ref[...] + EPS)
    zr, zi = z_ref[0, 0], z_ref[0, 1]
    fr, fi = kf_ref[0, 0], kf_ref[0, 1]
    br, bi = kb_ref[0, 0], kb_ref[0, 1]
    row = lax.broadcasted_iota(jnp.int32, (tk, BRANCH_W), 0) + i * tk
    first = row == 0
    kr = (fr + br) * s
    ki = jnp.where(first, fi + bi, fi - bi) * s
    yr = jnp.where(first, zr * kr, zr * kr - zi * ki)
    yi = jnp.where(first, zi * ki, zr * ki + zi * kr)
    y_ref[0, 0] = yr.astype(BF16)
    y_ref[0, 1] = yi.astype(BF16)


def _spec_prod(zf, kfs, ss, order, L, nb):
    tk = min(L, 512)
    zf4 = zf.reshape(nb, 2, L, BRANCH_W)
    kf4 = kfs.reshape(4, 2, L, BRANCH_W)
    y = pl.pallas_call(
        functools.partial(_spec_prod_kernel, tk=tk),
        grid=(nb, L // tk),
        in_specs=[pl.BlockSpec((1, 2, tk, BRANCH_W), lambda b, i: (b, 0, i, 0)),
                  pl.BlockSpec((1, 2, tk, BRANCH_W), lambda b, i: (order, 0, i, 0)),
                  pl.BlockSpec((1, 2, tk, BRANCH_W), lambda b, i: (2 + order, 0, i, 0)),
                  pl.BlockSpec((1, BRANCH_W), lambda b, i: (0, order)),
                  pl.BlockSpec((1, BRANCH_W), lambda b, i: (0, 2 + order))],
        out_specs=pl.BlockSpec((1, 2, tk, BRANCH_W), lambda b, i: (b, 0, i, 0)),
        out_shape=jax.ShapeDtypeStruct((nb, 2, L, BRANCH_W), BF16),
        compiler_params=_cp(("arbitrary", "arbitrary")),
        name="hy_spec_prod",
    )(zf4, kf4, kf4, ss, ss)
    return y.reshape(nb, 2 * L, BRANCH_W)


def _dft_inv_kernel(g_ref, y_ref, gate_ref, zp_ref, bias_ref, *rest):
    o_ref = rest[-1]
    conv = _dot(g_ref[...], y_ref[0])
    o_ref[...] = (gate_ref[...] * (conv + bias_ref[...] * zp_ref[...])).astype(o_ref.dtype)


def _dft_inv(inv, y, xs, gate_col, zprev, zprev_col, bias, L, nb, out_dtype, out_rows=None, out_base=0,
             prev=None):
    tm = min(L, 512)
    nt = L // tm
    args = [inv, y, xs, zprev, bias]
    in_specs = [pl.BlockSpec((tm, 2 * L), lambda i, b: (i, 0)),
                pl.BlockSpec((1, 2 * L, BRANCH_W), lambda i, b: (b, 0, 0)),
                pl.BlockSpec((tm, BRANCH_W), lambda i, b: (b * nt + i, gate_col)),
                pl.BlockSpec((tm, BRANCH_W), lambda i, b: (b * nt + i, zprev_col)),
                pl.BlockSpec((1, BRANCH_W), lambda i, b: (0, 0))]
    args, in_specs, alias = _into(prev, args, in_specs)
    return pl.pallas_call(
        _dft_inv_kernel,
        grid=(nt, nb),
        in_specs=in_specs,
        out_specs=pl.BlockSpec((tm, BRANCH_W), lambda i, b: ((out_base + b) * nt + i, 0)),
        out_shape=jax.ShapeDtypeStruct((out_rows or nb * L, BRANCH_W), out_dtype),
        input_output_aliases=alias,
        compiler_params=_cp(("arbitrary", "arbitrary")),
        name="hy_dft_inv",
    )(*args)


def _hyena(u_small, L, nseq, row_base, p, mats, out_rows, prev=None):
    (short_w, short_b, w1, b1, w2, b2, w3, b3, w4, freq, bias) = p
    fwd, inv = mats
    xs = _short_conv(u_small, short_w, short_b, L, nseq, row_base)
    hf, ss = _hyena_filter(L, w1, b1, w2, b2, w3, b3, w4, freq)
    kfs = _dft_fwd(fwd, hf, L, 4, lambda b: (0, b))
    zf = _dft_fwd(fwd, xs, L, nseq, lambda b: (b, 2))
    y = _spec_prod(zf, kfs, ss, 0, L, nseq)
    z1 = _dft_inv(inv, y, xs, 0, xs, 2, bias[0:1], L, nseq, F32)
    zf = _dft_fwd(fwd, z1, L, nseq, lambda b: (b, 0))
    y = _spec_prod(zf, kfs, ss, 1, L, nseq)
    return _dft_inv(inv, y, xs, 1, z1, 0, bias[1:2], L, nseq, BF16, out_rows, row_base, prev)


FFT_N1 = 64
FFT_KB = 16
FFT_COLS = 16384


def _fft_consts(L):
    N = 2 * L
    n2s = N // FFT_N1
    h = FFT_N1 // 2
    k1 = jnp.arange(FFT_N1, dtype=jnp.int32)
    ang_a = ((k1[:, None] * jnp.arange(h, dtype=jnp.int32)[None, :]) % FFT_N1).astype(F32) * (2.0 * math.pi / FFT_N1)
    ca, sa = jnp.cos(ang_a), jnp.sin(ang_a)
    fa = jnp.concatenate([ca, -sa], axis=0).astype(BF16)
    fai = (jnp.concatenate([ca.T, -sa.T], axis=1) / N).astype(BF16)
    k = k1[:, None, None] + FFT_N1 * jnp.arange(n2s, dtype=jnp.int32)[None, :, None]
    n2 = jnp.arange(n2s, dtype=jnp.int32)[None, None, :]
    ang = ((k * n2) % N).astype(F32) * (2.0 * math.pi / N)
    c, s = jnp.cos(ang), jnp.sin(ang)
    g = jnp.concatenate([jnp.concatenate([c, s], axis=2), jnp.concatenate([-s, c], axis=2)], axis=1)
    ct, st = jnp.swapaxes(c, 1, 2), jnp.swapaxes(s, 1, 2)
    gi = jnp.concatenate([jnp.concatenate([ct, -st], axis=2), jnp.concatenate([st, ct], axis=2)], axis=1)
    return fa, fai, g.astype(BF16), gi.astype(BF16)


def _fft_a_kernel(f_ref, z_ref, o_ref):
    o_ref[...] = _dot(f_ref[...], z_ref[0].astype(BF16)).astype(o_ref.dtype)


def _fft_a(fa, z3, zmap, nb, L):
    h = FFT_N1 // 2
    cols = (L // h) * BRANCH_W
    tn = min(cols, FFT_COLS)
    return pl.pallas_call(
        _fft_a_kernel,
        grid=(nb, cols // tn),
        in_specs=[pl.BlockSpec(fa.shape, lambda b, j: (0, 0)),
                  pl.BlockSpec((1, h, tn), lambda b, j: zmap(b) + (j,))],
        out_specs=pl.BlockSpec((2 * FFT_N1, tn), lambda b, j: (b, j)),
        out_shape=jax.ShapeDtypeStruct((nb * 2 * FFT_N1, cols), BF16),
        compiler_params=_cp(("arbitrary", "arbitrary")),
        name="hy_fft_a",
    )(fa, z3)


def _fft_spec_kernel(a_ref, g_ref, o_ref):
    for j in range(FFT_KB):
        s = _dot(g_ref[j], jnp.concatenate([a_ref[0, 0, j], a_ref[0, 1, j]], axis=0))
        half = s.shape[0] // 2
        o_ref[0, j, 0] = s[:half]
        o_ref[0, j, 1] = s[half:]


def _fft_spec(g, a5):
    nb, _, n1s, n2s, w = a5.shape
    return pl.pallas_call(
        _fft_spec_kernel,
        grid=(n1s // FFT_KB, nb),
        in_specs=[pl.BlockSpec((1, 2, FFT_KB, n2s, w), lambda i, b: (b, 0, i, 0, 0)),
                  pl.BlockSpec((FFT_KB, 2 * n2s, 2 * n2s), lambda i, b: (i, 0, 0))],
        out_specs=pl.BlockSpec((1, FFT_KB, 2, n2s, w), lambda i, b: (b, i, 0, 0, 0)),
        out_shape=jax.ShapeDtypeStruct((nb, n1s, 2, n2s, w), F32),
        compiler_params=_cp(("arbitrary", "arbitrary")),
        name="hy_fft_spec",
    )(a5, g)


def _fft_conv_kernel(a_ref, g_ref, gi_ref, sf_ref, sb_ref, ssf_ref, ssb_ref, o_ref):
    scale = lax.rsqrt(ssf_ref[...] + ssb_ref[...] + EPS)
    for j in range(FFT_KB):
        s = _dot(g_ref[j], jnp.concatenate([a_ref[0, 0, j], a_ref[0, 1, j]], axis=0))
        half = s.shape[0] // 2
        sr, si = s[:half], s[half:]
        kr = (sf_ref[0, j, 0] + sb_ref[0, j, 0]) * scale
        ki = (sf_ref[0, j, 1] - sb_ref[0, j, 1]) * scale
        y = jnp.concatenate([sr * kr - si * ki, sr * ki + si * kr], axis=0).astype(BF16)
        t = _dot(gi_ref[j], y)
        o_ref[0, 0, j] = t[:half].astype(BF16)
        o_ref[0, 1, j] = t[half:].astype(BF16)


def _fft_conv(g, gi, a5, kspec, ss, order):
    nb, _, n1s, n2s, w = a5.shape
    kblk = lambda d: pl.BlockSpec((1, FFT_KB, 2, n2s, w), lambda i, b: (d, i, 0, 0, 0))
    gblk = pl.BlockSpec((FFT_KB, 2 * n2s, 2 * n2s), lambda i, b: (i, 0, 0))
    ablk = pl.BlockSpec((1, 2, FFT_KB, n2s, w), lambda i, b: (b, 0, i, 0, 0))
    return pl.pallas_call(
        _fft_conv_kernel,
        grid=(n1s // FFT_KB, nb),
        in_specs=[ablk, gblk, gblk, kblk(order), kblk(2 + order),
                  pl.BlockSpec((1, w), lambda i, b: (0, order)),
                  pl.BlockSpec((1, w), lambda i, b: (0, 2 + order))],
        out_specs=ablk,
        out_shape=jax.ShapeDtypeStruct(a5.shape, BF16),
        compiler_params=_cp(("arbitrary", "arbitrary")),
        name="hy_fft_conv",
    )(a5, g, gi, kspec, kspec, ss, ss)


def _fft_ainv_kernel(f_ref, a_ref, gate_ref, zp_ref, bias_ref, o_ref):
    conv = _dot(f_ref[...], a_ref[...])
    o_ref[...] = (gate_ref[0] * (conv + bias_ref[...] * zp_ref[0])).astype(o_ref.dtype)


def _fft_ainv(fai, a2, gate3, gate_plane, zprev3, zprev_plane, bias, L, nb, out_dtype, out_view_rows):
    h = FFT_N1 // 2
    cols = (L // h) * BRANCH_W
    tn = min(cols, FFT_COLS)
    return pl.pallas_call(
        _fft_ainv_kernel,
        grid=(nb, cols // tn),
        in_specs=[pl.BlockSpec(fai.shape, lambda b, j: (0, 0)),
                  pl.BlockSpec((2 * FFT_N1, tn), lambda b, j: (b, j)),
                  pl.BlockSpec((1, h, tn), lambda b, j: (gate_plane, b, j)),
                  pl.BlockSpec((1, h, tn), lambda b, j: (zprev_plane, b, j)),
                  pl.BlockSpec((1, tn), lambda b, j: (0, 0))],
        out_specs=pl.BlockSpec((h, tn), lambda b, j: (b, j)),
        out_shape=jax.ShapeDtypeStruct((out_view_rows, cols), out_dtype),
        compiler_params=_cp(("arbitrary", "arbitrary")),
        name="hy_fft_ainv",
    )(fai, a2, gate3, zprev3, jnp.tile(bias, (1, tn // BRANCH_W)))


def _hyena_fft(u_small, L, nseq, p, consts, out_rows):
    (short_w, short_b, w1, b1, w2, b2, w3, b3, w4, freq, bias) = p
    fa, fai, g, gi = consts
    h = FFT_N1 // 2
    n2s = L // h
    cols = n2s * BRANCH_W
    xs3 = _short_conv(u_small, short_w, short_b, L, nseq, 0, planes=True).reshape(3, nseq * h, cols)
    hf4, ss = _hyena_filter(L, w1, b1, w2, b2, w3, b3, w4, freq, planes=True)
    ka = _fft_a(fa, hf4.reshape(4, h, cols), lambda b: (b, 0), 4, L)
    kspec = _fft_spec(g, ka.reshape(4, 2, FFT_N1, n2s, BRANCH_W))

    def conv(z3, zmap, order, gate_plane, zprev_plane, out_dtype, out_view_rows):
        a = _fft_a(fa, z3, zmap, nseq, L)
        a = _fft_conv(g, gi, a.reshape(nseq, 2, FFT_N1, n2s, BRANCH_W), kspec, ss, order)
        return _fft_ainv(fai, a.reshape(nseq * 2 * FFT_N1, cols), xs3, gate_plane, z3, zprev_plane,
                         bias[order:order + 1], L, nseq, out_dtype, out_view_rows)

    z1 = conv(xs3, lambda b: (2, b), 0, 0, 2, F32, nseq * h).reshape(1, nseq * h, cols)
    y = conv(z1, lambda b: (0, b), 1, 1, 0, BF16, out_rows // n2s)
    return y.reshape(out_rows, BRANCH_W)


def _conformer_kernel(a_ref, g_ref, w_ref, b_ref, lg_ref, lb_ref, *rest, L, tc):
    o_ref, pad_scr = rest[-2:]
    pad_scr[0:16, :] = jnp.zeros((16, BRANCH_W), F32)
    pad_scr[16 + L:32 + L, :] = jnp.zeros((16, BRANCH_W), F32)

    def glu(c, carry):
        t0 = pl.multiple_of(c * tc, tc)
        pad_scr[pl.ds(16 + t0, tc), :] = a_ref[pl.ds(t0, tc), :] * _sigmoid(g_ref[pl.ds(t0, tc), :])
        return carry

    lax.fori_loop(0, L // tc, glu, 0)
    b = b_ref[...]
    lg = lg_ref[...]
    lb = lb_ref[...]

    def body(c, carry):
        t0 = pl.multiple_of(c * tc, tc)
        win = pad_scr[pl.ds(t0, tc + 32), :]
        acc = jnp.zeros((tc, BRANCH_W), F32) + b
        for r in range(8):
            sh = win[r:r + tc + 24]
            for a in range(4):
                m = 8 * a + r
                if 1 <= m <= CF_WIDTH:
                    acc = acc + w_ref[m - 1:m, :] * sh[8 * a:8 * a + tc]
        mu = jnp.mean(acc, axis=-1, keepdims=True)
        xc = acc - mu
        var = jnp.mean(xc * xc, axis=-1, keepdims=True)
        y = xc * lax.rsqrt(var + EPS) * lg + lb
        o_ref[pl.ds(t0, tc), :] = (y * _sigmoid(y)).astype(o_ref.dtype)
        return carry

    lax.fori_loop(0, L // tc, body, 0)


def _conformer(u_small, L, nseq, row_base, p, out_rows, prev=None):
    dw_w, dw_b, ln_g, ln_b = p
    tc = 128
    wpad = jnp.pad(dw_w, ((0, 32 - CF_WIDTH), (0, 0)))
    row = lambda a: a.reshape(1, BRANCH_W)
    vec = pl.BlockSpec((1, BRANCH_W), lambda s: (0, 0))
    args = [u_small, u_small, wpad, row(dw_b), row(ln_g), row(ln_b)]
    in_specs = [pl.BlockSpec((L, BRANCH_W), lambda s: (row_base + s, 3)),
                pl.BlockSpec((L, BRANCH_W), lambda s: (row_base + s, 4)),
                pl.BlockSpec((32, BRANCH_W), lambda s: (0, 0)), vec, vec, vec]
    args, in_specs, alias = _into(prev, args, in_specs)
    return pl.pallas_call(
        functools.partial(_conformer_kernel, L=L, tc=tc),
        grid=(nseq,),
        in_specs=in_specs,
        out_specs=pl.BlockSpec((L, BRANCH_W), lambda s: (row_base + s, 0)),
        out_shape=jax.ShapeDtypeStruct((out_rows, BRANCH_W), BF16),
        input_output_aliases=alias,
        scratch_shapes=[pltpu.VMEM((L + 32, BRANCH_W), F32)],
        compiler_params=_cp(("arbitrary",)),
        name="conformer",
    )(*args)


def _rope_tables(S, pad_rows, head):
    half = head // 2
    nf = half // 2
    lane = np.arange(LANES)
    inv_lane = (ROPE_BASE ** (-(np.arange(nf)) / nf))[(lane % half) % nf]
    is_row = (lane % head) < half
    pos = jnp.arange(S, dtype=jnp.int32)
    rows = (pos // GRID_W).astype(F32)[:, None]
    cols = (pos % GRID_W).astype(F32)[:, None]
    ang = jnp.where(jnp.asarray(is_row)[None, :], rows, cols) * jnp.asarray(inv_lane, F32)[None, :]
    cos = jnp.concatenate([jnp.cos(ang), jnp.ones((pad_rows, LANES), F32)], axis=0)
    sin = jnp.concatenate([jnp.sin(ang), jnp.zeros((pad_rows, LANES), F32)], axis=0)
    return cos, sin


def _group_ones(width, group):
    idx = np.arange(width)
    return jnp.asarray((idx[:, None] // group) == (idx[None, :] // group), BF16)


def _head_norm_rope(x, ones, group, gain, cos, sin, nf, out_scale):
    w = x.shape[1]
    hi, lo = _split(x * x)
    ms = (_dot(hi, ones) + _dot(lo, ones)) * (1.0 / group)
    xn = x * lax.rsqrt(ms + EPS) * gain
    reps = w // LANES
    c = jnp.concatenate([cos] * reps, axis=1) if reps > 1 else cos
    s = jnp.concatenate([sin] * reps, axis=1) if reps > 1 else sin
    lane = lax.broadcasted_iota(jnp.int32, x.shape, 1)
    first = (lane % (2 * nf)) < nf
    rot = jnp.where(first, -pltpu.roll(xn, w - nf, 1), pltpu.roll(xn, nf, 1))
    return (xn * c + rot * s) * out_scale


def _qkv_kernel(gq_ref, gkv_ref, dq_ref, dk_ref, dv_ref, cg_ref, sg_ref, cd_ref, sd_ref,
                o64_ref, o32_ref, gqn_ref, gkn_ref, dqn_ref, dkn_ref,
                qg_ref, kg_ref, vag_ref, vbg_ref, qd_ref, kd_ref, vad_ref, vbd_ref):
    cg, sg, cd, sd = cg_ref[...], sg_ref[...], cd_ref[...], sd_ref[...]
    o64, o32 = o64_ref[...], o32_ref[...]
    q = _head_norm_rope(gq_ref[...], o64, HEAD_DIM, gqn_ref[...], cg, sg, HEAD_DIM // 4, LOG2E * HEAD_DIM ** -0.5)
    qg_ref[...] = q.astype(BF16)
    kv = gkv_ref[...]
    k = _head_norm_rope(kv[:, :LANES], o64[:LANES, :LANES], HEAD_DIM, gkn_ref[...], cg, sg, HEAD_DIM // 4, 1.0)
    v = kv[:, LANES:]
    kk = jnp.concatenate([k, k], axis=1)
    vv = jnp.concatenate([v, v], axis=1)
    quarter = lax.broadcasted_iota(jnp.int32, kk.shape, 1) // HEAD_DIM
    kr = pltpu.roll(kk, HEAD_DIM, 1)
    vr = pltpu.roll(vv, HEAD_DIM, 1)
    kg_ref[...] = jnp.where((quarter == 0) | (quarter == 3), kk, kr).astype(BF16)
    vag_ref[...] = jnp.where(quarter == 0, vv, jnp.where(quarter == 2, vr, 0.0)).astype(BF16)
    vbg_ref[...] = jnp.where(quarter == 1, vr, jnp.where(quarter == 3, vv, 0.0)).astype(BF16)
    qd = _head_norm_rope(dq_ref[...], o32, DIFF_D, dqn_ref[...], cd, sd, DIFF_D // 4, LOG2E * DIFF_D ** -0.5)
    qd_ref[...] = qd.astype(BF16)
    kd = _head_norm_rope(dk_ref[...], o32, DIFF_D, dkn_ref[...], cd, sd, DIFF_D // 4, 1.0)
    kd_ref[...] = kd.astype(BF16)
    vd = dv_ref[...]
    even = (lax.broadcasted_iota(jnp.int32, vd.shape, 1) // HEAD_DIM) % 2 == 0
    vad_ref[...] = jnp.where(even, vd, 0.0).astype(BF16)
    vbd_ref[...] = jnp.where(even, 0.0, vd).astype(BF16)


def _qkv_prep(u_small, gqn, gkn, dqn, dkn, tables, dims):
    n = u_small.shape[0]
    tm = dims["tm_small"]
    S = dims["S"]
    n_lat_tiles = dims["n_lat"] // tm
    per_seq = S // tm
    tmap = lambda i: (jnp.where(i < n_lat_tiles, i % per_seq, per_seq), 0)
    col = lambda c: pl.BlockSpec((tm, BRANCH_W), lambda i: (i, c))
    tab = pl.BlockSpec((tm, LANES), tmap)
    full = lambda a: pl.BlockSpec(a.shape, lambda i: (0,) * a.ndim)
    o64, o32 = _group_ones(BRANCH_W, HEAD_DIM), _group_ones(BRANCH_W, DIFF_D)
    gains = [jnp.tile(gqn, 4).reshape(1, 256), jnp.tile(gkn, 2).reshape(1, 128),
             jnp.tile(dqn.reshape(-1), 4).reshape(1, 256), jnp.tile(dkn.reshape(-1), 4).reshape(1, 256)]
    out = pl.BlockSpec((tm, BRANCH_W), lambda i: (i, 0))
    return pl.pallas_call(
        _qkv_kernel,
        grid=(n // tm,),
        in_specs=[col(5), col(6), col(7), col(8), col(9), tab, tab, tab, tab, full(o64), full(o32)]
        + [full(g) for g in gains],
        out_specs=[out] * 8,
        out_shape=[jax.ShapeDtypeStruct((n, BRANCH_W), BF16)] * 8,
        compiler_params=_cp(("arbitrary",)),
        name="qkv_prep",
    )(u_small, u_small, u_small, u_small, u_small, *tables, o64, o32, *gains)


def _lane_pick(lane_lo, a, b):
    return jnp.where(lane_lo, a, b)


def _flash_kernel(*refs, segs, tk, tq, diff, lam_init):
    q_ref, o_ref = refs[0], refs[-1]
    seg_refs = [refs[1 + 3 * i:4 + 3 * i] for i in range(len(segs))]
    aux_ref = refs[1 + 3 * len(segs)]
    lane = lax.broadcasted_iota(jnp.int32, (1, LANES), 1)
    lane_lo = lane < HEAD_DIM
    if diff:
        masks = [(lane >= g * DIFF_D) & (lane < (g + 1) * DIFF_D) for g in range(4)]
        acc_of = [0, 1, 0, 1]
        use_a = [True, True, False, False]
        n_acc = 2
    else:
        masks = [lane_lo, ~lane_lo]
        acc_of = [0, 0]
        use_a = [True, False]
        n_acc = 1
    R = len(masks)
    pairs = [slice(p * LANES, (p + 1) * LANES) for p in range(2)]
    qsts = [jnp.concatenate([jnp.where(m, q_ref[:, ps], jnp.zeros((tq, LANES), BF16)) for m in masks], axis=0)
            for ps in pairs]
    lo_hi = []
    for a in range(n_acc):
        rs = [r for r in range(R) if acc_of[r] == a]
        lo_hi.append(([r for r in rs if use_a[r]][0], [r for r in rs if not use_a[r]][0]))

    def rows(x, r):
        return x[r * tq:(r + 1) * tq]

    shift = jnp.max(aux_ref[5:6, :])

    def run(fixed):
        def chunk(kv_refs, t0, size, carry):
            k_ref, va_ref, vb_ref = kv_refs
            new = []
            for p, ps in enumerate(pairs):
                m_run, l_run, accs = carry[p]
                k = k_ref[pl.ds(t0, size), ps]
                s = lax.dot_general(qsts[p], k, (((1,), (1,)), ((), ())), preferred_element_type=F32)
                if fixed:
                    m_new = m_run
                    pr = jnp.exp2(s - shift)
                    l_new = l_run + jnp.sum(pr, axis=-1, keepdims=True)
                else:
                    m_new = jnp.maximum(m_run, jnp.max(s, axis=-1, keepdims=True))
                    alpha = jnp.exp2(m_run - m_new)
                    pr = jnp.exp2(s - m_new)
                    l_new = alpha * l_run + jnp.sum(pr, axis=-1, keepdims=True)
                prb = pr.astype(BF16)
                va = va_ref[pl.ds(t0, size), ps]
                vb = vb_ref[pl.ds(t0, size), ps]
                new_accs = []
                for a, (r_lo, r_hi) in enumerate(lo_hi):
                    upd = _dot(rows(prb, r_lo), va) + _dot(rows(prb, r_hi), vb)
                    if fixed:
                        new_accs.append(accs[a] + upd)
                    else:
                        al = _lane_pick(lane_lo, rows(alpha, r_lo), rows(alpha, r_hi))
                        new_accs.append(accs[a] * al + upd)
                new.append((m_new, l_new, tuple(new_accs)))
            return tuple(new)

        one = (jnp.full((R * tq, 1), -jnp.inf, F32), jnp.zeros((R * tq, 1), F32),
               tuple(jnp.zeros((tq, LANES), F32) for _ in range(n_acc)))
        carry = (one, one)
        for kv_refs, T in zip(seg_refs, segs):
            n_main = T // tk
            if n_main:
                carry = lax.fori_loop(
                    0, n_main, lambda c, cr, kv_refs=kv_refs: chunk(kv_refs, pl.multiple_of(c * tk, tk), tk, cr),
                    carry, unroll=4 if (fixed and not diff and n_main % 4 == 0) else 2 if n_main % 2 == 0 else 1)
            if T - n_main * tk:
                carry = chunk(kv_refs, n_main * tk, T - n_main * tk, carry)

        outs = []
        for p in range(2):
            _, l_fin, accs = carry[p]
            inv_l = 1.0 / l_fin
            norm = [accs[a] * _lane_pick(lane_lo, rows(inv_l, r_lo), rows(inv_l, r_hi))
                    for a, (r_lo, r_hi) in enumerate(lo_hi)]
            if diff:
                aux = aux_ref[...]
                lam = (jnp.exp(jnp.sum(aux[0:1] * aux[1:2], axis=-1, keepdims=True))
                       - jnp.exp(jnp.sum(aux[2:3] * aux[3:4], axis=-1, keepdims=True)) + lam_init)
                o = norm[0] - lam * norm[1]
                sq = o * o
                s_lo = jnp.sum(jnp.where(lane_lo, sq, 0.0), axis=-1, keepdims=True)
                s_hi = jnp.sum(jnp.where(lane_lo, 0.0, sq), axis=-1, keepdims=True)
                ms = _lane_pick(lane_lo, s_lo, s_hi) * (1.0 / HEAD_DIM)
                o = o * lax.rsqrt(ms + EPS) * aux[4:5] * (1.0 - lam_init)
            else:
                o = norm[0]
            outs.append(o)
        o_ref[...] = jnp.concatenate(outs, axis=1).astype(o_ref.dtype)

    @pl.when(shift < FIXED_SHIFT_LIMIT)
    def _():
        run(True)

    @pl.when(jnp.logical_not(shift < FIXED_SHIFT_LIMIT))
    def _():
        run(False)


def _into(prev, args, in_specs):
    if prev is None:
        return args, in_specs, {}
    return args + [prev], in_specs + [pl.BlockSpec(memory_space=pl.ANY)], {len(args): 0}


def _flash(q, k, va, vb, aux, nb, Lq, q_base, segs, diff, lam_init, out_rows, out_base, prev=None):
    tq = min(512, Lq)
    tk = 1024
    nt = Lq // tq
    q_base, out_base = q_base // tq, out_base // tq
    args = [q]
    in_specs = [pl.BlockSpec((tq, BRANCH_W), lambda b, i: (q_base + b * nt + i, 0))]
    for length, base in segs:
        spec = pl.BlockSpec((length, BRANCH_W), lambda b, i, base=base: (base + b, 0))
        args += [k, va, vb]
        in_specs += [spec, spec, spec]
    args.append(aux)
    in_specs.append(pl.BlockSpec((8, LANES), lambda b, i: (0, 0)))
    args, in_specs, alias = _into(prev, args, in_specs)
    return pl.pallas_call(
        functools.partial(_flash_kernel, segs=tuple(s[0] for s in segs), tk=tk, tq=tq, diff=diff,
                          lam_init=lam_init),
        grid=(nb, nt),
        in_specs=in_specs,
        out_specs=pl.BlockSpec((tq, BRANCH_W), lambda b, i: (out_base + b * nt + i, 0)),
        out_shape=jax.ShapeDtypeStruct((out_rows, BRANCH_W), BF16),
        input_output_aliases=alias,
        compiler_params=_cp(("arbitrary", "arbitrary"), VMEM_LIMIT_BIG),
        name="flash_diff" if diff else "flash_gqa",
    )(*args)


def _merge_kernel(y0, y1, y2, y3, g_ref, wb_ref, wo_ref, x_ref, mod_ref, o_ref):
    d = D_MODEL
    acc = g_ref[:, 0:d].astype(F32) * _dot(y0[...], wb_ref[0])
    acc = acc + g_ref[:, d:2 * d].astype(F32) * _dot(y1[...], wb_ref[1])
    acc = acc + g_ref[:, 2 * d:3 * d].astype(F32) * _dot(y2[...], wb_ref[2])
    acc = acc + g_ref[:, 3 * d:4 * d].astype(F32) * _dot(y3[...], wb_ref[3])
    mix = _dot(acc.astype(BF16), wo_ref[...])
    o_ref[...] = x_ref[...] + mod_ref[0][2:3, :] * mix


def _merge(ys, gate, wb, wo, x, mods_l, dims):
    n, d = ys[0].shape[0], x.shape[1]
    tm = dims["tm_small"]
    mrow = dims["mod_row"](tm)
    yspec = pl.BlockSpec((tm, BRANCH_W), lambda i: (i, 0))
    return pl.pallas_call(
        _merge_kernel,
        grid=(n // tm,),
        in_specs=[yspec] * 4 + [pl.BlockSpec((tm, GATE_COLS), lambda i: (i, 0)),
                                pl.BlockSpec(wb.shape, lambda i: (0, 0, 0)),
                                pl.BlockSpec(wo.shape, lambda i: (0, 0)),
                                pl.BlockSpec((tm, d), lambda i: (i, 0)),
                                pl.BlockSpec((1, 6, d), lambda i: (mrow(i), 0, 0))],
        out_specs=pl.BlockSpec((tm, d), lambda i: (i, 0)),
        out_shape=jax.ShapeDtypeStruct((n, d), F32),
        compiler_params=_cp(("arbitrary",)),
        name="merge",
    )(*ys, gate, wb, wo, x, mods_l)


def _ffn_kernel(x_ref, mod_ref, g_ref, w1_ref, w3_ref, w2_ref, o_ref, h_scr, acc_scr):
    j = pl.program_id(1)

    @pl.when(j == 0)
    def _():
        m = mod_ref[0]
        h_scr[...] = _norm_mod(x_ref[...], g_ref[...], m[3:4, :], m[4:5, :]).astype(BF16)
        acc_scr[...] = jnp.zeros_like(acc_scr)

    h = h_scr[...]
    a = _dot(h, w1_ref[...])
    b = _dot(h, w3_ref[...])
    t = (a * _sigmoid(a) * b).astype(BF16)
    acc_scr[...] += _dot(t, w2_ref[...])

    @pl.when(j == pl.num_programs(1) - 1)
    def _():
        o_ref[...] = x_ref[...] + mod_ref[0][5:6, :] * acc_scr[...]


def _ffn(x, mods_l, gain, w1, w3, w2, dims):
    n, d = x.shape
    tm, tf = dims["tm_big"], 256
    mrow = dims["mod_row"](tm)
    return pl.pallas_call(
        _ffn_kernel,
        grid=(n // tm, w1.shape[1] // tf),
        in_specs=[pl.BlockSpec((tm, d), lambda i, j: (i, 0)),
                  pl.BlockSpec((1, 6, d), lambda i, j: (mrow(i), 0, 0)),
                  pl.BlockSpec((1, d), lambda i, j: (0, 0)),
                  pl.BlockSpec((d, tf), lambda i, j: (0, j)),
                  pl.BlockSpec((d, tf), lambda i, j: (0, j)),
                  pl.BlockSpec((tf, d), lambda i, j: (j, 0))],
        out_specs=pl.BlockSpec((tm, d), lambda i, j: (i, 0)),
        out_shape=jax.ShapeDtypeStruct((n, d), F32),
        scratch_shapes=[pltpu.VMEM((tm, d), BF16), pltpu.VMEM((tm, d), F32)],
        compiler_params=_cp(("arbitrary", "arbitrary")),
        name="ffn_dense",
    )(x, mods_l, gain, w1, w3, w2)


def _route_kernel(x_ref, mod_ref, g_ref, wr_ref, tok_ref, gate_ref):
    m = mod_ref[0]
    h = _norm_mod(x_ref[...], g_ref[...], m[3:4, :], m[4:5, :])
    tok_ref[...] = h.astype(BF16)
    logits = _dot3(h, wr_ref[...])
    lane = lax.broadcasted_iota(jnp.int32, logits.shape, 1)
    lg = jnp.where(lane < N_EXPERTS, logits, -jnp.inf)
    m1 = jnp.max(lg, axis=-1, keepdims=True)
    i1 = jnp.min(jnp.where(lg == m1, lane, LANES), axis=-1, keepdims=True)
    lg2 = jnp.where(lane == i1, -jnp.inf, lg)
    m2 = jnp.max(lg2, axis=-1, keepdims=True)
    i2 = jnp.min(jnp.where(lg2 == m2, lane, LANES), axis=-1, keepdims=True)
    e2 = jnp.exp(m2 - m1)
    g1 = 1.0 / (1.0 + e2)
    gate_ref[...] = jnp.where(lane == i1, g1, jnp.where(lane == i2, e2 * g1, 0.0))


def _route(x, mods_l, gain, w_router, dims):
    n, d = x.shape
    tm = dims["tm_small"]
    mrow = dims["mod_row"](tm)
    wr = jnp.pad(w_router, ((0, 0), (0, LANES - N_EXPERTS)))
    return pl.pallas_call(
        _route_kernel,
        grid=(n // tm,),
        in_specs=[pl.BlockSpec((tm, d), lambda i: (i, 0)),
                  pl.BlockSpec((1, 6, d), lambda i: (mrow(i), 0, 0)),
                  pl.BlockSpec((1, d), lambda i: (0, 0)),
                  pl.BlockSpec((d, LANES), lambda i: (0, 0))],
        out_specs=[pl.BlockSpec((tm, d), lambda i: (i, 0)), pl.BlockSpec((tm, LANES), lambda i: (i, 0))],
        out_shape=[jax.ShapeDtypeStruct((n, d), BF16), jax.ShapeDtypeStruct((n, LANES), F32)],
        compiler_params=_cp(("arbitrary",)),
        name="moe_route",
    )(x, mods_l, gain, wr)


MOE_SUB = 128
MOE_GATHER = 256
MOE_GRAN = 64


def _moe_kernel(tok_ref, gate_ref, tri_ref, w1_ref, w3_ref, w2_ref, x_ref, mod_ref, o_ref,
                rank_scr, rank_t_scr, mask_t_scr, xe_scr, ye_scr, cnt_smem, *, tm):
    e = pl.program_id(1)
    j = pl.program_id(2)
    ne = pl.num_programs(1)
    nj = pl.num_programs(2)
    lane = lax.broadcasted_iota(jnp.int32, (1, LANES), 1)

    @pl.when((e == 0) & (j == 0))
    def _():
        tri = tri_ref[...]
        carry = jnp.zeros((1, LANES), F32)
        def put_counts(blk, counts):
            for ee in range(N_EXPERTS):
                cnt_smem[blk * N_EXPERTS + ee] = jnp.sum(jnp.where(lane == ee, counts, 0.0)).astype(jnp.int32)

        for blk in range(tm // 256):
            put_counts(blk, carry)
            rs = slice(blk * 256, (blk + 1) * 256)
            msk = (gate_ref[rs, :] > 0.0).astype(BF16)
            rank_scr[rs, :] = _dot(tri, msk) + carry
            carry = carry + jnp.sum(msk.astype(F32), axis=0, keepdims=True)
        rank_t_scr[...] = jnp.transpose(rank_scr[...])
        mask_t_scr[...] = jnp.transpose((gate_ref[...] > 0.0).astype(F32))
        put_counts(tm // 256, carry)
        o_ref[...] = jnp.zeros_like(o_ref)
        ye_scr[...] = jnp.zeros_like(ye_scr)

    cnt = cnt_smem[(tm // 256) * N_EXPERTS + e]

    n_sub = (cnt + (MOE_GATHER - 1)) // MOE_GATHER

    @pl.when(j == 0)
    def _():
        rk = rank_t_scr[pl.ds(e, 1), :]
        mk = mask_t_scr[pl.ds(e, 1), :]

        def gather(sb, carry):
            r0 = pl.multiple_of(sb * MOE_GATHER, MOE_GATHER)
            r_iota = lax.broadcasted_iota(jnp.int32, (MOE_GATHER, tm), 0) + r0
            sel = jnp.where((rk == r_iota.astype(F32)) & (mk > 0.0), 1.0, 0.0).astype(BF16)
            xe_scr[pl.ds(r0, MOE_GATHER), :] = _dot(sel, tok_ref[...]).astype(BF16)
            return carry

        lax.fori_loop(0, n_sub, gather, 0)

    def expert_rows(r0, size):
        xe = xe_scr[pl.ds(r0, size), :]
        a = _dot(xe, w1_ref[0])
        b = _dot(xe, w3_ref[0])
        t = (a * _sigmoid(a) * b).astype(BF16)
        y = _dot(t, w2_ref[0])
        ye_scr[pl.ds(r0, size), :] = jnp.where(j == 0, y, ye_scr[pl.ds(r0, size), :] + y)

    n_gran = (cnt + (MOE_GRAN - 1)) // MOE_GRAN
    n_big = n_gran // 4
    rem = n_gran % 4

    def big(i, carry):
        expert_rows(pl.multiple_of(i * 256, 256), 256)
        return carry

    lax.fori_loop(0, n_big - 1, big, 0)
    for r in range(min(4, (tm - 256) // MOE_GRAN + 1)):
        @pl.when((n_big >= 1) & (rem == r))
        def _():
            expert_rows(pl.multiple_of((n_big - 1) * 256, 256), 256 + r * MOE_GRAN)

    @pl.when((n_big == 0) & (rem >= 2))
    def _():
        expert_rows(0, 128)

    @pl.when((n_big == 0) & (rem % 2 == 1))
    def _():
        expert_rows(pl.multiple_of((rem // 2) * 128, 128), 64)

    @pl.when(j == nj - 1)
    def _():
        def scatter(rc, carry):
            r0 = pl.multiple_of(rc * 256, 256)
            gt = gate_ref[pl.ds(r0, 256), :]
            g_e = jnp.sum(jnp.where(lane == e, gt, 0.0), axis=-1, keepdims=True)
            r_e = jnp.sum(jnp.where(lane == e, rank_scr[pl.ds(r0, 256), :], 0.0), axis=-1, keepdims=True)
            lo = cnt_smem[rc * N_EXPERTS + e]
            hi = cnt_smem[(rc + 1) * N_EXPERTS + e]

            def scatter_sub(sb, c2):
                c0 = pl.multiple_of(sb * MOE_SUB, MOE_SUB)
                c_iota = lax.broadcasted_iota(jnp.int32, (256, MOE_SUB), 1) + c0
                sel_t = jnp.where((r_e == c_iota.astype(F32)) & (g_e > 0.0), 1.0, 0.0).astype(BF16)
                ye = ye_scr[pl.ds(c0, MOE_SUB), :].astype(BF16)
                o_ref[pl.ds(r0, 256), :] += g_e * _dot(sel_t, ye)
                return c2

            lax.fori_loop(lo // MOE_SUB, (hi + (MOE_SUB - 1)) // MOE_SUB, scatter_sub, 0)

            @pl.when(e == ne - 1)
            def _():
                o_ref[pl.ds(r0, 256), :] = (x_ref[pl.ds(r0, 256), :]
                                            + mod_ref[0][5:6, :] * o_ref[pl.ds(r0, 256), :])
            return carry

        lax.fori_loop(0, tm // 256, scatter, 0)


def _moe(tok, gates, w1, w3, w2, x, mods_l, dims):
    n, d = tok.shape
    tm = dims["tm_moe"] if n % dims["tm_moe"] == 0 else dims["tm_big"]
    mrow = dims["mod_row"](tm)
    ne, _, ff = w1.shape
    tf = 896
    tri =jnp.asarray(np.tril(np.ones((256, 256)), -1), BF16)
    once = pl.Buffered(1)
    return pl.pallas_call(
        functools.partial(_moe_kernel, tm=tm),
        grid=(n // tm, ne, ff // tf),
        in_specs=[pl.BlockSpec((tm, d), lambda i, e, j: (i, 0), pipeline_mode=once),
                  pl.BlockSpec((tm, LANES), lambda i, e, j: (i, 0), pipeline_mode=once),
                  pl.BlockSpec((256, 256), lambda i, e, j: (0, 0), pipeline_mode=once),
                  pl.BlockSpec((1, d, tf), lambda i, e, j: (e, 0, j)),
                  pl.BlockSpec((1, d, tf), lambda i, e, j: (e, 0, j)),
                  pl.BlockSpec((1, tf, d), lambda i, e, j: (e, j, 0)),
                  pl.BlockSpec((tm, d), lambda i, e, j: (i, 0), pipeline_mode=once),
                  pl.BlockSpec((1, 6, d), lambda i, e, j: (mrow(i), 0, 0))],
        out_specs=pl.BlockSpec((tm, d), lambda i, e, j: (i, 0), pipeline_mode=once),
        out_shape=jax.ShapeDtypeStruct((n, d), F32),
        scratch_shapes=[pltpu.VMEM((tm, LANES), F32), pltpu.VMEM((LANES, tm), F32), pltpu.VMEM((LANES, tm), F32),
                        pltpu.VMEM((tm, d), BF16), pltpu.VMEM((tm, d), F32), pltpu.SMEM(((tm // 256 + 1) * N_EXPERTS,), jnp.int32)],
        compiler_params=_cp(("arbitrary", "arbitrary", "arbitrary"), VMEM_LIMIT_BIG),
        name="moe_experts",
    )(tok, gates, tri, w1, w3, w2, x, mods_l)


def _make_dims(B, S, n_ctx):
    n_lat, n = B * S, B * (S + n_ctx)

    def pick(prefs):
        for t in prefs:
            if S % t == 0 and n_lat % t == 0 and (n - n_lat) % t == 0:
                return t
        raise ValueError("no row tile divides the latent and context token counts")

    def mod_row(tm):
        return lambda i: jnp.minimum((i * tm) // S, B)

    return {"B": B, "S": S, "n_ctx": n_ctx, "n_lat": n_lat, "n": n,
            "tm_big": pick((1024, 512, 256)), "tm_small": pick((512, 256)), "tm_moe": pick((2048, 1024, 512, 256)),
            "mod_row": mod_row}


def kernel(x, c, ctx, c_ctx, w_ada, b_ada, norm_mix, norm_ffn, w_in, hy_short_w, hy_short_b, hy_w1, hy_b1, hy_w2, hy_b2, hy_w3, hy_b3, hy_w4, hy_freq, hy_bias, cf_dw_w, cf_dw_b, cf_ln_g, cf_ln_b, gqa_qn, gqa_kn, diff_qn, diff_kn, diff_lq1, diff_lk1, diff_lq2, diff_lk2, diff_subln, w_branch, w_out, ffn_w1, ffn_w3, ffn_w2, moe_router, moe_w1, moe_w3, moe_w2):
    B, S, D = x.shape
    n_ctx = ctx.shape[1]
    depth = w_ada.shape[0]
    dims = _make_dims(B, S, n_ctx)
    n_lat = dims["n_lat"]
    tm_s = dims["tm_small"]

    c16 = jnp.concatenate([c, c_ctx[None, :], jnp.zeros((16 - B - 1, D), F32)], axis=0)
    mods = _ada_mods(c16, w_ada, b_ada).reshape(depth, 16, 6, D)

    xa = jnp.concatenate([x.reshape(n_lat, D), ctx.reshape(B * n_ctx, D)], axis=0)
    tables = _rope_tables(S, tm_s, HEAD_DIM) + _rope_tables(S, tm_s, DIFF_D)
    fft_lat = _fft_consts(S)
    mats_ctx = _dft_mats(n_ctx)

    for l in range(depth):
        lam_init = 0.8 - 0.6 * math.exp(-0.3 * l)
        u_small, gate = _in_proj(xa, mods[l], norm_mix[l].reshape(1, D), w_in[l].astype(BF16), dims)

        hy_p = (hy_short_w[l], hy_short_b[l], hy_w1[l], hy_b1[l], hy_w2[l], hy_b2[l], hy_w3[l], hy_b3[l],
                hy_w4[l], hy_freq[l], hy_bias[l])
        cf_p = (cf_dw_w[l], cf_dw_b[l], cf_ln_g[l], cf_ln_b[l])
        last = l == depth - 1
        n_out = n_lat if last else dims["n"]
        ctx_base = n_lat // n_ctx
        y_hy = _hyena_fft(u_small, S, B, hy_p, fft_lat, n_out)
        y_cf = _conformer(u_small, S, B, 0, cf_p, n_out)
        if not last:
            y_hy = _hyena(u_small, n_ctx, B, ctx_base, hy_p, mats_ctx, n_out, y_hy)
            y_cf = _conformer(u_small, n_ctx, B, ctx_base, cf_p, n_out, y_cf)

        qg, kg, vag, vbg, qd, kd, vad, vbd = _qkv_prep(u_small, gqa_qn[l], gqa_kn[l], diff_qn[l], diff_kn[l],
                                                      tables, dims)

        aux = jnp.zeros((8, LANES), F32)
        aux = aux.at[0, :DIFF_D].set(diff_lq1[l]).at[1, :DIFF_D].set(diff_lk1[l])
        aux = aux.at[2, :DIFF_D].set(diff_lq2[l]).at[3, :DIFF_D].set(diff_lk2[l])
        aux = aux.at[4, :].set(jnp.tile(diff_subln[l], 2))

        def score_bound(qn, kn, d):
            return 1.02 * d * jnp.max(jnp.abs(qn)) * jnp.max(jnp.abs(kn)) * (LOG2E * d ** -0.5) + 0.1

        bounds = (score_bound(gqa_qn[l], gqa_kn[l], HEAD_DIM), score_bound(diff_qn[l], diff_kn[l], DIFF_D))
        y_att = []
        for (q, k, va, vb, is_diff) in ((qg, kg, vag, vbg, False), (qd, kd, vad, vbd, True)):
            aux = aux.at[5, :].set(bounds[int(is_diff)])
            y = _flash(q, k, va, vb, aux, B, S, 0, ((S, 0), (n_ctx, ctx_base)), is_diff, lam_init, n_out, 0)
            if not last:
                y = _flash(q, k, va, vb, aux, B, n_ctx, n_lat, ((n_ctx, ctx_base),), is_diff, lam_init,
                           n_out, n_lat, y)
            y_att.append(y)

        xa = _merge((y_hy, y_cf, y_att[0], y_att[1]), gate, w_branch[l].astype(BF16), w_out[l].astype(BF16),
                    xa, mods[l], dims)

        i = l // 2
        if l % 2 == 0:
            xa = _ffn(xa, mods[l], norm_ffn[l].reshape(1, D), ffn_w1[i].astype(BF16), ffn_w3[i].astype(BF16),
                      ffn_w2[i].astype(BF16), dims)
        else:
            tok, gates = _route(xa, mods[l], norm_ffn[l].reshape(1, D), moe_router[i], dims)
            xa = _moe(tok, gates, moe_w1[i].astype(BF16), moe_w3[i].astype(BF16), moe_w2[i].astype(BF16),
                      xa, mods[l], dims)
    return xa[:n_lat].reshape(B, S, D)
```

```python
import functools
import math

import numpy as np
import jax
import jax.numpy as jnp
from jax import lax
from jax.experimental import pallas as pl
from jax.experimental.pallas import tpu as pltpu

F32 = jnp.float32
BF16 = jnp.bfloat16

D_MODEL = 1024
GRID_W = 64
BRANCH_W = 256
EPS = 1e-6
HY_EMB = 33
HY_EMB_PAD = 128
HY_FFN = 64
HY_TARGET = 1e-2
HY_FAST = 0.3
HY_SLOW = 1.5
CF_WIDTH = 31
HEAD_DIM = 64
DIFF_D = 32
ROPE_BASE = 10000.0
N_EXPERTS = 8
LANES = 128
LOG2E = 1.4426950408889634
FIXED_SHIFT_LIMIT = 60.0
SMALL_COLS = 2560
GATE_COLS = 4 * D_MODEL
VMEM_LIMIT = 48 * 1024 * 1024
VMEM_LIMIT_BIG = 56 * 1024 * 1024


def _cp(sem, vmem=VMEM_LIMIT):
    return pltpu.CompilerParams(dimension_semantics=sem, vmem_limit_bytes=vmem)


def _sigmoid(x):
    return 1.0 / (1.0 + jnp.exp(-x))


def _dot(a, b):
    return jnp.dot(a, b, preferred_element_type=F32)


def _split(a):
    hi = a.astype(BF16)
    lo = (a - hi.astype(F32)).astype(BF16)
    return hi, lo


def _dot3(a, b):
    ah, al = _split(a)
    bh, bl = _split(b)
    return _dot(ah, bh) + (_dot(al, bh) + _dot(ah, bl))


def _norm_mod(x, gain, shift, scale):
    ms = jnp.mean(x * x, axis=-1, keepdims=True)
    return (x * lax.rsqrt(ms + EPS) * gain) * (1.0 + scale) + shift


def _ada_kernel(c_ref, w_ref, b_ref, o_ref):
    c = c_ref[...]
    s = c * _sigmoid(c)
    o_ref[0] = _dot3(s, w_ref[0]) + b_ref[0]


def _ada_mods(c16, w_ada, b_ada):
    depth, d, n6 = w_ada.shape
    tn = 512
    return pl.pallas_call(
        _ada_kernel,
        grid=(depth, n6 // tn),
        in_specs=[pl.BlockSpec((16, d), lambda l, j: (0, 0)),
                  pl.BlockSpec((1, d, tn), lambda l, j: (l, 0, j)),
                  pl.BlockSpec((1, 1, tn), lambda l, j: (l, 0, j))],
        out_specs=pl.BlockSpec((1, 16, tn), lambda l, j: (l, 0, j)),
        out_shape=jax.ShapeDtypeStruct((depth, 16, n6), F32),
        compiler_params=_cp(("arbitrary", "arbitrary")),
        name="ada_mods",
    )(c16, w_ada, b_ada.reshape(depth, 1, n6))


def _in_proj_kernel(x_ref, mod_ref, g_ref, w_ref, o_ref, h_scr, *, gate):
    @pl.when(pl.program_id(1) == 0)
    def _():
        m = mod_ref[0]
        h_scr[...] = _norm_mod(x_ref[...], g_ref[...], m[0:1, :], m[1:2, :]).astype(BF16)

    r = _dot(h_scr[...], w_ref[...])
    o_ref[...] = _sigmoid(r).astype(o_ref.dtype) if gate else r


def _in_proj(x, mods_l, gain, w_bf, dims):
    n, d = x.shape
    tm = dims["tm_big"]
    mrow = dims["mod_row"](tm)

    def call(w, tn, gate, dtype):
        n_cols = w.shape[1]
        return pl.pallas_call(
            functools.partial(_in_proj_kernel, gate=gate),
            grid=(n // tm, n_cols // tn),
            in_specs=[pl.BlockSpec((tm, d), lambda i, j: (i, 0)),
                      pl.BlockSpec((1, 6, d), lambda i, j: (mrow(i), 0, 0)),
                      pl.BlockSpec((1, d), lambda i, j: (0, 0)),
                      pl.BlockSpec((d, tn), lambda i, j: (0, j))],
            out_specs=pl.BlockSpec((tm, tn), lambda i, j: (i, j)),
            out_shape=jax.ShapeDtypeStruct((n, n_cols), dtype),
            scratch_shapes=[pltpu.VMEM((tm, d), BF16)],
            compiler_params=_cp(("arbitrary", "arbitrary")),
            name="in_proj_gate" if gate else "in_proj_mix",
        )(x, mods_l, gain, w)

    return (call(w_bf[:, :SMALL_COLS], SMALL_COLS // 2, False, F32),
            call(w_bf[:, SMALL_COLS:], GATE_COLS // 2, True, BF16))


def _short_conv_kernel(u_ref, w_ref, b_ref, o_ref, pad_scr, *, L, tc, planes):
    if planes:
        o_ref = o_ref.at[0]
    pad_scr[0:8, :] = jnp.zeros((8, BRANCH_W), F32)
    pad_scr[8 + L:16 + L, :] = jnp.zeros((8, BRANCH_W), F32)
    pad_scr[8:8 + L, :] = u_ref[...]
    w = w_ref[0]
    b = b_ref[0]

    def body(c, carry):
        t0 = pl.multiple_of(c * tc, tc)
        win = pad_scr[pl.ds(t0, tc + 16), :]
        acc = b + w[0:1, :] * win[7:7 + tc]
        acc = acc + w[1:2, :] * win[8:8 + tc]
        acc = acc + w[2:3, :] * win[9:9 + tc]
        o_ref[pl.ds(t0, tc), :] = acc
        return carry

    lax.fori_loop(0, L // tc, body, 0)


def _short_conv(u_small, w, b, L, nseq, row_base, planes=False):
    w3 = jnp.transpose(w.reshape(3, 3, BRANCH_W), (1, 0, 2))
    w3 = jnp.pad(w3, ((0, 0), (0, 5), (0, 0)))
    b3 = b.reshape(3, 1, BRANCH_W)
    tc = min(L, 256)
    return pl.pallas_call(
        functools.partial(_short_conv_kernel, L=L, tc=tc, planes=planes),
        grid=(nseq, 3),
        in_specs=[pl.BlockSpec((L, BRANCH_W), lambda s, j: (row_base + s, j)),
                  pl.BlockSpec((1, 8, BRANCH_W), lambda s, j: (j, 0, 0)),
                  pl.BlockSpec((1, 1, BRANCH_W), lambda s, j: (j, 0, 0))],
        out_specs=(pl.BlockSpec((1, L, BRANCH_W), lambda s, j: (j, s, 0)) if planes
                   else pl.BlockSpec((L, BRANCH_W), lambda s, j: (s, j))),
        out_shape=jax.ShapeDtypeStruct((3, nseq * L, BRANCH_W) if planes else (nseq * L, 3 * BRANCH_W), F32),
        scratch_shapes=[pltpu.VMEM((L + 16, BRANCH_W), F32)],
        compiler_params=_cp(("arbitrary", "arbitrary")),
        name="hy_short_conv",
    )(u_small, w3, b3)


def _filter_consts(L):
    t = np.linspace(0.0, 1.0, L)[:, None]
    bands = (HY_EMB - 1) // 2
    fr = np.linspace(1e-4, bands - 1, bands)[None, :]
    wpos = 2.0 * math.pi * np.arange(L)[:, None] / L
    z = np.concatenate([t, np.cos(fr * wpos), -np.sin(fr * wpos)], axis=-1)
    z = np.pad(z, ((0, 0), (0, HY_EMB_PAD - HY_EMB)))
    deltas = np.abs(np.linspace(math.log(HY_TARGET) / HY_SLOW, math.log(HY_TARGET) / HY_FAST, BRANCH_W))
    win = np.exp(-t * deltas[None, :])
    return jnp.asarray(z, F32), jnp.asarray(win, F32)


def _filter_kernel(z_ref, w1, b1, w2, b2, w3, b3, w4, fq, win_ref, hf_ref, ss_ref, *, tr, planes):
    i = pl.program_id(0)
    f = fq[...]
    a = jnp.sin(f * (_dot3(z_ref[...], w1[...]) + b1[...]))
    a = jnp.sin(f * (_dot3(a, w2[...]) + b2[...]))
    a = jnp.sin(f * (_dot3(a, w3[...]) + b3[...]))
    h = _dot3(a, w4[...])
    win = win_ref[...]
    h = h * jnp.concatenate([win, win, win, win], axis=1)
    row = lax.broadcasted_iota(jnp.int32, h.shape, 0) + i * tr
    col = lax.broadcasted_iota(jnp.int32, h.shape, 1)
    h = jnp.where((row == 0) & (col >= 2 * BRANCH_W), 0.0, h)
    if planes:
        for d in range(4):
            hf_ref[d] = h[:, d * BRANCH_W:(d + 1) * BRANCH_W]
    else:
        hf_ref[...] = h

    @pl.when(i == 0)
    def _():
        ss_ref[...] = jnp.zeros_like(ss_ref)

    ss_ref[...] += jnp.sum(h * h, axis=0, keepdims=True)


def _hyena_filter(L, w1, b1, w2, b2, w3, b3, w4, freq, planes=False):
    z, win = _filter_consts(L)
    tr = min(L, 256)
    w1p = jnp.pad(w1, ((0, HY_EMB_PAD - HY_EMB), (0, 0)))
    full = lambda a: pl.BlockSpec(a.shape, lambda i: (0,) * a.ndim)
    args = [w1p, b1.reshape(1, -1), w2, b2.reshape(1, -1), w3, b3.reshape(1, -1), w4, freq.reshape(1, -1)]
    n_out = w4.shape[1]
    return pl.pallas_call(
        functools.partial(_filter_kernel, tr=tr, planes=planes),
        grid=(L // tr,),
        in_specs=[pl.BlockSpec((tr, HY_EMB_PAD), lambda i: (i, 0))] + [full(a) for a in args]
        + [pl.BlockSpec((tr, BRANCH_W), lambda i: (i, 0))],
        out_specs=[pl.BlockSpec((4, tr, BRANCH_W), lambda i: (0, i, 0)) if planes
                   else pl.BlockSpec((tr, n_out), lambda i: (i, 0)),
                   pl.BlockSpec((1, n_out), lambda i: (0, 0))],
        out_shape=[jax.ShapeDtypeStruct((4, L, BRANCH_W) if planes else (L, n_out), F32),
                   jax.ShapeDtypeStruct((1, n_out), F32)],
        compiler_params=_cp(("arbitrary",)),
        name="hy_filter",
    )(z, *args, win)


def _dft_mats(L):
    N = 2 * L
    blk = 64
    k = jnp.arange(L, dtype=jnp.int32)[:, None]
    nh = jnp.arange(L // blk, dtype=jnp.int32)[None, :]
    nl = jnp.arange(blk, dtype=jnp.int32)[None, :]
    w = 2.0 * math.pi / N
    a = ((k * (blk * nh)) % N).astype(F32) * w
    b = ((k * nl) % N).astype(F32) * w
    ca, sa, cb, sb = jnp.cos(a), jnp.sin(a), jnp.cos(b), jnp.sin(b)
    cos = (ca[:, :, None] * cb[:, None, :] - sa[:, :, None] * sb[:, None, :]).reshape(L, L)
    sin = (sa[:, :, None] * cb[:, None, :] + ca[:, :, None] * sb[:, None, :]).reshape(L, L)
    alt = jnp.where(jnp.arange(L) % 2 == 0, 1.0, -1.0).astype(F32)
    first = (jnp.arange(L) == 0)
    s_f = jnp.where(first[:, None], alt[None, :], -sin)
    fwd = jnp.concatenate([cos, s_f], axis=0).astype(BF16)
    colscale = jnp.where(first, 1.0 / N, 2.0 / N).astype(F32)
    g_c = cos * colscale[None, :]
    g_s = jnp.where(first[None, :], alt[:, None] / N, -sin * (2.0 / N))
    inv = jnp.concatenate([g_c, g_s], axis=1).astype(BF16)
    return fwd, inv


def _dft_fwd_kernel(f_ref, z_ref, o_ref):
    o_ref[0] = _dot(f_ref[...], z_ref[...].astype(BF16))


def _dft_fwd(fwd, z2d, L, nb, zmap):
    tm = min(2 * L, 1024)
    return pl.pallas_call(
        _dft_fwd_kernel,
        grid=(2 * L // tm, nb),
        in_specs=[pl.BlockSpec((tm, L), lambda i, b: (i, 0)),
                  pl.BlockSpec((L, BRANCH_W), lambda i, b: zmap(b))],
        out_specs=pl.BlockSpec((1, tm, BRANCH_W), lambda i, b: (b, i, 0)),
        out_shape=jax.ShapeDtypeStruct((nb, 2 * L, BRANCH_W), F32),
        compiler_params=_cp(("arbitrary", "arbitrary")),
        name="hy_dft_fwd",
    )(fwd, z2d)


def _spec_prod_kernel(z_ref, kf_ref, kb_ref, sf_ref, sb_ref, y_ref, *, tk):
    i = pl.program_id(1)
    s = lax.rsqrt(sf_ref[...] + sb_ref[...] + EPS)
    zr, zi = z_ref[0, 0], z_ref[0, 1]
    fr, fi = kf_ref[0, 0], kf_ref[0, 1]
    br, bi = kb_ref[0, 0], kb_ref[0, 1]
    row = lax.broadcasted_iota(jnp.int32, (tk, BRANCH_W), 0) + i * tk
    first = row == 0
    kr = (fr + br) * s
    ki = jnp.where(first, fi + bi, fi - bi) * s
    yr = jnp.where(first, zr * kr, zr * kr - zi * ki)
    yi = jnp.where(first, zi * ki, zr * ki + zi * kr)
    y_ref[0, 0] = yr.astype(BF16)
    y_ref[0, 1] = yi.astype(BF16)


def _spec_prod(zf, kfs, ss, order, L, nb):
    tk = min(L, 512)
    zf4 = zf.reshape(nb, 2, L, BRANCH_W)
    kf4 = kfs.reshape(4, 2, L, BRANCH_W)
    y = pl.pallas_call(
        functools.partial(_spec_prod_kernel, tk=tk),
        grid=(nb, L // tk),
        in_specs=[pl.BlockSpec((1, 2, tk, BRANCH_W), lambda b, i: (b, 0, i, 0)),
                  pl.BlockSpec((1, 2, tk, BRANCH_W), lambda b, i: (order, 0, i, 0)),
                  pl.BlockSpec((1, 2, tk, BRANCH_W), lambda b, i: (2 + order, 0, i, 0)),
                  pl.BlockSpec((1, BRANCH_W), lambda b, i: (0, order)),
                  pl.BlockSpec((1, BRANCH_W), lambda b, i: (0, 2 + order))],
        out_specs=pl.BlockSpec((1, 2, tk, BRANCH_W), lambda b, i: (b, 0, i, 0)),
        out_shape=jax.ShapeDtypeStruct((nb, 2, L, BRANCH_W), BF16),
        compiler_params=_cp(("arbitrary", "arbitrary")),
        name="hy_spec_prod",
    )(zf4, kf4, kf4, ss, ss)
    return y.reshape(nb, 2 * L, BRANCH_W)


def _dft_inv_kernel(g_ref, y_ref, gate_ref, zp_ref, bias_ref, *rest):
    o_ref = rest[-1]
    conv = _dot(g_ref[...], y_ref[0])
    o_ref[...] = (gate_ref[...] * (conv + bias_ref[...] * zp_ref[...])).astype(o_ref.dtype)


def _dft_inv(inv, y, xs, gate_col, zprev, zprev_col, bias, L, nb, out_dtype, out_rows=None, out_base=0,
             prev=None):
    tm = min(L, 512)
    nt = L // tm
    args = [inv, y, xs, zprev, bias]
    in_specs = [pl.BlockSpec((tm, 2 * L), lambda i, b: (i, 0)),
                pl.BlockSpec((1, 2 * L, BRANCH_W), lambda i, b: (b, 0, 0)),
                pl.BlockSpec((tm, BRANCH_W), lambda i, b: (b * nt + i, gate_col)),
                pl.BlockSpec((tm, BRANCH_W), lambda i, b: (b * nt + i, zprev_col)),
                pl.BlockSpec((1, BRANCH_W), lambda i, b: (0, 0))]
    args, in_specs, alias = _into(prev, args, in_specs)
    return pl.pallas_call(
        _dft_inv_kernel,
        grid=(nt, nb),
        in_specs=in_specs,
        out_specs=pl.BlockSpec((tm, BRANCH_W), lambda i, b: ((out_base + b) * nt + i, 0)),
        out_shape=jax.ShapeDtypeStruct((out_rows or nb * L, BRANCH_W), out_dtype),
        input_output_aliases=alias,
        compiler_params=_cp(("arbitrary", "arbitrary")),
        name="hy_dft_inv",
    )(*args)


def _hyena(u_small, L, nseq, row_base, p, mats, out_rows, prev=None):
    (short_w, short_b, w1, b1, w2, b2, w3, b3, w4, freq, bias) = p
    fwd, inv = mats
    xs = _short_conv(u_small, short_w, short_b, L, nseq, row_base)
    hf, ss = _hyena_filter(L, w1, b1, w2, b2, w3, b3, w4, freq)
    kfs = _dft_fwd(fwd, hf, L, 4, lambda b: (0, b))
    zf = _dft_fwd(fwd, xs, L, nseq, lambda b: (b, 2))
    y = _spec_prod(zf, kfs, ss, 0, L, nseq)
    z1 = _dft_inv(inv, y, xs, 0, xs, 2, bias[0:1], L, nseq, F32)
    zf = _dft_fwd(fwd, z1, L, nseq, lambda b: (b, 0))
    y = _spec_prod(zf, kfs, ss, 1, L, nseq)
    return _dft_inv(inv, y, xs, 1, z1, 0, bias[1:2], L, nseq, BF16, out_rows, row_base, prev)


FFT_N1 = 64
FFT_KB = 16
FFT_COLS = 16384


def _fft_consts(L):
    N = 2 * L
    n2s = N // FFT_N1
    h = FFT_N1 // 2
    k1 = jnp.arange(FFT_N1, dtype=jnp.int32)
    ang_a = ((k1[:, None] * jnp.arange(h, dtype=jnp.int32)[None, :]) % FFT_N1).astype(F32) * (2.0 * math.pi / FFT_N1)
    ca, sa = jnp.cos(ang_a), jnp.sin(ang_a)
    fa = jnp.concatenate([ca, -sa], axis=0).astype(BF16)
    fai = (jnp.concatenate([ca.T, -sa.T], axis=1) / N).astype(BF16)
    k = k1[:, None, None] + FFT_N1 * jnp.arange(n2s, dtype=jnp.int32)[None, :, None]
    n2 = jnp.arange(n2s, dtype=jnp.int32)[None, None, :]
    ang = ((k * n2) % N).astype(F32) * (2.0 * math.pi / N)
    c, s = jnp.cos(ang), jnp.sin(ang)
    g = jnp.concatenate([jnp.concatenate([c, s], axis=2), jnp.concatenate([-s, c], axis=2)], axis=1)
    ct, st = jnp.swapaxes(c, 1, 2), jnp.swapaxes(s, 1, 2)
    gi = jnp.concatenate([jnp.concatenate([ct, -st], axis=2), jnp.concatenate([st, ct], axis=2)], axis=1)
    return fa, fai, g.astype(BF16), gi.astype(BF16)


def _fft_a_kernel(f_ref, z_ref, o_ref):
    o_ref[...] = _dot(f_ref[...], z_ref[0].astype(BF16)).astype(o_ref.dtype)


def _fft_a(fa, z3, zmap, nb, L):
    h = FFT_N1 // 2
    cols = (L // h) * BRANCH_W
    tn = min(cols, FFT_COLS)
    return pl.pallas_call(
        _fft_a_kernel,
        grid=(nb, cols // tn),
        in_specs=[pl.BlockSpec(fa.shape, lambda b, j: (0, 0)),
                  pl.BlockSpec((1, h, tn), lambda b, j: zmap(b) + (j,))],
        out_specs=pl.BlockSpec((2 * FFT_N1, tn), lambda b, j: (b, j)),
        out_shape=jax.ShapeDtypeStruct((nb * 2 * FFT_N1, cols), BF16),
        compiler_params=_cp(("arbitrary", "arbitrary")),
        name="hy_fft_a",
    )(fa, z3)


def _fft_spec_kernel(a_ref, g_ref, o_ref):
    for j in range(FFT_KB):
        s = _dot(g_ref[j], jnp.concatenate([a_ref[0, 0, j], a_ref[0, 1, j]], axis=0))
        half = s.shape[0] // 2
        o_ref[0, j, 0] = s[:half]
        o_ref[0, j, 1] = s[half:]


def _fft_spec(g, a5):
    nb, _, n1s, n2s, w = a5.shape
    return pl.pallas_call(
        _fft_spec_kernel,
        grid=(n1s // FFT_KB, nb),
        in_specs=[pl.BlockSpec((1, 2, FFT_KB, n2s, w), lambda i, b: (b, 0, i, 0, 0)),
                  pl.BlockSpec((FFT_KB, 2 * n2s, 2 * n2s), lambda i, b: (i, 0, 0))],
        out_specs=pl.BlockSpec((1, FFT_KB, 2, n2s, w), lambda i, b: (b, i, 0, 0, 0)),
        out_shape=jax.ShapeDtypeStruct((nb, n1s, 2, n2s, w), F32),
        compiler_params=_cp(("arbitrary", "arbitrary")),
        name="hy_fft_spec",
    )(a5, g)


def _fft_conv_kernel(a_ref, g_ref, gi_ref, sf_ref, sb_ref, ssf_ref, ssb_ref, o_ref):
    scale = lax.rsqrt(ssf_ref[...] + ssb_ref[...] + EPS)
    for j in range(FFT_KB):
        s = _dot(g_ref[j], jnp.concatenate([a_ref[0, 0, j], a_ref[0, 1, j]], axis=0))
        half = s.shape[0] // 2
        sr, si = s[:half], s[half:]
        kr = (sf_ref[0, j, 0] + sb_ref[0, j, 0]) * scale
        ki = (sf_ref[0, j, 1] - sb_ref[0, j, 1]) * scale
        y = jnp.concatenate([sr * kr - si * ki, sr * ki + si * kr], axis=0).astype(BF16)
        t = _dot(gi_ref[j], y)
        o_ref[0, 0, j] = t[:half].astype(BF16)
        o_ref[0, 1, j] = t[half:].astype(BF16)


def _fft_conv(g, gi, a5, kspec, ss, order):
    nb, _, n1s, n2s, w = a5.shape
    kblk = lambda d: pl.BlockSpec((1, FFT_KB, 2, n2s, w), lambda i, b: (d, i, 0, 0, 0))
    gblk = pl.BlockSpec((FFT_KB, 2 * n2s, 2 * n2s), lambda i, b: (i, 0, 0))
    ablk = pl.BlockSpec((1, 2, FFT_KB, n2s, w), lambda i, b: (b, 0, i, 0, 0))
    return pl.pallas_call(
        _fft_conv_kernel,
        grid=(n1s // FFT_KB, nb),
        in_specs=[ablk, gblk, gblk, kblk(order), kblk(2 + order),
                  pl.BlockSpec((1, w), lambda i, b: (0, order)),
                  pl.BlockSpec((1, w), lambda i, b: (0, 2 + order))],
        out_specs=ablk,
        out_shape=jax.ShapeDtypeStruct(a5.shape, BF16),
        compiler_params=_cp(("arbitrary", "arbitrary")),
        name="hy_fft_conv",
    )(a5, g, gi, kspec, kspec, ss, ss)


def _fft_ainv_kernel(f_ref, a_ref, gate_ref, zp_ref, bias_ref, o_ref):
    conv = _dot(f_ref[...], a_ref[...])
    o_ref[...] = (gate_ref[0] * (conv + bias_ref[...] * zp_ref[0])).astype(o_ref.dtype)


def _fft_ainv(fai, a2, gate3, gate_plane, zprev3, zprev_plane, bias, L, nb, out_dtype, out_view_rows):
    h = FFT_N1 // 2
    cols = (L // h) * BRANCH_W
    tn = min(cols, FFT_COLS)
    return pl.pallas_call(
        _fft_ainv_kernel,
        grid=(nb, cols // tn),
        in_specs=[pl.BlockSpec(fai.shape, lambda b, j: (0, 0)),
                  pl.BlockSpec((2 * FFT_N1, tn), lambda b, j: (b, j)),
                  pl.BlockSpec((1, h, tn), lambda b, j: (gate_plane, b, j)),
                  pl.BlockSpec((1, h, tn), lambda b, j: (zprev_plane, b, j)),
                  pl.BlockSpec((1, tn), lambda b, j: (0, 0))],
        out_specs=pl.BlockSpec((h, tn), lambda b, j: (b, j)),
        out_shape=jax.ShapeDtypeStruct((out_view_rows, cols), out_dtype),
        compiler_params=_cp(("arbitrary", "arbitrary")),
        name="hy_fft_ainv",
    )(fai, a2, gate3, zprev3, jnp.tile(bias, (1, tn // BRANCH_W)))


def _hyena_fft(u_small, L, nseq, p, consts, out_rows):
    (short_w, short_b, w1, b1, w2, b2, w3, b3, w4, freq, bias) = p
    fa, fai, g, gi = consts
    h = FFT_N1 // 2
    n2s = L // h
    cols = n2s * BRANCH_W
    xs3 = _short_conv(u_small, short_w, short_b, L, nseq, 0, planes=True).reshape(3, nseq * h, cols)
    hf4, ss = _hyena_filter(L, w1, b1, w2, b2, w3, b3, w4, freq, planes=True)
    ka = _fft_a(fa, hf4.reshape(4, h, cols), lambda b: (b, 0), 4, L)
    kspec = _fft_spec(g, ka.reshape(4, 2, FFT_N1, n2s, BRANCH_W))

    def conv(z3, zmap, order, gate_plane, zprev_plane, out_dtype, out_view_rows):
        a = _fft_a(fa, z3, zmap, nseq, L)
        a = _fft_conv(g, gi, a.reshape(nseq, 2, FFT_N1, n2s, BRANCH_W), kspec, ss, order)
        return _fft_ainv(fai, a.reshape(nseq * 2 * FFT_N1, cols), xs3, gate_plane, z3, zprev_plane,
                         bias[order:order + 1], L, nseq, out_dtype, out_view_rows)

    z1 = conv(xs3, lambda b: (2, b), 0, 0, 2, F32, nseq * h).reshape(1, nseq * h, cols)
    y = conv(z1, lambda b: (0, b), 1, 1, 0, BF16, out_rows // n2s)
    return y.reshape(out_rows, BRANCH_W)


def _conformer_kernel(a_ref, g_ref, w_ref, b_ref, lg_ref, lb_ref, *rest, L, tc):
    o_ref, pad_scr = rest[-2:]
    pad_scr[0:16, :] = jnp.zeros((16, BRANCH_W), F32)
    pad_scr[16 + L:32 + L, :] = jnp.zeros((16, BRANCH_W), F32)

    def glu(c, carry):
        t0 = pl.multiple_of(c * tc, tc)
        pad_scr[pl.ds(16 + t0, tc), :] = a_ref[pl.ds(t0, tc), :] * _sigmoid(g_ref[pl.ds(t0, tc), :])
        return carry

    lax.fori_loop(0, L // tc, glu, 0)
    b = b_ref[...]
    lg = lg_ref[...]
    lb = lb_ref[...]

    def body(c, carry):
        t0 = pl.multiple_of(c * tc, tc)
        win = pad_scr[pl.ds(t0, tc + 32), :]
        acc = jnp.zeros((tc, BRANCH_W), F32) + b
        for r in range(8):
            sh = win[r:r + tc + 24]
            for a in range(4):
                m = 8 * a + r
                if 1 <= m <= CF_WIDTH:
                    acc = acc + w_ref[m - 1:m, :] * sh[8 * a:8 * a + tc]
        mu = jnp.mean(acc, axis=-1, keepdims=True)
        xc = acc - mu
        var = jnp.mean(xc * xc, axis=-1, keepdims=True)
        y = xc * lax.rsqrt(var + EPS) * lg + lb
        o_ref[pl.ds(t0, tc), :] = (y * _sigmoid(y)).astype(o_ref.dtype)
        return carry

    lax.fori_loop(0, L // tc, body, 0)


def _conformer(u_small, L, nseq, row_base, p, out_rows, prev=None):
    dw_w, dw_b, ln_g, ln_b = p
    tc = 128
    wpad = jnp.pad(dw_w, ((0, 32 - CF_WIDTH), (0, 0)))
    row = lambda a: a.reshape(1, BRANCH_W)
    vec = pl.BlockSpec((1, BRANCH_W), lambda s: (0, 0))
    args = [u_small, u_small, wpad, row(dw_b), row(ln_g), row(ln_b)]
    in_specs = [pl.BlockSpec((L, BRANCH_W), lambda s: (row_base + s, 3)),
                pl.BlockSpec((L, BRANCH_W), lambda s: (row_base + s, 4)),
                pl.BlockSpec((32, BRANCH_W), lambda s: (0, 0)), vec, vec, vec]
    args, in_specs, alias = _into(prev, args, in_specs)
    return pl.pallas_call(
        functools.partial(_conformer_kernel, L=L, tc=tc),
        grid=(nseq,),
        in_specs=in_specs,
        out_specs=pl.BlockSpec((L, BRANCH_W), lambda s: (row_base + s, 0)),
        out_shape=jax.ShapeDtypeStruct((out_rows, BRANCH_W), BF16),
        input_output_aliases=alias,
        scratch_shapes=[pltpu.VMEM((L + 32, BRANCH_W), F32)],
        compiler_params=_cp(("arbitrary",)),
        name="conformer",
    )(*args)


def _rope_tables(S, pad_rows, head):
    half = head // 2
    nf = half // 2
    lane = np.arange(LANES)
    inv_lane = (ROPE_BASE ** (-(np.arange(nf)) / nf))[(lane % half) % nf]
    is_row = (lane % head) < half
    pos = jnp.arange(S, dtype=jnp.int32)
    rows = (pos // GRID_W).astype(F32)[:, None]
    cols = (pos % GRID_W).astype(F32)[:, None]
    ang = jnp.where(jnp.asarray(is_row)[None, :], rows, cols) * jnp.asarray(inv_lane, F32)[None, :]
    cos = jnp.concatenate([jnp.cos(ang), jnp.ones((pad_rows, LANES), F32)], axis=0)
    sin = jnp.concatenate([jnp.sin(ang), jnp.zeros((pad_rows, LANES), F32)], axis=0)
    return cos, sin


def _group_ones(width, group):
    idx = np.arange(width)
    return jnp.asarray((idx[:, None] // group) == (idx[None, :] // group), BF16)


def _head_norm_rope(x, ones, group, gain, cos, sin, nf, out_scale):
    w = x.shape[1]
    hi, lo = _split(x * x)
    ms = (_dot(hi, ones) + _dot(lo, ones)) * (1.0 / group)
    xn = x * lax.rsqrt(ms + EPS) * gain
    reps = w // LANES
    c = jnp.concatenate([cos] * reps, axis=1) if reps > 1 else cos
    s = jnp.concatenate([sin] * reps, axis=1) if reps > 1 else sin
    lane = lax.broadcasted_iota(jnp.int32, x.shape, 1)
    first = (lane % (2 * nf)) < nf
    rot = jnp.where(first, -pltpu.roll(xn, w - nf, 1), pltpu.roll(xn, nf, 1))
    return (xn * c + rot * s) * out_scale


def _qkv_kernel(gq_ref, gkv_ref, dq_ref, dk_ref, dv_ref, cg_ref, sg_ref, cd_ref, sd_ref,
                o64_ref, o32_ref, gqn_ref, gkn_ref, dqn_ref, dkn_ref,
                qg_ref, kg_ref, vag_ref, vbg_ref, qd_ref, kd_ref, vad_ref, vbd_ref):
    cg, sg, cd, sd = cg_ref[...], sg_ref[...], cd_ref[...], sd_ref[...]
    o64, o32 = o64_ref[...], o32_ref[...]
    q = _head_norm_rope(gq_ref[...], o64, HEAD_DIM, gqn_ref[...], cg, sg, HEAD_DIM // 4, LOG2E * HEAD_DIM ** -0.5)
    qg_ref[...] = q.astype(BF16)
    kv = gkv_ref[...]
    k = _head_norm_rope(kv[:, :LANES], o64[:LANES, :LANES], HEAD_DIM, gkn_ref[...], cg, sg, HEAD_DIM // 4, 1.0)
    v = kv[:, LANES:]
    kk = jnp.concatenate([k, k], axis=1)
    vv = jnp.concatenate([v, v], axis=1)
    quarter = lax.broadcasted_iota(jnp.int32, kk.shape, 1) // HEAD_DIM
    kr = pltpu.roll(kk, HEAD_DIM, 1)
    vr = pltpu.roll(vv, HEAD_DIM, 1)
    kg_ref[...] = jnp.where((quarter == 0) | (quarter == 3), kk, kr).astype(BF16)
    vag_ref[...] = jnp.where(quarter == 0, vv, jnp.where(quarter == 2, vr, 0.0)).astype(BF16)
    vbg_ref[...] = jnp.where(quarter == 1, vr, jnp.where(quarter == 3, vv, 0.0)).astype(BF16)
    qd = _head_norm_rope(dq_ref[...], o32, DIFF_D, dqn_ref[...], cd, sd, DIFF_D // 4, LOG2E * DIFF_D ** -0.5)
    qd_ref[...] = qd.astype(BF16)
    kd = _head_norm_rope(dk_ref[...], o32, DIFF_D, dkn_ref[...], cd, sd, DIFF_D // 4, 1.0)
    kd_ref[...] = kd.astype(BF16)
    vd = dv_ref[...]
    even = (lax.broadcasted_iota(jnp.int32, vd.shape, 1) // HEAD_DIM) % 2 == 0
    vad_ref[...] = jnp.where(even, vd, 0.0).astype(BF16)
    vbd_ref[...] = jnp.where(even, 0.0, vd).astype(BF16)


def _qkv_prep(u_small, gqn, gkn, dqn, dkn, tables, dims):
    n = u_small.shape[0]
    tm = dims["tm_small"]
    S = dims["S"]
    n_lat_tiles = dims["n_lat"] // tm
    per_seq = S // tm
    tmap = lambda i: (jnp.where(i < n_lat_tiles, i % per_seq, per_seq), 0)
    col = lambda c: pl.BlockSpec((tm, BRANCH_W), lambda i: (i, c))
    tab = pl.BlockSpec((tm, LANES), tmap)
    full = lambda a: pl.BlockSpec(a.shape, lambda i: (0,) * a.ndim)
    o64, o32 = _group_ones(BRANCH_W, HEAD_DIM), _group_ones(BRANCH_W, DIFF_D)
    gains = [jnp.tile(gqn, 4).reshape(1, 256), jnp.tile(gkn, 2).reshape(1, 128),
             jnp.tile(dqn.reshape(-1), 4).reshape(1, 256), jnp.tile(dkn.reshape(-1), 4).reshape(1, 256)]
    out = pl.BlockSpec((tm, BRANCH_W), lambda i: (i, 0))
    return pl.pallas_call(
        _qkv_kernel,
        grid=(n // tm,),
        in_specs=[col(5), col(6), col(7), col(8), col(9), tab, tab, tab, tab, full(o64), full(o32)]
        + [full(g) for g in gains],
        out_specs=[out] * 8,
        out_shape=[jax.ShapeDtypeStruct((n, BRANCH_W), BF16)] * 8,
        compiler_params=_cp(("arbitrary",)),
        name="qkv_prep",
    )(u_small, u_small, u_small, u_small, u_small, *tables, o64, o32, *gains)


def _lane_pick(lane_lo, a, b):
    return jnp.where(lane_lo, a, b)


def _flash_kernel(*refs, segs, tk, tq, diff, lam_init):
    q_ref, o_ref = refs[0], refs[-1]
    seg_refs = [refs[1 + 3 * i:4 + 3 * i] for i in range(len(segs))]
    aux_ref = refs[1 + 3 * len(segs)]
    lane = lax.broadcasted_iota(jnp.int32, (1, LANES), 1)
    lane_lo = lane < HEAD_DIM
    if diff:
        masks = [(lane >= g * DIFF_D) & (lane < (g + 1) * DIFF_D) for g in range(4)]
        acc_of = [0, 1, 0, 1]
        use_a = [True, True, False, False]
        n_acc = 2
    else:
        masks = [lane_lo, ~lane_lo]
        acc_of = [0, 0]
        use_a = [True, False]
        n_acc = 1
    R = len(masks)
    pairs = [slice(p * LANES, (p + 1) * LANES) for p in range(2)]
    qsts = [jnp.concatenate([jnp.where(m, q_ref[:, ps], jnp.zeros((tq, LANES), BF16)) for m in masks], axis=0)
            for ps in pairs]
    lo_hi = []
    for a in range(n_acc):
        rs = [r for r in range(R) if acc_of[r] == a]
        lo_hi.append(([r for r in rs if use_a[r]][0], [r for r in rs if not use_a[r]][0]))

    def rows(x, r):
        return x[r * tq:(r + 1) * tq]

    shift = jnp.max(aux_ref[5:6, :])

    def run(fixed):
        def chunk(kv_refs, t0, size, carry):
            k_ref, va_ref, vb_ref = kv_refs
            new = []
            for p, ps in enumerate(pairs):
                m_run, l_run, accs = carry[p]
                k = k_ref[pl.ds(t0, size), ps]
                s = lax.dot_general(qsts[p], k, (((1,), (1,)), ((), ())), preferred_element_type=F32)
                if fixed:
                    m_new = m_run
                    pr = jnp.exp2(s - shift)
                    l_new = l_run + jnp.sum(pr, axis=-1, keepdims=True)
                else:
                    m_new = jnp.maximum(m_run, jnp.max(s, axis=-1, keepdims=True))
                    alpha = jnp.exp2(m_run - m_new)
                    pr = jnp.exp2(s - m_new)
                    l_new = alpha * l_run + jnp.sum(pr, axis=-1, keepdims=True)
                prb = pr.astype(BF16)
                va = va_ref[pl.ds(t0, size), ps]
                vb = vb_ref[pl.ds(t0, size), ps]
                new_accs = []
                for a, (r_lo, r_hi) in enumerate(lo_hi):
                    upd = _dot(rows(prb, r_lo), va) + _dot(rows(prb, r_hi), vb)
                    if fixed:
                        new_accs.append(accs[a] + upd)
                    else:
                        al = _lane_pick(lane_lo, rows(alpha, r_lo), rows(alpha, r_hi))
                        new_accs.append(accs[a] * al + upd)
                new.append((m_new, l_new, tuple(new_accs)))
            return tuple(new)

        one = (jnp.full((R * tq, 1), -jnp.inf, F32), jnp.zeros((R * tq, 1), F32),
               tuple(jnp.zeros((tq, LANES), F32) for _ in range(n_acc)))
        carry = (one, one)
        for kv_refs, T in zip(seg_refs, segs):
            n_main = T // tk
            if n_main:
                carry = lax.fori_loop(
                    0, n_main, lambda c, cr, kv_refs=kv_refs: chunk(kv_refs, pl.multiple_of(c * tk, tk), tk, cr),
                    carry, unroll=4 if (fixed and not diff and n_main % 4 == 0) else 2 if n_main % 2 == 0 else 1)
            if T - n_main * tk:
                carry = chunk(kv_refs, n_main * tk, T - n_main * tk, carry)

        outs = []
        for p in range(2):
            _, l_fin, accs = carry[p]
            inv_l = 1.0 / l_fin
            norm = [accs[a] * _lane_pick(lane_lo, rows(inv_l, r_lo), rows(inv_l, r_hi))
                    for a, (r_lo, r_hi) in enumerate(lo_hi)]
            if diff:
                aux = aux_ref[...]
                lam = (jnp.exp(jnp.sum(aux[0:1] * aux[1:2], axis=-1, keepdims=True))
                       - jnp.exp(jnp.sum(aux[2:3] * aux[3:4], axis=-1, keepdims=True)) + lam_init)
                o = norm[0] - lam * norm[1]
                sq = o * o
                s_lo = jnp.sum(jnp.where(lane_lo, sq, 0.0), axis=-1, keepdims=True)
                s_hi = jnp.sum(jnp.where(lane_lo, 0.0, sq), axis=-1, keepdims=True)
                ms = _lane_pick(lane_lo, s_lo, s_hi) * (1.0 / HEAD_DIM)
                o = o * lax.rsqrt(ms + EPS) * aux[4:5] * (1.0 - lam_init)
            else:
                o = norm[0]
            outs.append(o)
        o_ref[...] = jnp.concatenate(outs, axis=1).astype(o_ref.dtype)

    @pl.when(shift < FIXED_SHIFT_LIMIT)
    def _():
        run(True)

    @pl.when(jnp.logical_not(shift < FIXED_SHIFT_LIMIT))
    def _():
        run(False)


def _into(prev, args, in_specs):
    if prev is None:
        return args, in_specs, {}
    return args + [prev], in_specs + [pl.BlockSpec(memory_space=pl.ANY)], {len(args): 0}


def _flash(q, k, va, vb, aux, nb, Lq, q_base, segs, diff, lam_init, out_rows, out_base, prev=None):
    tq = min(512, Lq)
    tk = 1024
    nt = Lq // tq
    q_base, out_base = q_base // tq, out_base // tq
    args = [q]
    in_specs = [pl.BlockSpec((tq, BRANCH_W), lambda b, i: (q_base + b * nt + i, 0))]
    for length, base in segs:
        spec = pl.BlockSpec((length, BRANCH_W), lambda b, i, base=base: (base + b, 0))
        args += [k, va, vb]
        in_specs += [spec, spec, spec]
    args.append(aux)
    in_specs.append(pl.BlockSpec((8, LANES), lambda b, i: (0, 0)))
    args, in_specs, alias = _into(prev, args, in_specs)
    return pl.pallas_call(
        functools.partial(_flash_kernel, segs=tuple(s[0] for s in segs), tk=tk, tq=tq, diff=diff,
                          lam_init=lam_init),
        grid=(nb, nt),
        in_specs=in_specs,
        out_specs=pl.BlockSpec((tq, BRANCH_W), lambda b, i: (out_base + b * nt + i, 0)),
        out_shape=jax.ShapeDtypeStruct((out_rows, BRANCH_W), BF16),
        input_output_aliases=alias,
        compiler_params=_cp(("arbitrary", "arbitrary"), VMEM_LIMIT_BIG),
        name="flash_diff" if diff else "flash_gqa",
    )(*args)


def _merge_kernel(y0, y1, y2, y3, g_ref, wb_ref, wo_ref, x_ref, mod_ref, o_ref):
    d = D_MODEL
    acc = g_ref[:, 0:d].astype(F32) * _dot(y0[...], wb_ref[0])
    acc = acc + g_ref[:, d:2 * d].astype(F32) * _dot(y1[...], wb_ref[1])
    acc = acc + g_ref[:, 2 * d:3 * d].astype(F32) * _dot(y2[...], wb_ref[2])
    acc = acc + g_ref[:, 3 * d:4 * d].astype(F32) * _dot(y3[...], wb_ref[3])
    mix = _dot(acc.astype(BF16), wo_ref[...])
    o_ref[...] = x_ref[...] + mod_ref[0][2:3, :] * mix


def _merge(ys, gate, wb, wo, x, mods_l, dims):
    n, d = ys[0].shape[0], x.shape[1]
    tm = dims["tm_small"]
    mrow = dims["mod_row"](tm)
    yspec = pl.BlockSpec((tm, BRANCH_W), lambda i: (i, 0))
    return pl.pallas_call(
        _merge_kernel,
        grid=(n // tm,),
        in_specs=[yspec] * 4 + [pl.BlockSpec((tm, GATE_COLS), lambda i: (i, 0)),
                                pl.BlockSpec(wb.shape, lambda i: (0, 0, 0)),
                                pl.BlockSpec(wo.shape, lambda i: (0, 0)),
                                pl.BlockSpec((tm, d), lambda i: (i, 0)),
                                pl.BlockSpec((1, 6, d), lambda i: (mrow(i), 0, 0))],
        out_specs=pl.BlockSpec((tm, d), lambda i: (i, 0)),
        out_shape=jax.ShapeDtypeStruct((n, d), F32),
        compiler_params=_cp(("arbitrary",)),
        name="merge",
    )(*ys, gate, wb, wo, x, mods_l)


def _ffn_kernel(x_ref, mod_ref, g_ref, w1_ref, w3_ref, w2_ref, o_ref, h_scr, acc_scr):
    j = pl.program_id(1)

    @pl.when(j == 0)
    def _():
        m = mod_ref[0]
        h_scr[...] = _norm_mod(x_ref[...], g_ref[...], m[3:4, :], m[4:5, :]).astype(BF16)
        acc_scr[...] = jnp.zeros_like(acc_scr)

    h = h_scr[...]
    a = _dot(h, w1_ref[...])
    b = _dot(h, w3_ref[...])
    t = (a * _sigmoid(a) * b).astype(BF16)
    acc_scr[...] += _dot(t, w2_ref[...])

    @pl.when(j == pl.num_programs(1) - 1)
    def _():
        o_ref[...] = x_ref[...] + mod_ref[0][5:6, :] * acc_scr[...]


def _ffn(x, mods_l, gain, w1, w3, w2, dims):
    n, d = x.shape
    tm = dims["tm_small"]
    tf = w1.shape[1] // 2 if (w1.shape[1] // 2) % LANES == 0 else 256
    mrow = dims["mod_row"](tm)
    return pl.pallas_call(
        _ffn_kernel,
        grid=(n // tm, w1.shape[1] // tf),
        in_specs=[pl.BlockSpec((tm, d), lambda i, j: (i, 0)),
                  pl.BlockSpec((1, 6, d), lambda i, j: (mrow(i), 0, 0)),
                  pl.BlockSpec((1, d), lambda i, j: (0, 0)),
                  pl.BlockSpec((d, tf), lambda i, j: (0, j)),
                  pl.BlockSpec((d, tf), lambda i, j: (0, j)),
                  pl.BlockSpec((tf, d), lambda i, j: (j, 0))],
        out_specs=pl.BlockSpec((tm, d), lambda i, j: (i, 0)),
        out_shape=jax.ShapeDtypeStruct((n, d), F32),
        scratch_shapes=[pltpu.VMEM((tm, d), BF16), pltpu.VMEM((tm, d), F32)],
        compiler_params=_cp(("arbitrary", "arbitrary")),
        name="ffn_dense",
    )(x, mods_l, gain, w1, w3, w2)


def _route_kernel(x_ref, mod_ref, g_ref, wr_ref, tok_ref, gate_ref):
    m = mod_ref[0]
    h = _norm_mod(x_ref[...], g_ref[...], m[3:4, :], m[4:5, :])
    tok_ref[...] = h.astype(BF16)
    logits = _dot3(h, wr_ref[...])
    lane = lax.broadcasted_iota(jnp.int32, logits.shape, 1)
    lg = jnp.where(lane < N_EXPERTS, logits, -jnp.inf)
    m1 = jnp.max(lg, axis=-1, keepdims=True)
    i1 = jnp.min(jnp.where(lg == m1, lane, LANES), axis=-1, keepdims=True)
    lg2 = jnp.where(lane == i1, -jnp.inf, lg)
    m2 = jnp.max(lg2, axis=-1, keepdims=True)
    i2 = jnp.min(jnp.where(lg2 == m2, lane, LANES), axis=-1, keepdims=True)
    e2 = jnp.exp(m2 - m1)
    g1 = 1.0 / (1.0 + e2)
    gate_ref[...] = jnp.where(lane == i1, g1, jnp.where(lane == i2, e2 * g1, 0.0))


def _route(x, mods_l, gain, w_router, dims):
    n, d = x.shape
    tm = dims["tm_small"]
    mrow = dims["mod_row"](tm)
    wr = jnp.pad(w_router, ((0, 0), (0, LANES - N_EXPERTS)))
    return pl.pallas_call(
        _route_kernel,
        grid=(n // tm,),
        in_specs=[pl.BlockSpec((tm, d), lambda i: (i, 0)),
                  pl.BlockSpec((1, 6, d), lambda i: (mrow(i), 0, 0)),
                  pl.BlockSpec((1, d), lambda i: (0, 0)),
                  pl.BlockSpec((d, LANES), lambda i: (0, 0))],
        out_specs=[pl.BlockSpec((tm, d), lambda i: (i, 0)), pl.BlockSpec((tm, LANES), lambda i: (i, 0))],
        out_shape=[jax.ShapeDtypeStruct((n, d), BF16), jax.ShapeDtypeStruct((n, LANES), F32)],
        compiler_params=_cp(("arbitrary",)),
        name="moe_route",
    )(x, mods_l, gain, wr)


MOE_SUB = 128
MOE_GATHER = 256
MOE_GRAN = 64


def _moe_kernel(tok_ref, gate_ref, tri_ref, w1_ref, w3_ref, w2_ref, x_ref, mod_ref, o_ref,
                rank_scr, rank_t_scr, mask_t_scr, xe_scr, ye_scr, cnt_smem, *, tm):
    e = pl.program_id(1)
    j = pl.program_id(2)
    ne = pl.num_programs(1)
    nj = pl.num_programs(2)
    lane = lax.broadcasted_iota(jnp.int32, (1, LANES), 1)

    @pl.when((e == 0) & (j == 0))
    def _():
        tri = tri_ref[...]
        carry = jnp.zeros((1, LANES), F32)
        def put_counts(blk, counts):
            for ee in range(N_EXPERTS):
                cnt_smem[blk * N_EXPERTS + ee] = jnp.sum(jnp.where(lane == ee, counts, 0.0)).astype(jnp.int32)

        for blk in range(tm // 256):
            put_counts(blk, carry)
            rs = slice(blk * 256, (blk + 1) * 256)
            msk = (gate_ref[rs, :] > 0.0).astype(BF16)
            rank_scr[rs, :] = _dot(tri, msk) + carry
            carry = carry + jnp.sum(msk.astype(F32), axis=0, keepdims=True)
        rank_t_scr[...] = jnp.transpose(rank_scr[...])
        mask_t_scr[...] = jnp.transpose((gate_ref[...] > 0.0).astype(F32))
        put_counts(tm // 256, carry)
        o_ref[...] = jnp.zeros_like(o_ref)
        ye_scr[...] = jnp.zeros_like(ye_scr)

    cnt = cnt_smem[(tm // 256) * N_EXPERTS + e]

    n_sub = (cnt + (MOE_GATHER - 1)) // MOE_GATHER

    @pl.when(j == 0)
    def _():
        rk = rank_t_scr[pl.ds(e, 1), :]
        mk = mask_t_scr[pl.ds(e, 1), :]

        def gather(sb, carry):
            r0 = pl.multiple_of(sb * MOE_GATHER, MOE_GATHER)
            r_iota = lax.broadcasted_iota(jnp.int32, (MOE_GATHER, tm), 0) + r0
            sel = jnp.where((rk == r_iota.astype(F32)) & (mk > 0.0), 1.0, 0.0).astype(BF16)
            xe_scr[pl.ds(r0, MOE_GATHER), :] = _dot(sel, tok_ref[...]).astype(BF16)
            return carry

        lax.fori_loop(0, n_sub, gather, 0)

    def expert_rows(r0, size):
        xe = xe_scr[pl.ds(r0, size), :]
        a = _dot(xe, w1_ref[0])
        b = _dot(xe, w3_ref[0])
        t = (a * _sigmoid(a) * b).astype(BF16)
        y = _dot(t, w2_ref[0])
        ye_scr[pl.ds(r0, size), :] = jnp.where(j == 0, y, ye_scr[pl.ds(r0, size), :] + y)

    n_gran = (cnt + (MOE_GRAN - 1)) // MOE_GRAN
    n_big = n_gran // 4
    rem = n_gran % 4

    def big(i, carry):
        expert_rows(pl.multiple_of(i * 256, 256), 256)
        return carry

    lax.fori_loop(0, n_big - 1, big, 0)
    for r in range(min(4, (tm - 256) // MOE_GRAN + 1)):
        @pl.when((n_big >= 1) & (rem == r))
        def _():
            expert_rows(pl.multiple_of((n_big - 1) * 256, 256), 256 + r * MOE_GRAN)

    @pl.when((n_big == 0) & (rem >= 2))
    def _():
        expert_rows(0, 128)

    @pl.when((n_big == 0) & (rem % 2 == 1))
    def _():
        expert_rows(pl.multiple_of((rem // 2) * 128, 128), 64)

    @pl.when(j == nj - 1)
    def _():
        def scatter(rc, carry):
            r0 = pl.multiple_of(rc * 256, 256)
            gt = gate_ref[pl.ds(r0, 256), :]
            g_e = jnp.sum(jnp.where(lane == e, gt, 0.0), axis=-1, keepdims=True)
            r_e = jnp.sum(jnp.where(lane == e, rank_scr[pl.ds(r0, 256), :], 0.0), axis=-1, keepdims=True)
            lo = cnt_smem[rc * N_EXPERTS + e]
            hi = cnt_smem[(rc + 1) * N_EXPERTS + e]

            def scatter_sub(sb, c2):
                c0 = pl.multiple_of(sb * MOE_SUB, MOE_SUB)
                c_iota = lax.broadcasted_iota(jnp.int32, (256, MOE_SUB), 1) + c0
                sel_t = jnp.where((r_e == c_iota.astype(F32)) & (g_e > 0.0), 1.0, 0.0).astype(BF16)
                ye = ye_scr[pl.ds(c0, MOE_SUB), :].astype(BF16)
                o_ref[pl.ds(r0, 256), :] += g_e * _dot(sel_t, ye)
                return c2

            lax.fori_loop(lo // MOE_SUB, (hi + (MOE_SUB - 1)) // MOE_SUB, scatter_sub, 0)

            @pl.when(e == ne - 1)
            def _():
                o_ref[pl.ds(r0, 256), :] = (x_ref[pl.ds(r0, 256), :]
                                            + mod_ref[0][5:6, :] * o_ref[pl.ds(r0, 256), :])
            return carry

        lax.fori_loop(0, tm // 256, scatter, 0)


def _moe(tok, gates, w1, w3, w2, x, mods_l, dims):
    n, d = tok.shape
    tm = dims["tm_moe"] if n % dims["tm_moe"] == 0 else dims["tm_big"]
    mrow = dims["mod_row"](tm)
    ne, _, ff = w1.shape
    tf = 896
    tri =jnp.asarray(np.tril(np.ones((256, 256)), -1), BF16)
    once = pl.Buffered(1)
    return pl.pallas_call(
        functools.partial(_moe_kernel, tm=tm),
        grid=(n // tm, ne, ff // tf),
        in_specs=[pl.BlockSpec((tm, d), lambda i, e, j: (i, 0), pipeline_mode=once),
                  pl.BlockSpec((tm, LANES), lambda i, e, j: (i, 0), pipeline_mode=once),
                  pl.BlockSpec((256, 256), lambda i, e, j: (0, 0), pipeline_mode=once),
                  pl.BlockSpec((1, d, tf), lambda i, e, j: (e, 0, j)),
                  pl.BlockSpec((1, d, tf), lambda i, e, j: (e, 0, j)),
                  pl.BlockSpec((1, tf, d), lambda i, e, j: (e, j, 0)),
                  pl.BlockSpec((tm, d), lambda i, e, j: (i, 0), pipeline_mode=once),
                  pl.BlockSpec((1, 6, d), lambda i, e, j: (mrow(i), 0, 0))],
        out_specs=pl.BlockSpec((tm, d), lambda i, e, j: (i, 0), pipeline_mode=once),
        out_shape=jax.ShapeDtypeStruct((n, d), F32),
        scratch_shapes=[pltpu.VMEM((tm, LANES), F32), pltpu.VMEM((LANES, tm), F32), pltpu.VMEM((LANES, tm), F32),
                        pltpu.VMEM((tm, d), BF16), pltpu.VMEM((tm, d), F32), pltpu.SMEM(((tm // 256 + 1) * N_EXPERTS,), jnp.int32)],
        compiler_params=_cp(("arbitrary", "arbitrary", "arbitrary"), VMEM_LIMIT_BIG),
        name="moe_experts",
    )(tok, gates, tri, w1, w3, w2, x, mods_l)


def _make_dims(B, S, n_ctx):
    n_lat, n = B * S, B * (S + n_ctx)

    def pick(prefs):
        for t in prefs:
            if S % t == 0 and n_lat % t == 0 and (n - n_lat) % t == 0:
                return t
        raise ValueError("no row tile divides the latent and context token counts")

    def mod_row(tm):
        return lambda i: jnp.minimum((i * tm) // S, B)

    return {"B": B, "S": S, "n_ctx": n_ctx, "n_lat": n_lat, "n": n,
            "tm_big": pick((1024, 512, 256)), "tm_small": pick((512, 256)), "tm_moe": pick((2048, 1024, 512, 256)),
            "mod_row": mod_row}


def kernel(x, c, ctx, c_ctx, w_ada, b_ada, norm_mix, norm_ffn, w_in, hy_short_w, hy_short_b, hy_w1, hy_b1, hy_w2, hy_b2, hy_w3, hy_b3, hy_w4, hy_freq, hy_bias, cf_dw_w, cf_dw_b, cf_ln_g, cf_ln_b, gqa_qn, gqa_kn, diff_qn, diff_kn, diff_lq1, diff_lk1, diff_lq2, diff_lk2, diff_subln, w_branch, w_out, ffn_w1, ffn_w3, ffn_w2, moe_router, moe_w1, moe_w3, moe_w2):
    B, S, D = x.shape
    n_ctx = ctx.shape[1]
    depth = w_ada.shape[0]
    dims = _make_dims(B, S, n_ctx)
    n_lat = dims["n_lat"]
    tm_s = dims["tm_small"]

    c16 = jnp.concatenate([c, c_ctx[None, :], jnp.zeros((16 - B - 1, D), F32)], axis=0)
    mods = _ada_mods(c16, w_ada, b_ada).reshape(depth, 16, 6, D)

    xa = jnp.concatenate([x.reshape(n_lat, D), ctx.reshape(B * n_ctx, D)], axis=0)
    tables = _rope_tables(S, tm_s, HEAD_DIM) + _rope_tables(S, tm_s, DIFF_D)
    fft_lat = _fft_consts(S)
    mats_ctx = _dft_mats(n_ctx)

    for l in range(depth):
        lam_init = 0.8 - 0.6 * math.exp(-0.3 * l)
        u_small, gate = _in_proj(xa, mods[l], norm_mix[l].reshape(1, D), w_in[l].astype(BF16), dims)

        hy_p = (hy_short_w[l], hy_short_b[l], hy_w1[l], hy_b1[l], hy_w2[l], hy_b2[l], hy_w3[l], hy_b3[l],
                hy_w4[l], hy_freq[l], hy_bias[l])
        cf_p = (cf_dw_w[l], cf_dw_b[l], cf_ln_g[l], cf_ln_b[l])
        last = l == depth - 1
        n_out = n_lat if last else dims["n"]
        ctx_base = n_lat // n_ctx
        y_hy = _hyena_fft(u_small, S, B, hy_p, fft_lat, n_out)
        y_cf = _conformer(u_small, S, B, 0, cf_p, n_out)
        if not last:
            y_hy = _hyena(u_small, n_ctx, B, ctx_base, hy_p, mats_ctx, n_out, y_hy)
            y_cf = _conformer(u_small, n_ctx, B, ctx_base, cf_p, n_out, y_cf)

        qg, kg, vag, vbg, qd, kd, vad, vbd = _qkv_prep(u_small, gqa_qn[l], gqa_kn[l], diff_qn[l], diff_kn[l],
                                                      tables, dims)

        aux = jnp.zeros((8, LANES), F32)
        aux = aux.at[0, :DIFF_D].set(diff_lq1[l]).at[1, :DIFF_D].set(diff_lk1[l])
        aux = aux.at[2, :DIFF_D].set(diff_lq2[l]).at[3, :DIFF_D].set(diff_lk2[l])
        aux = aux.at[4, :].set(jnp.tile(diff_subln[l], 2))

        def score_bound(qn, kn, d):
            return 1.02 * d * jnp.max(jnp.abs(qn)) * jnp.max(jnp.abs(kn)) * (LOG2E * d ** -0.5) + 0.1

        bounds = (score_bound(gqa_qn[l], gqa_kn[l], HEAD_DIM), score_bound(diff_qn[l], diff_kn[l], DIFF_D))
        y_att = []
        for (q, k, va, vb, is_diff) in ((qg, kg, vag, vbg, False), (qd, kd, vad, vbd, True)):
            aux = aux.at[5, :].set(bounds[int(is_diff)])
            y = _flash(q, k, va, vb, aux, B, S, 0, ((S, 0), (n_ctx, ctx_base)), is_diff, lam_init, n_out, 0)
            if not last:
                y = _flash(q, k, va, vb, aux, B, n_ctx, n_lat, ((n_ctx, ctx_base),), is_diff, lam_init,
                           n_out, n_lat, y)
            y_att.append(y)

        xa = _merge((y_hy, y_cf, y_att[0], y_att[1]), gate, w_branch[l].astype(BF16), w_out[l].astype(BF16),
                    xa, mods[l], dims)

        i = l // 2
        if l % 2 == 0:
            xa = _ffn(xa, mods[l], norm_ffn[l].reshape(1, D), ffn_w1[i].astype(BF16), ffn_w3[i].astype(BF16),
                      ffn_w2[i].astype(BF16), dims)
        else:
            tok, gates = _route(xa, mods[l], norm_ffn[l].reshape(1, D), moe_router[i], dims)
            xa = _moe(tok, gates, moe_w1[i].astype(BF16), moe_w3[i].astype(BF16), moe_w2[i].astype(BF16),
                      xa, mods[l], dims)
    return xa[:n_lat].reshape(B, S, D)
```
